```python
import math
import jax, jax.numpy as jnp
from jax import lax
import numpy as np

D_MODEL = 1024
BATCH = 2
SEQ = 8192
DEPTH = 1

CHUNK = 64
Q_BLOCK = 128
D_PLE = 256
D_MIX = D_MODEL
RET_WIDTH = D_MIX // 2
RET_HEADS = 4
RET_HEAD_DIM = RET_WIDTH // RET_HEADS
MLA_HEADS = 4
MLA_NOPE_DIM = 128
MLA_ROPE_DIM = 64
MLA_V_DIM = (D_MIX - RET_WIDTH) // MLA_HEADS
MLA_QK_DIM = MLA_NOPE_DIM + MLA_ROPE_DIM
Q_LORA = 256
KV_LORA = 128
D_FF = -(-8 * D_MODEL // (3 * 256)) * 256
ROPE_BASE = 10000.0
RMS_EPS = 1e-6
GN_EPS = 1e-5

IN_SIZES = (RET_WIDTH, RET_WIDTH, RET_WIDTH, RET_WIDTH, Q_LORA, KV_LORA, MLA_ROPE_DIM)
IN_COLS = sum(IN_SIZES)
IN_SPLITS = tuple(int(v) for v in np.cumsum(IN_SIZES)[:-1])

kernel_name = "hybrid_retention_mla_block"


def rmsnorm(x, g):
    xf = x.astype(jnp.float32)
    y = xf * lax.rsqrt(jnp.mean(xf * xf, axis=-1, keepdims=True) + RMS_EPS)
    return (y * g.astype(jnp.float32)).astype(x.dtype)


def rope(x, pos):
    half = x.shape[-1] // 2
    inv = jnp.exp(-math.log(ROPE_BASE) * jnp.arange(half, dtype=jnp.float32) / half)
    ang = pos.astype(jnp.float32)[..., None] * inv
    cos = jnp.cos(ang)[:, :, None, :]
    sin = jnp.sin(ang)[:, :, None, :]
    xf = x.astype(jnp.float32)
    x1, x2 = xf[..., :half], xf[..., half:]
    return jnp.concatenate([x1 * cos - x2 * sin, x2 * cos + x1 * sin], axis=-1).astype(x.dtype)


def retention(q, k, v, gate, pos):
    B, S, H, d = q.shape
    n = S // CHUNK
    dt = q.dtype
    q = rope(q, pos)
    k = rope(k, pos) * (d ** -0.5)
    q = q.reshape(B, n, CHUNK, H, d)
    k = k.reshape(B, n, CHUNK, H, d)
    v = v.reshape(B, n, CHUNK, H, d)
    log_g = jnp.log1p(-jnp.exp2(-5.0 - jnp.arange(H, dtype=jnp.float32)))
    idx = jnp.arange(CHUNK, dtype=jnp.float32)
    intra_dec = jnp.exp(log_g[:, None, None] * jnp.abs(idx[:, None] - idx[None, :]))
    q_dec = jnp.exp(log_g[:, None] * (idx + 1.0))
    k_dec = jnp.exp(log_g[:, None] * (CHUNK - 1.0 - idx))
    c_dec = jnp.exp(log_g * CHUNK)[None, :, None, None]
    s = jnp.einsum('bnihd,bnjhd->bnhij', q, k) * intra_dec.astype(dt)
    o_intra = jnp.einsum('bnhij,bnjhe->bnihe', s, v)
    kv = jnp.einsum('bnjhd,hj,bnjhe->nbhde', k, k_dec.astype(dt), v)

    def step(state, kv_c):
        return c_dec * state + kv_c.astype(jnp.float32), state

    _, prev = lax.scan(step, jnp.zeros((B, H, d, d), jnp.float32), kv)
    o_cross = jnp.einsum('bnihd,nbhde,hi->bnihe', q, prev.astype(dt), q_dec.astype(dt))
    o = (o_intra + o_cross).reshape(B, S, H, d).astype(jnp.float32)
    mu = jnp.mean(o, axis=-1, keepdims=True)
    var = jnp.mean(jnp.square(o - mu), axis=-1, keepdims=True)
    o = ((o - mu) * lax.rsqrt(var + GN_EPS)).reshape(B, S, H * d).astype(dt)
    return o * jax.nn.silu(gate)


def mla(c_q, c_kv, k_pe, pos, q_norm_g, w_uq, kv_norm_g, w_ukv):
    B, S, _ = c_q.shape
    H = MLA_HEADS
    q = (rmsnorm(c_q, q_norm_g) @ w_uq).reshape(B, S, H, MLA_QK_DIM)
    q = jnp.concatenate([q[..., :MLA_NOPE_DIM], rope(q[..., MLA_NOPE_DIM:], pos)], axis=-1)
    kv = (rmsnorm(c_kv, kv_norm_g) @ w_ukv).reshape(B, S, H, MLA_NOPE_DIM + MLA_V_DIM)
    k_nope, v = kv[..., :MLA_NOPE_DIM], kv[..., MLA_NOPE_DIM:]
    k_rot = rope(k_pe[:, :, None, :], pos)
    k = jnp.concatenate([k_nope, jnp.broadcast_to(k_rot, (B, S, H, MLA_ROPE_DIM))], axis=-1)
    nb = S // Q_BLOCK
    qb = q.reshape(B, nb, Q_BLOCK, H, MLA_QK_DIM).transpose(1, 0, 2, 3, 4)
    key_chunk = jnp.arange(S) // CHUNK
    scale = MLA_QK_DIM ** -0.5
    neg = jnp.finfo(jnp.float32).min

    def attend(args):
        q_blk, b = args
        q_chunk = (b * Q_BLOCK + jnp.arange(Q_BLOCK)) // CHUNK
        mask = key_chunk[None, :] <= q_chunk[:, None]
        s = jnp.einsum('bqhd,bkhd->bhqk', q_blk, k).astype(jnp.float32) * scale
        s = jnp.where(mask[None, None], s, neg)
        prob = jax.nn.softmax(s, axis=-1).astype(v.dtype)
        return jnp.einsum('bhqk,bkhe->bqhe', prob, v)

    out = lax.map(attend, (qb, jnp.arange(nb)))
    return out.transpose(1, 0, 2, 3, 4).reshape(B, S, H * MLA_V_DIM)


def setup_inputs(seed: int = 0) -> dict:
    key = jax.random.key(seed)
    ks = jax.random.split(key, 20)
    f32 = jnp.float32

    def w(k, shape, fan_in):
        return jax.random.normal(k, shape, f32) * (fan_in ** -0.5)

    def gain(k, shape):
        return 1.0 + 0.02 * jax.random.normal(k, shape, f32)

    start = jax.random.randint(ks[2], (BATCH, 1), 0, 4096, dtype=jnp.int32)
    positions = (start + jnp.arange(SEQ, dtype=jnp.int32)[None, :]).astype(jnp.int32)
    return {
        "x": jax.random.normal(ks[0], (BATCH, SEQ, D_MODEL), f32),
        "p": jax.random.normal(ks[1], (DEPTH, BATCH, SEQ, D_PLE), f32),
        "positions": positions,
        "mix_norm_g": gain(ks[3], (DEPTH, D_MODEL)),
        "w_in": w(ks[4], (DEPTH, D_MODEL, IN_COLS), D_MODEL),
        "q_norm_g": gain(ks[5], (DEPTH, Q_LORA)),
        "w_uq": w(ks[6], (DEPTH, Q_LORA, MLA_HEADS * MLA_QK_DIM), Q_LORA),
        "kv_norm_g": gain(ks[7], (DEPTH, KV_LORA)),
        "w_ukv": w(ks[8], (DEPTH, KV_LORA, MLA_HEADS * (MLA_NOPE_DIM + MLA_V_DIM)), KV_LORA),
        "w_o": w(ks[9], (DEPTH, D_MIX, D_MODEL), D_MIX),
        "ffn_norm_g": gain(ks[10], (DEPTH, D_MODEL)),
        "w_ffn_gate": w(ks[11], (DEPTH, D_MODEL, D_FF), D_MODEL),
        "w_ffn_up": w(ks[12], (DEPTH, D_MODEL, D_FF), D_MODEL),
        "w_ffn_down": w(ks[13], (DEPTH, D_FF, D_MODEL), D_FF),
        "ple_norm_g": gain(ks[14], (DEPTH, D_MODEL)),
        "w_ple_gate": w(ks[15], (DEPTH, D_MODEL, D_MODEL), D_MODEL),
        "w_ple_proj": w(ks[16], (DEPTH, D_PLE, D_MODEL), D_PLE),
        "final_norm_g": gain(ks[17], (D_MODEL,)),
    }


def reference(x, p, positions, mix_norm_g, w_in, q_norm_g, w_uq, kv_norm_g, w_ukv, w_o,
              ffn_norm_g, w_ffn_gate, w_ffn_up, w_ffn_down, ple_norm_g, w_ple_gate,
              w_ple_proj, final_norm_g):
    B, S, _ = x.shape
    for i in range(DEPTH):
        h = rmsnorm(x, mix_norm_g[i])
        z = h @ w_in[i]
        rq, rk, rv, rg, c_q, c_kv, k_pe = jnp.split(z, IN_SPLITS, axis=-1)
        shp = (B, S, RET_HEADS, RET_HEAD_DIM)
        ret = retention(rq.reshape(shp), rk.reshape(shp), rv.reshape(shp), rg, positions)
        att = mla(c_q, c_kv, k_pe, positions, q_norm_g[i], w_uq[i], kv_norm_g[i], w_ukv[i])
        x = x + jnp.concatenate([ret, att], axis=-1) @ w_o[i]
        h = rmsnorm(x, ffn_norm_g[i])
        x = x + (jax.nn.silu(h @ w_ffn_gate[i]) * (h @ w_ffn_up[i])) @ w_ffn_down[i]
        gate = jax.nn.sigmoid(rmsnorm(x, ple_norm_g[i]) @ w_ple_gate[i])
        x = x + gate * (p[i] @ w_ple_proj[i])
    return rmsnorm(x, final_norm_g)
```

```python
import functools
import math

import numpy as np
import jax
import jax.numpy as jnp
from jax import lax
from jax.experimental import pallas as pl
from jax.experimental.pallas import tpu as pltpu

F32 = jnp.float32
BF16 = jnp.bfloat16

CHUNK = 64
RET_HEADS = 4
RET_HEAD_DIM = 128
MLA_HEADS = 4
MLA_NOPE_DIM = 128
MLA_ROPE_DIM = 64
MLA_V_DIM = 128
MLA_QK_DIM = MLA_NOPE_DIM + MLA_ROPE_DIM
ROPE_BASE = 10000.0
RMS_EPS = 1e-6
GN_EPS = 1e-5

LANES = 128
MLA_QK_PAD = 2 * LANES
VMEM_LIMIT_BYTES = 56 * 1024 * 1024

IN_PROJ_ROWS = 512
RET_BLOCK = 256
ATTN_BLOCK = 512
OUT_ROWS = 256


def _rms(x, g):
    return x * lax.rsqrt(jnp.mean(x * x, axis=-1, keepdims=True) + RMS_EPS) * g


def _sigmoid(x):
    return 1.0 / (1.0 + jnp.exp(-x))


def _dot(a, b):
    return jnp.dot(a, b, preferred_element_type=F32)


def _dot_nt(a, b):
    return lax.dot_general(a, b, (((1,), (1,)), ((), ())), preferred_element_type=F32)


def _dot_tn(a, b):
    return lax.dot_general(a, b, (((0,), (0,)), ((), ())), preferred_element_type=F32)


def _rot_half_lanes(x):
    return pltpu.roll(x, LANES // 2, 1)


def _in_proj_kernel(x_ref, pos_ref, gmix_ref, win_ref, qg_ref, wuq_ref, kvg_ref, wukv_ref, rc_ref,
                    rq_ref, rk_ref, rv_ref, rg_ref, q_ref, kn_ref, kr_ref, v_ref, *, q_scale):
    h = _rms(x_ref[...], gmix_ref[...]).astype(BF16)
    z = _dot(h, win_ref[...])

    pos = pos_ref[...].astype(F32)
    rc = rc_ref[...]
    ang_r = pos * rc[0:1, :]
    cos_r = jnp.cos(ang_r)
    sin_r = jnp.sin(ang_r) * rc[1:2, :]
    ang_m = pos * rc[2:3, :]
    cos_m = jnp.cos(ang_m) * rc[3:4, :]
    sin_m = jnp.sin(ang_m) * rc[3:4, :]

    d = RET_HEAD_DIM
    w = RET_HEADS * d
    k_scale = d ** -0.5
    for hh in range(RET_HEADS):
        q = z[:, hh * d:(hh + 1) * d]
        rq_ref[:, hh * d:(hh + 1) * d] = (q * cos_r + _rot_half_lanes(q) * sin_r).astype(BF16)
        k = z[:, w + hh * d:w + (hh + 1) * d]
        rk_ref[:, hh * d:(hh + 1) * d] = ((k * cos_r + _rot_half_lanes(k) * sin_r) * k_scale).astype(BF16)
    rv_ref[...] = z[:, 2 * w:3 * w].astype(BF16)
    g = z[:, 3 * w:4 * w]
    rg_ref[...] = g * _sigmoid(g)

    o = 4 * w
    q_lora = qg_ref.shape[-1]
    kv_lora = kvg_ref.shape[-1]
    cq = z[:, o:o + q_lora]
    ckv = z[:, o + q_lora:o + q_lora + kv_lora]
    kpe = z[:, o + q_lora + kv_lora:o + q_lora + kv_lora + LANES]
    kr_ref[...] = (kpe * cos_m + _rot_half_lanes(kpe) * sin_m).astype(BF16)

    qq = _dot(_rms(cq, qg_ref[...]).astype(BF16), wuq_ref[...])
    for hh in range(MLA_HEADS):
        b = hh * MLA_QK_PAD
        q_ref[:, b:b + LANES] = (qq[:, b:b + LANES] * q_scale).astype(BF16)
        y = qq[:, b + LANES:b + 2 * LANES]
        q_ref[:, b + LANES:b + 2 * LANES] = (
            (y * cos_m + _rot_half_lanes(y) * sin_m) * q_scale).astype(BF16)

    kv = _dot(_rms(ckv, kvg_ref[...]).astype(BF16), wukv_ref[...])
    hk = MLA_HEADS * MLA_NOPE_DIM
    kn_ref[...] = kv[:, :hk].astype(BF16)
    v_ref[...] = kv[:, hk:].astype(BF16)


def _in_proj(x2, pos2, gmix, win, qg, wuq, kvg, wukv, rc):
    n, dm = x2.shape
    tm = IN_PROJ_ROWS
    w = RET_HEADS * RET_HEAD_DIM
    const = lambda i: (0, 0)
    row = lambda i: (i, 0)
    full = lambda a: pl.BlockSpec(a.shape, const)
    out_shapes = (
        jax.ShapeDtypeStruct((n, w), BF16),
        jax.ShapeDtypeStruct((n, w), BF16),
        jax.ShapeDtypeStruct((n, w), BF16),
        jax.ShapeDtypeStruct((n, w), F32),
        jax.ShapeDtypeStruct((n, MLA_HEADS * MLA_QK_PAD), BF16),
        jax.ShapeDtypeStruct((n, MLA_HEADS * MLA_NOPE_DIM), BF16),
        jax.ShapeDtypeStruct((n, LANES), BF16),
        jax.ShapeDtypeStruct((n, MLA_HEADS * MLA_V_DIM), BF16),
    )
    out_specs = tuple(pl.BlockSpec((tm, s.shape[1]), row) for s in out_shapes)
    return pl.pallas_call(
        functools.partial(_in_proj_kernel, q_scale=MLA_QK_DIM ** -0.5),
        out_shape=out_shapes,
        grid=(n // tm,),
        in_specs=[pl.BlockSpec((tm, dm), row), pl.BlockSpec((tm, 1), row),
                  full(gmix), full(win), full(qg), full(wuq), full(kvg), full(wukv), full(rc)],
        out_specs=out_specs,
        compiler_params=pltpu.CompilerParams(
            dimension_semantics=("arbitrary",), vmem_limit_bytes=VMEM_LIMIT_BYTES),
        name="in_proj",
    )(x2, pos2, gmix, win, qg, wuq, kvg, wukv, rc)


def _retention_kernel(rq_ref, rk_ref, rv_ref, rg_ref, dmat_ref, qdec_ref, kdec_ref, out_ref,
                      state_ref, *, block_decay):
    @pl.when(pl.program_id(1) == 0)
    def _():
        state_ref[...] = jnp.zeros_like(state_ref)

    d = RET_HEAD_DIM
    for hh in range(RET_HEADS):
        sl = slice(hh * d, (hh + 1) * d)
        q = rq_ref[:, sl]
        k = rk_ref[:, sl]
        v = rv_ref[:, sl]
        s = _dot_nt(q, k) * dmat_ref[hh]
        o = _dot(s.astype(BF16), v)
        st = state_ref[hh]
        o = o + _dot(q, st.astype(BF16)) * qdec_ref[hh]
        kd = (k.astype(F32) * kdec_ref[hh]).astype(BF16)
        state_ref[hh] = block_decay[hh] * st + _dot_tn(kd, v)
        mu = jnp.mean(o, axis=-1, keepdims=True)
        oc = o - mu
        var = jnp.mean(oc * oc, axis=-1, keepdims=True)
        out_ref[:, sl] = (oc * lax.rsqrt(var + GN_EPS) * rg_ref[:, sl]).astype(BF16)


def _retention(rq, rk, rv, rg, batch, seq):
    n, w = rq.shape
    blk = RET_BLOCK
    nb = seq // blk
    log_g = np.log1p(-np.exp2(-5.0 - np.arange(RET_HEADS, dtype=np.float64)))
    idx = jnp.arange(blk, dtype=F32)
    lg = jnp.asarray(log_g, F32)
    dist = jnp.abs(idx[:, None] - idx[None, :])
    visible = (idx[None, :] // CHUNK) <= (idx[:, None] // CHUNK)
    dmat = jnp.where(visible[None], jnp.exp(lg[:, None, None] * dist[None]), 0.0)
    qdec = jnp.broadcast_to(jnp.exp(lg[:, None] * (idx + 1.0))[:, :, None], (RET_HEADS, blk, LANES))
    kdec = jnp.broadcast_to(jnp.exp(lg[:, None] * (blk - 1.0 - idx))[:, :, None], (RET_HEADS, blk, LANES))
    block_decay = tuple(float(math.exp(g * blk)) for g in log_g)

    row = lambda b, j: (b * nb + j, 0)
    const3 = lambda b, j: (0, 0, 0)
    return pl.pallas_call(
        functools.partial(_retention_kernel, block_decay=block_decay),
        out_shape=jax.ShapeDtypeStruct((n, w), BF16),
        grid=(batch, nb),
        in_specs=[pl.BlockSpec((blk, w), row)] * 4 + [
            pl.BlockSpec(dmat.shape, const3), pl.BlockSpec(qdec.shape, const3),
            pl.BlockSpec(kdec.shape, const3)],
        out_specs=pl.BlockSpec((blk, w), row),
        scratch_shapes=[pltpu.VMEM((RET_HEADS, RET_HEAD_DIM, RET_HEAD_DIM), F32)],
        compiler_params=pltpu.CompilerParams(
            dimension_semantics=("arbitrary", "arbitrary"), vmem_limit_bytes=VMEM_LIMIT_BYTES),
        name="retention",
    )(rq, rk, rv, rg, dmat, qdec, kdec)


def _attention_kernel(q_ref, kn_ref, kr_ref, v_ref, out_ref, m_ref, l_ref, acc_ref):
    t = q_ref.shape[0]
    i = pl.program_id(2)
    q = q_ref[...]
    m_ref[...] = jnp.full_like(m_ref, -1e30)
    l_ref[...] = jnp.zeros_like(l_ref)
    acc_ref[...] = jnp.zeros_like(acc_ref)

    def step(start, masked):
        k = jnp.concatenate([kn_ref[pl.ds(start, t), :], kr_ref[pl.ds(start, t), :]], axis=1)
        s = _dot_nt(q, k)
        if masked:
            qc = lax.broadcasted_iota(jnp.int32, (t, t), 0) // CHUNK
            kc = lax.broadcasted_iota(jnp.int32, (t, t), 1) // CHUNK
            s = jnp.where(kc <= qc, s, -1e30)
        m_old = m_ref[...]
        m_new = jnp.maximum(m_old, jnp.max(s, axis=-1, keepdims=True))
        alpha = jnp.exp(m_old - m_new)
        p = jnp.exp(s - m_new)
        l_ref[...] = alpha * l_ref[...] + jnp.sum(p, axis=-1, keepdims=True)
        acc_ref[...] = alpha * acc_ref[...] + _dot(p.astype(BF16), v_ref[pl.ds(start, t), :])
        m_ref[...] = m_new

    def body(c, carry):
        step(pl.multiple_of(c * t, t), False)
        return carry

    lax.fori_loop(0, i, body, 0)
    step(pl.multiple_of(i * t, t), True)
    out_ref[...] = (acc_ref[...] / l_ref[...]).astype(BF16)


def _attention(q, kn, kr, v, batch, seq):
    n = q.shape[0]
    t = ATTN_BLOCK
    nq = seq // t
    return pl.pallas_call(
        _attention_kernel,
        out_shape=jax.ShapeDtypeStruct((n, MLA_HEADS * MLA_V_DIM), BF16),
        grid=(batch, MLA_HEADS, nq),
        in_specs=[pl.BlockSpec((t, MLA_QK_PAD), lambda b, h, i: (b * nq + i, h)),
                  pl.BlockSpec((seq, MLA_NOPE_DIM), lambda b, h, i: (b, h)),
                  pl.BlockSpec((seq, LANES), lambda b, h, i: (b, 0)),
                  pl.BlockSpec((seq, MLA_V_DIM), lambda b, h, i: (b, h))],
        out_specs=pl.BlockSpec((t, MLA_V_DIM), lambda b, h, i: (b * nq + i, h)),
        scratch_shapes=[pltpu.VMEM((t, 1), F32), pltpu.VMEM((t, 1), F32),
                        pltpu.VMEM((t, MLA_V_DIM), F32)],
        compiler_params=pltpu.CompilerParams(
            dimension_semantics=("arbitrary", "arbitrary", "arbitrary"),
            vmem_limit_bytes=VMEM_LIMIT_BYTES),
        name="attention",
    )(q, kn, kr, v)


def _out_ffn_kernel(x_ref, ret_ref, att_ref, p_ref, wo_ref, gffn_ref, wg_ref, wu_ref, wd_ref,
                    gple_ref, wpg_ref, wpp_ref, gfin_ref, out_ref, *, final_norm):
    a = jnp.concatenate([ret_ref[...], att_ref[...]], axis=1)
    x1 = x_ref[...] + _dot(a, wo_ref[...])
    h = _rms(x1, gffn_ref[...]).astype(BF16)
    g = _dot(h, wg_ref[...])
    u = _dot(h, wu_ref[...])
    act = (g * _sigmoid(g) * u).astype(BF16)
    x2 = x1 + _dot(act, wd_ref[...])
    gate = _sigmoid(_dot(_rms(x2, gple_ref[...]).astype(BF16), wpg_ref[...]))
    x3 = x2 + gate * _dot(p_ref[...].astype(BF16), wpp_ref[...])
    if final_norm:
        x3 = _rms(x3, gfin_ref[...])
    out_ref[...] = x3


def _out_ffn(x2, ret, att, p2, wo, gffn, wg, wu, wd, gple, wpg, wpp, gfin, final_norm):
    n, dm = x2.shape
    tm = OUT_ROWS
    row = lambda i: (i, 0)
    const = lambda i: (0, 0)
    resident = lambda a: pl.BlockSpec(a.shape, const, pipeline_mode=pl.Buffered(1))
    return pl.pallas_call(
        functools.partial(_out_ffn_kernel, final_norm=final_norm),
        out_shape=jax.ShapeDtypeStruct((n, dm), F32),
        grid=(n // tm,),
        in_specs=[pl.BlockSpec((tm, dm), row), pl.BlockSpec((tm, ret.shape[1]), row),
                  pl.BlockSpec((tm, att.shape[1]), row), pl.BlockSpec((tm, p2.shape[1]), row),
                  resident(wo), resident(gffn), resident(wg), resident(wu), resident(wd),
                  resident(gple), resident(wpg), resident(wpp), resident(gfin)],
        out_specs=pl.BlockSpec((tm, dm), row),
        compiler_params=pltpu.CompilerParams(
            dimension_semantics=("arbitrary",), vmem_limit_bytes=VMEM_LIMIT_BYTES),
        name="out_ffn",
    )(x2, ret, att, p2, wo, gffn, wg, wu, wd, gple, wpg, wpp, gfin)


def _rot_cols(w):
    half = w.shape[-1] // 2
    return jnp.concatenate([-w[..., half:], w[..., :half]], axis=-1)


def _prep_in_weights(w_in, w_uq, w_ukv):
    w = RET_HEADS * RET_HEAD_DIM
    kpe = w_in[:, -MLA_ROPE_DIM:]
    win = jnp.concatenate([w_in, _rot_cols(kpe)], axis=1).astype(BF16)
    q_lora = w_uq.shape[0]
    uq = w_uq.reshape(q_lora, MLA_HEADS, MLA_QK_DIM)
    uq_rope = uq[:, :, MLA_NOPE_DIM:]
    wuq = jnp.concatenate([uq, _rot_cols(uq_rope)], axis=-1).reshape(q_lora, MLA_HEADS * MLA_QK_PAD)
    kv_lora = w_ukv.shape[0]
    ukv = w_ukv.reshape(kv_lora, MLA_HEADS, MLA_NOPE_DIM + MLA_V_DIM)
    wukv = jnp.concatenate([ukv[:, :, :MLA_NOPE_DIM].reshape(kv_lora, -1),
                            ukv[:, :, MLA_NOPE_DIM:].reshape(kv_lora, -1)], axis=1)
    return win, wuq.astype(BF16), wukv.astype(BF16)


def _rope_consts():
    def inv(half):
        return jnp.exp(-math.log(ROPE_BASE) * jnp.arange(half, dtype=F32) / half)
    inv_r = inv(RET_HEAD_DIM // 2)
    inv_m = inv(MLA_ROPE_DIM // 2)
    half = LANES // 2
    rows = [
        jnp.concatenate([inv_r, inv_r]),
        jnp.concatenate([-jnp.ones(half, F32), jnp.ones(half, F32)]),
        jnp.concatenate([inv_m, inv_m, jnp.zeros(half, F32)]),
        jnp.concatenate([jnp.ones(half, F32), jnp.zeros(half, F32)]),
    ]
    return jnp.concatenate([jnp.stack(rows), jnp.zeros((4, LANES), F32)], axis=0)


def kernel(x, p, positions, mix_norm_g, w_in, q_norm_g, w_uq, kv_norm_g, w_ukv, w_o, ffn_norm_g,
           w_ffn_gate, w_ffn_up, w_ffn_down, ple_norm_g, w_ple_gate, w_ple_proj, final_norm_g):
    batch, seq, dm = x.shape
    depth = w_in.shape[0]
    n = batch * seq
    x2 = x.reshape(n, dm)
    pos2 = positions.reshape(n, 1)
    rc = _rope_consts()
    vec = lambda g: g.reshape(1, -1).astype(F32)
    for i in range(depth):
        win, wuq, wukv = _prep_in_weights(w_in[i], w_uq[i], w_ukv[i])
        rq, rk, rv, rg, q, kn, kr, v = _in_proj(
            x2, pos2, vec(mix_norm_g[i]), win, vec(q_norm_g[i]), wuq, vec(kv_norm_g[i]), wukv, rc)
        ret = _retention(rq, rk, rv, rg, batch, seq)
        att = _attention(q, kn, kr, v, batch, seq)
        x2 = _out_ffn(
            x2, ret, att, p[i].reshape(n, -1), w_o[i].astype(BF16), vec(ffn_norm_g[i]),
            w_ffn_gate[i].astype(BF16), w_ffn_up[i].astype(BF16), w_ffn_down[i].astype(BF16),
            vec(ple_norm_g[i]), w_ple_gate[i].astype(BF16), w_ple_proj[i].astype(BF16),
            vec(final_norm_g), final_norm=(i == depth - 1))
    return x2.reshape(batch, seq, dm)
```

```python
import functools
import math

import numpy as np
import jax
import jax.numpy as jnp
from jax import lax
from jax.experimental import pallas as pl
from jax.experimental.pallas import tpu as pltpu

F32 = jnp.float32
BF16 = jnp.bfloat16

CHUNK = 64
RET_HEADS = 4
RET_HEAD_DIM = 128
MLA_HEADS = 4
MLA_NOPE_DIM = 128
MLA_ROPE_DIM = 64
MLA_V_DIM = 128
MLA_QK_DIM = MLA_NOPE_DIM + MLA_ROPE_DIM
ROPE_BASE = 10000.0
RMS_EPS = 1e-6
GN_EPS = 1e-5

LANES = 128
MLA_QK_PAD = 2 * LANES
VMEM_LIMIT_BYTES = 56 * 1024 * 1024

IN_PROJ_ROWS = 512
RET_BLOCK = 256
ATTN_Q_BLOCK = 1024
ATTN_KV_BLOCK = 512
OUT_ROWS = 256


def _rms(x, g):
    return x * lax.rsqrt(jnp.mean(x * x, axis=-1, keepdims=True) + RMS_EPS) * g


def _sigmoid(x):
    return 1.0 / (1.0 + jnp.exp(-x))


def _dot(a, b):
    return jnp.dot(a, b, preferred_element_type=F32)


def _dot_nt(a, b):
    return lax.dot_general(a, b, (((1,), (1,)), ((), ())), preferred_element_type=F32)


def _dot_tn(a, b):
    return lax.dot_general(a, b, (((0,), (0,)), ((), ())), preferred_element_type=F32)


def _rot_half_lanes(x):
    return pltpu.roll(x, LANES // 2, 1)


def _in_proj_kernel(x_ref, pos_ref, gmix_ref, win_ref, qg_ref, wuq_ref, kvg_ref, wuk_ref, wuvt_ref,
                    rc_ref, rq_ref, rk_ref, rv_ref, rg_ref, q_ref, kn_ref, kr_ref, vt_ref, *, q_scale):
    h = _rms(x_ref[...], gmix_ref[...]).astype(BF16)
    z = _dot(h, win_ref[...])

    pos = pos_ref[...].astype(F32)
    rc = rc_ref[...]
    ang_r = pos * rc[0:1, :]
    cos_r = jnp.cos(ang_r)
    sin_r = jnp.sin(ang_r) * rc[1:2, :]
    ang_m = pos * rc[2:3, :]
    cos_m = jnp.cos(ang_m) * rc[3:4, :]
    sin_m = jnp.sin(ang_m) * rc[3:4, :]

    d = RET_HEAD_DIM
    w = RET_HEADS * d
    k_scale = d ** -0.5
    for hh in range(RET_HEADS):
        q = z[:, hh * d:(hh + 1) * d]
        rq_ref[:, hh * d:(hh + 1) * d] = (q * cos_r + _rot_half_lanes(q) * sin_r).astype(BF16)
        k = z[:, w + hh * d:w + (hh + 1) * d]
        rk_ref[:, hh * d:(hh + 1) * d] = ((k * cos_r + _rot_half_lanes(k) * sin_r) * k_scale).astype(BF16)
    rv_ref[...] = z[:, 2 * w:3 * w].astype(BF16)
    g = z[:, 3 * w:4 * w]
    rg_ref[...] = g * _sigmoid(g)

    o = 4 * w
    q_lora = qg_ref.shape[-1]
    kv_lora = kvg_ref.shape[-1]
    cq = z[:, o:o + q_lora]
    ckv = z[:, o + q_lora:o + q_lora + kv_lora]
    kpe = z[:, o + q_lora + kv_lora:o + q_lora + kv_lora + LANES]
    kr_ref[...] = (kpe * cos_m + _rot_half_lanes(kpe) * sin_m).astype(BF16)

    qq = _dot(_rms(cq, qg_ref[...]).astype(BF16), wuq_ref[...])
    for hh in range(MLA_HEADS):
        b = hh * MLA_QK_PAD
        q_ref[:, b:b + LANES] = (qq[:, b:b + LANES] * q_scale).astype(BF16)
        y = qq[:, b + LANES:b + 2 * LANES]
        q_ref[:, b + LANES:b + 2 * LANES] = (
            (y * cos_m + _rot_half_lanes(y) * sin_m) * q_scale).astype(BF16)

    ckvn = _rms(ckv, kvg_ref[...]).astype(BF16)
    kn_ref[...] = _dot(ckvn, wuk_ref[...]).astype(BF16)
    vt = _dot_nt(wuvt_ref[...], ckvn).astype(BF16)
    tk = vt_ref.shape[-1]
    for j in range(vt_ref.shape[0]):
        vt_ref[j] = vt[:, j * tk:(j + 1) * tk]


def _in_proj(x2, pos2, gmix, win, qg, wuq, kvg, wuk, wuvt, rc):
    n, dm = x2.shape
    tm = IN_PROJ_ROWS
    w = RET_HEADS * RET_HEAD_DIM
    const = lambda i: (0, 0)
    row = lambda i: (i, 0)
    full = lambda a: pl.BlockSpec(a.shape, const)
    out_shapes = (
        jax.ShapeDtypeStruct((n, w), BF16),
        jax.ShapeDtypeStruct((n, w), BF16),
        jax.ShapeDtypeStruct((n, w), BF16),
        jax.ShapeDtypeStruct((n, w), F32),
        jax.ShapeDtypeStruct((n, MLA_HEADS * MLA_QK_PAD), BF16),
        jax.ShapeDtypeStruct((n, MLA_HEADS * MLA_NOPE_DIM), BF16),
        jax.ShapeDtypeStruct((n, LANES), BF16),
    )
    tk = ATTN_KV_BLOCK
    vt_shape = jax.ShapeDtypeStruct((n // tk, MLA_HEADS * MLA_V_DIM, tk), BF16)
    out_specs = tuple(pl.BlockSpec((tm, s.shape[1]), row) for s in out_shapes) + (
        pl.BlockSpec((tm // tk, MLA_HEADS * MLA_V_DIM, tk), lambda i: (i, 0, 0)),)
    q_scale = MLA_QK_DIM ** -0.5 * math.log2(math.e)
    return pl.pallas_call(
        functools.partial(_in_proj_kernel, q_scale=q_scale),
        out_shape=out_shapes + (vt_shape,),
        grid=(n // tm,),
        in_specs=[pl.BlockSpec((tm, dm), row), pl.BlockSpec((tm, 1), row),
                  full(gmix), full(win), full(qg), full(wuq), full(kvg), full(wuk), full(wuvt),
                  full(rc)],
        out_specs=out_specs,
        compiler_params=pltpu.CompilerParams(
            dimension_semantics=("arbitrary",), vmem_limit_bytes=VMEM_LIMIT_BYTES),
        name="in_proj",
    )(x2, pos2, gmix, win, qg, wuq, kvg, wuk, wuvt, rc)


def _retention_kernel(rq_ref, rk_ref, rv_ref, rg_ref, dmat_ref, qdec_ref, kdec_ref, out_ref,
                      state_ref, *, block_decay):
    @pl.when(pl.program_id(1) == 0)
    def _():
        state_ref[...] = jnp.zeros_like(state_ref)

    d = RET_HEAD_DIM
    for hh in range(RET_HEADS):
        sl = slice(hh * d, (hh + 1) * d)
        q = rq_ref[:, sl]
        k = rk_ref[:, sl]
        v = rv_ref[:, sl]
        s = _dot_nt(q, k) * dmat_ref[hh]
        o = _dot(s.astype(BF16), v)
        st = state_ref[hh]
        o = o + _dot(q, st.astype(BF16)) * qdec_ref[hh]
        kd = (k.astype(F32) * kdec_ref[hh]).astype(BF16)
        state_ref[hh] = block_decay[hh] * st + _dot_tn(kd, v)
        mu = jnp.mean(o, axis=-1, keepdims=True)
        oc = o - mu
        var = jnp.mean(oc * oc, axis=-1, keepdims=True)
        out_ref[:, sl] = (oc * lax.rsqrt(var + GN_EPS) * rg_ref[:, sl]).astype(BF16)


def _retention(rq, rk, rv, rg, batch, seq):
    n, w = rq.shape
    blk = RET_BLOCK
    nb = seq // blk
    log_g = np.log1p(-np.exp2(-5.0 - np.arange(RET_HEADS, dtype=np.float64)))
    idx = jnp.arange(blk, dtype=F32)
    lg = jnp.asarray(log_g, F32)
    dist = jnp.abs(idx[:, None] - idx[None, :])
    visible = (idx[None, :] // CHUNK) <= (idx[:, None] // CHUNK)
    dmat = jnp.where(visible[None], jnp.exp(lg[:, None, None] * dist[None]), 0.0)
    qdec = jnp.broadcast_to(jnp.exp(lg[:, None] * (idx + 1.0))[:, :, None], (RET_HEADS, blk, LANES))
    kdec = jnp.broadcast_to(jnp.exp(lg[:, None] * (blk - 1.0 - idx))[:, :, None], (RET_HEADS, blk, LANES))
    block_decay = tuple(float(math.exp(g * blk)) for g in log_g)

    row = lambda b, j: (b * nb + j, 0)
    const3 = lambda b, j: (0, 0, 0)
    return pl.pallas_call(
        functools.partial(_retention_kernel, block_decay=block_decay),
        out_shape=jax.ShapeDtypeStruct((n, w), BF16),
        grid=(batch, nb),
        in_specs=[pl.BlockSpec((blk, w), row)] * 4 + [
            pl.BlockSpec(dmat.shape, const3), pl.BlockSpec(qdec.shape, const3),
            pl.BlockSpec(kdec.shape, const3)],
        out_specs=pl.BlockSpec((blk, w), row),
        scratch_shapes=[pltpu.VMEM((RET_HEADS, RET_HEAD_DIM, RET_HEAD_DIM), F32)],
        compiler_params=pltpu.CompilerParams(
            dimension_semantics=("arbitrary", "arbitrary"), vmem_limit_bytes=VMEM_LIMIT_BYTES),
        name="retention",
    )(rq, rk, rv, rg, dmat, qdec, kdec)


def _attention_kernel(q_ref, kn_ref, kr_ref, vt_ref, out_ref, m_ref, l_ref, acc_ref):
    tq = q_ref.shape[0]
    tk = vt_ref.shape[-1]
    i = pl.program_id(2)
    q = q_ref[...]
    m_ref[...] = jnp.full_like(m_ref, -1e30)
    l_ref[...] = jnp.zeros_like(l_ref)
    acc_ref[...] = jnp.zeros_like(acc_ref)

    def keys(start, size):
        return jnp.concatenate(
            [kn_ref[pl.ds(start, size), :], kr_ref[pl.ds(start, size), :]], axis=1)

    def update(st, vt, cols):
        m_old = m_ref[:, cols]
        m_new = jnp.maximum(m_old, jnp.max(st, axis=0, keepdims=True))
        alpha = jnp.exp2(m_old - m_new)
        p = jnp.exp2(st - m_new)
        l_ref[:, cols] = alpha * l_ref[:, cols] + jnp.sum(p, axis=0, keepdims=True)
        acc_ref[:, cols] = alpha * acc_ref[:, cols] + _dot(vt, p.astype(BF16))
        m_ref[:, cols] = m_new

    per_q = tq // tk

    def body(c, carry):
        st = _dot_nt(keys(pl.multiple_of(c * tq, tq), tq), q)
        vt = jnp.concatenate([vt_ref[c * per_q + j] for j in range(per_q)], axis=1)
        update(st, vt, slice(None))
        return carry

    lax.fori_loop(0, i, body, 0)
    for d in range(per_q):
        c = i * per_q + d
        lo = d * tk
        st = _dot_nt(keys(pl.multiple_of(c * tk, tk), tk), q[lo:, :])
        kc = lax.broadcasted_iota(jnp.int32, (tk, tk), 0) // CHUNK
        qc = lax.broadcasted_iota(jnp.int32, (tk, tk), 1) // CHUNK
        head = jnp.where(kc <= qc, st[:, :tk], -1e30)
        st = head if lo + tk == tq else jnp.concatenate([head, st[:, tk:]], axis=1)
        update(st, vt_ref[c], slice(lo, tq))
    o = acc_ref[...] * (1.0 / l_ref[...])
    out_ref[...] = o.T.astype(BF16)


def _attention(q, kn, kr, vt, batch, seq):
    n = q.shape[0]
    tq = ATTN_Q_BLOCK
    tk = ATTN_KV_BLOCK
    nq = seq // tq
    return pl.pallas_call(
        _attention_kernel,
        out_shape=jax.ShapeDtypeStruct((n, MLA_HEADS * MLA_V_DIM), BF16),
        grid=(batch, MLA_HEADS, nq),
        in_specs=[pl.BlockSpec((tq, MLA_QK_PAD), lambda b, h, i: (b * nq + i, h)),
                  pl.BlockSpec((seq, MLA_NOPE_DIM), lambda b, h, i: (b, h)),
                  pl.BlockSpec((seq, LANES), lambda b, h, i: (b, 0)),
                  pl.BlockSpec((seq // tk, MLA_V_DIM, tk), lambda b, h, i: (b, h, 0))],
        out_specs=pl.BlockSpec((tq, MLA_V_DIM), lambda b, h, i: (b * nq + i, h)),
        scratch_shapes=[pltpu.VMEM((1, tq), F32), pltpu.VMEM((1, tq), F32),
                        pltpu.VMEM((MLA_V_DIM, tq), F32)],
        compiler_params=pltpu.CompilerParams(
            dimension_semantics=("arbitrary", "arbitrary", "arbitrary"),
            vmem_limit_bytes=VMEM_LIMIT_BYTES),
        name="attention",
    )(q, kn, kr, vt)


def _out_ffn_kernel(x_ref, ret_ref, att_ref, p_ref, wo_ref, gffn_ref, wg_ref, wu_ref, wd_ref,
                    gple_ref, wpg_ref, wpp_ref, gfin_ref, out_ref, *, final_norm):
    a = jnp.concatenate([ret_ref[...], att_ref[...]], axis=1)
    x1 = x_ref[...] + _dot(a, wo_ref[...])
    h = _rms(x1, gffn_ref[...]).astype(BF16)
    g = _dot(h, wg_ref[...])
    u = _dot(h, wu_ref[...])
    act = (g * _sigmoid(g) * u).astype(BF16)
    x2 = x1 + _dot(act, wd_ref[...])
    gate = _sigmoid(_dot(_rms(x2, gple_ref[...]).astype(BF16), wpg_ref[...]))
    x3 = x2 + gate * _dot(p_ref[...].astype(BF16), wpp_ref[...])
    if final_norm:
        x3 = _rms(x3, gfin_ref[...])
    out_ref[...] = x3


def _out_ffn(x2, ret, att, p2, wo, gffn, wg, wu, wd, gple, wpg, wpp, gfin, final_norm):
    n, dm = x2.shape
    tm = OUT_ROWS
    row = lambda i: (i, 0)
    const = lambda i: (0, 0)
    resident = lambda a: pl.BlockSpec(a.shape, const, pipeline_mode=pl.Buffered(1))
    return pl.pallas_call(
        functools.partial(_out_ffn_kernel, final_norm=final_norm),
        out_shape=jax.ShapeDtypeStruct((n, dm), F32),
        grid=(n // tm,),
        in_specs=[pl.BlockSpec((tm, dm), row), pl.BlockSpec((tm, ret.shape[1]), row),
                  pl.BlockSpec((tm, att.shape[1]), row), pl.BlockSpec((tm, p2.shape[1]), row),
                  resident(wo), resident(gffn), resident(wg), resident(wu), resident(wd),
                  resident(gple), resident(wpg), resident(wpp), resident(gfin)],
        out_specs=pl.BlockSpec((tm, dm), row),
        compiler_params=pltpu.CompilerParams(
            dimension_semantics=("arbitrary",), vmem_limit_bytes=VMEM_LIMIT_BYTES),
        name="out_ffn",
    )(x2, ret, att, p2, wo, gffn, wg, wu, wd, gple, wpg, wpp, gfin)


def _rot_cols(w):
    half = w.shape[-1] // 2
    return jnp.concatenate([-w[..., half:], w[..., :half]], axis=-1)


def _prep_in_weights(w_in, w_uq, w_ukv):
    kpe = w_in[:, -MLA_ROPE_DIM:]
    win = jnp.concatenate([w_in, _rot_cols(kpe)], axis=1).astype(BF16)
    q_lora = w_uq.shape[0]
    uq = w_uq.reshape(q_lora, MLA_HEADS, MLA_QK_DIM)
    uq_rope = uq[:, :, MLA_NOPE_DIM:]
    wuq = jnp.concatenate([uq, _rot_cols(uq_rope)], axis=-1).reshape(q_lora, MLA_HEADS * MLA_QK_PAD)
    kv_lora = w_ukv.shape[0]
    ukv = w_ukv.reshape(kv_lora, MLA_HEADS, MLA_NOPE_DIM + MLA_V_DIM)
    wuk = ukv[:, :, :MLA_NOPE_DIM].reshape(kv_lora, -1)
    wuvt = ukv[:, :, MLA_NOPE_DIM:].reshape(kv_lora, -1).T
    return win, wuq.astype(BF16), wuk.astype(BF16), wuvt.astype(BF16)


def _rope_consts():
    def inv(half):
        return jnp.exp(-math.log(ROPE_BASE) * jnp.arange(half, dtype=F32) / half)
    inv_r = inv(RET_HEAD_DIM // 2)
    inv_m = inv(MLA_ROPE_DIM // 2)
    half = LANES // 2
    rows = [
        jnp.concatenate([inv_r, inv_r]),
        jnp.concatenate([-jnp.ones(half, F32), jnp.ones(half, F32)]),
        jnp.concatenate([inv_m, inv_m, jnp.zeros(half, F32)]),
        jnp.concatenate([jnp.ones(half, F32), jnp.zeros(half, F32)]),
    ]
    return jnp.concatenate([jnp.stack(rows), jnp.zeros((4, LANES), F32)], axis=0)


def kernel(x, p, positions, mix_norm_g, w_in, q_norm_g, w_uq, kv_norm_g, w_ukv, w_o, ffn_norm_g,
           w_ffn_gate, w_ffn_up, w_ffn_down, ple_norm_g, w_ple_gate, w_ple_proj, final_norm_g):
    batch, seq, dm = x.shape
    depth = w_in.shape[0]
    n = batch * seq
    x2 = x.reshape(n, dm)
    pos2 = positions.reshape(n, 1)
    rc = _rope_consts()
    vec = lambda g: g.reshape(1, -1).astype(F32)
    for i in range(depth):
        win, wuq, wuk, wuvt = _prep_in_weights(w_in[i], w_uq[i], w_ukv[i])
        rq, rk, rv, rg, q, kn, kr, vt = _in_proj(
            x2, pos2, vec(mix_norm_g[i]), win, vec(q_norm_g[i]), wuq, vec(kv_norm_g[i]), wuk, wuvt,
            rc)
        ret = _retention(rq, rk, rv, rg, batch, seq)
        att = _attention(q, kn, kr, vt, batch, seq)
        x2 = _out_ffn(
            x2, ret, att, p[i].reshape(n, -1), w_o[i].astype(BF16), vec(ffn_norm_g[i]),
            w_ffn_gate[i].astype(BF16), w_ffn_up[i].astype(BF16), w_ffn_down[i].astype(BF16),
            vec(ple_norm_g[i]), w_ple_gate[i].astype(BF16), w_ple_proj[i].astype(BF16),
            vec(final_norm_g), final_norm=(i == depth - 1))
    return x2.reshape(batch, seq, dm)
```

```python
import functools
import math

import numpy as np
import jax
import jax.numpy as jnp
from jax import lax
from jax.experimental import pallas as pl
from jax.experimental.pallas import tpu as pltpu

F32 = jnp.float32
BF16 = jnp.bfloat16

CHUNK = 64
RET_HEADS = 4
RET_HEAD_DIM = 128
MLA_HEADS = 4
MLA_NOPE_DIM = 128
MLA_ROPE_DIM = 64
MLA_V_DIM = 128
MLA_QK_DIM = MLA_NOPE_DIM + MLA_ROPE_DIM
ROPE_BASE = 10000.0
RMS_EPS = 1e-6
GN_EPS = 1e-5

LANES = 128
MLA_QK_PAD = 2 * LANES
VMEM_LIMIT_BYTES = 56 * 1024 * 1024

IN_PROJ_ROWS = 512
RET_BLOCK = 256
ATTN_Q_BLOCK = 1024
ATTN_KV_BLOCK = 512
OUT_ROWS = 256


def _rms(x, g):
    return x * lax.rsqrt(jnp.mean(x * x, axis=-1, keepdims=True) + RMS_EPS) * g


def _sigmoid(x):
    return 1.0 / (1.0 + jnp.exp(-x))


def _dot(a, b):
    return jnp.dot(a, b, preferred_element_type=F32)


def _dot_nt(a, b):
    return lax.dot_general(a, b, (((1,), (1,)), ((), ())), preferred_element_type=F32)


def _dot_tn(a, b):
    return lax.dot_general(a, b, (((0,), (0,)), ((), ())), preferred_element_type=F32)


def _rot_half_lanes(x):
    return pltpu.roll(x, LANES // 2, 1)


def _in_proj_kernel(x_ref, pos_ref, gmix_ref, win_ref, qg_ref, wuq_ref, kvg_ref, wuk_ref, wuvt_ref,
                    rc_ref, rq_ref, rk_ref, rv_ref, rg_ref, q_ref, kn_ref, kr_ref, vt_ref, *, q_scale):
    h = _rms(x_ref[...], gmix_ref[...]).astype(BF16)
    z = _dot(h, win_ref[...])

    pos = pos_ref[...].astype(F32)
    rc = rc_ref[...]
    ang_r = pos * rc[0:1, :]
    cos_r = jnp.cos(ang_r)
    sin_r = jnp.sin(ang_r) * rc[1:2, :]
    ang_m = pos * rc[2:3, :]
    cos_m = jnp.cos(ang_m) * rc[3:4, :]
    sin_m = jnp.sin(ang_m) * rc[3:4, :]

    d = RET_HEAD_DIM
    w = RET_HEADS * d
    k_scale = d ** -0.5
    for hh in range(RET_HEADS):
        q = z[:, hh * d:(hh + 1) * d]
        rq_ref[:, hh * d:(hh + 1) * d] = (q * cos_r + _rot_half_lanes(q) * sin_r).astype(BF16)
        k = z[:, w + hh * d:w + (hh + 1) * d]
        rk_ref[:, hh * d:(hh + 1) * d] = ((k * cos_r + _rot_half_lanes(k) * sin_r) * k_scale).astype(BF16)
    rv_ref[...] = z[:, 2 * w:3 * w].astype(BF16)
    g = z[:, 3 * w:4 * w]
    rg_ref[...] = g * _sigmoid(g)

    o = 4 * w
    q_lora = qg_ref.shape[-1]
    kv_lora = kvg_ref.shape[-1]
    cq = z[:, o:o + q_lora]
    ckv = z[:, o + q_lora:o + q_lora + kv_lora]
    kpe = z[:, o + q_lora + kv_lora:o + q_lora + kv_lora + LANES]
    kr_ref[...] = (kpe * cos_m + _rot_half_lanes(kpe) * sin_m).astype(BF16)

    qq = _dot(_rms(cq, qg_ref[...]).astype(BF16), wuq_ref[...])
    for hh in range(MLA_HEADS):
        b = hh * MLA_QK_PAD
        q_ref[:, b:b + LANES] = (qq[:, b:b + LANES] * q_scale).astype(BF16)
        y = qq[:, b + LANES:b + 2 * LANES]
        q_ref[:, b + LANES:b + 2 * LANES] = (
            (y * cos_m + _rot_half_lanes(y) * sin_m) * q_scale).astype(BF16)

    ckvn = _rms(ckv, kvg_ref[...]).astype(BF16)
    kn_ref[...] = _dot(ckvn, wuk_ref[...]).astype(BF16)
    vt = _dot_nt(wuvt_ref[...], ckvn).astype(BF16)
    tk = vt_ref.shape[-1]
    for j in range(vt_ref.shape[0]):
        vt_ref[j] = vt[:, j * tk:(j + 1) * tk]


def _in_proj(x2, pos2, gmix, win, qg, wuq, kvg, wuk, wuvt, rc):
    n, dm = x2.shape
    tm = IN_PROJ_ROWS
    w = RET_HEADS * RET_HEAD_DIM
    const = lambda i: (0, 0)
    row = lambda i: (i, 0)
    full = lambda a: pl.BlockSpec(a.shape, const)
    out_shapes = (
        jax.ShapeDtypeStruct((n, w), BF16),
        jax.ShapeDtypeStruct((n, w), BF16),
        jax.ShapeDtypeStruct((n, w), BF16),
        jax.ShapeDtypeStruct((n, w), F32),
        jax.ShapeDtypeStruct((n, MLA_HEADS * MLA_QK_PAD), BF16),
        jax.ShapeDtypeStruct((n, MLA_HEADS * MLA_NOPE_DIM), BF16),
        jax.ShapeDtypeStruct((n, LANES), BF16),
    )
    tk = ATTN_KV_BLOCK
    vt_shape = jax.ShapeDtypeStruct((n // tk, MLA_HEADS * MLA_V_DIM, tk), BF16)
    out_specs = tuple(pl.BlockSpec((tm, s.shape[1]), row) for s in out_shapes) + (
        pl.BlockSpec((tm // tk, MLA_HEADS * MLA_V_DIM, tk), lambda i: (i, 0, 0)),)
    q_scale = MLA_QK_DIM ** -0.5 * math.log2(math.e)
    return pl.pallas_call(
        functools.partial(_in_proj_kernel, q_scale=q_scale),
        out_shape=out_shapes + (vt_shape,),
        grid=(n // tm,),
        in_specs=[pl.BlockSpec((tm, dm), row), pl.BlockSpec((tm, 1), row),
                  full(gmix), full(win), full(qg), full(wuq), full(kvg), full(wuk), full(wuvt),
                  full(rc)],
        out_specs=out_specs,
        compiler_params=pltpu.CompilerParams(
            dimension_semantics=("arbitrary",), vmem_limit_bytes=VMEM_LIMIT_BYTES),
        name="in_proj",
    )(x2, pos2, gmix, win, qg, wuq, kvg, wuk, wuvt, rc)


def _retention_kernel(rq_ref, rk_ref, rv_ref, rg_ref, dmat_ref, qdec_ref, kdec_ref, out_ref,
                      state_ref, *, block_decay):
    @pl.when(pl.program_id(1) == 0)
    def _():
        state_ref[...] = jnp.zeros_like(state_ref)

    d = RET_HEAD_DIM
    for hh in range(RET_HEADS):
        sl = slice(hh * d, (hh + 1) * d)
        q = rq_ref[:, sl]
        k = rk_ref[:, sl]
        v = rv_ref[:, sl]
        s = _dot_nt(q, k) * dmat_ref[hh]
        o = _dot(s.astype(BF16), v)
        st = state_ref[hh]
        o = o + _dot(q, st.astype(BF16)) * qdec_ref[hh]
        kd = (k.astype(F32) * kdec_ref[hh]).astype(BF16)
        state_ref[hh] = block_decay[hh] * st + _dot_tn(kd, v)
        mu = jnp.mean(o, axis=-1, keepdims=True)
        oc = o - mu
        var = jnp.mean(oc * oc, axis=-1, keepdims=True)
        out_ref[:, sl] = (oc * lax.rsqrt(var + GN_EPS) * rg_ref[:, sl]).astype(BF16)


def _retention(rq, rk, rv, rg, batch, seq):
    n, w = rq.shape
    blk = RET_BLOCK
    nb = seq // blk
    log_g = np.log1p(-np.exp2(-5.0 - np.arange(RET_HEADS, dtype=np.float64)))
    idx = jnp.arange(blk, dtype=F32)
    lg = jnp.asarray(log_g, F32)
    dist = jnp.abs(idx[:, None] - idx[None, :])
    visible = (idx[None, :] // CHUNK) <= (idx[:, None] // CHUNK)
    dmat = jnp.where(visible[None], jnp.exp(lg[:, None, None] * dist[None]), 0.0)
    qdec = jnp.broadcast_to(jnp.exp(lg[:, None] * (idx + 1.0))[:, :, None], (RET_HEADS, blk, LANES))
    kdec = jnp.broadcast_to(jnp.exp(lg[:, None] * (blk - 1.0 - idx))[:, :, None], (RET_HEADS, blk, LANES))
    block_decay = tuple(float(math.exp(g * blk)) for g in log_g)

    row = lambda b, j: (b * nb + j, 0)
    const3 = lambda b, j: (0, 0, 0)
    return pl.pallas_call(
        functools.partial(_retention_kernel, block_decay=block_decay),
        out_shape=jax.ShapeDtypeStruct((n, w), BF16),
        grid=(batch, nb),
        in_specs=[pl.BlockSpec((blk, w), row)] * 4 + [
            pl.BlockSpec(dmat.shape, const3), pl.BlockSpec(qdec.shape, const3),
            pl.BlockSpec(kdec.shape, const3)],
        out_specs=pl.BlockSpec((blk, w), row),
        scratch_shapes=[pltpu.VMEM((RET_HEADS, RET_HEAD_DIM, RET_HEAD_DIM), F32)],
        compiler_params=pltpu.CompilerParams(
            dimension_semantics=("arbitrary", "arbitrary"), vmem_limit_bytes=VMEM_LIMIT_BYTES),
        name="retention",
    )(rq, rk, rv, rg, dmat, qdec, kdec)


def _attention_kernel(q_ref, kn_ref, kr_ref, vt_ref, out_ref,
                      m_ref, l_ref, acc_ref, s_ref, smax_ref, p_ref):
    tq = q_ref.shape[0]
    gw = tq // 2
    per_q = tq // vt_ref.shape[-1]
    i = pl.program_id(2)
    m_ref[...] = jnp.full_like(m_ref, -1e30)
    l_ref[...] = jnp.zeros_like(l_ref)
    acc_ref[...] = jnp.zeros_like(acc_ref)

    def keys(j):
        start = pl.multiple_of(j * tq, tq)
        return jnp.concatenate(
            [kn_ref[pl.ds(start, tq), :], kr_ref[pl.ds(start, tq), :]], axis=1)

    def values_t(j):
        return jnp.concatenate([vt_ref[j * per_q + t] for t in range(per_q)], axis=1)

    def scores(g, k):
        st = _dot_nt(k, q_ref[g * gw:(g + 1) * gw, :])
        s_ref[g] = st
        smax_ref[g] = jnp.max(st, axis=0, keepdims=True)

    def accumulate(g, vt, diagonal):
        cols = slice(g * gw, (g + 1) * gw)
        st = s_ref[g]
        if diagonal:
            kc = lax.broadcasted_iota(jnp.int32, (tq, gw), 0) // CHUNK
            qc = (lax.broadcasted_iota(jnp.int32, (tq, gw), 1) + g * gw) // CHUNK
            st = jnp.where(kc <= qc, st, -1e30)
            smax = jnp.max(st, axis=0, keepdims=True)
        else:
            smax = smax_ref[g]
        m_old = m_ref[:, cols]
        m_new = jnp.maximum(m_old, smax)
        alpha = jnp.exp2(m_old - m_new)
        p = jnp.exp2(st - m_new)
        l_ref[:, cols] = alpha * l_ref[:, cols] + jnp.sum(p, axis=0, keepdims=True)
        p_ref[g] = p.astype(BF16)
        acc_ref[:, cols] = alpha * acc_ref[:, cols] + _dot(vt, p_ref[g])
        m_ref[:, cols] = m_new

    def body(j, carry):
        vt = values_t(j)
        scores(1, keys(j))
        accumulate(0, vt, False)
        scores(0, keys(j + 1))
        accumulate(1, vt, False)
        return carry

    scores(0, keys(0))
    lax.fori_loop(0, i, body, 0)
    vt = values_t(i)
    scores(1, keys(i))
    accumulate(0, vt, True)
    accumulate(1, vt, True)
    o = acc_ref[...] * (1.0 / l_ref[...])
    out_ref[...] = o.T.astype(BF16)


def _attention(q, kn, kr, vt, batch, seq):
    n = q.shape[0]
    tq = ATTN_Q_BLOCK
    tk = ATTN_KV_BLOCK
    nq = seq // tq
    return pl.pallas_call(
        _attention_kernel,
        out_shape=jax.ShapeDtypeStruct((n, MLA_HEADS * MLA_V_DIM), BF16),
        grid=(batch, MLA_HEADS, nq),
        in_specs=[pl.BlockSpec((tq, MLA_QK_PAD), lambda b, h, i: (b * nq + i, h)),
                  pl.BlockSpec((seq, MLA_NOPE_DIM), lambda b, h, i: (b, h)),
                  pl.BlockSpec((seq, LANES), lambda b, h, i: (b, 0)),
                  pl.BlockSpec((seq // tk, MLA_V_DIM, tk), lambda b, h, i: (b, h, 0))],
        out_specs=pl.BlockSpec((tq, MLA_V_DIM), lambda b, h, i: (b * nq + i, h)),
        scratch_shapes=[pltpu.VMEM((1, tq), F32), pltpu.VMEM((1, tq), F32),
                        pltpu.VMEM((MLA_V_DIM, tq), F32),
                        pltpu.VMEM((2, tq, tq // 2), F32), pltpu.VMEM((2, 1, tq // 2), F32),
                        pltpu.VMEM((2, tq, tq // 2), BF16)],
        compiler_params=pltpu.CompilerParams(
            dimension_semantics=("arbitrary", "arbitrary", "arbitrary"),
            vmem_limit_bytes=VMEM_LIMIT_BYTES),
        name="attention",
    )(q, kn, kr, vt)


def _out_ffn_kernel(x_ref, ret_ref, att_ref, p_ref, wo_ref, gffn_ref, wg_ref, wu_ref, wd_ref,
                    gple_ref, wpg_ref, wpp_ref, gfin_ref, out_ref, *, final_norm):
    a = jnp.concatenate([ret_ref[...], att_ref[...]], axis=1)
    x1 = x_ref[...] + _dot(a, wo_ref[...])
    h = _rms(x1, gffn_ref[...]).astype(BF16)
    g = _dot(h, wg_ref[...])
    u = _dot(h, wu_ref[...])
    act = (g * _sigmoid(g) * u).astype(BF16)
    x2 = x1 + _dot(act, wd_ref[...])
    gate = _sigmoid(_dot(_rms(x2, gple_ref[...]).astype(BF16), wpg_ref[...]))
    x3 = x2 + gate * _dot(p_ref[...].astype(BF16), wpp_ref[...])
    if final_norm:
        x3 = _rms(x3, gfin_ref[...])
    out_ref[...] = x3


def _out_ffn(x2, ret, att, p2, wo, gffn, wg, wu, wd, gple, wpg, wpp, gfin, final_norm):
    n, dm = x2.shape
    tm = OUT_ROWS
    row = lambda i: (i, 0)
    const = lambda i: (0, 0)
    resident = lambda a: pl.BlockSpec(a.shape, const, pipeline_mode=pl.Buffered(1))
    return pl.pallas_call(
        functools.partial(_out_ffn_kernel, final_norm=final_norm),
        out_shape=jax.ShapeDtypeStruct((n, dm), F32),
        grid=(n // tm,),
        in_specs=[pl.BlockSpec((tm, dm), row), pl.BlockSpec((tm, ret.shape[1]), row),
                  pl.BlockSpec((tm, att.shape[1]), row), pl.BlockSpec((tm, p2.shape[1]), row),
                  resident(wo), resident(gffn), resident(wg), resident(wu), resident(wd),
                  resident(gple), resident(wpg), resident(wpp), resident(gfin)],
        out_specs=pl.BlockSpec((tm, dm), row),
        compiler_params=pltpu.CompilerParams(
            dimension_semantics=("arbitrary",), vmem_limit_bytes=VMEM_LIMIT_BYTES),
        name="out_ffn",
    )(x2, ret, att, p2, wo, gffn, wg, wu, wd, gple, wpg, wpp, gfin)


def _rot_cols(w):
    half = w.shape[-1] // 2
    return jnp.concatenate([-w[..., half:], w[..., :half]], axis=-1)


def _prep_in_weights(w_in, w_uq, w_ukv):
    kpe = w_in[:, -MLA_ROPE_DIM:]
    win = jnp.concatenate([w_in, _rot_cols(kpe)], axis=1).astype(BF16)
    q_lora = w_uq.shape[0]
    uq = w_uq.reshape(q_lora, MLA_HEADS, MLA_QK_DIM)
    uq_rope = uq[:, :, MLA_NOPE_DIM:]
    wuq = jnp.concatenate([uq, _rot_cols(uq_rope)], axis=-1).reshape(q_lora, MLA_HEADS * MLA_QK_PAD)
    kv_lora = w_ukv.shape[0]
    ukv = w_ukv.reshape(kv_lora, MLA_HEADS, MLA_NOPE_DIM + MLA_V_DIM)
    wuk = ukv[:, :, :MLA_NOPE_DIM].reshape(kv_lora, -1)
    wuvt = ukv[:, :, MLA_NOPE_DIM:].reshape(kv_lora, -1).T
    return win, wuq.astype(BF16), wuk.astype(BF16), wuvt.astype(BF16)


def _rope_consts():
    def inv(half):
        return jnp.exp(-math.log(ROPE_BASE) * jnp.arange(half, dtype=F32) / half)
    inv_r = inv(RET_HEAD_DIM // 2)
    inv_m = inv(MLA_ROPE_DIM // 2)
    half = LANES // 2
    rows = [
        jnp.concatenate([inv_r, inv_r]),
        jnp.concatenate([-jnp.ones(half, F32), jnp.ones(half, F32)]),
        jnp.concatenate([inv_m, inv_m, jnp.zeros(half, F32)]),
        jnp.concatenate([jnp.ones(half, F32), jnp.zeros(half, F32)]),
    ]
    return jnp.concatenate([jnp.stack(rows), jnp.zeros((4, LANES), F32)], axis=0)


def kernel(x, p, positions, mix_norm_g, w_in, q_norm_g, w_uq, kv_norm_g, w_ukv, w_o, ffn_norm_g,
           w_ffn_gate, w_ffn_up, w_ffn_down, ple_norm_g, w_ple_gate, w_ple_proj, final_norm_g):
    batch, seq, dm = x.shape
    depth = w_in.shape[0]
    n = batch * seq
    x2 = x.reshape(n, dm)
    pos2 = positions.reshape(n, 1)
    rc = _rope_consts()
    vec = lambda g: g.reshape(1, -1).astype(F32)
    for i in range(depth):
        win, wuq, wuk, wuvt = _prep_in_weights(w_in[i], w_uq[i], w_ukv[i])
        rq, rk, rv, rg, q, kn, kr, vt = _in_proj(
            x2, pos2, vec(mix_norm_g[i]), win, vec(q_norm_g[i]), wuq, vec(kv_norm_g[i]), wuk, wuvt,
            rc)
        ret = _retention(rq, rk, rv, rg, batch, seq)
        att = _attention(q, kn, kr, vt, batch, seq)
        x2 = _out_ffn(
            x2, ret, att, p[i].reshape(n, -1), w_o[i].astype(BF16), vec(ffn_norm_g[i]),
            w_ffn_gate[i].astype(BF16), w_ffn_up[i].astype(BF16), w_ffn_down[i].astype(BF16),
            vec(ple_norm_g[i]), w_ple_gate[i].astype(BF16), w_ple_proj[i].astype(BF16),
            vec(final_norm_g), final_norm=(i == depth - 1))
    return x2.reshape(batch, seq, dm)
```

```python
import functools
import math

import numpy as np
import jax
import jax.numpy as jnp
from jax import lax
from jax.experimental import pallas as pl
from jax.experimental.pallas import tpu as pltpu

F32 = jnp.float32
BF16 = jnp.bfloat16

CHUNK = 64
RET_HEADS = 4
RET_HEAD_DIM = 128
MLA_HEADS = 4
MLA_NOPE_DIM = 128
MLA_ROPE_DIM = 64
MLA_V_DIM = 128
MLA_QK_DIM = MLA_NOPE_DIM + MLA_ROPE_DIM
ROPE_BASE = 10000.0
RMS_EPS = 1e-6
GN_EPS = 1e-5

LANES = 128
MLA_QK_PAD = 2 * LANES
VMEM_LIMIT_BYTES = 56 * 1024 * 1024

IN_PROJ_ROWS = 512
RET_BLOCK = 256
ATTN_Q_BLOCK = 1024
ATTN_KV_BLOCK = 512
ATTN_MAX_JUMP = 32.0
ATTN_KEY_SLICE = 256
ATTN_LOOKAHEAD = 3
OUT_ROWS = 256


def _rms(x, g):
    return x * lax.rsqrt(jnp.mean(x * x, axis=-1, keepdims=True) + RMS_EPS) * g


def _sigmoid(x):
    return 1.0 / (1.0 + jnp.exp(-x))


def _dot(a, b):
    return jnp.dot(a, b, preferred_element_type=F32)


def _dot_nt(a, b):
    return lax.dot_general(a, b, (((1,), (1,)), ((), ())), preferred_element_type=F32)


def _dot_tn(a, b):
    return lax.dot_general(a, b, (((0,), (0,)), ((), ())), preferred_element_type=F32)


def _rot_half_lanes(x):
    return pltpu.roll(x, LANES // 2, 1)


def _in_proj_kernel(x_ref, pos_ref, gmix_ref, win_ref, qg_ref, wuq_ref, kvg_ref, wuk_ref, wuvt_ref,
                    rc_ref, rq_ref, rk_ref, rv_ref, rg_ref, q_ref, kn_ref, kr_ref, vt_ref, *, q_scale):
    h = _rms(x_ref[...], gmix_ref[...]).astype(BF16)
    z = _dot(h, win_ref[...])

    anchor = lax.bitcast_convert_type(z[:, 0:1], jnp.uint32)
    zero = lax.shift_right_logical(lax.shift_right_logical(anchor, jnp.uint32(16)), jnp.uint32(16))
    pos = (pos_ref[...] + zero.astype(jnp.int32)).astype(F32)
    ang = pos * rc_ref[0:1, :]
    c = jnp.cos(ang)
    s = jnp.sin(ang)
    lane = lax.broadcasted_iota(jnp.int32, (1, LANES), 1)
    half = LANES // 2
    quarter = LANES // 4
    c_hi, s_hi = _rot_half_lanes(c), _rot_half_lanes(s)
    cos_r = jnp.where(lane < half, c, c_hi)
    sin_r = jnp.where(lane < half, -s, s_hi)
    c_q3 = pltpu.roll(c, 3 * quarter, 1)
    s_q3 = pltpu.roll(s, 3 * quarter, 1)
    cos_m = jnp.where(lane < quarter, c_hi, jnp.where(lane < half, c_q3, 0.0))
    sin_m = jnp.where(lane < quarter, s_hi, jnp.where(lane < half, s_q3, 0.0))

    d = RET_HEAD_DIM
    w = RET_HEADS * d
    k_scale = d ** -0.5
    for hh in range(RET_HEADS):
        q = z[:, hh * d:(hh + 1) * d]
        rq_ref[:, hh * d:(hh + 1) * d] = (q * cos_r + _rot_half_lanes(q) * sin_r).astype(BF16)
        k = z[:, w + hh * d:w + (hh + 1) * d]
        rk_ref[:, hh * d:(hh + 1) * d] = ((k * cos_r + _rot_half_lanes(k) * sin_r) * k_scale).astype(BF16)
    rv_ref[...] = z[:, 2 * w:3 * w].astype(BF16)
    g = z[:, 3 * w:4 * w]
    rg_ref[...] = g * _sigmoid(g)

    o = 4 * w
    q_lora = qg_ref.shape[-1]
    kv_lora = kvg_ref.shape[-1]
    cq = z[:, o:o + q_lora]
    ckv = z[:, o + q_lora:o + q_lora + kv_lora]
    kpe = z[:, o + q_lora + kv_lora:o + q_lora + kv_lora + LANES]
    kr_ref[...] = (kpe * cos_m + _rot_half_lanes(kpe) * sin_m).astype(BF16)

    qq = _dot(_rms(cq, qg_ref[...]).astype(BF16), wuq_ref[...])
    for hh in range(MLA_HEADS):
        b = hh * MLA_QK_PAD
        q_ref[:, b:b + LANES] = (qq[:, b:b + LANES] * q_scale).astype(BF16)
        y = qq[:, b + LANES:b + 2 * LANES]
        q_ref[:, b + LANES:b + 2 * LANES] = (
            (y * cos_m + _rot_half_lanes(y) * sin_m) * q_scale).astype(BF16)

    ckvn = _rms(ckv, kvg_ref[...]).astype(BF16)
    kn_ref[...] = _dot(ckvn, wuk_ref[...]).astype(BF16)
    vt = _dot_nt(wuvt_ref[...], ckvn).astype(BF16)
    tk = vt_ref.shape[-1]
    for j in range(vt_ref.shape[0]):
        vt_ref[j] = vt[:, j * tk:(j + 1) * tk]


def _in_proj(x2, pos2, gmix, win, qg, wuq, kvg, wuk, wuvt, rc):
    n, dm = x2.shape
    tm = IN_PROJ_ROWS
    w = RET_HEADS * RET_HEAD_DIM
    const = lambda i: (0, 0)
    row = lambda i: (i, 0)
    full = lambda a: pl.BlockSpec(a.shape, const)
    out_shapes = (
        jax.ShapeDtypeStruct((n, w), BF16),
        jax.ShapeDtypeStruct((n, w), BF16),
        jax.ShapeDtypeStruct((n, w), BF16),
        jax.ShapeDtypeStruct((n, w), F32),
        jax.ShapeDtypeStruct((n, MLA_HEADS * MLA_QK_PAD), BF16),
        jax.ShapeDtypeStruct((n, MLA_HEADS * MLA_NOPE_DIM), BF16),
        jax.ShapeDtypeStruct((n, LANES), BF16),
    )
    tk = ATTN_KV_BLOCK
    vt_shape = jax.ShapeDtypeStruct((n // tk, MLA_HEADS * MLA_V_DIM, tk), BF16)
    out_specs = tuple(pl.BlockSpec((tm, s.shape[1]), row) for s in out_shapes) + (
        pl.BlockSpec((tm // tk, MLA_HEADS * MLA_V_DIM, tk), lambda i: (i, 0, 0)),)
    q_scale = MLA_QK_DIM ** -0.5 * math.log2(math.e)
    return pl.pallas_call(
        functools.partial(_in_proj_kernel, q_scale=q_scale),
        out_shape=out_shapes + (vt_shape,),
        grid=(n // tm,),
        in_specs=[pl.BlockSpec((tm, dm), row), pl.BlockSpec((tm, 1), row),
                  full(gmix), full(win), full(qg), full(wuq), full(kvg), full(wuk), full(wuvt),
                  full(rc)],
        out_specs=out_specs,
        compiler_params=pltpu.CompilerParams(
            dimension_semantics=("arbitrary",), vmem_limit_bytes=VMEM_LIMIT_BYTES),
        name="in_proj",
    )(x2, pos2, gmix, win, qg, wuq, kvg, wuk, wuvt, rc)


def _retention_kernel(rq_ref, rk_ref, rv_ref, rg_ref, dmat_ref, qdec_ref, kdec_ref, out_ref,
                      state_ref, *, block_decay):
    @pl.when(pl.program_id(1) == 0)
    def _():
        state_ref[...] = jnp.zeros_like(state_ref)

    d = RET_HEAD_DIM
    for hh in range(RET_HEADS):
        sl = slice(hh * d, (hh + 1) * d)
        q = rq_ref[:, sl]
        k = rk_ref[:, sl]
        v = rv_ref[:, sl]
        s = _dot_nt(q, k) * dmat_ref[hh]
        o = _dot(s.astype(BF16), v)
        st = state_ref[hh]
        o = o + _dot(q, st.astype(BF16)) * qdec_ref[hh]
        kd = (k.astype(F32) * kdec_ref[hh]).astype(BF16)
        state_ref[hh] = block_decay[hh] * st + _dot_tn(kd, v)
        mu = jnp.mean(o, axis=-1, keepdims=True)
        oc = o - mu
        var = jnp.mean(oc * oc, axis=-1, keepdims=True)
        out_ref[:, sl] = (oc * lax.rsqrt(var + GN_EPS) * rg_ref[:, sl]).astype(BF16)


def _retention(rq, rk, rv, rg, batch, seq):
    n, w = rq.shape
    blk = RET_BLOCK
    nb = seq // blk
    log_g = np.log1p(-np.exp2(-5.0 - np.arange(RET_HEADS, dtype=np.float64)))
    idx = jnp.arange(blk, dtype=F32)
    lg = jnp.asarray(log_g, F32)
    dist = jnp.abs(idx[:, None] - idx[None, :])
    visible = (idx[None, :] // CHUNK) <= (idx[:, None] // CHUNK)
    dmat = jnp.where(visible[None], jnp.exp(lg[:, None, None] * dist[None]), 0.0)
    qdec = jnp.broadcast_to(jnp.exp(lg[:, None] * (idx + 1.0))[:, :, None], (RET_HEADS, blk, LANES))
    kdec = jnp.broadcast_to(jnp.exp(lg[:, None] * (blk - 1.0 - idx))[:, :, None], (RET_HEADS, blk, LANES))
    block_decay = tuple(float(math.exp(g * blk)) for g in log_g)

    row = lambda b, j: (b * nb + j, 0)
    const3 = lambda b, j: (0, 0, 0)
    return pl.pallas_call(
        functools.partial(_retention_kernel, block_decay=block_decay),
        out_shape=jax.ShapeDtypeStruct((n, w), BF16),
        grid=(batch, nb),
        in_specs=[pl.BlockSpec((blk, w), row)] * 4 + [
            pl.BlockSpec(dmat.shape, const3), pl.BlockSpec(qdec.shape, const3),
            pl.BlockSpec(kdec.shape, const3)],
        out_specs=pl.BlockSpec((blk, w), row),
        scratch_shapes=[pltpu.VMEM((RET_HEADS, RET_HEAD_DIM, RET_HEAD_DIM), F32)],
        compiler_params=pltpu.CompilerParams(
            dimension_semantics=("arbitrary", "arbitrary"), vmem_limit_bytes=VMEM_LIMIT_BYTES),
        name="retention",
    )(rq, rk, rv, rg, dmat, qdec, kdec)


def _attention_kernel(q_ref, kn_ref, kr_ref, vt_ref, out_ref,
                      m_ref, l_ref, acc_ref, jump_ref):
    tq = q_ref.shape[0]
    gw = tq // 2
    tk = vt_ref.shape[-1]
    per_q = tq // tk
    i = pl.program_id(2)

    def keys(j, size=tq):
        start = pl.multiple_of(j * tq, tq)
        return jnp.concatenate(
            [kn_ref[pl.ds(start, size), :], kr_ref[pl.ds(start, size), :]], axis=1)

    def values_t(j, blocks=per_q):
        return jnp.concatenate([vt_ref[j * per_q + t] for t in range(blocks)], axis=1)

    def reset():
        m_ref[...] = jnp.full_like(m_ref, -1e30)
        l_ref[...] = jnp.zeros_like(l_ref)
        acc_ref[...] = jnp.zeros_like(acc_ref)

    def two_pass(g, k, vt, diagonal):
        cols = slice(g * gw, (g + 1) * gw)
        st = _dot_nt(k, q_ref[g * gw:(g + 1) * gw, :])
        if diagonal:
            kc = lax.broadcasted_iota(jnp.int32, st.shape, 0) // CHUNK
            qc = (lax.broadcasted_iota(jnp.int32, st.shape, 1) + g * gw) // CHUNK
            st = jnp.where(kc <= qc, st, -1e30)
        m_old = m_ref[:, cols]
        m_new = jnp.maximum(m_old, jnp.max(st, axis=0, keepdims=True))
        alpha = jnp.exp2(m_old - m_new)
        p = jnp.exp2(st - m_new)
        l_ref[:, cols] = alpha * l_ref[:, cols] + jnp.sum(p, axis=0, keepdims=True)
        acc_ref[:, cols] = alpha * acc_ref[:, cols] + _dot(vt, p.astype(BF16))
        m_ref[:, cols] = m_new

    def diagonal_block():
        two_pass(0, keys(i, gw), values_t(i, per_q // 2), True)
        two_pass(1, keys(i), values_t(i), True)

    def single_pass(j, items):
        k = keys(j)
        vt = values_t(j)
        groups = (0, 1)
        cols = [slice(g * gw, (g + 1) * gw) for g in groups]
        qs = [q_ref[c, :] for c in cols]
        m_old = [m_ref[:, c] for c in cols]
        l_new = [l_ref[:, c] for c in cols]
        acc_new = [acc_ref[:, c] for c in cols]
        smax = [None, None]

        def qk(item):
            g, a, _ = item
            return _dot_nt(k[a:a + ATTN_KEY_SLICE, :], qs[g])

        pending = [qk(item) for item in items[:ATTN_LOOKAHEAD]]
        for n, (g, a, masked) in enumerate(items):
            if n + ATTN_LOOKAHEAD < len(items):
                pending.append(qk(items[n + ATTN_LOOKAHEAD]))
            st = pending.pop(0)
            if masked:
                kc = (lax.broadcasted_iota(jnp.int32, st.shape, 0) + a) // CHUNK
                qc = (lax.broadcasted_iota(jnp.int32, st.shape, 1) + g * gw) // CHUNK
                st = jnp.where(kc <= qc, st, -1e30)
            p = jnp.exp2(st - m_old[g])
            cmax = jnp.max(st, axis=0, keepdims=True)
            smax[g] = cmax if smax[g] is None else jnp.maximum(smax[g], cmax)
            l_new[g] = l_new[g] + jnp.sum(p, axis=0, keepdims=True)
            acc_new[g] = acc_new[g] + _dot(vt[:, a:a + ATTN_KEY_SLICE], p.astype(BF16))
        for g in groups:
            m_new = jnp.maximum(m_old[g], smax[g])
            alpha = jnp.exp2(m_old[g] - m_new)
            l_ref[:, cols[g]] = alpha * l_new[g]
            acc_ref[:, cols[g]] = alpha * acc_new[g]
            m_ref[:, cols[g]] = m_new
            jump_ref[:, cols[g]] = jnp.maximum(jump_ref[:, cols[g]], smax[g] - m_old[g])

    slices = range(0, tq, ATTN_KEY_SLICE)
    full_items = [(g, a, False) for a in slices for g in (0, 1)]
    diag_items = [(g, a, a >= g * gw) for a in slices for g in (0, 1) if a < (g + 1) * gw]

    def fast_body(j, carry):
        single_pass(j, full_items)
        return carry

    def safe_body(j, carry):
        k = keys(j)
        vt = values_t(j)
        two_pass(0, k, vt, False)
        two_pass(1, k, vt, False)
        return carry

    m_ref[...] = jnp.max(_dot_nt(keys(i, CHUNK), q_ref[...]), axis=0, keepdims=True)
    l_ref[...] = jnp.zeros_like(l_ref)
    acc_ref[...] = jnp.zeros_like(acc_ref)
    jump_ref[...] = jnp.zeros_like(jump_ref)
    lax.fori_loop(0, i, fast_body, 0)
    single_pass(i, diag_items)

    @pl.when(jnp.max(jump_ref[...]) > ATTN_MAX_JUMP)
    def _():
        reset()
        diagonal_block()
        lax.fori_loop(0, i, safe_body, 0)

    o = acc_ref[...] * (1.0 / l_ref[...])
    out_ref[...] = o.T.astype(BF16)


def _attention(q, kn, kr, vt, batch, seq):
    n = q.shape[0]
    tq = ATTN_Q_BLOCK
    tk = ATTN_KV_BLOCK
    nq = seq // tq
    return pl.pallas_call(
        _attention_kernel,
        out_shape=jax.ShapeDtypeStruct((n, MLA_HEADS * MLA_V_DIM), BF16),
        grid=(batch, MLA_HEADS, nq),
        in_specs=[pl.BlockSpec((tq, MLA_QK_PAD), lambda b, h, i: (b * nq + i, h)),
                  pl.BlockSpec((seq, MLA_NOPE_DIM), lambda b, h, i: (b, h)),
                  pl.BlockSpec((seq, LANES), lambda b, h, i: (b, 0)),
                  pl.BlockSpec((seq // tk, MLA_V_DIM, tk), lambda b, h, i: (b, h, 0))],
        out_specs=pl.BlockSpec((tq, MLA_V_DIM), lambda b, h, i: (b * nq + i, h)),
        scratch_shapes=[pltpu.VMEM((1, tq), F32), pltpu.VMEM((1, tq), F32),
                        pltpu.VMEM((MLA_V_DIM, tq), F32), pltpu.VMEM((1, tq), F32)],
        compiler_params=pltpu.CompilerParams(
            dimension_semantics=("arbitrary", "arbitrary", "arbitrary"),
            vmem_limit_bytes=VMEM_LIMIT_BYTES),
        name="attention",
    )(q, kn, kr, vt)


def _out_ffn_kernel(x_ref, ret_ref, att_ref, p_ref, wo_ref, gffn_ref, wg_ref, wu_ref, wd_ref,
                    gple_ref, wpg_ref, wpp_ref, gfin_ref, out_ref, *, final_norm):
    a = jnp.concatenate([ret_ref[...], att_ref[...]], axis=1)
    x1 = x_ref[...] + _dot(a, wo_ref[...])
    h = _rms(x1, gffn_ref[...]).astype(BF16)
    g = _dot(h, wg_ref[...])
    u = _dot(h, wu_ref[...])
    act = (g * _sigmoid(g) * u).astype(BF16)
    x2 = x1 + _dot(act, wd_ref[...])
    gate = _sigmoid(_dot(_rms(x2, gple_ref[...]).astype(BF16), wpg_ref[...]))
    x3 = x2 + gate * _dot(p_ref[...].astype(BF16), wpp_ref[...])
    if final_norm:
        x3 = _rms(x3, gfin_ref[...])
    out_ref[...] = x3


def _out_ffn(x2, ret, att, p2, wo, gffn, wg, wu, wd, gple, wpg, wpp, gfin, final_norm):
    n, dm = x2.shape
    tm = OUT_ROWS
    row = lambda i: (i, 0)
    const = lambda i: (0, 0)
    resident = lambda a: pl.BlockSpec(a.shape, const, pipeline_mode=pl.Buffered(1))
    return pl.pallas_call(
        functools.partial(_out_ffn_kernel, final_norm=final_norm),
        out_shape=jax.ShapeDtypeStruct((n, dm), F32),
        grid=(n // tm,),
        in_specs=[pl.BlockSpec((tm, dm), row), pl.BlockSpec((tm, ret.shape[1]), row),
                  pl.BlockSpec((tm, att.shape[1]), row), pl.BlockSpec((tm, p2.shape[1]), row),
                  resident(wo), resident(gffn), resident(wg), resident(wu), resident(wd),
                  resident(gple), resident(wpg), resident(wpp), resident(gfin)],
        out_specs=pl.BlockSpec((tm, dm), row),
        compiler_params=pltpu.CompilerParams(
            dimension_semantics=("arbitrary",), vmem_limit_bytes=VMEM_LIMIT_BYTES),
        name="out_ffn",
    )(x2, ret, att, p2, wo, gffn, wg, wu, wd, gple, wpg, wpp, gfin)


def _rot_cols(w):
    half = w.shape[-1] // 2
    return jnp.concatenate([-w[..., half:], w[..., :half]], axis=-1)


def _prep_in_weights(w_in, w_uq, w_ukv):
    kpe = w_in[:, -MLA_ROPE_DIM:]
    win = jnp.concatenate([w_in, _rot_cols(kpe)], axis=1).astype(BF16)
    q_lora = w_uq.shape[0]
    uq = w_uq.reshape(q_lora, MLA_HEADS, MLA_QK_DIM)
    uq_rope = uq[:, :, MLA_NOPE_DIM:]
    wuq = jnp.concatenate([uq, _rot_cols(uq_rope)], axis=-1).reshape(q_lora, MLA_HEADS * MLA_QK_PAD)
    kv_lora = w_ukv.shape[0]
    ukv = w_ukv.reshape(kv_lora, MLA_HEADS, MLA_NOPE_DIM + MLA_V_DIM)
    wuk = ukv[:, :, :MLA_NOPE_DIM].reshape(kv_lora, -1)
    wuvt = ukv[:, :, MLA_NOPE_DIM:].reshape(kv_lora, -1).T
    return win, wuq.astype(BF16), wuk.astype(BF16), wuvt.astype(BF16)


def _rope_consts():
    def inv(half):
        return jnp.exp(-math.log(ROPE_BASE) * jnp.arange(half, dtype=F32) / half)
    inv_r = inv(RET_HEAD_DIM // 2)
    inv_m = inv(MLA_ROPE_DIM // 2)
    row = jnp.concatenate([inv_r, inv_m, jnp.zeros(LANES - inv_r.size - inv_m.size, F32)])
    return jnp.broadcast_to(row[None, :], (8, LANES))


def kernel(x, p, positions, mix_norm_g, w_in, q_norm_g, w_uq, kv_norm_g, w_ukv, w_o, ffn_norm_g,
           w_ffn_gate, w_ffn_up, w_ffn_down, ple_norm_g, w_ple_gate, w_ple_proj, final_norm_g):
    batch, seq, dm = x.shape
    depth = w_in.shape[0]
    n = batch * seq
    x2 = x.reshape(n, dm)
    pos2 = positions.reshape(n, 1)
    rc = _rope_consts()
    vec = lambda g: g.reshape(1, -1).astype(F32)
    for i in range(depth):
        win, wuq, wuk, wuvt = _prep_in_weights(w_in[i], w_uq[i], w_ukv[i])
        rq, rk, rv, rg, q, kn, kr, vt = _in_proj(
            x2, pos2, vec(mix_norm_g[i]), win, vec(q_norm_g[i]), wuq, vec(kv_norm_g[i]), wuk, wuvt,
            rc)
        ret = _retention(rq, rk, rv, rg, batch, seq)
        att = _attention(q, kn, kr, vt, batch, seq)
        x2 = _out_ffn(
            x2, ret, att, p[i].reshape(n, -1), w_o[i].astype(BF16), vec(ffn_norm_g[i]),
            w_ffn_gate[i].astype(BF16), w_ffn_up[i].astype(BF16), w_ffn_down[i].astype(BF16),
            vec(ple_norm_g[i]), w_ple_gate[i].astype(BF16), w_ple_proj[i].astype(BF16),
            vec(final_norm_g), final_norm=(i == depth - 1))
    return x2.reshape(batch, seq, dm)
```

```python
import functools
import math

import numpy as np
import jax
import jax.numpy as jnp
from jax import lax
from jax.experimental import pallas as pl
from jax.experimental.pallas import tpu as pltpu

F32 = jnp.float32
BF16 = jnp.bfloat16

CHUNK = 64
RET_HEADS = 4
RET_HEAD_DIM = 128
MLA_HEADS = 4
MLA_NOPE_DIM = 128
MLA_ROPE_DIM = 64
MLA_V_DIM = 128
MLA_QK_DIM = MLA_NOPE_DIM + MLA_ROPE_DIM
ROPE_BASE = 10000.0
RMS_EPS = 1e-6
GN_EPS = 1e-5

LANES = 128
MLA_QK_PAD = 2 * LANES
VMEM_LIMIT_BYTES = 56 * 1024 * 1024

IN_PROJ_ROWS = 512
RET_BLOCK = 256
ATTN_Q_BLOCK = 1024
ATTN_KV_BLOCK = 512
ATTN_MAX_JUMP = 32.0
ATTN_KEY_SLICE = 256
ATTN_LOOKAHEAD = 3
OUT_ROWS = 256


def _rms(x, g):
    return x * lax.rsqrt(jnp.mean(x * x, axis=-1, keepdims=True) + RMS_EPS) * g


def _sigmoid(x):
    return 1.0 / (1.0 + jnp.exp(-x))


def _dot(a, b):
    return jnp.dot(a, b, preferred_element_type=F32)


def _dot_nt(a, b):
    return lax.dot_general(a, b, (((1,), (1,)), ((), ())), preferred_element_type=F32)


def _dot_tn(a, b):
    return lax.dot_general(a, b, (((0,), (0,)), ((), ())), preferred_element_type=F32)


def _rot_half_lanes(x):
    return pltpu.roll(x, LANES // 2, 1)


def _in_proj_kernel(x_ref, pos_ref, gmix_ref, win_ref, qg_ref, wuq_ref, kvg_ref, wuk_ref, wuvt_ref,
                    rc_ref, rq_ref, rk_ref, rv_ref, rg_ref, q_ref, kn_ref, kr_ref, vt_ref, *, q_scale):
    h = _rms(x_ref[...], gmix_ref[...]).astype(BF16)
    z = _dot(h, win_ref[...])

    anchor = lax.bitcast_convert_type(z[:, 0:1], jnp.uint32)
    zero = lax.shift_right_logical(lax.shift_right_logical(anchor, jnp.uint32(16)), jnp.uint32(16))
    pos = (pos_ref[...] + zero.astype(jnp.int32)).astype(F32)
    ang = pos * rc_ref[0:1, :]
    c = jnp.cos(ang)
    s = jnp.sin(ang)
    lane = lax.broadcasted_iota(jnp.int32, (1, LANES), 1)
    half = LANES // 2
    quarter = LANES // 4
    c_hi, s_hi = _rot_half_lanes(c), _rot_half_lanes(s)
    cos_r = jnp.where(lane < half, c, c_hi)
    sin_r = jnp.where(lane < half, -s, s_hi)
    c_q3 = pltpu.roll(c, 3 * quarter, 1)
    s_q3 = pltpu.roll(s, 3 * quarter, 1)
    cos_m = jnp.where(lane < quarter, c_hi, jnp.where(lane < half, c_q3, 0.0))
    sin_m = jnp.where(lane < quarter, s_hi, jnp.where(lane < half, s_q3, 0.0))

    d = RET_HEAD_DIM
    w = RET_HEADS * d
    k_scale = d ** -0.5
    for hh in range(RET_HEADS):
        q = z[:, hh * d:(hh + 1) * d]
        rq_ref[:, hh * d:(hh + 1) * d] = (q * cos_r + _rot_half_lanes(q) * sin_r).astype(BF16)
        k = z[:, w + hh * d:w + (hh + 1) * d]
        rk_ref[:, hh * d:(hh + 1) * d] = ((k * cos_r + _rot_half_lanes(k) * sin_r) * k_scale).astype(BF16)
    rv_ref[...] = z[:, 2 * w:3 * w].astype(BF16)
    g = z[:, 3 * w:4 * w]
    rg_ref[...] = g * _sigmoid(g)

    o = 4 * w
    q_lora = qg_ref.shape[-1]
    kv_lora = kvg_ref.shape[-1]
    cq = z[:, o:o + q_lora]
    ckv = z[:, o + q_lora:o + q_lora + kv_lora]
    kpe = z[:, o + q_lora + kv_lora:o + q_lora + kv_lora + LANES]
    kr_ref[...] = (kpe * cos_m + _rot_half_lanes(kpe) * sin_m).astype(BF16)

    qq = _dot(_rms(cq, qg_ref[...]).astype(BF16), wuq_ref[...])
    for hh in range(MLA_HEADS):
        b = hh * MLA_QK_PAD
        q_ref[:, b:b + LANES] = (qq[:, b:b + LANES] * q_scale).astype(BF16)
        y = qq[:, b + LANES:b + 2 * LANES]
        q_ref[:, b + LANES:b + 2 * LANES] = (
            (y * cos_m + _rot_half_lanes(y) * sin_m) * q_scale).astype(BF16)

    ckvn = _rms(ckv, kvg_ref[...]).astype(BF16)
    kn_ref[...] = _dot(ckvn, wuk_ref[...]).astype(BF16)
    vt = _dot_nt(wuvt_ref[...], ckvn).astype(BF16)
    tk = vt_ref.shape[-1]
    for j in range(vt_ref.shape[0]):
        vt_ref[j] = vt[:, j * tk:(j + 1) * tk]


def _in_proj(x2, pos2, gmix, win, qg, wuq, kvg, wuk, wuvt, rc):
    n, dm = x2.shape
    tm = IN_PROJ_ROWS
    w = RET_HEADS * RET_HEAD_DIM
    const = lambda i: (0, 0)
    row = lambda i: (i, 0)
    full = lambda a: pl.BlockSpec(a.shape, const)
    out_shapes = (
        jax.ShapeDtypeStruct((n, w), BF16),
        jax.ShapeDtypeStruct((n, w), BF16),
        jax.ShapeDtypeStruct((n, w), BF16),
        jax.ShapeDtypeStruct((n, w), F32),
        jax.ShapeDtypeStruct((n, MLA_HEADS * MLA_QK_PAD), BF16),
        jax.ShapeDtypeStruct((n, MLA_HEADS * MLA_NOPE_DIM), BF16),
        jax.ShapeDtypeStruct((n, LANES), BF16),
    )
    tk = ATTN_KV_BLOCK
    vt_shape = jax.ShapeDtypeStruct((n // tk, MLA_HEADS * MLA_V_DIM, tk), BF16)
    out_specs = tuple(pl.BlockSpec((tm, s.shape[1]), row) for s in out_shapes) + (
        pl.BlockSpec((tm // tk, MLA_HEADS * MLA_V_DIM, tk), lambda i: (i, 0, 0)),)
    q_scale = MLA_QK_DIM ** -0.5 * math.log2(math.e)
    return pl.pallas_call(
        functools.partial(_in_proj_kernel, q_scale=q_scale),
        out_shape=out_shapes + (vt_shape,),
        grid=(n // tm,),
        in_specs=[pl.BlockSpec((tm, dm), row), pl.BlockSpec((tm, 1), row),
                  full(gmix), full(win), full(qg), full(wuq), full(kvg), full(wuk), full(wuvt),
                  full(rc)],
        out_specs=out_specs,
        compiler_params=pltpu.CompilerParams(
            dimension_semantics=("arbitrary",), vmem_limit_bytes=VMEM_LIMIT_BYTES),
        name="in_proj",
    )(x2, pos2, gmix, win, qg, wuq, kvg, wuk, wuvt, rc)


def _retention_kernel(rq_ref, rk_ref, rv_ref, rg_ref, dmat_ref, qdec_ref, kdec_ref, out_ref,
                      state_ref, *, block_decay):
    @pl.when(pl.program_id(1) == 0)
    def _():
        state_ref[...] = jnp.zeros_like(state_ref)

    d = RET_HEAD_DIM
    for hh in range(RET_HEADS):
        sl = slice(hh * d, (hh + 1) * d)
        q = rq_ref[:, sl]
        k = rk_ref[:, sl]
        v = rv_ref[:, sl]
        s = _dot_nt(q, k) * dmat_ref[hh]
        o = _dot(s.astype(BF16), v)
        st = state_ref[hh]
        o = o + _dot(q, st.astype(BF16)) * qdec_ref[hh]
        kd = (k.astype(F32) * kdec_ref[hh]).astype(BF16)
        state_ref[hh] = block_decay[hh] * st + _dot_tn(kd, v)
        mu = jnp.mean(o, axis=-1, keepdims=True)
        oc = o - mu
        var = jnp.mean(oc * oc, axis=-1, keepdims=True)
        out_ref[:, sl] = (oc * lax.rsqrt(var + GN_EPS) * rg_ref[:, sl]).astype(BF16)


def _retention(rq, rk, rv, rg, batch, seq):
    n, w = rq.shape
    blk = RET_BLOCK
    nb = seq // blk
    log_g = np.log1p(-np.exp2(-5.0 - np.arange(RET_HEADS, dtype=np.float64)))
    idx = jnp.arange(blk, dtype=F32)
    lg = jnp.asarray(log_g, F32)
    dist = jnp.abs(idx[:, None] - idx[None, :])
    visible = (idx[None, :] // CHUNK) <= (idx[:, None] // CHUNK)
    dmat = jnp.where(visible[None], jnp.exp(lg[:, None, None] * dist[None]), 0.0)
    qdec = jnp.broadcast_to(jnp.exp(lg[:, None] * (idx + 1.0))[:, :, None], (RET_HEADS, blk, LANES))
    kdec = jnp.broadcast_to(jnp.exp(lg[:, None] * (blk - 1.0 - idx))[:, :, None], (RET_HEADS, blk, LANES))
    block_decay = tuple(float(math.exp(g * blk)) for g in log_g)

    row = lambda b, j: (b * nb + j, 0)
    const3 = lambda b, j: (0, 0, 0)
    return pl.pallas_call(
        functools.partial(_retention_kernel, block_decay=block_decay),
        out_shape=jax.ShapeDtypeStruct((n, w), BF16),
        grid=(batch, nb),
        in_specs=[pl.BlockSpec((blk, w), row)] * 4 + [
            pl.BlockSpec(dmat.shape, const3), pl.BlockSpec(qdec.shape, const3),
            pl.BlockSpec(kdec.shape, const3)],
        out_specs=pl.BlockSpec((blk, w), row),
        scratch_shapes=[pltpu.VMEM((RET_HEADS, RET_HEAD_DIM, RET_HEAD_DIM), F32)],
        compiler_params=pltpu.CompilerParams(
            dimension_semantics=("arbitrary", "arbitrary"), vmem_limit_bytes=VMEM_LIMIT_BYTES),
        name="retention",
    )(rq, rk, rv, rg, dmat, qdec, kdec)


def _attention_kernel(q_ref, kn_ref, kr_ref, vt_ref, out_ref,
                      m_ref, l_ref, acc_ref, jump_ref, qt_ref):
    tq = q_ref.shape[0]
    gw = tq // 2
    tk = vt_ref.shape[-1]
    per_q = tq // tk
    i = pl.program_id(2)

    def keys(j, size=tq):
        start = pl.multiple_of(j * tq, tq)
        return jnp.concatenate(
            [kn_ref[pl.ds(start, size), :], kr_ref[pl.ds(start, size), :]], axis=1)

    def values_t(j, blocks=per_q):
        return jnp.concatenate([vt_ref[j * per_q + t] for t in range(blocks)], axis=1)

    def reset():
        m_ref[...] = jnp.full_like(m_ref, -1e30)
        l_ref[...] = jnp.zeros_like(l_ref)
        acc_ref[...] = jnp.zeros_like(acc_ref)

    def two_pass(g, k, vt, diagonal):
        cols = slice(g * gw, (g + 1) * gw)
        st = _dot_nt(k, q_ref[g * gw:(g + 1) * gw, :])
        if diagonal:
            kc = lax.broadcasted_iota(jnp.int32, st.shape, 0) // CHUNK
            qc = (lax.broadcasted_iota(jnp.int32, st.shape, 1) + g * gw) // CHUNK
            st = jnp.where(kc <= qc, st, -1e30)
        m_old = m_ref[:, cols]
        m_new = jnp.maximum(m_old, jnp.max(st, axis=0, keepdims=True))
        alpha = jnp.exp2(m_old - m_new)
        p = jnp.exp2(st - m_new)
        l_ref[:, cols] = alpha * l_ref[:, cols] + jnp.sum(p, axis=0, keepdims=True)
        acc_ref[:, cols] = alpha * acc_ref[:, cols] + _dot(vt, p.astype(BF16))
        m_ref[:, cols] = m_new

    def diagonal_block():
        two_pass(0, keys(i, gw), values_t(i, per_q // 2), True)
        two_pass(1, keys(i), values_t(i), True)

    def single_pass(j, items):
        k = keys(j)
        vt = values_t(j)
        groups = (0, 1)
        cols = [slice(g * gw, (g + 1) * gw) for g in groups]
        qs = [qt_ref[:, c] for c in cols]
        m_old = [m_ref[:, c] for c in cols]
        l_new = [l_ref[:, c] for c in cols]
        acc_new = [acc_ref[:, c] for c in cols]
        smax = [None, None]

        def qk(item):
            g, a, _ = item
            return _dot(k[a:a + ATTN_KEY_SLICE, :], qs[g])

        pending = [qk(item) for item in items[:ATTN_LOOKAHEAD]]
        for n, (g, a, masked) in enumerate(items):
            if n + ATTN_LOOKAHEAD < len(items):
                pending.append(qk(items[n + ATTN_LOOKAHEAD]))
            st = pending.pop(0)
            if masked:
                kc = (lax.broadcasted_iota(jnp.int32, st.shape, 0) + a) // CHUNK
                qc = (lax.broadcasted_iota(jnp.int32, st.shape, 1) + g * gw) // CHUNK
                st = jnp.where(kc <= qc, st, -1e30)
            p = jnp.exp2(st - m_old[g])
            cmax = jnp.max(st, axis=0, keepdims=True)
            smax[g] = cmax if smax[g] is None else jnp.maximum(smax[g], cmax)
            l_new[g] = l_new[g] + jnp.sum(p, axis=0, keepdims=True)
            acc_new[g] = acc_new[g] + _dot(vt[:, a:a + ATTN_KEY_SLICE], p.astype(BF16))
        for g in groups:
            m_new = jnp.maximum(m_old[g], smax[g])
            alpha = jnp.exp2(m_old[g] - m_new)
            l_ref[:, cols[g]] = alpha * l_new[g]
            acc_ref[:, cols[g]] = alpha * acc_new[g]
            m_ref[:, cols[g]] = m_new
            jump_ref[:, cols[g]] = jnp.maximum(jump_ref[:, cols[g]], smax[g] - m_old[g])

    slices = range(0, tq, ATTN_KEY_SLICE)
    full_items = [(g, a, False) for a in slices for g in (0, 1)]
    diag_items = [(g, a, a >= g * gw) for a in slices for g in (0, 1) if a < (g + 1) * gw]

    def fast_body(j, carry):
        single_pass(j, full_items)
        return carry

    def safe_body(j, carry):
        k = keys(j)
        vt = values_t(j)
        two_pass(0, k, vt, False)
        two_pass(1, k, vt, False)
        return carry

    qt_ref[...] = q_ref[...].astype(F32).T.astype(BF16)
    m_ref[...] = jnp.max(_dot_nt(keys(i, CHUNK), q_ref[...]), axis=0, keepdims=True)
    l_ref[...] = jnp.zeros_like(l_ref)
    acc_ref[...] = jnp.zeros_like(acc_ref)
    jump_ref[...] = jnp.zeros_like(jump_ref)
    lax.fori_loop(0, i, fast_body, 0)
    single_pass(i, diag_items)

    @pl.when(jnp.max(jump_ref[...]) > ATTN_MAX_JUMP)
    def _():
        reset()
        diagonal_block()
        lax.fori_loop(0, i, safe_body, 0)

    o = acc_ref[...] * (1.0 / l_ref[...])
    out_ref[...] = o.T.astype(BF16)


def _attention(q, kn, kr, vt, batch, seq):
    n = q.shape[0]
    tq = ATTN_Q_BLOCK
    tk = ATTN_KV_BLOCK
    nq = seq // tq
    return pl.pallas_call(
        _attention_kernel,
        out_shape=jax.ShapeDtypeStruct((n, MLA_HEADS * MLA_V_DIM), BF16),
        grid=(batch, MLA_HEADS, nq),
        in_specs=[pl.BlockSpec((tq, MLA_QK_PAD), lambda b, h, i: (b * nq + i, h)),
                  pl.BlockSpec((seq, MLA_NOPE_DIM), lambda b, h, i: (b, h)),
                  pl.BlockSpec((seq, LANES), lambda b, h, i: (b, 0)),
                  pl.BlockSpec((seq // tk, MLA_V_DIM, tk), lambda b, h, i: (b, h, 0))],
        out_specs=pl.BlockSpec((tq, MLA_V_DIM), lambda b, h, i: (b * nq + i, h)),
        scratch_shapes=[pltpu.VMEM((1, tq), F32), pltpu.VMEM((1, tq), F32),
                        pltpu.VMEM((MLA_V_DIM, tq), F32), pltpu.VMEM((1, tq), F32),
                        pltpu.VMEM((MLA_QK_PAD, tq), BF16)],
        compiler_params=pltpu.CompilerParams(
            dimension_semantics=("arbitrary", "arbitrary", "arbitrary"),
            vmem_limit_bytes=VMEM_LIMIT_BYTES),
        name="attention",
    )(q, kn, kr, vt)


def _out_ffn_kernel(x_ref, ret_ref, att_ref, p_ref, wo_ref, gffn_ref, wg_ref, wu_ref, wd_ref,
                    gple_ref, wpg_ref, wpp_ref, gfin_ref, out_ref, *, final_norm):
    a = jnp.concatenate([ret_ref[...], att_ref[...]], axis=1)
    x1 = x_ref[...] + _dot(a, wo_ref[...])
    h = _rms(x1, gffn_ref[...]).astype(BF16)
    g = _dot(h, wg_ref[...])
    u = _dot(h, wu_ref[...])
    act = (g * _sigmoid(g) * u).astype(BF16)
    x2 = x1 + _dot(act, wd_ref[...])
    gate = _sigmoid(_dot(_rms(x2, gple_ref[...]).astype(BF16), wpg_ref[...]))
    x3 = x2 + gate * _dot(p_ref[...].astype(BF16), wpp_ref[...])
    if final_norm:
        x3 = _rms(x3, gfin_ref[...])
    out_ref[...] = x3


def _out_ffn(x2, ret, att, p2, wo, gffn, wg, wu, wd, gple, wpg, wpp, gfin, final_norm):
    n, dm = x2.shape
    tm = OUT_ROWS
    row = lambda i: (i, 0)
    const = lambda i: (0, 0)
    resident = lambda a: pl.BlockSpec(a.shape, const, pipeline_mode=pl.Buffered(1))
    return pl.pallas_call(
        functools.partial(_out_ffn_kernel, final_norm=final_norm),
        out_shape=jax.ShapeDtypeStruct((n, dm), F32),
        grid=(n // tm,),
        in_specs=[pl.BlockSpec((tm, dm), row), pl.BlockSpec((tm, ret.shape[1]), row),
                  pl.BlockSpec((tm, att.shape[1]), row), pl.BlockSpec((tm, p2.shape[1]), row),
                  resident(wo), resident(gffn), resident(wg), resident(wu), resident(wd),
                  resident(gple), resident(wpg), resident(wpp), resident(gfin)],
        out_specs=pl.BlockSpec((tm, dm), row),
        compiler_params=pltpu.CompilerParams(
            dimension_semantics=("arbitrary",), vmem_limit_bytes=VMEM_LIMIT_BYTES),
        name="out_ffn",
    )(x2, ret, att, p2, wo, gffn, wg, wu, wd, gple, wpg, wpp, gfin)


def _rot_cols(w):
    half = w.shape[-1] // 2
    return jnp.concatenate([-w[..., half:], w[..., :half]], axis=-1)


def _prep_in_weights(w_in, w_uq, w_ukv):
    kpe = w_in[:, -MLA_ROPE_DIM:]
    win = jnp.concatenate([w_in, _rot_cols(kpe)], axis=1).astype(BF16)
    q_lora = w_uq.shape[0]
    uq = w_uq.reshape(q_lora, MLA_HEADS, MLA_QK_DIM)
    uq_rope = uq[:, :, MLA_NOPE_DIM:]
    wuq = jnp.concatenate([uq, _rot_cols(uq_rope)], axis=-1).reshape(q_lora, MLA_HEADS * MLA_QK_PAD)
    kv_lora = w_ukv.shape[0]
    ukv = w_ukv.reshape(kv_lora, MLA_HEADS, MLA_NOPE_DIM + MLA_V_DIM)
    wuk = ukv[:, :, :MLA_NOPE_DIM].reshape(kv_lora, -1)
    wuvt = ukv[:, :, MLA_NOPE_DIM:].reshape(kv_lora, -1).T
    return win, wuq.astype(BF16), wuk.astype(BF16), wuvt.astype(BF16)


def _rope_consts():
    def inv(half):
        return jnp.exp(-math.log(ROPE_BASE) * jnp.arange(half, dtype=F32) / half)
    inv_r = inv(RET_HEAD_DIM // 2)
    inv_m = inv(MLA_ROPE_DIM // 2)
    row = jnp.concatenate([inv_r, inv_m, jnp.zeros(LANES - inv_r.size - inv_m.size, F32)])
    return jnp.broadcast_to(row[None, :], (8, LANES))


def kernel(x, p, positions, mix_norm_g, w_in, q_norm_g, w_uq, kv_norm_g, w_ukv, w_o, ffn_norm_g,
           w_ffn_gate, w_ffn_up, w_ffn_down, ple_norm_g, w_ple_gate, w_ple_proj, final_norm_g):
    batch, seq, dm = x.shape
    depth = w_in.shape[0]
    n = batch * seq
    x2 = x.reshape(n, dm)
    pos2 = positions.reshape(n, 1)
    rc = _rope_consts()
    vec = lambda g: g.reshape(1, -1).astype(F32)
    for i in range(depth):
        win, wuq, wuk, wuvt = _prep_in_weights(w_in[i], w_uq[i], w_ukv[i])
        rq, rk, rv, rg, q, kn, kr, vt = _in_proj(
            x2, pos2, vec(mix_norm_g[i]), win, vec(q_norm_g[i]), wuq, vec(kv_norm_g[i]), wuk, wuvt,
            rc)
        ret = _retention(rq, rk, rv, rg, batch, seq)
        att = _attention(q, kn, kr, vt, batch, seq)
        x2 = _out_ffn(
            x2, ret, att, p[i].reshape(n, -1), w_o[i].astype(BF16), vec(ffn_norm_g[i]),
            w_ffn_gate[i].astype(BF16), w_ffn_up[i].astype(BF16), w_ffn_down[i].astype(BF16),
            vec(ple_norm_g[i]), w_ple_gate[i].astype(BF16), w_ple_proj[i].astype(BF16),
            vec(final_norm_g), final_norm=(i == depth - 1))
    return x2.reshape(batch, seq, dm)
```

```python
import functools
import math

import numpy as np
import jax
import jax.numpy as jnp
from jax import lax
from jax.experimental import pallas as pl
from jax.experimental.pallas import tpu as pltpu

F32 = jnp.float32
BF16 = jnp.bfloat16

CHUNK = 64
RET_HEADS = 4
RET_HEAD_DIM = 128
MLA_HEADS = 4
MLA_NOPE_DIM = 128
MLA_ROPE_DIM = 64
MLA_V_DIM = 128
MLA_QK_DIM = MLA_NOPE_DIM + MLA_ROPE_DIM
ROPE_BASE = 10000.0
RMS_EPS = 1e-6
GN_EPS = 1e-5

LANES = 128
MLA_QK_PAD = 2 * LANES
VMEM_LIMIT_BYTES = 56 * 1024 * 1024

IN_PROJ_ROWS = 512
RET_BLOCK = 256
ATTN_Q_BLOCK = 1024
ATTN_KV_BLOCK = 512
ATTN_MAX_JUMP = 32.0
ATTN_KEY_SLICE = 256
ATTN_LOOKAHEAD = 3
OUT_ROWS = 512


def _rms(x, g):
    return x * lax.rsqrt(jnp.mean(x * x, axis=-1, keepdims=True) + RMS_EPS) * g


def _sigmoid(x):
    return 1.0 / (1.0 + jnp.exp(-x))


def _dot(a, b):
    return jnp.dot(a, b, preferred_element_type=F32)


def _dot_nt(a, b):
    return lax.dot_general(a, b, (((1,), (1,)), ((), ())), preferred_element_type=F32)


def _dot_tn(a, b):
    return lax.dot_general(a, b, (((0,), (0,)), ((), ())), preferred_element_type=F32)


def _rot_half_lanes(x):
    return pltpu.roll(x, LANES // 2, 1)


def _in_proj_kernel(x_ref, pos_ref, gmix_ref, win_ref, qg_ref, wuq_ref, kvg_ref, wuk_ref, wuvt_ref,
                    rc_ref, rq_ref, rk_ref, rv_ref, rg_ref, q_ref, kn_ref, kr_ref, vt_ref, *, q_scale):
    h = _rms(x_ref[...], gmix_ref[...]).astype(BF16)
    z = _dot(h, win_ref[...])

    anchor = lax.bitcast_convert_type(z[:, 0:1], jnp.uint32)
    zero = lax.shift_right_logical(lax.shift_right_logical(anchor, jnp.uint32(16)), jnp.uint32(16))
    pos = (pos_ref[...] + zero.astype(jnp.int32)).astype(F32)
    ang = pos * rc_ref[0:1, :]
    c = jnp.cos(ang)
    s = jnp.sin(ang)
    lane = lax.broadcasted_iota(jnp.int32, (1, LANES), 1)
    half = LANES // 2
    quarter = LANES // 4
    c_hi, s_hi = _rot_half_lanes(c), _rot_half_lanes(s)
    cos_r = jnp.where(lane < half, c, c_hi)
    sin_r = jnp.where(lane < half, -s, s_hi)
    c_q3 = pltpu.roll(c, 3 * quarter, 1)
    s_q3 = pltpu.roll(s, 3 * quarter, 1)
    cos_m = jnp.where(lane < quarter, c_hi, jnp.where(lane < half, c_q3, 0.0))
    sin_m = jnp.where(lane < quarter, s_hi, jnp.where(lane < half, s_q3, 0.0))

    d = RET_HEAD_DIM
    w = RET_HEADS * d
    k_scale = d ** -0.5
    for hh in range(RET_HEADS):
        q = z[:, hh * d:(hh + 1) * d]
        rq_ref[:, hh * d:(hh + 1) * d] = (q * cos_r + _rot_half_lanes(q) * sin_r).astype(BF16)
        k = z[:, w + hh * d:w + (hh + 1) * d]
        rk_ref[:, hh * d:(hh + 1) * d] = ((k * cos_r + _rot_half_lanes(k) * sin_r) * k_scale).astype(BF16)
    rv_ref[...] = z[:, 2 * w:3 * w].astype(BF16)
    g = z[:, 3 * w:4 * w]
    rg_ref[...] = g * _sigmoid(g)

    o = 4 * w
    q_lora = qg_ref.shape[-1]
    kv_lora = kvg_ref.shape[-1]
    cq = z[:, o:o + q_lora]
    ckv = z[:, o + q_lora:o + q_lora + kv_lora]
    kpe = z[:, o + q_lora + kv_lora:o + q_lora + kv_lora + LANES]
    kr_ref[...] = (kpe * cos_m + _rot_half_lanes(kpe) * sin_m).astype(BF16)

    qq = _dot(_rms(cq, qg_ref[...]).astype(BF16), wuq_ref[...])
    for hh in range(MLA_HEADS):
        b = hh * MLA_QK_PAD
        q_ref[:, b:b + LANES] = (qq[:, b:b + LANES] * q_scale).astype(BF16)
        y = qq[:, b + LANES:b + 2 * LANES]
        q_ref[:, b + LANES:b + 2 * LANES] = (
            (y * cos_m + _rot_half_lanes(y) * sin_m) * q_scale).astype(BF16)

    ckvn = _rms(ckv, kvg_ref[...]).astype(BF16)
    kn_ref[...] = _dot(ckvn, wuk_ref[...]).astype(BF16)
    vt = _dot_nt(wuvt_ref[...], ckvn).astype(BF16)
    tk = vt_ref.shape[-1]
    for j in range(vt_ref.shape[0]):
        vt_ref[j] = vt[:, j * tk:(j + 1) * tk]


def _in_proj(x2, pos2, gmix, win, qg, wuq, kvg, wuk, wuvt, rc):
    n, dm = x2.shape
    tm = IN_PROJ_ROWS
    w = RET_HEADS * RET_HEAD_DIM
    const = lambda i: (0, 0)
    row = lambda i: (i, 0)
    full = lambda a: pl.BlockSpec(a.shape, const)
    out_shapes = (
        jax.ShapeDtypeStruct((n, w), BF16),
        jax.ShapeDtypeStruct((n, w), BF16),
        jax.ShapeDtypeStruct((n, w), BF16),
        jax.ShapeDtypeStruct((n, w), F32),
        jax.ShapeDtypeStruct((n, MLA_HEADS * MLA_QK_PAD), BF16),
        jax.ShapeDtypeStruct((n, MLA_HEADS * MLA_NOPE_DIM), BF16),
        jax.ShapeDtypeStruct((n, LANES), BF16),
    )
    tk = ATTN_KV_BLOCK
    vt_shape = jax.ShapeDtypeStruct((n // tk, MLA_HEADS * MLA_V_DIM, tk), BF16)
    out_specs = tuple(pl.BlockSpec((tm, s.shape[1]), row) for s in out_shapes) + (
        pl.BlockSpec((tm // tk, MLA_HEADS * MLA_V_DIM, tk), lambda i: (i, 0, 0)),)
    q_scale = MLA_QK_DIM ** -0.5 * math.log2(math.e)
    return pl.pallas_call(
        functools.partial(_in_proj_kernel, q_scale=q_scale),
        out_shape=out_shapes + (vt_shape,),
        grid=(n // tm,),
        in_specs=[pl.BlockSpec((tm, dm), row), pl.BlockSpec((tm, 1), row),
                  full(gmix), full(win), full(qg), full(wuq), full(kvg), full(wuk), full(wuvt),
                  full(rc)],
        out_specs=out_specs,
        compiler_params=pltpu.CompilerParams(
            dimension_semantics=("arbitrary",), vmem_limit_bytes=VMEM_LIMIT_BYTES),
        name="in_proj",
    )(x2, pos2, gmix, win, qg, wuq, kvg, wuk, wuvt, rc)


def _retention_kernel(rq_ref, rk_ref, rv_ref, rg_ref, dmat_ref, qdec_ref, kdec_ref, out_ref,
                      state_ref, *, block_decay):
    @pl.when(pl.program_id(0) == 0)
    def _():
        state_ref[...] = jnp.zeros_like(state_ref)

    d = RET_HEAD_DIM
    chains = [(b, hh) for b in range(rq_ref.shape[0]) for hh in range(RET_HEADS)]

    def front(c):
        b, hh = c
        sl = slice(hh * d, (hh + 1) * d)
        q = rq_ref[b, :, sl]
        v = rv_ref[b, :, sl]
        kt = rk_ref[b, :, sl].astype(F32).T
        st = state_ref[b, hh]
        scores = _dot(q, kt.astype(BF16))
        carried = _dot(q, st.astype(BF16))
        update = _dot((kt * kdec_ref[hh][0:1, :]).astype(BF16), v)
        return scores, carried, update, st, v

    fronts = {0: front(chains[0])}
    for n, (b, hh) in enumerate(chains):
        if n + 1 < len(chains):
            fronts[n + 1] = front(chains[n + 1])
        scores, carried, update, st, v = fronts.pop(n)
        sl = slice(hh * d, (hh + 1) * d)
        state_ref[b, hh] = block_decay[hh] * st + update
        o = _dot((scores * dmat_ref[hh]).astype(BF16), v) + carried * qdec_ref[hh]
        mu = jnp.mean(o, axis=-1, keepdims=True)
        oc = o - mu
        var = jnp.mean(oc * oc, axis=-1, keepdims=True)
        out_ref[b, :, sl] = (oc * lax.rsqrt(var + GN_EPS) * rg_ref[b, :, sl]).astype(BF16)


def _retention(rq, rk, rv, rg, batch, seq):
    n, w = rq.shape
    blk = RET_BLOCK
    nb = seq // blk
    log_g = np.log1p(-np.exp2(-5.0 - np.arange(RET_HEADS, dtype=np.float64)))
    idx = jnp.arange(blk, dtype=F32)
    lg = jnp.asarray(log_g, F32)
    dist = jnp.abs(idx[:, None] - idx[None, :])
    visible = (idx[None, :] // CHUNK) <= (idx[:, None] // CHUNK)
    dmat = jnp.where(visible[None], jnp.exp(lg[:, None, None] * dist[None]), 0.0)
    qdec = jnp.broadcast_to(jnp.exp(lg[:, None] * (idx + 1.0))[:, :, None], (RET_HEADS, blk, LANES))
    kdec = jnp.broadcast_to(jnp.exp(lg[:, None] * (blk - 1.0 - idx))[:, None, :], (RET_HEADS, 8, blk))
    block_decay = tuple(float(math.exp(g * blk)) for g in log_g)

    seq_block = pl.BlockSpec((batch, blk, w), lambda j: (0, j, 0))
    const3 = lambda j: (0, 0, 0)
    shaped = lambda a: a.reshape(batch, seq, w)
    out = pl.pallas_call(
        functools.partial(_retention_kernel, block_decay=block_decay),
        out_shape=jax.ShapeDtypeStruct((batch, seq, w), BF16),
        grid=(nb,),
        in_specs=[seq_block] * 4 + [
            pl.BlockSpec(dmat.shape, const3), pl.BlockSpec(qdec.shape, const3),
            pl.BlockSpec(kdec.shape, const3)],
        out_specs=seq_block,
        scratch_shapes=[pltpu.VMEM((batch, RET_HEADS, RET_HEAD_DIM, RET_HEAD_DIM), F32)],
        compiler_params=pltpu.CompilerParams(
            dimension_semantics=("arbitrary",), vmem_limit_bytes=VMEM_LIMIT_BYTES),
        name="retention",
    )(shaped(rq), shaped(rk), shaped(rv), shaped(rg), dmat, qdec, kdec)
    return out.reshape(n, w)


def _attention_kernel(q_ref, kn_ref, kr_ref, vt_ref, out_ref,
                      m_ref, l_ref, acc_ref, jump_ref, qt_ref):
    tq = q_ref.shape[0]
    gw = tq // 2
    tk = vt_ref.shape[-1]
    per_q = tq // tk
    i = pl.program_id(2)

    def keys(j, size=tq):
        start = pl.multiple_of(j * tq, tq)
        return jnp.concatenate(
            [kn_ref[pl.ds(start, size), :], kr_ref[pl.ds(start, size), :]], axis=1)

    def values_t(j, blocks=per_q):
        return jnp.concatenate([vt_ref[j * per_q + t] for t in range(blocks)], axis=1)

    def reset():
        m_ref[...] = jnp.full_like(m_ref, -1e30)
        l_ref[...] = jnp.zeros_like(l_ref)
        acc_ref[...] = jnp.zeros_like(acc_ref)

    def two_pass(g, k, vt, diagonal):
        cols = slice(g * gw, (g + 1) * gw)
        st = _dot_nt(k, q_ref[g * gw:(g + 1) * gw, :])
        if diagonal:
            kc = lax.broadcasted_iota(jnp.int32, st.shape, 0) // CHUNK
            qc = (lax.broadcasted_iota(jnp.int32, st.shape, 1) + g * gw) // CHUNK
            st = jnp.where(kc <= qc, st, -1e30)
        m_old = m_ref[:, cols]
        m_new = jnp.maximum(m_old, jnp.max(st, axis=0, keepdims=True))
        alpha = jnp.exp2(m_old - m_new)
        p = jnp.exp2(st - m_new)
        l_ref[:, cols] = alpha * l_ref[:, cols] + jnp.sum(p, axis=0, keepdims=True)
        acc_ref[:, cols] = alpha * acc_ref[:, cols] + _dot(vt, p.astype(BF16))
        m_ref[:, cols] = m_new

    def diagonal_block():
        two_pass(0, keys(i, gw), values_t(i, per_q // 2), True)
        two_pass(1, keys(i), values_t(i), True)

    def single_pass(j, items):
        k = keys(j)
        vt = values_t(j)
        groups = (0, 1)
        cols = [slice(g * gw, (g + 1) * gw) for g in groups]
        qs = [qt_ref[:, c] for c in cols]
        m_old = [m_ref[:, c] for c in cols]
        l_new = [l_ref[:, c] for c in cols]
        acc_new = [acc_ref[:, c] for c in cols]
        smax = [None, None]

        def qk(item):
            g, a, _ = item
            return _dot(k[a:a + ATTN_KEY_SLICE, :], qs[g])

        pending = [qk(item) for item in items[:ATTN_LOOKAHEAD]]
        for n, (g, a, masked) in enumerate(items):
            if n + ATTN_LOOKAHEAD < len(items):
                pending.append(qk(items[n + ATTN_LOOKAHEAD]))
            st = pending.pop(0)
            if masked:
                kc = (lax.broadcasted_iota(jnp.int32, st.shape, 0) + a) // CHUNK
                qc = (lax.broadcasted_iota(jnp.int32, st.shape, 1) + g * gw) // CHUNK
                st = jnp.where(kc <= qc, st, -1e30)
            p = jnp.exp2(st - m_old[g])
            cmax = jnp.max(st, axis=0, keepdims=True)
            smax[g] = cmax if smax[g] is None else jnp.maximum(smax[g], cmax)
            l_new[g] = l_new[g] + jnp.sum(p, axis=0, keepdims=True)
            acc_new[g] = acc_new[g] + _dot(vt[:, a:a + ATTN_KEY_SLICE], p.astype(BF16))
        for g in groups:
            m_new = jnp.maximum(m_old[g], smax[g])
            alpha = jnp.exp2(m_old[g] - m_new)
            l_ref[:, cols[g]] = alpha * l_new[g]
            acc_ref[:, cols[g]] = alpha * acc_new[g]
            m_ref[:, cols[g]] = m_new
            jump_ref[:, cols[g]] = jnp.maximum(jump_ref[:, cols[g]], smax[g] - m_old[g])

    slices = range(0, tq, ATTN_KEY_SLICE)
    full_items = [(g, a, False) for a in slices for g in (0, 1)]
    diag_items = [(g, a, a >= g * gw) for a in slices for g in (0, 1) if a < (g + 1) * gw]

    def fast_body(j, carry):
        single_pass(j, full_items)
        return carry

    def safe_body(j, carry):
        k = keys(j)
        vt = values_t(j)
        two_pass(0, k, vt, False)
        two_pass(1, k, vt, False)
        return carry

    qt_ref[...] = q_ref[...].astype(F32).T.astype(BF16)
    m_ref[...] = jnp.max(_dot_nt(keys(i, CHUNK), q_ref[...]), axis=0, keepdims=True)
    l_ref[...] = jnp.zeros_like(l_ref)
    acc_ref[...] = jnp.zeros_like(acc_ref)
    jump_ref[...] = jnp.zeros_like(jump_ref)
    lax.fori_loop(0, i, fast_body, 0)
    single_pass(i, diag_items)

    @pl.when(jnp.max(jump_ref[...]) > ATTN_MAX_JUMP)
    def _():
        reset()
        diagonal_block()
        lax.fori_loop(0, i, safe_body, 0)

    o = acc_ref[...] * (1.0 / l_ref[...])
    out_ref[...] = o.T.astype(BF16)


def _attention(q, kn, kr, vt, batch, seq):
    n = q.shape[0]
    tq = ATTN_Q_BLOCK
    tk = ATTN_KV_BLOCK
    nq = seq // tq
    return pl.pallas_call(
        _attention_kernel,
        out_shape=jax.ShapeDtypeStruct((n, MLA_HEADS * MLA_V_DIM), BF16),
        grid=(batch, MLA_HEADS, nq),
        in_specs=[pl.BlockSpec((tq, MLA_QK_PAD), lambda b, h, i: (b * nq + i, h)),
                  pl.BlockSpec((seq, MLA_NOPE_DIM), lambda b, h, i: (b, h)),
                  pl.BlockSpec((seq, LANES), lambda b, h, i: (b, 0)),
                  pl.BlockSpec((seq // tk, MLA_V_DIM, tk), lambda b, h, i: (b, h, 0))],
        out_specs=pl.BlockSpec((tq, MLA_V_DIM), lambda b, h, i: (b * nq + i, h)),
        scratch_shapes=[pltpu.VMEM((1, tq), F32), pltpu.VMEM((1, tq), F32),
                        pltpu.VMEM((MLA_V_DIM, tq), F32), pltpu.VMEM((1, tq), F32),
                        pltpu.VMEM((MLA_QK_PAD, tq), BF16)],
        compiler_params=pltpu.CompilerParams(
            dimension_semantics=("arbitrary", "arbitrary", "arbitrary"),
            vmem_limit_bytes=VMEM_LIMIT_BYTES),
        name="attention",
    )(q, kn, kr, vt)


def _out_ffn_kernel(x_ref, ret_ref, att_ref, p_ref, wo_ref, gffn_ref, wg_ref, wu_ref, wd_ref,
                    gple_ref, wpg_ref, wpp_ref, gfin_ref, out_ref, *, final_norm):
    a = jnp.concatenate([ret_ref[...], att_ref[...]], axis=1)
    x1 = x_ref[...] + _dot(a, wo_ref[...])
    h = _rms(x1, gffn_ref[...]).astype(BF16)
    g = _dot(h, wg_ref[...])
    u = _dot(h, wu_ref[...])
    act = (g * _sigmoid(g) * u).astype(BF16)
    x2 = x1 + _dot(act, wd_ref[...])
    gate = _sigmoid(_dot(_rms(x2, gple_ref[...]).astype(BF16), wpg_ref[...]))
    x3 = x2 + gate * _dot(p_ref[...].astype(BF16), wpp_ref[...])
    if final_norm:
        x3 = _rms(x3, gfin_ref[...])
    out_ref[...] = x3


def _out_ffn(x2, ret, att, p2, wo, gffn, wg, wu, wd, gple, wpg, wpp, gfin, final_norm):
    n, dm = x2.shape
    tm = OUT_ROWS
    row = lambda i: (i, 0)
    const = lambda i: (0, 0)
    resident = lambda a: pl.BlockSpec(a.shape, const, pipeline_mode=pl.Buffered(1))
    return pl.pallas_call(
        functools.partial(_out_ffn_kernel, final_norm=final_norm),
        out_shape=jax.ShapeDtypeStruct((n, dm), F32),
        grid=(n // tm,),
        in_specs=[pl.BlockSpec((tm, dm), row), pl.BlockSpec((tm, ret.shape[1]), row),
                  pl.BlockSpec((tm, att.shape[1]), row), pl.BlockSpec((tm, p2.shape[1]), row),
                  resident(wo), resident(gffn), resident(wg), resident(wu), resident(wd),
                  resident(gple), resident(wpg), resident(wpp), resident(gfin)],
        out_specs=pl.BlockSpec((tm, dm), row),
        compiler_params=pltpu.CompilerParams(
            dimension_semantics=("arbitrary",), vmem_limit_bytes=VMEM_LIMIT_BYTES),
        name="out_ffn",
    )(x2, ret, att, p2, wo, gffn, wg, wu, wd, gple, wpg, wpp, gfin)


def _rot_cols(w):
    half = w.shape[-1] // 2
    return jnp.concatenate([-w[..., half:], w[..., :half]], axis=-1)


def _prep_in_weights(w_in, w_uq, w_ukv):
    kpe = w_in[:, -MLA_ROPE_DIM:]
    win = jnp.concatenate([w_in, _rot_cols(kpe)], axis=1).astype(BF16)
    q_lora = w_uq.shape[0]
    uq = w_uq.reshape(q_lora, MLA_HEADS, MLA_QK_DIM)
    uq_rope = uq[:, :, MLA_NOPE_DIM:]
    wuq = jnp.concatenate([uq, _rot_cols(uq_rope)], axis=-1).reshape(q_lora, MLA_HEADS * MLA_QK_PAD)
    kv_lora = w_ukv.shape[0]
    ukv = w_ukv.reshape(kv_lora, MLA_HEADS, MLA_NOPE_DIM + MLA_V_DIM)
    wuk = ukv[:, :, :MLA_NOPE_DIM].reshape(kv_lora, -1)
    wuvt = ukv[:, :, MLA_NOPE_DIM:].reshape(kv_lora, -1).T
    return win, wuq.astype(BF16), wuk.astype(BF16), wuvt.astype(BF16)


def _rope_consts():
    def inv(half):
        return jnp.exp(-math.log(ROPE_BASE) * jnp.arange(half, dtype=F32) / half)
    inv_r = inv(RET_HEAD_DIM // 2)
    inv_m = inv(MLA_ROPE_DIM // 2)
    row = jnp.concatenate([inv_r, inv_m, jnp.zeros(LANES - inv_r.size - inv_m.size, F32)])
    return jnp.broadcast_to(row[None, :], (8, LANES))


def kernel(x, p, positions, mix_norm_g, w_in, q_norm_g, w_uq, kv_norm_g, w_ukv, w_o, ffn_norm_g,
           w_ffn_gate, w_ffn_up, w_ffn_down, ple_norm_g, w_ple_gate, w_ple_proj, final_norm_g):
    batch, seq, dm = x.shape
    depth = w_in.shape[0]
    n = batch * seq
    x2 = x.reshape(n, dm)
    pos2 = positions.reshape(n, 1)
    rc = _rope_consts()
    vec = lambda g: g.reshape(1, -1).astype(F32)
    for i in range(depth):
        win, wuq, wuk, wuvt = _prep_in_weights(w_in[i], w_uq[i], w_ukv[i])
        rq, rk, rv, rg, q, kn, kr, vt = _in_proj(
            x2, pos2, vec(mix_norm_g[i]), win, vec(q_norm_g[i]), wuq, vec(kv_norm_g[i]), wuk, wuvt,
            rc)
        ret = _retention(rq, rk, rv, rg, batch, seq)
        att = _attention(q, kn, kr, vt, batch, seq)
        x2 = _out_ffn(
            x2, ret, att, p[i].reshape(n, -1), w_o[i].astype(BF16), vec(ffn_norm_g[i]),
            w_ffn_gate[i].astype(BF16), w_ffn_up[i].astype(BF16), w_ffn_down[i].astype(BF16),
            vec(ple_norm_g[i]), w_ple_gate[i].astype(BF16), w_ple_proj[i].astype(BF16),
            vec(final_norm_g), final_norm=(i == depth - 1))
    return x2.reshape(batch, seq, dm)
```

```python
import functools
import math

import numpy as np
import jax
import jax.numpy as jnp
from jax import lax
from jax.experimental import pallas as pl
from jax.experimental.pallas import tpu as pltpu

F32 = jnp.float32
BF16 = jnp.bfloat16

CHUNK = 64
RET_HEADS = 4
RET_HEAD_DIM = 128
MLA_HEADS = 4
MLA_NOPE_DIM = 128
MLA_ROPE_DIM = 64
MLA_V_DIM = 128
MLA_QK_DIM = MLA_NOPE_DIM + MLA_ROPE_DIM
ROPE_BASE = 10000.0
RMS_EPS = 1e-6
GN_EPS = 1e-5

LANES = 128
MLA_QK_PAD = 2 * LANES
VMEM_LIMIT_BYTES = 56 * 1024 * 1024

IN_PROJ_ROWS = 1024
IN_PROJ_PARTS = 4
RET_BLOCK = 256
ATTN_Q_BLOCK = 1024
ATTN_KV_BLOCK = 512
ATTN_MAX_JUMP = 32.0
ATTN_KEY_SLICE = 256
ATTN_LOOKAHEAD = 3
OUT_ROWS = 512
OUT_PARTS = 2


def _rms(x, g):
    return x * lax.rsqrt(jnp.mean(x * x, axis=-1, keepdims=True) + RMS_EPS) * g


def _sigmoid(x):
    return 1.0 / (1.0 + jnp.exp(-x))


def _dot(a, b):
    return jnp.dot(a, b, preferred_element_type=F32)


def _dot_nt(a, b):
    return lax.dot_general(a, b, (((1,), (1,)), ((), ())), preferred_element_type=F32)


def _dot_tn(a, b):
    return lax.dot_general(a, b, (((0,), (0,)), ((), ())), preferred_element_type=F32)


def _rot_half_lanes(x):
    return pltpu.roll(x, LANES // 2, 1)


def _in_proj_kernel(x_ref, pos_ref, gmix_ref, win_ref, qg_ref, wuq_ref, kvg_ref, wuk_ref, wuvt_ref,
                    rc_ref, rq_ref, rk_ref, rv_ref, rg_ref, q_ref, kn_ref, kr_ref, vt_ref, *, q_scale):
    tm = x_ref.shape[0]
    part = tm // IN_PROJ_PARTS
    parts = [slice(i * part, (i + 1) * part) for i in range(IN_PROJ_PARTS)]
    zs = [_dot(_rms(x_ref[r, :], gmix_ref[...]).astype(BF16), win_ref[...]) for r in parts]

    lane = lax.broadcasted_iota(jnp.int32, (1, LANES), 1)
    half = LANES // 2
    quarter = LANES // 4
    d = RET_HEAD_DIM
    w = RET_HEADS * d
    k_scale = d ** -0.5
    o = 4 * w
    q_lora = qg_ref.shape[-1]
    kv_lora = kvg_ref.shape[-1]
    tk = vt_ref.shape[-1]
    vt_cols = min(tk, part)

    for r, z in zip(parts, zs):
        anchor = lax.bitcast_convert_type(z[:, 0:1], jnp.uint32)
        zero = lax.shift_right_logical(lax.shift_right_logical(anchor, jnp.uint32(16)), jnp.uint32(16))
        pos = (pos_ref[r, :] + zero.astype(jnp.int32)).astype(F32)
        ang = pos * rc_ref[0:1, :]
        c = jnp.cos(ang)
        s = jnp.sin(ang)
        c_hi, s_hi = _rot_half_lanes(c), _rot_half_lanes(s)
        cos_r = jnp.where(lane < half, c, c_hi)
        sin_r = jnp.where(lane < half, -s, s_hi)
        c_q3 = pltpu.roll(c, 3 * quarter, 1)
        s_q3 = pltpu.roll(s, 3 * quarter, 1)
        cos_m = jnp.where(lane < quarter, c_hi, jnp.where(lane < half, c_q3, 0.0))
        sin_m = jnp.where(lane < quarter, s_hi, jnp.where(lane < half, s_q3, 0.0))

        for hh in range(RET_HEADS):
            q = z[:, hh * d:(hh + 1) * d]
            rq_ref[r, hh * d:(hh + 1) * d] = (q * cos_r + _rot_half_lanes(q) * sin_r).astype(BF16)
            k = z[:, w + hh * d:w + (hh + 1) * d]
            rk_ref[r, hh * d:(hh + 1) * d] = (
                (k * cos_r + _rot_half_lanes(k) * sin_r) * k_scale).astype(BF16)
        rv_ref[r, :] = z[:, 2 * w:3 * w].astype(BF16)
        g = z[:, 3 * w:4 * w]
        rg_ref[r, :] = g * _sigmoid(g)

        cq = z[:, o:o + q_lora]
        ckv = z[:, o + q_lora:o + q_lora + kv_lora]
        kpe = z[:, o + q_lora + kv_lora:o + q_lora + kv_lora + LANES]
        kr_ref[r, :] = (kpe * cos_m + _rot_half_lanes(kpe) * sin_m).astype(BF16)

        qq = _dot(_rms(cq, qg_ref[...]).astype(BF16), wuq_ref[...])
        for hh in range(MLA_HEADS):
            b = hh * MLA_QK_PAD
            q_ref[r, b:b + LANES] = (qq[:, b:b + LANES] * q_scale).astype(BF16)
            y = qq[:, b + LANES:b + 2 * LANES]
            q_ref[r, b + LANES:b + 2 * LANES] = (
                (y * cos_m + _rot_half_lanes(y) * sin_m) * q_scale).astype(BF16)

        ckvn = _rms(ckv, kvg_ref[...]).astype(BF16)
        kn_ref[r, :] = _dot(ckvn, wuk_ref[...]).astype(BF16)
        vt = _dot_nt(wuvt_ref[...], ckvn).astype(BF16)
        for a in range(r.start, r.stop, vt_cols):
            vt_ref[a // tk, :, a % tk:a % tk + vt_cols] = vt[:, a - r.start:a - r.start + vt_cols]


def _in_proj(x2, pos2, gmix, win, qg, wuq, kvg, wuk, wuvt, rc):
    n, dm = x2.shape
    tm = IN_PROJ_ROWS
    w = RET_HEADS * RET_HEAD_DIM
    const = lambda i: (0, 0)
    row = lambda i: (i, 0)
    full = lambda a: pl.BlockSpec(a.shape, const)
    out_shapes = (
        jax.ShapeDtypeStruct((n, w), BF16),
        jax.ShapeDtypeStruct((n, w), BF16),
        jax.ShapeDtypeStruct((n, w), BF16),
        jax.ShapeDtypeStruct((n, w), F32),
        jax.ShapeDtypeStruct((n, MLA_HEADS * MLA_QK_PAD), BF16),
        jax.ShapeDtypeStruct((n, MLA_HEADS * MLA_NOPE_DIM), BF16),
        jax.ShapeDtypeStruct((n, LANES), BF16),
    )
    tk = ATTN_KV_BLOCK
    vt_shape = jax.ShapeDtypeStruct((n // tk, MLA_HEADS * MLA_V_DIM, tk), BF16)
    out_specs = tuple(pl.BlockSpec((tm, s.shape[1]), row) for s in out_shapes) + (
        pl.BlockSpec((tm // tk, MLA_HEADS * MLA_V_DIM, tk), lambda i: (i, 0, 0)),)
    q_scale = MLA_QK_DIM ** -0.5 * math.log2(math.e)
    return pl.pallas_call(
        functools.partial(_in_proj_kernel, q_scale=q_scale),
        out_shape=out_shapes + (vt_shape,),
        grid=(n // tm,),
        in_specs=[pl.BlockSpec((tm, dm), row), pl.BlockSpec((tm, 1), row),
                  full(gmix), full(win), full(qg), full(wuq), full(kvg), full(wuk), full(wuvt),
                  full(rc)],
        out_specs=out_specs,
        compiler_params=pltpu.CompilerParams(
            dimension_semantics=("arbitrary",), vmem_limit_bytes=VMEM_LIMIT_BYTES),
        name="in_proj",
    )(x2, pos2, gmix, win, qg, wuq, kvg, wuk, wuvt, rc)


def _retention_kernel(rq_ref, rk_ref, rv_ref, rg_ref, dmat_ref, qdec_ref, kdec_ref, out_ref,
                      state_ref, *, block_decay):
    @pl.when(pl.program_id(0) == 0)
    def _():
        state_ref[...] = jnp.zeros_like(state_ref)

    d = RET_HEAD_DIM
    chains = [(b, hh) for b in range(rq_ref.shape[0]) for hh in range(RET_HEADS)]

    def front(c):
        b, hh = c
        sl = slice(hh * d, (hh + 1) * d)
        q = rq_ref[b, :, sl]
        v = rv_ref[b, :, sl]
        kt = rk_ref[b, :, sl].astype(F32).T
        st = state_ref[b, hh]
        scores = _dot(q, kt.astype(BF16))
        carried = _dot(q, st.astype(BF16))
        update = _dot((kt * kdec_ref[hh][0:1, :]).astype(BF16), v)
        return scores, carried, update, st, v

    fronts = {0: front(chains[0])}
    for n, (b, hh) in enumerate(chains):
        if n + 1 < len(chains):
            fronts[n + 1] = front(chains[n + 1])
        scores, carried, update, st, v = fronts.pop(n)
        sl = slice(hh * d, (hh + 1) * d)
        state_ref[b, hh] = block_decay[hh] * st + update
        o = _dot((scores * dmat_ref[hh]).astype(BF16), v) + carried * qdec_ref[hh]
        mu = jnp.mean(o, axis=-1, keepdims=True)
        oc = o - mu
        var = jnp.mean(oc * oc, axis=-1, keepdims=True)
        out_ref[b, :, sl] = (oc * lax.rsqrt(var + GN_EPS) * rg_ref[b, :, sl]).astype(BF16)


def _retention(rq, rk, rv, rg, batch, seq):
    n, w = rq.shape
    blk = RET_BLOCK
    nb = seq // blk
    log_g = np.log1p(-np.exp2(-5.0 - np.arange(RET_HEADS, dtype=np.float64)))
    idx = jnp.arange(blk, dtype=F32)
    lg = jnp.asarray(log_g, F32)
    dist = jnp.abs(idx[:, None] - idx[None, :])
    visible = (idx[None, :] // CHUNK) <= (idx[:, None] // CHUNK)
    dmat = jnp.where(visible[None], jnp.exp(lg[:, None, None] * dist[None]), 0.0)
    qdec = jnp.broadcast_to(jnp.exp(lg[:, None] * (idx + 1.0))[:, :, None], (RET_HEADS, blk, LANES))
    kdec = jnp.broadcast_to(jnp.exp(lg[:, None] * (blk - 1.0 - idx))[:, None, :], (RET_HEADS, 8, blk))
    block_decay = tuple(float(math.exp(g * blk)) for g in log_g)

    seq_block = pl.BlockSpec((batch, blk, w), lambda j: (0, j, 0))
    const3 = lambda j: (0, 0, 0)
    shaped = lambda a: a.reshape(batch, seq, w)
    out = pl.pallas_call(
        functools.partial(_retention_kernel, block_decay=block_decay),
        out_shape=jax.ShapeDtypeStruct((batch, seq, w), BF16),
        grid=(nb,),
        in_specs=[seq_block] * 4 + [
            pl.BlockSpec(dmat.shape, const3), pl.BlockSpec(qdec.shape, const3),
            pl.BlockSpec(kdec.shape, const3)],
        out_specs=seq_block,
        scratch_shapes=[pltpu.VMEM((batch, RET_HEADS, RET_HEAD_DIM, RET_HEAD_DIM), F32)],
        compiler_params=pltpu.CompilerParams(
            dimension_semantics=("arbitrary",), vmem_limit_bytes=VMEM_LIMIT_BYTES),
        name="retention",
    )(shaped(rq), shaped(rk), shaped(rv), shaped(rg), dmat, qdec, kdec)
    return out.reshape(n, w)


def _attention_kernel(q_ref, kn_ref, kr_ref, vt_ref, out_ref,
                      m_ref, l_ref, acc_ref, jump_ref, qt_ref):
    tq = q_ref.shape[0]
    gw = tq // 2
    tk = vt_ref.shape[-1]
    per_q = tq // tk
    i = pl.program_id(2)

    def keys(j, size=tq):
        start = pl.multiple_of(j * tq, tq)
        return jnp.concatenate(
            [kn_ref[pl.ds(start, size), :], kr_ref[pl.ds(start, size), :]], axis=1)

    def values_t(j, blocks=per_q):
        return jnp.concatenate([vt_ref[j * per_q + t] for t in range(blocks)], axis=1)

    def reset():
        m_ref[...] = jnp.full_like(m_ref, -1e30)
        l_ref[...] = jnp.zeros_like(l_ref)
        acc_ref[...] = jnp.zeros_like(acc_ref)

    def two_pass(g, k, vt, diagonal):
        cols = slice(g * gw, (g + 1) * gw)
        st = _dot_nt(k, q_ref[g * gw:(g + 1) * gw, :])
        if diagonal:
            kc = lax.broadcasted_iota(jnp.int32, st.shape, 0) // CHUNK
            qc = (lax.broadcasted_iota(jnp.int32, st.shape, 1) + g * gw) // CHUNK
            st = jnp.where(kc <= qc, st, -1e30)
        m_old = m_ref[:, cols]
        m_new = jnp.maximum(m_old, jnp.max(st, axis=0, keepdims=True))
        alpha = jnp.exp2(m_old - m_new)
        p = jnp.exp2(st - m_new)
        l_ref[:, cols] = alpha * l_ref[:, cols] + jnp.sum(p, axis=0, keepdims=True)
        acc_ref[:, cols] = alpha * acc_ref[:, cols] + _dot(vt, p.astype(BF16))
        m_ref[:, cols] = m_new

    def diagonal_block():
        two_pass(0, keys(i, gw), values_t(i, per_q // 2), True)
        two_pass(1, keys(i), values_t(i), True)

    def single_pass(j, items):
        k = keys(j)
        vt = values_t(j)
        groups = (0, 1)
        cols = [slice(g * gw, (g + 1) * gw) for g in groups]
        qs = [qt_ref[:, c] for c in cols]
        m_old = [m_ref[:, c] for c in cols]
        l_new = [l_ref[:, c] for c in cols]
        acc_new = [acc_ref[:, c] for c in cols]
        smax = [None, None]

        def qk(item):
            g, a, _ = item
            return _dot(k[a:a + ATTN_KEY_SLICE, :], qs[g])

        pending = [qk(item) for item in items[:ATTN_LOOKAHEAD]]
        for n, (g, a, masked) in enumerate(items):
            if n + ATTN_LOOKAHEAD < len(items):
                pending.append(qk(items[n + ATTN_LOOKAHEAD]))
            st = pending.pop(0)
            if masked:
                kc = (lax.broadcasted_iota(jnp.int32, st.shape, 0) + a) // CHUNK
                qc = (lax.broadcasted_iota(jnp.int32, st.shape, 1) + g * gw) // CHUNK
                st = jnp.where(kc <= qc, st, -1e30)
            p = jnp.exp2(st - m_old[g])
            cmax = jnp.max(st, axis=0, keepdims=True)
            smax[g] = cmax if smax[g] is None else jnp.maximum(smax[g], cmax)
            l_new[g] = l_new[g] + jnp.sum(p, axis=0, keepdims=True)
            acc_new[g] = acc_new[g] + _dot(vt[:, a:a + ATTN_KEY_SLICE], p.astype(BF16))
        for g in groups:
            m_new = jnp.maximum(m_old[g], smax[g])
            alpha = jnp.exp2(m_old[g] - m_new)
            l_ref[:, cols[g]] = alpha * l_new[g]
            acc_ref[:, cols[g]] = alpha * acc_new[g]
            m_ref[:, cols[g]] = m_new
            jump_ref[:, cols[g]] = jnp.maximum(jump_ref[:, cols[g]], smax[g] - m_old[g])

    slices = range(0, tq, ATTN_KEY_SLICE)
    full_items = [(g, a, False) for a in slices for g in (0, 1)]
    diag_items = [(g, a, a >= g * gw) for a in slices for g in (0, 1) if a < (g + 1) * gw]

    def fast_body(j, carry):
        single_pass(j, full_items)
        return carry

    def safe_body(j, carry):
        k = keys(j)
        vt = values_t(j)
        two_pass(0, k, vt, False)
        two_pass(1, k, vt, False)
        return carry

    qt_ref[...] = q_ref[...].astype(F32).T.astype(BF16)
    m_ref[...] = jnp.max(_dot_nt(keys(i, CHUNK), q_ref[...]), axis=0, keepdims=True)
    l_ref[...] = jnp.zeros_like(l_ref)
    acc_ref[...] = jnp.zeros_like(acc_ref)
    jump_ref[...] = jnp.zeros_like(jump_ref)
    lax.fori_loop(0, i, fast_body, 0)
    single_pass(i, diag_items)

    @pl.when(jnp.max(jump_ref[...]) > ATTN_MAX_JUMP)
    def _():
        reset()
        diagonal_block()
        lax.fori_loop(0, i, safe_body, 0)

    o = acc_ref[...] * (1.0 / l_ref[...])
    out_ref[...] = o.T.astype(BF16)


def _attention(q, kn, kr, vt, batch, seq):
    n = q.shape[0]
    tq = ATTN_Q_BLOCK
    tk = ATTN_KV_BLOCK
    nq = seq // tq
    return pl.pallas_call(
        _attention_kernel,
        out_shape=jax.ShapeDtypeStruct((n, MLA_HEADS * MLA_V_DIM), BF16),
        grid=(batch, MLA_HEADS, nq),
        in_specs=[pl.BlockSpec((tq, MLA_QK_PAD), lambda b, h, i: (b * nq + i, h)),
                  pl.BlockSpec((seq, MLA_NOPE_DIM), lambda b, h, i: (b, h)),
                  pl.BlockSpec((seq, LANES), lambda b, h, i: (b, 0)),
                  pl.BlockSpec((seq // tk, MLA_V_DIM, tk), lambda b, h, i: (b, h, 0))],
        out_specs=pl.BlockSpec((tq, MLA_V_DIM), lambda b, h, i: (b * nq + i, h)),
        scratch_shapes=[pltpu.VMEM((1, tq), F32), pltpu.VMEM((1, tq), F32),
                        pltpu.VMEM((MLA_V_DIM, tq), F32), pltpu.VMEM((1, tq), F32),
                        pltpu.VMEM((MLA_QK_PAD, tq), BF16)],
        compiler_params=pltpu.CompilerParams(
            dimension_semantics=("arbitrary", "arbitrary", "arbitrary"),
            vmem_limit_bytes=VMEM_LIMIT_BYTES),
        name="attention",
    )(q, kn, kr, vt)


def _out_ffn_kernel(x_ref, ret_ref, att_ref, p_ref, wo_ref, gffn_ref, wg_ref, wu_ref, wd_ref,
                    gple_ref, wpg_ref, wpp_ref, gfin_ref, out_ref, *, final_norm):
    tm = x_ref.shape[0]
    part = tm // OUT_PARTS
    halves = tuple(slice(i * part, (i + 1) * part) for i in range(OUT_PARTS))
    both = range(len(halves))
    o = [_dot(jnp.concatenate([ret_ref[r, :], att_ref[r, :]], axis=1), wo_ref[...]) for r in halves]
    pe = [_dot(p_ref[r, :].astype(BF16), wpp_ref[...]) for r in halves]
    x1 = [x_ref[r, :] + o[i] for i, r in enumerate(halves)]
    h = [_rms(x1[i], gffn_ref[...]).astype(BF16) for i in both]
    gu = [(_dot(h[i], wg_ref[...]), _dot(h[i], wu_ref[...])) for i in both]
    act = [(g * _sigmoid(g) * u).astype(BF16) for g, u in gu]
    x2 = [x1[i] + _dot(act[i], wd_ref[...]) for i in both]
    h2 = [_rms(x2[i], gple_ref[...]).astype(BF16) for i in both]
    gate = [_sigmoid(_dot(h2[i], wpg_ref[...])) for i in both]
    for i, r in enumerate(halves):
        x3 = x2[i] + gate[i] * pe[i]
        if final_norm:
            x3 = _rms(x3, gfin_ref[...])
        out_ref[r, :] = x3


def _out_ffn(x2, ret, att, p2, wo, gffn, wg, wu, wd, gple, wpg, wpp, gfin, final_norm):
    n, dm = x2.shape
    tm = OUT_ROWS
    row = lambda i: (i, 0)
    const = lambda i: (0, 0)
    resident = lambda a: pl.BlockSpec(a.shape, const, pipeline_mode=pl.Buffered(1))
    return pl.pallas_call(
        functools.partial(_out_ffn_kernel, final_norm=final_norm),
        out_shape=jax.ShapeDtypeStruct((n, dm), F32),
        grid=(n // tm,),
        in_specs=[pl.BlockSpec((tm, dm), row), pl.BlockSpec((tm, ret.shape[1]), row),
                  pl.BlockSpec((tm, att.shape[1]), row), pl.BlockSpec((tm, p2.shape[1]), row),
                  resident(wo), resident(gffn), resident(wg), resident(wu), resident(wd),
                  resident(gple), resident(wpg), resident(wpp), resident(gfin)],
        out_specs=pl.BlockSpec((tm, dm), row),
        compiler_params=pltpu.CompilerParams(
            dimension_semantics=("arbitrary",), vmem_limit_bytes=VMEM_LIMIT_BYTES),
        name="out_ffn",
    )(x2, ret, att, p2, wo, gffn, wg, wu, wd, gple, wpg, wpp, gfin)


def _rot_cols(w):
    half = w.shape[-1] // 2
    return jnp.concatenate([-w[..., half:], w[..., :half]], axis=-1)


def _prep_in_weights(w_in, w_uq, w_ukv):
    kpe = w_in[:, -MLA_ROPE_DIM:]
    win = jnp.concatenate([w_in, _rot_cols(kpe)], axis=1).astype(BF16)
    q_lora = w_uq.shape[0]
    uq = w_uq.reshape(q_lora, MLA_HEADS, MLA_QK_DIM)
    uq_rope = uq[:, :, MLA_NOPE_DIM:]
    wuq = jnp.concatenate([uq, _rot_cols(uq_rope)], axis=-1).reshape(q_lora, MLA_HEADS * MLA_QK_PAD)
    kv_lora = w_ukv.shape[0]
    ukv = w_ukv.reshape(kv_lora, MLA_HEADS, MLA_NOPE_DIM + MLA_V_DIM)
    wuk = ukv[:, :, :MLA_NOPE_DIM].reshape(kv_lora, -1)
    wuvt = ukv[:, :, MLA_NOPE_DIM:].reshape(kv_lora, -1).T
    return win, wuq.astype(BF16), wuk.astype(BF16), wuvt.astype(BF16)


def _rope_consts():
    def inv(half):
        return jnp.exp(-math.log(ROPE_BASE) * jnp.arange(half, dtype=F32) / half)
    inv_r = inv(RET_HEAD_DIM // 2)
    inv_m = inv(MLA_ROPE_DIM // 2)
    row = jnp.concatenate([inv_r, inv_m, jnp.zeros(LANES - inv_r.size - inv_m.size, F32)])
    return jnp.broadcast_to(row[None, :], (8, LANES))


def kernel(x, p, positions, mix_norm_g, w_in, q_norm_g, w_uq, kv_norm_g, w_ukv, w_o, ffn_norm_g,
           w_ffn_gate, w_ffn_up, w_ffn_down, ple_norm_g, w_ple_gate, w_ple_proj, final_norm_g):
    batch, seq, dm = x.shape
    depth = w_in.shape[0]
    n = batch * seq
    x2 = x.reshape(n, dm)
    pos2 = positions.reshape(n, 1)
    rc = _rope_consts()
    vec = lambda g: g.reshape(1, -1).astype(F32)
    for i in range(depth):
        win, wuq, wuk, wuvt = _prep_in_weights(w_in[i], w_uq[i], w_ukv[i])
        rq, rk, rv, rg, q, kn, kr, vt = _in_proj(
            x2, pos2, vec(mix_norm_g[i]), win, vec(q_norm_g[i]), wuq, vec(kv_norm_g[i]), wuk, wuvt,
            rc)
        ret = _retention(rq, rk, rv, rg, batch, seq)
        att = _attention(q, kn, kr, vt, batch, seq)
        x2 = _out_ffn(
            x2, ret, att, p[i].reshape(n, -1), w_o[i].astype(BF16), vec(ffn_norm_g[i]),
            w_ffn_gate[i].astype(BF16), w_ffn_up[i].astype(BF16), w_ffn_down[i].astype(BF16),
            vec(ple_norm_g[i]), w_ple_gate[i].astype(BF16), w_ple_proj[i].astype(BF16),
            vec(final_norm_g), final_norm=(i == depth - 1))
    return x2.reshape(batch, seq, dm)
```

```python
import functools
import math

import numpy as np
import jax
import jax.numpy as jnp
from jax import lax
from jax.experimental import pallas as pl
from jax.experimental.pallas import tpu as pltpu

F32 = jnp.float32
BF16 = jnp.bfloat16

CHUNK = 64
RET_HEADS = 4
RET_HEAD_DIM = 128
MLA_HEADS = 4
MLA_NOPE_DIM = 128
MLA_ROPE_DIM = 64
MLA_V_DIM = 128
MLA_QK_DIM = MLA_NOPE_DIM + MLA_ROPE_DIM
ROPE_BASE = 10000.0
RMS_EPS = 1e-6
GN_EPS = 1e-5

LANES = 128
MLA_QK_PAD = 2 * LANES
VMEM_LIMIT_BYTES = 56 * 1024 * 1024

IN_PROJ_ROWS = 1024
IN_PROJ_PARTS = 4
RET_BLOCK = 256
ATTN_Q_BLOCK = 2048
ATTN_Q_GROUP = 512
ATTN_KEY_BLOCK = 1024
ATTN_KV_BLOCK = 512
ATTN_MAX_JUMP = 32.0
ATTN_KEY_SLICE = 256
ATTN_LOOKAHEAD = 3
OUT_ROWS = 512
OUT_PARTS = 2


def _rms(x, g):
    return x * lax.rsqrt(jnp.mean(x * x, axis=-1, keepdims=True) + RMS_EPS) * g


def _sigmoid(x):
    return 1.0 / (1.0 + jnp.exp(-x))


def _dot(a, b):
    return jnp.dot(a, b, preferred_element_type=F32)


def _dot_nt(a, b):
    return lax.dot_general(a, b, (((1,), (1,)), ((), ())), preferred_element_type=F32)


def _dot_tn(a, b):
    return lax.dot_general(a, b, (((0,), (0,)), ((), ())), preferred_element_type=F32)


def _rot_half_lanes(x):
    return pltpu.roll(x, LANES // 2, 1)


def _in_proj_kernel(x_ref, pos_ref, gmix_ref, win_ref, qg_ref, wuq_ref, kvg_ref, wuk_ref, wuvt_ref,
                    rc_ref, rq_ref, rk_ref, rv_ref, rg_ref, q_ref, kn_ref, kr_ref, vt_ref, *, q_scale):
    tm = x_ref.shape[0]
    part = tm // IN_PROJ_PARTS
    parts = [slice(i * part, (i + 1) * part) for i in range(IN_PROJ_PARTS)]
    zs = [_dot(_rms(x_ref[r, :], gmix_ref[...]).astype(BF16), win_ref[...]) for r in parts]

    lane = lax.broadcasted_iota(jnp.int32, (1, LANES), 1)
    half = LANES // 2
    quarter = LANES // 4
    d = RET_HEAD_DIM
    w = RET_HEADS * d
    k_scale = d ** -0.5
    o = 4 * w
    q_lora = qg_ref.shape[-1]
    kv_lora = kvg_ref.shape[-1]
    tk = vt_ref.shape[-1]
    vt_cols = min(tk, part)

    for r, z in zip(parts, zs):
        anchor = lax.bitcast_convert_type(z[:, 0:1], jnp.uint32)
        zero = lax.shift_right_logical(lax.shift_right_logical(anchor, jnp.uint32(16)), jnp.uint32(16))
        pos = (pos_ref[r, :] + zero.astype(jnp.int32)).astype(F32)
        ang = pos * rc_ref[0:1, :]
        c = jnp.cos(ang)
        s = jnp.sin(ang)
        c_hi, s_hi = _rot_half_lanes(c), _rot_half_lanes(s)
        cos_r = jnp.where(lane < half, c, c_hi)
        sin_r = jnp.where(lane < half, -s, s_hi)
        c_q3 = pltpu.roll(c, 3 * quarter, 1)
        s_q3 = pltpu.roll(s, 3 * quarter, 1)
        cos_m = jnp.where(lane < quarter, c_hi, jnp.where(lane < half, c_q3, 0.0))
        sin_m = jnp.where(lane < quarter, s_hi, jnp.where(lane < half, s_q3, 0.0))

        for hh in range(RET_HEADS):
            q = z[:, hh * d:(hh + 1) * d]
            rq_ref[r, hh * d:(hh + 1) * d] = (q * cos_r + _rot_half_lanes(q) * sin_r).astype(BF16)
            k = z[:, w + hh * d:w + (hh + 1) * d]
            rk_ref[r, hh * d:(hh + 1) * d] = (
                (k * cos_r + _rot_half_lanes(k) * sin_r) * k_scale).astype(BF16)
        rv_ref[r, :] = z[:, 2 * w:3 * w].astype(BF16)
        g = z[:, 3 * w:4 * w]
        rg_ref[r, :] = g * _sigmoid(g)

        cq = z[:, o:o + q_lora]
        ckv = z[:, o + q_lora:o + q_lora + kv_lora]
        kpe = z[:, o + q_lora + kv_lora:o + q_lora + kv_lora + LANES]
        kr_ref[r, :] = (kpe * cos_m + _rot_half_lanes(kpe) * sin_m).astype(BF16)

        qq = _dot(_rms(cq, qg_ref[...]).astype(BF16), wuq_ref[...])
        for hh in range(MLA_HEADS):
            b = hh * MLA_QK_PAD
            q_ref[r, b:b + LANES] = (qq[:, b:b + LANES] * q_scale).astype(BF16)
            y = qq[:, b + LANES:b + 2 * LANES]
            q_ref[r, b + LANES:b + 2 * LANES] = (
                (y * cos_m + _rot_half_lanes(y) * sin_m) * q_scale).astype(BF16)

        ckvn = _rms(ckv, kvg_ref[...]).astype(BF16)
        kn_ref[r, :] = _dot(ckvn, wuk_ref[...]).astype(BF16)
        vt = _dot_nt(wuvt_ref[...], ckvn).astype(BF16)
        for a in range(r.start, r.stop, vt_cols):
            vt_ref[a // tk, :, a % tk:a % tk + vt_cols] = vt[:, a - r.start:a - r.start + vt_cols]


def _in_proj(x2, pos2, gmix, win, qg, wuq, kvg, wuk, wuvt, rc):
    n, dm = x2.shape
    tm = IN_PROJ_ROWS
    w = RET_HEADS * RET_HEAD_DIM
    const = lambda i: (0, 0)
    row = lambda i: (i, 0)
    full = lambda a: pl.BlockSpec(a.shape, const)
    out_shapes = (
        jax.ShapeDtypeStruct((n, w), BF16),
        jax.ShapeDtypeStruct((n, w), BF16),
        jax.ShapeDtypeStruct((n, w), BF16),
        jax.ShapeDtypeStruct((n, w), F32),
        jax.ShapeDtypeStruct((n, MLA_HEADS * MLA_QK_PAD), BF16),
        jax.ShapeDtypeStruct((n, MLA_HEADS * MLA_NOPE_DIM), BF16),
        jax.ShapeDtypeStruct((n, LANES), BF16),
    )
    tk = ATTN_KV_BLOCK
    vt_shape = jax.ShapeDtypeStruct((n // tk, MLA_HEADS * MLA_V_DIM, tk), BF16)
    out_specs = tuple(pl.BlockSpec((tm, s.shape[1]), row) for s in out_shapes) + (
        pl.BlockSpec((tm // tk, MLA_HEADS * MLA_V_DIM, tk), lambda i: (i, 0, 0)),)
    q_scale = MLA_QK_DIM ** -0.5 * math.log2(math.e)
    return pl.pallas_call(
        functools.partial(_in_proj_kernel, q_scale=q_scale),
        out_shape=out_shapes + (vt_shape,),
        grid=(n // tm,),
        in_specs=[pl.BlockSpec((tm, dm), row), pl.BlockSpec((tm, 1), row),
                  full(gmix), full(win), full(qg), full(wuq), full(kvg), full(wuk), full(wuvt),
                  full(rc)],
        out_specs=out_specs,
        compiler_params=pltpu.CompilerParams(
            dimension_semantics=("arbitrary",), vmem_limit_bytes=VMEM_LIMIT_BYTES),
        name="in_proj",
    )(x2, pos2, gmix, win, qg, wuq, kvg, wuk, wuvt, rc)


def _retention_kernel(rq_ref, rk_ref, rv_ref, rg_ref, dmat_ref, qdec_ref, kdec_ref, out_ref,
                      state_ref, *, block_decay):
    @pl.when(pl.program_id(0) == 0)
    def _():
        state_ref[...] = jnp.zeros_like(state_ref)

    d = RET_HEAD_DIM
    chains = [(b, hh) for b in range(rq_ref.shape[0]) for hh in range(RET_HEADS)]

    def front(c):
        b, hh = c
        sl = slice(hh * d, (hh + 1) * d)
        q = rq_ref[b, :, sl]
        v = rv_ref[b, :, sl]
        kt = rk_ref[b, :, sl].astype(F32).T
        st = state_ref[b, hh]
        scores = _dot(q, kt.astype(BF16))
        carried = _dot(q, st.astype(BF16))
        update = _dot((kt * kdec_ref[hh][0:1, :]).astype(BF16), v)
        return scores, carried, update, st, v

    fronts = {0: front(chains[0])}
    for n, (b, hh) in enumerate(chains):
        if n + 1 < len(chains):
            fronts[n + 1] = front(chains[n + 1])
        scores, carried, update, st, v = fronts.pop(n)
        sl = slice(hh * d, (hh + 1) * d)
        state_ref[b, hh] = block_decay[hh] * st + update
        o = _dot((scores * dmat_ref[hh]).astype(BF16), v) + carried * qdec_ref[hh]
        mu = jnp.mean(o, axis=-1, keepdims=True)
        oc = o - mu
        var = jnp.mean(oc * oc, axis=-1, keepdims=True)
        out_ref[b, :, sl] = (oc * lax.rsqrt(var + GN_EPS) * rg_ref[b, :, sl]).astype(BF16)


def _retention(rq, rk, rv, rg, batch, seq):
    n, w = rq.shape
    blk = RET_BLOCK
    nb = seq // blk
    log_g = np.log1p(-np.exp2(-5.0 - np.arange(RET_HEADS, dtype=np.float64)))
    idx = jnp.arange(blk, dtype=F32)
    lg = jnp.asarray(log_g, F32)
    dist = jnp.abs(idx[:, None] - idx[None, :])
    visible = (idx[None, :] // CHUNK) <= (idx[:, None] // CHUNK)
    dmat = jnp.where(visible[None], jnp.exp(lg[:, None, None] * dist[None]), 0.0)
    qdec = jnp.broadcast_to(jnp.exp(lg[:, None] * (idx + 1.0))[:, :, None], (RET_HEADS, blk, LANES))
    kdec = jnp.broadcast_to(jnp.exp(lg[:, None] * (blk - 1.0 - idx))[:, None, :], (RET_HEADS, 8, blk))
    block_decay = tuple(float(math.exp(g * blk)) for g in log_g)

    seq_block = pl.BlockSpec((batch, blk, w), lambda j: (0, j, 0))
    const3 = lambda j: (0, 0, 0)
    shaped = lambda a: a.reshape(batch, seq, w)
    out = pl.pallas_call(
        functools.partial(_retention_kernel, block_decay=block_decay),
        out_shape=jax.ShapeDtypeStruct((batch, seq, w), BF16),
        grid=(nb,),
        in_specs=[seq_block] * 4 + [
            pl.BlockSpec(dmat.shape, const3), pl.BlockSpec(qdec.shape, const3),
            pl.BlockSpec(kdec.shape, const3)],
        out_specs=seq_block,
        scratch_shapes=[pltpu.VMEM((batch, RET_HEADS, RET_HEAD_DIM, RET_HEAD_DIM), F32)],
        compiler_params=pltpu.CompilerParams(
            dimension_semantics=("arbitrary",), vmem_limit_bytes=VMEM_LIMIT_BYTES),
        name="retention",
    )(shaped(rq), shaped(rk), shaped(rv), shaped(rg), dmat, qdec, kdec)
    return out.reshape(n, w)


def _attention_kernel(q_ref, kn_ref, kr_ref, vt_ref, out_ref,
                      m_ref, l_ref, acc_ref, jump_ref, qt_ref):
    tq = q_ref.shape[0]
    gw = ATTN_Q_GROUP
    kb = ATTN_KEY_BLOCK
    tk = vt_ref.shape[-1]
    groups = range(tq // gw)
    diag_blocks = tq // kb
    i = pl.program_id(2)
    first_diag = i * diag_blocks

    def keys(j, size=kb):
        start = pl.multiple_of(j * kb, kb)
        return jnp.concatenate(
            [kn_ref[pl.ds(start, size), :], kr_ref[pl.ds(start, size), :]], axis=1)

    def values_t(j, size=kb):
        return jnp.concatenate([vt_ref[j * (kb // tk) + t] for t in range(size // tk)], axis=1)

    def chunk_mask(shape, key0, g):
        kc = (lax.broadcasted_iota(jnp.int32, shape, 0) + key0) // CHUNK
        qc = (lax.broadcasted_iota(jnp.int32, shape, 1) + g * gw) // CHUNK
        return kc <= qc

    def two_pass(g, k, vt, key0=None):
        cols = slice(g * gw, (g + 1) * gw)
        st = _dot_nt(k, q_ref[g * gw:(g + 1) * gw, :])
        if key0 is not None:
            st = jnp.where(chunk_mask(st.shape, key0, g), st, -1e30)
        m_old = m_ref[:, cols]
        m_new = jnp.maximum(m_old, jnp.max(st, axis=0, keepdims=True))
        alpha = jnp.exp2(m_old - m_new)
        p = jnp.exp2(st - m_new)
        l_ref[:, cols] = alpha * l_ref[:, cols] + jnp.sum(p, axis=0, keepdims=True)
        acc_ref[:, cols] = alpha * acc_ref[:, cols] + _dot(vt, p.astype(BF16))
        m_ref[:, cols] = m_new

    def single_pass(j, items, key0):
        k = keys(j)
        vt = values_t(j)
        used = sorted({g for g, _, _ in items})
        cols = {g: slice(g * gw, (g + 1) * gw) for g in used}
        qs = {g: qt_ref[:, cols[g]] for g in used}
        m_old = {g: m_ref[:, cols[g]] for g in used}
        l_new = {g: l_ref[:, cols[g]] for g in used}
        acc_new = {g: acc_ref[:, cols[g]] for g in used}
        smax = {}

        def qk(item):
            g, a, _ = item
            return _dot(k[a:a + ATTN_KEY_SLICE, :], qs[g])

        pending = [qk(item) for item in items[:ATTN_LOOKAHEAD]]
        for n, (g, a, masked) in enumerate(items):
            if n + ATTN_LOOKAHEAD < len(items):
                pending.append(qk(items[n + ATTN_LOOKAHEAD]))
            st = pending.pop(0)
            if masked:
                st = jnp.where(chunk_mask(st.shape, key0 + a, g), st, -1e30)
            p = jnp.exp2(st - m_old[g])
            cmax = jnp.max(st, axis=0, keepdims=True)
            smax[g] = cmax if g not in smax else jnp.maximum(smax[g], cmax)
            l_new[g] = l_new[g] + jnp.sum(p, axis=0, keepdims=True)
            acc_new[g] = acc_new[g] + _dot(vt[:, a:a + ATTN_KEY_SLICE], p.astype(BF16))
        for g in used:
            m_new = jnp.maximum(m_old[g], smax[g])
            alpha = jnp.exp2(m_old[g] - m_new)
            l_ref[:, cols[g]] = alpha * l_new[g]
            acc_ref[:, cols[g]] = alpha * acc_new[g]
            m_ref[:, cols[g]] = m_new
            jump_ref[:, cols[g]] = jnp.maximum(jump_ref[:, cols[g]], smax[g] - m_old[g])

    slices = range(0, kb, ATTN_KEY_SLICE)
    full_items = [(g, a, False) for a in slices for g in groups]

    def diag_items(d):
        return [(g, a, d * kb + a >= g * gw) for a in slices for g in groups
                if d * kb + a < (g + 1) * gw]

    def fast_body(j, carry):
        single_pass(j, full_items, 0)
        return carry

    def safe_body(j, carry):
        k = keys(j)
        vt = values_t(j)
        for g in groups:
            two_pass(g, k, vt)
        return carry

    qt_ref[...] = q_ref[...].astype(F32).T.astype(BF16)
    m_ref[...] = jnp.max(_dot_nt(keys(first_diag, CHUNK), q_ref[...]), axis=0, keepdims=True)
    l_ref[...] = jnp.zeros_like(l_ref)
    acc_ref[...] = jnp.zeros_like(acc_ref)
    jump_ref[...] = jnp.zeros_like(jump_ref)
    lax.fori_loop(0, first_diag, fast_body, 0)
    for d in range(diag_blocks):
        single_pass(first_diag + d, diag_items(d), d * kb)

    @pl.when(jnp.max(jump_ref[...]) > ATTN_MAX_JUMP)
    def _():
        m_ref[...] = jnp.full_like(m_ref, -1e30)
        l_ref[...] = jnp.zeros_like(l_ref)
        acc_ref[...] = jnp.zeros_like(acc_ref)
        lax.fori_loop(0, first_diag, safe_body, 0)
        for d in range(diag_blocks):
            for g in groups:
                visible = min(kb, (g + 1) * gw - d * kb)
                if visible > 0:
                    two_pass(g, keys(first_diag + d, visible), values_t(first_diag + d, visible),
                             d * kb)

    o = acc_ref[...] * (1.0 / l_ref[...])
    out_ref[...] = o.T.astype(BF16)


def _attention(q, kn, kr, vt, batch, seq):
    n = q.shape[0]
    tq = ATTN_Q_BLOCK
    tk = ATTN_KV_BLOCK
    nq = seq // tq
    return pl.pallas_call(
        _attention_kernel,
        out_shape=jax.ShapeDtypeStruct((n, MLA_HEADS * MLA_V_DIM), BF16),
        grid=(batch, MLA_HEADS, nq),
        in_specs=[pl.BlockSpec((tq, MLA_QK_PAD), lambda b, h, i: (b * nq + i, h)),
                  pl.BlockSpec((seq, MLA_NOPE_DIM), lambda b, h, i: (b, h)),
                  pl.BlockSpec((seq, LANES), lambda b, h, i: (b, 0)),
                  pl.BlockSpec((seq // tk, MLA_V_DIM, tk), lambda b, h, i: (b, h, 0))],
        out_specs=pl.BlockSpec((tq, MLA_V_DIM), lambda b, h, i: (b * nq + i, h)),
        scratch_shapes=[pltpu.VMEM((1, tq), F32), pltpu.VMEM((1, tq), F32),
                        pltpu.VMEM((MLA_V_DIM, tq), F32), pltpu.VMEM((1, tq), F32),
                        pltpu.VMEM((MLA_QK_PAD, tq), BF16)],
        compiler_params=pltpu.CompilerParams(
            dimension_semantics=("arbitrary", "arbitrary", "arbitrary"),
            vmem_limit_bytes=VMEM_LIMIT_BYTES),
        name="attention",
    )(q, kn, kr, vt)


def _out_ffn_kernel(x_ref, ret_ref, att_ref, p_ref, wo_ref, gffn_ref, wg_ref, wu_ref, wd_ref,
                    gple_ref, wpg_ref, wpp_ref, gfin_ref, out_ref, *, final_norm):
    tm = x_ref.shape[0]
    part = tm // OUT_PARTS
    halves = tuple(slice(i * part, (i + 1) * part) for i in range(OUT_PARTS))
    both = range(len(halves))
    o = [_dot(jnp.concatenate([ret_ref[r, :], att_ref[r, :]], axis=1), wo_ref[...]) for r in halves]
    pe = [_dot(p_ref[r, :].astype(BF16), wpp_ref[...]) for r in halves]
    x1 = [x_ref[r, :] + o[i] for i, r in enumerate(halves)]
    h = [_rms(x1[i], gffn_ref[...]).astype(BF16) for i in both]
    gu = [(_dot(h[i], wg_ref[...]), _dot(h[i], wu_ref[...])) for i in both]
    act = [(g * _sigmoid(g) * u).astype(BF16) for g, u in gu]
    x2 = [x1[i] + _dot(act[i], wd_ref[...]) for i in both]
    h2 = [_rms(x2[i], gple_ref[...]).astype(BF16) for i in both]
    gate = [_sigmoid(_dot(h2[i], wpg_ref[...])) for i in both]
    for i, r in enumerate(halves):
        x3 = x2[i] + gate[i] * pe[i]
        if final_norm:
            x3 = _rms(x3, gfin_ref[...])
        out_ref[r, :] = x3


def _out_ffn(x2, ret, att, p2, wo, gffn, wg, wu, wd, gple, wpg, wpp, gfin, final_norm):
    n, dm = x2.shape
    tm = OUT_ROWS
    row = lambda i: (i, 0)
    const = lambda i: (0, 0)
    resident = lambda a: pl.BlockSpec(a.shape, const, pipeline_mode=pl.Buffered(1))
    return pl.pallas_call(
        functools.partial(_out_ffn_kernel, final_norm=final_norm),
        out_shape=jax.ShapeDtypeStruct((n, dm), F32),
        grid=(n // tm,),
        in_specs=[pl.BlockSpec((tm, dm), row), pl.BlockSpec((tm, ret.shape[1]), row),
                  pl.BlockSpec((tm, att.shape[1]), row), pl.BlockSpec((tm, p2.shape[1]), row),
                  resident(wo), resident(gffn), resident(wg), resident(wu), resident(wd),
                  resident(gple), resident(wpg), resident(wpp), resident(gfin)],
        out_specs=pl.BlockSpec((tm, dm), row),
        compiler_params=pltpu.CompilerParams(
            dimension_semantics=("arbitrary",), vmem_limit_bytes=VMEM_LIMIT_BYTES),
        name="out_ffn",
    )(x2, ret, att, p2, wo, gffn, wg, wu, wd, gple, wpg, wpp, gfin)


def _rot_cols(w):
    half = w.shape[-1] // 2
    return jnp.concatenate([-w[..., half:], w[..., :half]], axis=-1)


def _prep_in_weights(w_in, w_uq, w_ukv):
    kpe = w_in[:, -MLA_ROPE_DIM:]
    win = jnp.concatenate([w_in, _rot_cols(kpe)], axis=1).astype(BF16)
    q_lora = w_uq.shape[0]
    uq = w_uq.reshape(q_lora, MLA_HEADS, MLA_QK_DIM)
    uq_rope = uq[:, :, MLA_NOPE_DIM:]
    wuq = jnp.concatenate([uq, _rot_cols(uq_rope)], axis=-1).reshape(q_lora, MLA_HEADS * MLA_QK_PAD)
    kv_lora = w_ukv.shape[0]
    ukv = w_ukv.reshape(kv_lora, MLA_HEADS, MLA_NOPE_DIM + MLA_V_DIM)
    wuk = ukv[:, :, :MLA_NOPE_DIM].reshape(kv_lora, -1)
    wuvt = ukv[:, :, MLA_NOPE_DIM:].reshape(kv_lora, -1).T
    return win, wuq.astype(BF16), wuk.astype(BF16), wuvt.astype(BF16)


def _rope_consts():
    def inv(half):
        return jnp.exp(-math.log(ROPE_BASE) * jnp.arange(half, dtype=F32) / half)
    inv_r = inv(RET_HEAD_DIM // 2)
    inv_m = inv(MLA_ROPE_DIM // 2)
    row = jnp.concatenate([inv_r, inv_m, jnp.zeros(LANES - inv_r.size - inv_m.size, F32)])
    return jnp.broadcast_to(row[None, :], (8, LANES))


def kernel(x, p, positions, mix_norm_g, w_in, q_norm_g, w_uq, kv_norm_g, w_ukv, w_o, ffn_norm_g,
           w_ffn_gate, w_ffn_up, w_ffn_down, ple_norm_g, w_ple_gate, w_ple_proj, final_norm_g):
    batch, seq, dm = x.shape
    depth = w_in.shape[0]
    n = batch * seq
    x2 = x.reshape(n, dm)
    pos2 = positions.reshape(n, 1)
    rc = _rope_consts()
    vec = lambda g: g.reshape(1, -1).astype(F32)
    for i in range(depth):
        win, wuq, wuk, wuvt = _prep_in_weights(w_in[i], w_uq[i], w_ukv[i])
        rq, rk, rv, rg, q, kn, kr, vt = _in_proj(
            x2, pos2, vec(mix_norm_g[i]), win, vec(q_norm_g[i]), wuq, vec(kv_norm_g[i]), wuk, wuvt,
            rc)
        ret = _retention(rq, rk, rv, rg, batch, seq)
        att = _attention(q, kn, kr, vt, batch, seq)
        x2 = _out_ffn(
            x2, ret, att, p[i].reshape(n, -1), w_o[i].astype(BF16), vec(ffn_norm_g[i]),
            w_ffn_gate[i].astype(BF16), w_ffn_up[i].astype(BF16), w_ffn_down[i].astype(BF16),
            vec(ple_norm_g[i]), w_ple_gate[i].astype(BF16), w_ple_proj[i].astype(BF16),
            vec(final_norm_g), final_norm=(i == depth - 1))
    return x2.reshape(batch, seq, dm)
```

```python
import functools
import math

import numpy as np
import jax
import jax.numpy as jnp
from jax import lax
from jax.experimental import pallas as pl
from jax.experimental.pallas import tpu as pltpu

F32 = jnp.float32
BF16 = jnp.bfloat16

CHUNK = 64
RET_HEADS = 4
RET_HEAD_DIM = 128
MLA_HEADS = 4
MLA_NOPE_DIM = 128
MLA_ROPE_DIM = 64
MLA_V_DIM = 128
MLA_QK_DIM = MLA_NOPE_DIM + MLA_ROPE_DIM
ROPE_BASE = 10000.0
RMS_EPS = 1e-6
GN_EPS = 1e-5

LANES = 128
MLA_QK_PAD = 2 * LANES
VMEM_LIMIT_BYTES = 56 * 1024 * 1024

IN_PROJ_ROWS = 1024
IN_PROJ_PARTS = 4
RET_BLOCK = 256
ATTN_Q_BLOCK = 2048
ATTN_Q_GROUP = 512
ATTN_KEY_BLOCK = 1024
ATTN_KV_BLOCK = 512
ATTN_MAX_JUMP = 32.0
ATTN_KEY_SLICE = 256
ATTN_LOOKAHEAD = 3
OUT_ROWS = 512
OUT_PARTS = 2


def _rms(x, g):
    return x * lax.rsqrt(jnp.mean(x * x, axis=-1, keepdims=True) + RMS_EPS) * g


def _sigmoid(x):
    return 1.0 / (1.0 + jnp.exp(-x))


def _dot(a, b):
    return jnp.dot(a, b, preferred_element_type=F32)


def _dot_nt(a, b):
    return lax.dot_general(a, b, (((1,), (1,)), ((), ())), preferred_element_type=F32)


def _dot_tn(a, b):
    return lax.dot_general(a, b, (((0,), (0,)), ((), ())), preferred_element_type=F32)


def _rot_half_lanes(x):
    return pltpu.roll(x, LANES // 2, 1)


def _in_proj_kernel(x_ref, pos_ref, gmix_ref, win_ref, qg_ref, wuq_ref, kvg_ref, wuk_ref, wuvt_ref,
                    rc_ref, rq_ref, rk_ref, rv_ref, rg_ref, q_ref, kn_ref, kr_ref, vt_ref, *, q_scale):
    tm = x_ref.shape[0]
    part = tm // IN_PROJ_PARTS
    parts = [slice(i * part, (i + 1) * part) for i in range(IN_PROJ_PARTS)]
    zs = [_dot(_rms(x_ref[r, :], gmix_ref[...]).astype(BF16), win_ref[...]) for r in parts]

    lane = lax.broadcasted_iota(jnp.int32, (1, LANES), 1)
    half = LANES // 2
    quarter = LANES // 4
    d = RET_HEAD_DIM
    w = RET_HEADS * d
    k_scale = d ** -0.5
    o = 4 * w
    q_lora = qg_ref.shape[-1]
    kv_lora = kvg_ref.shape[-1]
    tk = vt_ref.shape[-1]
    vt_cols = min(tk, part)

    for r, z in zip(parts, zs):
        anchor = lax.bitcast_convert_type(z[:, 0:1], jnp.uint32)
        zero = lax.shift_right_logical(lax.shift_right_logical(anchor, jnp.uint32(16)), jnp.uint32(16))
        pos = (pos_ref[r, :] + zero.astype(jnp.int32)).astype(F32)
        ang = pos * rc_ref[0:1, :]
        c = jnp.cos(ang)
        s = jnp.sin(ang)
        c_hi, s_hi = _rot_half_lanes(c), _rot_half_lanes(s)
        cos_r = jnp.where(lane < half, c, c_hi)
        sin_r = jnp.where(lane < half, -s, s_hi)
        c_q3 = pltpu.roll(c, 3 * quarter, 1)
        s_q3 = pltpu.roll(s, 3 * quarter, 1)
        cos_m = jnp.where(lane < quarter, c_hi, jnp.where(lane < half, c_q3, 0.0))
        sin_m = jnp.where(lane < quarter, s_hi, jnp.where(lane < half, s_q3, 0.0))

        for hh in range(RET_HEADS):
            q = z[:, hh * d:(hh + 1) * d]
            rq_ref[r, hh * d:(hh + 1) * d] = (q * cos_r + _rot_half_lanes(q) * sin_r).astype(BF16)
            k = z[:, w + hh * d:w + (hh + 1) * d]
            rk_ref[r, hh * d:(hh + 1) * d] = (
                (k * cos_r + _rot_half_lanes(k) * sin_r) * k_scale).astype(BF16)
        rv_ref[r, :] = z[:, 2 * w:3 * w].astype(BF16)
        g = z[:, 3 * w:4 * w]
        rg_ref[r, :] = g * _sigmoid(g)

        cq = z[:, o:o + q_lora]
        ckv = z[:, o + q_lora:o + q_lora + kv_lora]
        def rope_low_half(y):
            rot = jnp.where(lane < quarter, -pltpu.roll(y, 3 * quarter, 1), pltpu.roll(y, quarter, 1))
            return y * cos_m + rot * sin_m

        kpe = z[:, o + q_lora + kv_lora:o + q_lora + kv_lora + LANES]
        kr_ref[r, :] = rope_low_half(kpe).astype(BF16)

        qq = _dot(_rms(cq, qg_ref[...]).astype(BF16), wuq_ref[...])
        for hh in range(MLA_HEADS):
            b = hh * MLA_QK_PAD
            q_ref[r, b:b + LANES] = (qq[:, b:b + LANES] * q_scale).astype(BF16)
            y = qq[:, b + LANES:b + 2 * LANES]
            q_ref[r, b + LANES:b + 2 * LANES] = (
                (y * cos_m + _rot_half_lanes(y) * sin_m) * q_scale).astype(BF16)

        ckvn = _rms(ckv, kvg_ref[...]).astype(BF16)
        kn_ref[r, :] = _dot(ckvn, wuk_ref[...]).astype(BF16)
        vt = _dot_nt(wuvt_ref[...], ckvn).astype(BF16)
        for a in range(r.start, r.stop, vt_cols):
            vt_ref[a // tk, :, a % tk:a % tk + vt_cols] = vt[:, a - r.start:a - r.start + vt_cols]


def _in_proj(x2, pos2, gmix, win, qg, wuq, kvg, wuk, wuvt, rc):
    n, dm = x2.shape
    tm = IN_PROJ_ROWS
    w = RET_HEADS * RET_HEAD_DIM
    const = lambda i: (0, 0)
    row = lambda i: (i, 0)
    full = lambda a: pl.BlockSpec(a.shape, const)
    out_shapes = (
        jax.ShapeDtypeStruct((n, w), BF16),
        jax.ShapeDtypeStruct((n, w), BF16),
        jax.ShapeDtypeStruct((n, w), BF16),
        jax.ShapeDtypeStruct((n, w), F32),
        jax.ShapeDtypeStruct((n, MLA_HEADS * MLA_QK_PAD), BF16),
        jax.ShapeDtypeStruct((n, MLA_HEADS * MLA_NOPE_DIM), BF16),
        jax.ShapeDtypeStruct((n, LANES), BF16),
    )
    tk = ATTN_KV_BLOCK
    vt_shape = jax.ShapeDtypeStruct((n // tk, MLA_HEADS * MLA_V_DIM, tk), BF16)
    out_specs = tuple(pl.BlockSpec((tm, s.shape[1]), row) for s in out_shapes) + (
        pl.BlockSpec((tm // tk, MLA_HEADS * MLA_V_DIM, tk), lambda i: (i, 0, 0)),)
    q_scale = MLA_QK_DIM ** -0.5 * math.log2(math.e)
    return pl.pallas_call(
        functools.partial(_in_proj_kernel, q_scale=q_scale),
        out_shape=out_shapes + (vt_shape,),
        grid=(n // tm,),
        in_specs=[pl.BlockSpec((tm, dm), row), pl.BlockSpec((tm, 1), row),
                  full(gmix), full(win), full(qg), full(wuq), full(kvg), full(wuk), full(wuvt),
                  full(rc)],
        out_specs=out_specs,
        compiler_params=pltpu.CompilerParams(
            dimension_semantics=("arbitrary",), vmem_limit_bytes=VMEM_LIMIT_BYTES),
        name="in_proj",
    )(x2, pos2, gmix, win, qg, wuq, kvg, wuk, wuvt, rc)


def _retention_kernel(rq_ref, rk_ref, rv_ref, rg_ref, dmat_ref, qdec_ref, kdec_ref, out_ref,
                      state_ref, *, block_decay):
    @pl.when(pl.program_id(0) == 0)
    def _():
        state_ref[...] = jnp.zeros_like(state_ref)

    d = RET_HEAD_DIM
    chains = [(b, hh) for b in range(rq_ref.shape[0]) for hh in range(RET_HEADS)]

    def front(c):
        b, hh = c
        sl = slice(hh * d, (hh + 1) * d)
        q = rq_ref[b, :, sl]
        v = rv_ref[b, :, sl]
        kt = rk_ref[b, :, sl].astype(F32).T
        st = state_ref[b, hh]
        scores = _dot(q, kt.astype(BF16))
        carried = _dot(q, st.astype(BF16))
        update = _dot((kt * kdec_ref[hh][0:1, :]).astype(BF16), v)
        return scores, carried, update, st, v

    fronts = {0: front(chains[0])}
    for n, (b, hh) in enumerate(chains):
        if n + 1 < len(chains):
            fronts[n + 1] = front(chains[n + 1])
        scores, carried, update, st, v = fronts.pop(n)
        sl = slice(hh * d, (hh + 1) * d)
        state_ref[b, hh] = block_decay[hh] * st + update
        o = _dot((scores * dmat_ref[hh]).astype(BF16), v) + carried * qdec_ref[hh]
        mu = jnp.mean(o, axis=-1, keepdims=True)
        oc = o - mu
        var = jnp.mean(oc * oc, axis=-1, keepdims=True)
        out_ref[b, :, sl] = (oc * lax.rsqrt(var + GN_EPS) * rg_ref[b, :, sl]).astype(BF16)


def _retention(rq, rk, rv, rg, batch, seq):
    n, w = rq.shape
    blk = RET_BLOCK
    nb = seq // blk
    log_g = np.log1p(-np.exp2(-5.0 - np.arange(RET_HEADS, dtype=np.float64)))
    idx = jnp.arange(blk, dtype=F32)
    lg = jnp.asarray(log_g, F32)
    dist = jnp.abs(idx[:, None] - idx[None, :])
    visible = (idx[None, :] // CHUNK) <= (idx[:, None] // CHUNK)
    dmat = jnp.where(visible[None], jnp.exp(lg[:, None, None] * dist[None]), 0.0)
    qdec = jnp.broadcast_to(jnp.exp(lg[:, None] * (idx + 1.0))[:, :, None], (RET_HEADS, blk, LANES))
    kdec = jnp.broadcast_to(jnp.exp(lg[:, None] * (blk - 1.0 - idx))[:, None, :], (RET_HEADS, 8, blk))
    block_decay = tuple(float(math.exp(g * blk)) for g in log_g)

    seq_block = pl.BlockSpec((batch, blk, w), lambda j: (0, j, 0))
    const3 = lambda j: (0, 0, 0)
    shaped = lambda a: a.reshape(batch, seq, w)
    out = pl.pallas_call(
        functools.partial(_retention_kernel, block_decay=block_decay),
        out_shape=jax.ShapeDtypeStruct((batch, seq, w), BF16),
        grid=(nb,),
        in_specs=[seq_block] * 4 + [
            pl.BlockSpec(dmat.shape, const3), pl.BlockSpec(qdec.shape, const3),
            pl.BlockSpec(kdec.shape, const3)],
        out_specs=seq_block,
        scratch_shapes=[pltpu.VMEM((batch, RET_HEADS, RET_HEAD_DIM, RET_HEAD_DIM), F32)],
        compiler_params=pltpu.CompilerParams(
            dimension_semantics=("arbitrary",), vmem_limit_bytes=VMEM_LIMIT_BYTES),
        name="retention",
    )(shaped(rq), shaped(rk), shaped(rv), shaped(rg), dmat, qdec, kdec)
    return out.reshape(n, w)


def _attention_kernel(q_ref, kn_ref, kr_ref, vt_ref, out_ref,
                      m_ref, l_ref, acc_ref, jump_ref, qt_ref):
    tq = q_ref.shape[0]
    gw = ATTN_Q_GROUP
    kb = ATTN_KEY_BLOCK
    tk = vt_ref.shape[-1]
    groups = range(tq // gw)
    diag_blocks = tq // kb
    i = pl.program_id(2)
    first_diag = i * diag_blocks

    def keys(j, size=kb):
        start = pl.multiple_of(j * kb, kb)
        return jnp.concatenate(
            [kn_ref[pl.ds(start, size), :], kr_ref[pl.ds(start, size), :]], axis=1)

    def values_t(j, size=kb):
        return jnp.concatenate([vt_ref[j * (kb // tk) + t] for t in range(size // tk)], axis=1)

    def chunk_mask(shape, key0, g):
        kc = (lax.broadcasted_iota(jnp.int32, shape, 0) + key0) // CHUNK
        qc = (lax.broadcasted_iota(jnp.int32, shape, 1) + g * gw) // CHUNK
        return kc <= qc

    def two_pass(g, k, vt, key0=None):
        cols = slice(g * gw, (g + 1) * gw)
        st = _dot_nt(k, q_ref[g * gw:(g + 1) * gw, :])
        if key0 is not None:
            st = jnp.where(chunk_mask(st.shape, key0, g), st, -1e30)
        m_old = m_ref[:, cols]
        m_new = jnp.maximum(m_old, jnp.max(st, axis=0, keepdims=True))
        alpha = jnp.exp2(m_old - m_new)
        p = jnp.exp2(st - m_new)
        l_ref[:, cols] = alpha * l_ref[:, cols] + jnp.sum(p, axis=0, keepdims=True)
        acc_ref[:, cols] = alpha * acc_ref[:, cols] + _dot(vt, p.astype(BF16))
        m_ref[:, cols] = m_new

    def single_pass(j, items, key0):
        k = keys(j)
        vt = values_t(j)
        used = sorted({g for g, _, _ in items})
        cols = {g: slice(g * gw, (g + 1) * gw) for g in used}
        qs = {g: qt_ref[:, cols[g]] for g in used}
        m_old = {g: m_ref[:, cols[g]] for g in used}
        l_new = {g: l_ref[:, cols[g]] for g in used}
        acc_new = {g: acc_ref[:, cols[g]] for g in used}
        smax = {}

        def qk(item):
            g, a, _ = item
            return _dot(k[a:a + ATTN_KEY_SLICE, :], qs[g])

        pending = [qk(item) for item in items[:ATTN_LOOKAHEAD]]
        for n, (g, a, masked) in enumerate(items):
            if n + ATTN_LOOKAHEAD < len(items):
                pending.append(qk(items[n + ATTN_LOOKAHEAD]))
            st = pending.pop(0)
            if masked:
                st = jnp.where(chunk_mask(st.shape, key0 + a, g), st, -1e30)
            p = jnp.exp2(st - m_old[g])
            cmax = jnp.max(st, axis=0, keepdims=True)
            smax[g] = cmax if g not in smax else jnp.maximum(smax[g], cmax)
            l_new[g] = l_new[g] + jnp.sum(p, axis=0, keepdims=True)
            acc_new[g] = acc_new[g] + _dot(vt[:, a:a + ATTN_KEY_SLICE], p.astype(BF16))
        for g in used:
            m_new = jnp.maximum(m_old[g], smax[g])
            alpha = jnp.exp2(m_old[g] - m_new)
            l_ref[:, cols[g]] = alpha * l_new[g]
            acc_ref[:, cols[g]] = alpha * acc_new[g]
            m_ref[:, cols[g]] = m_new
            jump_ref[:, cols[g]] = jnp.maximum(jump_ref[:, cols[g]], smax[g] - m_old[g])

    slices = range(0, kb, ATTN_KEY_SLICE)
    full_items = [(g, a, False) for a in slices for g in groups]

    def diag_items(d):
        return [(g, a, d * kb + a >= g * gw) for a in slices for g in groups
                if d * kb + a < (g + 1) * gw]

    def fast_body(j, carry):
        single_pass(j, full_items, 0)
        return carry

    def safe_body(j, carry):
        k = keys(j)
        vt = values_t(j)
        for g in groups:
            two_pass(g, k, vt)
        return carry

    qt_ref[...] = q_ref[...].astype(F32).T.astype(BF16)
    m_ref[...] = jnp.max(_dot_nt(keys(first_diag, CHUNK), q_ref[...]), axis=0, keepdims=True)
    l_ref[...] = jnp.zeros_like(l_ref)
    acc_ref[...] = jnp.zeros_like(acc_ref)
    jump_ref[...] = jnp.zeros_like(jump_ref)
    lax.fori_loop(0, first_diag, fast_body, 0)
    for d in range(diag_blocks):
        single_pass(first_diag + d, diag_items(d), d * kb)

    @pl.when(jnp.max(jump_ref[...]) > ATTN_MAX_JUMP)
    def _():
        m_ref[...] = jnp.full_like(m_ref, -1e30)
        l_ref[...] = jnp.zeros_like(l_ref)
        acc_ref[...] = jnp.zeros_like(acc_ref)
        lax.fori_loop(0, first_diag, safe_body, 0)
        for d in range(diag_blocks):
            for g in groups:
                visible = min(kb, (g + 1) * gw - d * kb)
                if visible > 0:
                    two_pass(g, keys(first_diag + d, visible), values_t(first_diag + d, visible),
                             d * kb)

    o = acc_ref[...] * (1.0 / l_ref[...])
    out_ref[...] = o.T.astype(BF16)


def _attention(q, kn, kr, vt, batch, seq):
    n = q.shape[0]
    tq = ATTN_Q_BLOCK
    tk = ATTN_KV_BLOCK
    nq = seq // tq
    return pl.pallas_call(
        _attention_kernel,
        out_shape=jax.ShapeDtypeStruct((n, MLA_HEADS * MLA_V_DIM), BF16),
        grid=(batch, MLA_HEADS, nq),
        in_specs=[pl.BlockSpec((tq, MLA_QK_PAD), lambda b, h, i: (b * nq + i, h)),
                  pl.BlockSpec((seq, MLA_NOPE_DIM), lambda b, h, i: (b, h)),
                  pl.BlockSpec((seq, LANES), lambda b, h, i: (b, 0)),
                  pl.BlockSpec((seq // tk, MLA_V_DIM, tk), lambda b, h, i: (b, h, 0))],
        out_specs=pl.BlockSpec((tq, MLA_V_DIM), lambda b, h, i: (b * nq + i, h)),
        scratch_shapes=[pltpu.VMEM((1, tq), F32), pltpu.VMEM((1, tq), F32),
                        pltpu.VMEM((MLA_V_DIM, tq), F32), pltpu.VMEM((1, tq), F32),
                        pltpu.VMEM((MLA_QK_PAD, tq), BF16)],
        compiler_params=pltpu.CompilerParams(
            dimension_semantics=("arbitrary", "arbitrary", "arbitrary"),
            vmem_limit_bytes=VMEM_LIMIT_BYTES),
        name="attention",
    )(q, kn, kr, vt)


def _out_ffn_kernel(x_ref, ret_ref, att_ref, p_ref, wo_ref, gffn_ref, wg_ref, wu_ref, wd_ref,
                    gple_ref, wpg_ref, wpp_ref, gfin_ref, out_ref, *, final_norm):
    tm = x_ref.shape[0]
    part = tm // OUT_PARTS
    halves = tuple(slice(i * part, (i + 1) * part) for i in range(OUT_PARTS))
    both = range(len(halves))
    o = [_dot(jnp.concatenate([ret_ref[r, :], att_ref[r, :]], axis=1), wo_ref[...]) for r in halves]
    pe = [_dot(p_ref[r, :].astype(BF16), wpp_ref[...]) for r in halves]
    x1 = [x_ref[r, :] + o[i] for i, r in enumerate(halves)]
    h = [_rms(x1[i], gffn_ref[...]).astype(BF16) for i in both]
    gu = [(_dot(h[i], wg_ref[...]), _dot(h[i], wu_ref[...])) for i in both]
    act = [(g * _sigmoid(g) * u).astype(BF16) for g, u in gu]
    x2 = [x1[i] + _dot(act[i], wd_ref[...]) for i in both]
    h2 = [_rms(x2[i], gple_ref[...]).astype(BF16) for i in both]
    gate = [_sigmoid(_dot(h2[i], wpg_ref[...])) for i in both]
    for i, r in enumerate(halves):
        x3 = x2[i] + gate[i] * pe[i]
        if final_norm:
            x3 = _rms(x3, gfin_ref[...])
        out_ref[r, :] = x3


def _out_ffn(x2, ret, att, p2, wo, gffn, wg, wu, wd, gple, wpg, wpp, gfin, final_norm):
    n, dm = x2.shape
    tm = OUT_ROWS
    row = lambda i: (i, 0)
    const = lambda i: (0, 0)
    resident = lambda a: pl.BlockSpec(a.shape, const, pipeline_mode=pl.Buffered(1))
    return pl.pallas_call(
        functools.partial(_out_ffn_kernel, final_norm=final_norm),
        out_shape=jax.ShapeDtypeStruct((n, dm), F32),
        grid=(n // tm,),
        in_specs=[pl.BlockSpec((tm, dm), row), pl.BlockSpec((tm, ret.shape[1]), row),
                  pl.BlockSpec((tm, att.shape[1]), row), pl.BlockSpec((tm, p2.shape[1]), row),
                  resident(wo), resident(gffn), resident(wg), resident(wu), resident(wd),
                  resident(gple), resident(wpg), resident(wpp), resident(gfin)],
        out_specs=pl.BlockSpec((tm, dm), row),
        compiler_params=pltpu.CompilerParams(
            dimension_semantics=("arbitrary",), vmem_limit_bytes=VMEM_LIMIT_BYTES),
        name="out_ffn",
    )(x2, ret, att, p2, wo, gffn, wg, wu, wd, gple, wpg, wpp, gfin)


def _prep_in_weights(w_in, w_uq, w_ukv):
    win = jnp.pad(w_in.astype(BF16), ((0, 0), (0, LANES - MLA_ROPE_DIM)))
    q_lora = w_uq.shape[0]
    uq = w_uq.reshape(q_lora, MLA_HEADS, MLA_QK_DIM)
    rope = uq[:, :, MLA_NOPE_DIM:]
    rot = jnp.concatenate([-rope[..., MLA_ROPE_DIM // 2:], rope[..., :MLA_ROPE_DIM // 2]], axis=-1)
    wuq = jnp.concatenate([uq, rot], axis=-1).reshape(q_lora, MLA_HEADS * MLA_QK_PAD)
    kv_lora = w_ukv.shape[0]
    ukv = w_ukv.reshape(kv_lora, MLA_HEADS, MLA_NOPE_DIM + MLA_V_DIM)
    wuk = ukv[:, :, :MLA_NOPE_DIM].reshape(kv_lora, -1)
    wuvt = ukv[:, :, MLA_NOPE_DIM:].reshape(kv_lora, -1).T
    return win, wuq.astype(BF16), wuk.astype(BF16), wuvt.astype(BF16)


def _rope_consts():
    def inv(half):
        return jnp.exp(-math.log(ROPE_BASE) * jnp.arange(half, dtype=F32) / half)
    inv_r = inv(RET_HEAD_DIM // 2)
    inv_m = inv(MLA_ROPE_DIM // 2)
    row = jnp.concatenate([inv_r, inv_m, jnp.zeros(LANES - inv_r.size - inv_m.size, F32)])
    return jnp.broadcast_to(row[None, :], (8, LANES))


def kernel(x, p, positions, mix_norm_g, w_in, q_norm_g, w_uq, kv_norm_g, w_ukv, w_o, ffn_norm_g,
           w_ffn_gate, w_ffn_up, w_ffn_down, ple_norm_g, w_ple_gate, w_ple_proj, final_norm_g):
    batch, seq, dm = x.shape
    depth = w_in.shape[0]
    n = batch * seq
    x2 = x.reshape(n, dm)
    pos2 = positions.reshape(n, 1)
    rc = _rope_consts()
    vec = lambda g: g.reshape(1, -1).astype(F32)
    for i in range(depth):
        win, wuq, wuk, wuvt = _prep_in_weights(w_in[i], w_uq[i], w_ukv[i])
        rq, rk, rv, rg, q, kn, kr, vt = _in_proj(
            x2, pos2, vec(mix_norm_g[i]), win, vec(q_norm_g[i]), wuq, vec(kv_norm_g[i]), wuk, wuvt,
            rc)
        ret = _retention(rq, rk, rv, rg, batch, seq)
        att = _attention(q, kn, kr, vt, batch, seq)
        x2 = _out_ffn(
            x2, ret, att, p[i].reshape(n, -1), w_o[i].astype(BF16), vec(ffn_norm_g[i]),
            w_ffn_gate[i].astype(BF16), w_ffn_up[i].astype(BF16), w_ffn_down[i].astype(BF16),
            vec(ple_norm_g[i]), w_ple_gate[i].astype(BF16), w_ple_proj[i].astype(BF16),
            vec(final_norm_g), final_norm=(i == depth - 1))
    return x2.reshape(batch, seq, dm)
```

```python
import functools
import math

import numpy as np
import jax
import jax.numpy as jnp
from jax import lax
from jax.experimental import pallas as pl
from jax.experimental.pallas import tpu as pltpu

F32 = jnp.float32
BF16 = jnp.bfloat16

CHUNK = 64
RET_HEADS = 4
RET_HEAD_DIM = 128
MLA_HEADS = 4
MLA_NOPE_DIM = 128
MLA_ROPE_DIM = 64
MLA_V_DIM = 128
MLA_QK_DIM = MLA_NOPE_DIM + MLA_ROPE_DIM
ROPE_BASE = 10000.0
RMS_EPS = 1e-6
GN_EPS = 1e-5

LANES = 128
MLA_QK_PAD = 2 * LANES
VMEM_LIMIT_BYTES = 56 * 1024 * 1024

IN_PROJ_ROWS = 1024
IN_PROJ_PARTS = 4
RET_BLOCK = 256
RET_STEP_BLOCKS = 4
ATTN_Q_BLOCK = 2048
ATTN_Q_GROUP = 512
ATTN_KEY_BLOCK = 1024
ATTN_KV_BLOCK = 512
ATTN_MAX_JUMP = 32.0
ATTN_KEY_SLICE = 256
ATTN_LOOKAHEAD = 3
OUT_ROWS = 512
OUT_PARTS = 2


def _rms(x, g):
    return x * lax.rsqrt(jnp.mean(x * x, axis=-1, keepdims=True) + RMS_EPS) * g


def _sigmoid(x):
    return 1.0 / (1.0 + jnp.exp(-x))


def _dot(a, b):
    return jnp.dot(a, b, preferred_element_type=F32)


def _dot_nt(a, b):
    return lax.dot_general(a, b, (((1,), (1,)), ((), ())), preferred_element_type=F32)


def _dot_tn(a, b):
    return lax.dot_general(a, b, (((0,), (0,)), ((), ())), preferred_element_type=F32)


def _rot_half_lanes(x):
    return pltpu.roll(x, LANES // 2, 1)


def _in_proj_kernel(x_ref, pos_ref, gmix_ref, win_ref, qg_ref, wuq_ref, kvg_ref, wuk_ref, wuvt_ref,
                    rc_ref, rq_ref, rk_ref, rv_ref, rg_ref, q_ref, kn_ref, kr_ref, vt_ref, *, q_scale):
    tm = x_ref.shape[0]
    part = tm // IN_PROJ_PARTS
    parts = [slice(i * part, (i + 1) * part) for i in range(IN_PROJ_PARTS)]
    zs = [_dot(_rms(x_ref[r, :], gmix_ref[...]).astype(BF16), win_ref[...]) for r in parts]

    lane = lax.broadcasted_iota(jnp.int32, (1, LANES), 1)
    half = LANES // 2
    quarter = LANES // 4
    d = RET_HEAD_DIM
    w = RET_HEADS * d
    k_scale = d ** -0.5
    o = 4 * w
    q_lora = qg_ref.shape[-1]
    kv_lora = kvg_ref.shape[-1]
    tk = vt_ref.shape[-1]
    vt_cols = min(tk, part)

    for r, z in zip(parts, zs):
        anchor = lax.bitcast_convert_type(z[:, 0:1], jnp.uint32)
        zero = lax.shift_right_logical(lax.shift_right_logical(anchor, jnp.uint32(16)), jnp.uint32(16))
        pos = (pos_ref[r, :] + zero.astype(jnp.int32)).astype(F32)
        ang = pos * rc_ref[0:1, :]
        c = jnp.cos(ang)
        s = jnp.sin(ang)
        c_hi, s_hi = _rot_half_lanes(c), _rot_half_lanes(s)
        cos_r = jnp.where(lane < half, c, c_hi)
        sin_r = jnp.where(lane < half, -s, s_hi)
        c_q3 = pltpu.roll(c, 3 * quarter, 1)
        s_q3 = pltpu.roll(s, 3 * quarter, 1)
        cos_m = jnp.where(lane < quarter, c_hi, jnp.where(lane < half, c_q3, 0.0))
        sin_m = jnp.where(lane < quarter, s_hi, jnp.where(lane < half, s_q3, 0.0))

        for hh in range(RET_HEADS):
            q = z[:, hh * d:(hh + 1) * d]
            rq_ref[r, hh * d:(hh + 1) * d] = (q * cos_r + _rot_half_lanes(q) * sin_r).astype(BF16)
            k = z[:, w + hh * d:w + (hh + 1) * d]
            rk_ref[r, hh * d:(hh + 1) * d] = (
                (k * cos_r + _rot_half_lanes(k) * sin_r) * k_scale).astype(BF16)
        rv_ref[r, :] = z[:, 2 * w:3 * w].astype(BF16)
        g = z[:, 3 * w:4 * w]
        rg_ref[r, :] = g * _sigmoid(g)

        cq = z[:, o:o + q_lora]
        ckv = z[:, o + q_lora:o + q_lora + kv_lora]
        def rope_low_half(y):
            rot = jnp.where(lane < quarter, -pltpu.roll(y, 3 * quarter, 1), pltpu.roll(y, quarter, 1))
            return y * cos_m + rot * sin_m

        kpe = z[:, o + q_lora + kv_lora:o + q_lora + kv_lora + LANES]
        kr_ref[r, :] = rope_low_half(kpe).astype(BF16)

        qq = _dot(_rms(cq, qg_ref[...]).astype(BF16), wuq_ref[...])
        for hh in range(MLA_HEADS):
            b = hh * MLA_QK_PAD
            q_ref[r, b:b + LANES] = (qq[:, b:b + LANES] * q_scale).astype(BF16)
            y = qq[:, b + LANES:b + 2 * LANES]
            q_ref[r, b + LANES:b + 2 * LANES] = (
                (y * cos_m + _rot_half_lanes(y) * sin_m) * q_scale).astype(BF16)

        ckvn = _rms(ckv, kvg_ref[...]).astype(BF16)
        kn_ref[r, :] = _dot(ckvn, wuk_ref[...]).astype(BF16)
        vt = _dot_nt(wuvt_ref[...], ckvn).astype(BF16)
        for a in range(r.start, r.stop, vt_cols):
            vt_ref[a // tk, :, a % tk:a % tk + vt_cols] = vt[:, a - r.start:a - r.start + vt_cols]


def _in_proj(x2, pos2, gmix, win, qg, wuq, kvg, wuk, wuvt, rc):
    n, dm = x2.shape
    tm = IN_PROJ_ROWS
    w = RET_HEADS * RET_HEAD_DIM
    const = lambda i: (0, 0)
    row = lambda i: (i, 0)
    full = lambda a: pl.BlockSpec(a.shape, const)
    out_shapes = (
        jax.ShapeDtypeStruct((n, w), BF16),
        jax.ShapeDtypeStruct((n, w), BF16),
        jax.ShapeDtypeStruct((n, w), BF16),
        jax.ShapeDtypeStruct((n, w), F32),
        jax.ShapeDtypeStruct((n, MLA_HEADS * MLA_QK_PAD), BF16),
        jax.ShapeDtypeStruct((n, MLA_HEADS * MLA_NOPE_DIM), BF16),
        jax.ShapeDtypeStruct((n, LANES), BF16),
    )
    tk = ATTN_KV_BLOCK
    vt_shape = jax.ShapeDtypeStruct((n // tk, MLA_HEADS * MLA_V_DIM, tk), BF16)
    out_specs = tuple(pl.BlockSpec((tm, s.shape[1]), row) for s in out_shapes) + (
        pl.BlockSpec((tm // tk, MLA_HEADS * MLA_V_DIM, tk), lambda i: (i, 0, 0)),)
    q_scale = MLA_QK_DIM ** -0.5 * math.log2(math.e)
    return pl.pallas_call(
        functools.partial(_in_proj_kernel, q_scale=q_scale),
        out_shape=out_shapes + (vt_shape,),
        grid=(n // tm,),
        in_specs=[pl.BlockSpec((tm, dm), row), pl.BlockSpec((tm, 1), row),
                  full(gmix), full(win), full(qg), full(wuq), full(kvg), full(wuk), full(wuvt),
                  full(rc)],
        out_specs=out_specs,
        compiler_params=pltpu.CompilerParams(
            dimension_semantics=("arbitrary",), vmem_limit_bytes=VMEM_LIMIT_BYTES),
        name="in_proj",
    )(x2, pos2, gmix, win, qg, wuq, kvg, wuk, wuvt, rc)


def _retention_kernel(rq_ref, rk_ref, rv_ref, rg_ref, dmat_ref, qdec_ref, kdec_ref, out_ref,
                      state_ref, *, block_decay):
    @pl.when(pl.program_id(0) == 0)
    def _():
        state_ref[...] = jnp.zeros_like(state_ref)

    d = RET_HEAD_DIM
    blk = dmat_ref.shape[-1]
    nblk = rq_ref.shape[1] // blk
    seqs = range(rq_ref.shape[0])
    chains = [(t, b, hh) for t in range(nblk) for b in seqs for hh in range(RET_HEADS)]
    state = {(b, hh): state_ref[b, hh] for b in seqs for hh in range(RET_HEADS)}

    def front(c):
        t, b, hh = c
        rows = slice(t * blk, (t + 1) * blk)
        sl = slice(hh * d, (hh + 1) * d)
        q = rq_ref[b, rows, sl]
        v = rv_ref[b, rows, sl]
        kt = rk_ref[b, rows, sl].astype(F32).T
        st = state[b, hh]
        scores = _dot(q, kt.astype(BF16))
        carried = _dot(q, st.astype(BF16))
        update = _dot((kt * kdec_ref[hh][0:1, :]).astype(BF16), v)
        state[b, hh] = block_decay[hh] * st + update
        return scores, carried, v

    fronts = {0: front(chains[0])}
    for n, (t, b, hh) in enumerate(chains):
        if n + 1 < len(chains):
            fronts[n + 1] = front(chains[n + 1])
        scores, carried, v = fronts.pop(n)
        rows = slice(t * blk, (t + 1) * blk)
        sl = slice(hh * d, (hh + 1) * d)
        o = _dot((scores * dmat_ref[hh]).astype(BF16), v) + carried * qdec_ref[hh]
        mu = jnp.mean(o, axis=-1, keepdims=True)
        oc = o - mu
        var = jnp.mean(oc * oc, axis=-1, keepdims=True)
        out_ref[b, rows, sl] = (oc * lax.rsqrt(var + GN_EPS) * rg_ref[b, rows, sl]).astype(BF16)
    for (b, hh), st in state.items():
        state_ref[b, hh] = st


def _retention(rq, rk, rv, rg, batch, seq):
    n, w = rq.shape
    blk = RET_BLOCK
    nb = seq // blk
    log_g = np.log1p(-np.exp2(-5.0 - np.arange(RET_HEADS, dtype=np.float64)))
    idx = jnp.arange(blk, dtype=F32)
    lg = jnp.asarray(log_g, F32)
    dist = jnp.abs(idx[:, None] - idx[None, :])
    visible = (idx[None, :] // CHUNK) <= (idx[:, None] // CHUNK)
    dmat = jnp.where(visible[None], jnp.exp(lg[:, None, None] * dist[None]), 0.0)
    qdec = jnp.broadcast_to(jnp.exp(lg[:, None] * (idx + 1.0))[:, :, None], (RET_HEADS, blk, LANES))
    kdec = jnp.broadcast_to(jnp.exp(lg[:, None] * (blk - 1.0 - idx))[:, None, :], (RET_HEADS, 8, blk))
    block_decay = tuple(float(math.exp(g * blk)) for g in log_g)

    seq_block = pl.BlockSpec((batch, RET_STEP_BLOCKS * blk, w), lambda j: (0, j, 0))
    const3 = lambda j: (0, 0, 0)
    shaped = lambda a: a.reshape(batch, seq, w)
    out = pl.pallas_call(
        functools.partial(_retention_kernel, block_decay=block_decay),
        out_shape=jax.ShapeDtypeStruct((batch, seq, w), BF16),
        grid=(nb // RET_STEP_BLOCKS,),
        in_specs=[seq_block] * 4 + [
            pl.BlockSpec(dmat.shape, const3), pl.BlockSpec(qdec.shape, const3),
            pl.BlockSpec(kdec.shape, const3)],
        out_specs=seq_block,
        scratch_shapes=[pltpu.VMEM((batch, RET_HEADS, RET_HEAD_DIM, RET_HEAD_DIM), F32)],
        compiler_params=pltpu.CompilerParams(
            dimension_semantics=("arbitrary",), vmem_limit_bytes=VMEM_LIMIT_BYTES),
        name="retention",
    )(shaped(rq), shaped(rk), shaped(rv), shaped(rg), dmat, qdec, kdec)
    return out.reshape(n, w)


def _attention_kernel(q_ref, kn_ref, kr_ref, vt_ref, out_ref,
                      m_ref, l_ref, acc_ref, jump_ref, qt_ref):
    tq = q_ref.shape[0]
    gw = ATTN_Q_GROUP
    kb = ATTN_KEY_BLOCK
    tk = vt_ref.shape[-1]
    groups = range(tq // gw)
    diag_blocks = tq // kb
    i = pl.program_id(2)
    first_diag = i * diag_blocks

    def keys(j, size=kb):
        start = pl.multiple_of(j * kb, kb)
        return jnp.concatenate(
            [kn_ref[pl.ds(start, size), :], kr_ref[pl.ds(start, size), :]], axis=1)

    def values_t(j, size=kb):
        return jnp.concatenate([vt_ref[j * (kb // tk) + t] for t in range(size // tk)], axis=1)

    def chunk_mask(shape, key0, g):
        kc = (lax.broadcasted_iota(jnp.int32, shape, 0) + key0) // CHUNK
        qc = (lax.broadcasted_iota(jnp.int32, shape, 1) + g * gw) // CHUNK
        return kc <= qc

    def two_pass(g, k, vt, key0=None):
        cols = slice(g * gw, (g + 1) * gw)
        st = _dot(k, qt_ref[:, cols])
        if key0 is not None:
            st = jnp.where(chunk_mask(st.shape, key0, g), st, -1e30)
        m_old = m_ref[:, cols]
        m_new = jnp.maximum(m_old, jnp.max(st, axis=0, keepdims=True))
        alpha = jnp.exp2(m_old - m_new)
        p = jnp.exp2(st - m_new)
        l_ref[:, cols] = alpha * l_ref[:, cols] + jnp.sum(p, axis=0, keepdims=True)
        acc_ref[:, cols] = alpha * acc_ref[:, cols] + _dot(vt, p.astype(BF16))
        m_ref[:, cols] = m_new

    def single_pass(j, items, key0):
        k = keys(j)
        vt = values_t(j)
        used = sorted({g for g, _, _ in items})
        cols = {g: slice(g * gw, (g + 1) * gw) for g in used}
        qs = {g: qt_ref[:, cols[g]] for g in used}
        m_old = {g: m_ref[:, cols[g]] for g in used}
        l_new = {g: l_ref[:, cols[g]] for g in used}
        acc_new = {g: acc_ref[:, cols[g]] for g in used}
        smax = {}

        def qk(item):
            g, a, _ = item
            return _dot(k[a:a + ATTN_KEY_SLICE, :], qs[g])

        pending = [qk(item) for item in items[:ATTN_LOOKAHEAD]]
        for n, (g, a, masked) in enumerate(items):
            if n + ATTN_LOOKAHEAD < len(items):
                pending.append(qk(items[n + ATTN_LOOKAHEAD]))
            st = pending.pop(0)
            if masked:
                st = jnp.where(chunk_mask(st.shape, key0 + a, g), st, -1e30)
            p = jnp.exp2(st - m_old[g])
            cmax = jnp.max(st, axis=0, keepdims=True)
            smax[g] = cmax if g not in smax else jnp.maximum(smax[g], cmax)
            l_new[g] = l_new[g] + jnp.sum(p, axis=0, keepdims=True)
            acc_new[g] = acc_new[g] + _dot(vt[:, a:a + ATTN_KEY_SLICE], p.astype(BF16))
        for g in used:
            m_new = jnp.maximum(m_old[g], smax[g])
            alpha = jnp.exp2(m_old[g] - m_new)
            l_ref[:, cols[g]] = alpha * l_new[g]
            acc_ref[:, cols[g]] = alpha * acc_new[g]
            m_ref[:, cols[g]] = m_new
            jump_ref[:, cols[g]] = jnp.maximum(jump_ref[:, cols[g]], smax[g] - m_old[g])

    slices = range(0, kb, ATTN_KEY_SLICE)
    full_items = [(g, a, False) for a in slices for g in groups]

    def diag_items(d):
        return [(g, a, d * kb + a >= g * gw) for a in slices for g in groups
                if d * kb + a < (g + 1) * gw]

    def fast_body(j, carry):
        single_pass(j, full_items, 0)
        return carry

    def safe_body(j, carry):
        k = keys(j)
        vt = values_t(j)
        for g in groups:
            two_pass(g, k, vt)
        return carry

    qt_ref[...] = q_ref[...].astype(F32).T.astype(BF16)
    m_ref[...] = jnp.max(_dot(keys(first_diag, CHUNK), qt_ref[...]), axis=0, keepdims=True)
    l_ref[...] = jnp.zeros_like(l_ref)
    acc_ref[...] = jnp.zeros_like(acc_ref)
    jump_ref[...] = jnp.zeros_like(jump_ref)
    lax.fori_loop(0, first_diag, fast_body, 0)
    for d in range(diag_blocks):
        single_pass(first_diag + d, diag_items(d), d * kb)

    @pl.when(jnp.max(jump_ref[...]) > ATTN_MAX_JUMP)
    def _():
        m_ref[...] = jnp.full_like(m_ref, -1e30)
        l_ref[...] = jnp.zeros_like(l_ref)
        acc_ref[...] = jnp.zeros_like(acc_ref)
        lax.fori_loop(0, first_diag, safe_body, 0)
        for d in range(diag_blocks):
            for g in groups:
                visible = min(kb, (g + 1) * gw - d * kb)
                if visible > 0:
                    two_pass(g, keys(first_diag + d, visible), values_t(first_diag + d, visible),
                             d * kb)

    o = acc_ref[...] * (1.0 / l_ref[...])
    out_ref[...] = o.T.astype(BF16)


def _attention(q, kn, kr, vt, batch, seq):
    n = q.shape[0]
    tq = ATTN_Q_BLOCK
    tk = ATTN_KV_BLOCK
    nq = seq // tq
    return pl.pallas_call(
        _attention_kernel,
        out_shape=jax.ShapeDtypeStruct((n, MLA_HEADS * MLA_V_DIM), BF16),
        grid=(batch, MLA_HEADS, nq),
        in_specs=[pl.BlockSpec((tq, MLA_QK_PAD), lambda b, h, i: (b * nq + i, h)),
                  pl.BlockSpec((seq, MLA_NOPE_DIM), lambda b, h, i: (b, h)),
                  pl.BlockSpec((seq, LANES), lambda b, h, i: (b, 0)),
                  pl.BlockSpec((seq // tk, MLA_V_DIM, tk), lambda b, h, i: (b, h, 0))],
        out_specs=pl.BlockSpec((tq, MLA_V_DIM), lambda b, h, i: (b * nq + i, h)),
        scratch_shapes=[pltpu.VMEM((1, tq), F32), pltpu.VMEM((1, tq), F32),
                        pltpu.VMEM((MLA_V_DIM, tq), F32), pltpu.VMEM((1, tq), F32),
                        pltpu.VMEM((MLA_QK_PAD, tq), BF16)],
        compiler_params=pltpu.CompilerParams(
            dimension_semantics=("arbitrary", "arbitrary", "arbitrary"),
            vmem_limit_bytes=VMEM_LIMIT_BYTES),
        name="attention",
    )(q, kn, kr, vt)


def _out_ffn_kernel(x_ref, ret_ref, att_ref, p_ref, wo_ref, gffn_ref, wg_ref, wu_ref, wd_ref,
                    gple_ref, wpg_ref, wpp_ref, gfin_ref, out_ref, *, final_norm):
    tm = x_ref.shape[0]
    part = tm // OUT_PARTS
    halves = tuple(slice(i * part, (i + 1) * part) for i in range(OUT_PARTS))
    both = range(len(halves))
    o = [_dot(jnp.concatenate([ret_ref[r, :], att_ref[r, :]], axis=1), wo_ref[...]) for r in halves]
    pe = [_dot(p_ref[r, :].astype(BF16), wpp_ref[...]) for r in halves]
    x1 = [x_ref[r, :] + o[i] for i, r in enumerate(halves)]
    h = [_rms(x1[i], gffn_ref[...]).astype(BF16) for i in both]
    gu = [(_dot(h[i], wg_ref[...]), _dot(h[i], wu_ref[...])) for i in both]
    act = [(g * _sigmoid(g) * u).astype(BF16) for g, u in gu]
    x2 = [x1[i] + _dot(act[i], wd_ref[...]) for i in both]
    h2 = [_rms(x2[i], gple_ref[...]).astype(BF16) for i in both]
    gate = [_sigmoid(_dot(h2[i], wpg_ref[...])) for i in both]
    for i, r in enumerate(halves):
        x3 = x2[i] + gate[i] * pe[i]
        if final_norm:
            x3 = _rms(x3, gfin_ref[...])
        out_ref[r, :] = x3


def _out_ffn(x2, ret, att, p2, wo, gffn, wg, wu, wd, gple, wpg, wpp, gfin, final_norm):
    n, dm = x2.shape
    tm = OUT_ROWS
    row = lambda i: (i, 0)
    const = lambda i: (0, 0)
    resident = lambda a: pl.BlockSpec(a.shape, const, pipeline_mode=pl.Buffered(1))
    return pl.pallas_call(
        functools.partial(_out_ffn_kernel, final_norm=final_norm),
        out_shape=jax.ShapeDtypeStruct((n, dm), F32),
        grid=(n // tm,),
        in_specs=[pl.BlockSpec((tm, dm), row), pl.BlockSpec((tm, ret.shape[1]), row),
                  pl.BlockSpec((tm, att.shape[1]), row), pl.BlockSpec((tm, p2.shape[1]), row),
                  resident(wo), resident(gffn), resident(wg), resident(wu), resident(wd),
                  resident(gple), resident(wpg), resident(wpp), resident(gfin)],
        out_specs=pl.BlockSpec((tm, dm), row),
        compiler_params=pltpu.CompilerParams(
            dimension_semantics=("arbitrary",), vmem_limit_bytes=VMEM_LIMIT_BYTES),
        name="out_ffn",
    )(x2, ret, att, p2, wo, gffn, wg, wu, wd, gple, wpg, wpp, gfin)


def _prep_in_weights(w_in, w_uq, w_ukv):
    win = jnp.pad(w_in.astype(BF16), ((0, 0), (0, LANES - MLA_ROPE_DIM)))
    q_lora = w_uq.shape[0]
    uq = w_uq.reshape(q_lora, MLA_HEADS, MLA_QK_DIM)
    rope = uq[:, :, MLA_NOPE_DIM:]
    rot = jnp.concatenate([-rope[..., MLA_ROPE_DIM // 2:], rope[..., :MLA_ROPE_DIM // 2]], axis=-1)
    wuq = jnp.concatenate([uq, rot], axis=-1).reshape(q_lora, MLA_HEADS * MLA_QK_PAD)
    kv_lora = w_ukv.shape[0]
    ukv = w_ukv.reshape(kv_lora, MLA_HEADS, MLA_NOPE_DIM + MLA_V_DIM)
    wuk = ukv[:, :, :MLA_NOPE_DIM].reshape(kv_lora, -1)
    wuvt = ukv[:, :, MLA_NOPE_DIM:].reshape(kv_lora, -1).T
    return win, wuq.astype(BF16), wuk.astype(BF16), wuvt.astype(BF16)


def _rope_consts():
    def inv(half):
        return jnp.exp(-math.log(ROPE_BASE) * jnp.arange(half, dtype=F32) / half)
    inv_r = inv(RET_HEAD_DIM // 2)
    inv_m = inv(MLA_ROPE_DIM // 2)
    row = jnp.concatenate([inv_r, inv_m, jnp.zeros(LANES - inv_r.size - inv_m.size, F32)])
    return jnp.broadcast_to(row[None, :], (8, LANES))


def kernel(x, p, positions, mix_norm_g, w_in, q_norm_g, w_uq, kv_norm_g, w_ukv, w_o, ffn_norm_g,
           w_ffn_gate, w_ffn_up, w_ffn_down, ple_norm_g, w_ple_gate, w_ple_proj, final_norm_g):
    batch, seq, dm = x.shape
    depth = w_in.shape[0]
    n = batch * seq
    x2 = x.reshape(n, dm)
    pos2 = positions.reshape(n, 1)
    rc = _rope_consts()
    vec = lambda g: g.reshape(1, -1).astype(F32)
    for i in range(depth):
        win, wuq, wuk, wuvt = _prep_in_weights(w_in[i], w_uq[i], w_ukv[i])
        rq, rk, rv, rg, q, kn, kr, vt = _in_proj(
            x2, pos2, vec(mix_norm_g[i]), win, vec(q_norm_g[i]), wuq, vec(kv_norm_g[i]), wuk, wuvt,
            rc)
        ret = _retention(rq, rk, rv, rg, batch, seq)
        att = _attention(q, kn, kr, vt, batch, seq)
        x2 = _out_ffn(
            x2, ret, att, p[i].reshape(n, -1), w_o[i].astype(BF16), vec(ffn_norm_g[i]),
            w_ffn_gate[i].astype(BF16), w_ffn_up[i].astype(BF16), w_ffn_down[i].astype(BF16),
            vec(ple_norm_g[i]), w_ple_gate[i].astype(BF16), w_ple_proj[i].astype(BF16),
            vec(final_norm_g), final_norm=(i == depth - 1))
    return x2.reshape(batch, seq, dm)
```

```python
import functools
import math

import numpy as np
import jax
import jax.numpy as jnp
from jax import lax
from jax.experimental import pallas as pl
from jax.experimental.pallas import tpu as pltpu

F32 = jnp.float32
BF16 = jnp.bfloat16

CHUNK = 64
RET_HEADS = 4
RET_HEAD_DIM = 128
MLA_HEADS = 4
MLA_NOPE_DIM = 128
MLA_ROPE_DIM = 64
MLA_V_DIM = 128
MLA_QK_DIM = MLA_NOPE_DIM + MLA_ROPE_DIM
ROPE_BASE = 10000.0
RMS_EPS = 1e-6
GN_EPS = 1e-5

LANES = 128
MLA_QK_PAD = 2 * LANES
VMEM_LIMIT_BYTES = 56 * 1024 * 1024

IN_PROJ_ROWS = 1024
IN_PROJ_PARTS = 4
RET_BLOCK = 256
RET_STEP_BLOCKS = 4
ATTN_Q_BLOCK = 2048
ATTN_Q_GROUP = 512
ATTN_KEY_BLOCK = 1024
ATTN_KV_BLOCK = 512
ATTN_MAX_JUMP = 32.0
ATTN_KEY_SLICE = 256
ATTN_LOOKAHEAD = 3
OUT_ROWS = 512
OUT_PARTS = 2


def _rms(x, g):
    return x * lax.rsqrt(jnp.mean(x * x, axis=-1, keepdims=True) + RMS_EPS) * g


def _sigmoid(x):
    return 1.0 / (1.0 + jnp.exp(-x))


def _dot(a, b):
    return jnp.dot(a, b, preferred_element_type=F32)


def _dot_nt(a, b):
    return lax.dot_general(a, b, (((1,), (1,)), ((), ())), preferred_element_type=F32)


def _rot_half_lanes(x):
    return pltpu.roll(x, LANES // 2, 1)


def _in_proj_kernel(x_ref, pos_ref, gmix_ref, win_ref, qg_ref, wuq_ref, kvg_ref, wuk_ref, wuvt_ref,
                    rc_ref, rq_ref, rk_ref, rv_ref, rg_ref, q_ref, kn_ref, kr_ref, vt_ref, *, q_scale):
    tm = x_ref.shape[0]
    part = tm // IN_PROJ_PARTS
    parts = [slice(i * part, (i + 1) * part) for i in range(IN_PROJ_PARTS)]
    zs = [_dot(_rms(x_ref[r, :], gmix_ref[...]).astype(BF16), win_ref[...]) for r in parts]

    lane = lax.broadcasted_iota(jnp.int32, (1, LANES), 1)
    half = LANES // 2
    quarter = LANES // 4
    d = RET_HEAD_DIM
    w = RET_HEADS * d
    k_scale = d ** -0.5
    o = 4 * w
    q_lora = qg_ref.shape[-1]
    kv_lora = kvg_ref.shape[-1]
    tk = vt_ref.shape[-1]
    vt_cols = min(tk, part)

    for r, z in zip(parts, zs):
        anchor = lax.bitcast_convert_type(z[:, 0:1], jnp.uint32)
        zero = lax.shift_right_logical(lax.shift_right_logical(anchor, jnp.uint32(16)), jnp.uint32(16))
        pos = (pos_ref[r, :] + zero.astype(jnp.int32)).astype(F32)
        ang = pos * rc_ref[0:1, :]
        c = jnp.cos(ang)
        s = jnp.sin(ang)
        c_hi, s_hi = _rot_half_lanes(c), _rot_half_lanes(s)
        cos_r = jnp.where(lane < half, c, c_hi)
        sin_r = jnp.where(lane < half, -s, s_hi)
        c_q3 = pltpu.roll(c, 3 * quarter, 1)
        s_q3 = pltpu.roll(s, 3 * quarter, 1)
        cos_m = jnp.where(lane < quarter, c_hi, jnp.where(lane < half, c_q3, 0.0))
        sin_m = jnp.where(lane < quarter, s_hi, jnp.where(lane < half, s_q3, 0.0))

        for hh in range(RET_HEADS):
            q = z[:, hh * d:(hh + 1) * d]
            rq_ref[r, hh * d:(hh + 1) * d] = (q * cos_r + _rot_half_lanes(q) * sin_r).astype(BF16)
            k = z[:, w + hh * d:w + (hh + 1) * d]
            rk_ref[r, hh * d:(hh + 1) * d] = (
                (k * cos_r + _rot_half_lanes(k) * sin_r) * k_scale).astype(BF16)
        rv_ref[r, :] = z[:, 2 * w:3 * w].astype(BF16)
        g = z[:, 3 * w:4 * w]
        rg_ref[r, :] = g * _sigmoid(g)

        cq = z[:, o:o + q_lora]
        ckv = z[:, o + q_lora:o + q_lora + kv_lora]

        def rope_low_half(y):
            rot = jnp.where(lane < quarter, -pltpu.roll(y, 3 * quarter, 1), pltpu.roll(y, quarter, 1))
            return y * cos_m + rot * sin_m

        kpe = z[:, o + q_lora + kv_lora:o + q_lora + kv_lora + LANES]
        kr_ref[r, :] = rope_low_half(kpe).astype(BF16)

        qq = _dot(_rms(cq, qg_ref[...]).astype(BF16), wuq_ref[...])
        for hh in range(MLA_HEADS):
            b = hh * MLA_QK_PAD
            q_ref[r, b:b + LANES] = (qq[:, b:b + LANES] * q_scale).astype(BF16)
            y = qq[:, b + LANES:b + 2 * LANES]
            q_ref[r, b + LANES:b + 2 * LANES] = (
                (y * cos_m + _rot_half_lanes(y) * sin_m) * q_scale).astype(BF16)

        ckvn = _rms(ckv, kvg_ref[...]).astype(BF16)
        kn_ref[r, :] = _dot(ckvn, wuk_ref[...]).astype(BF16)
        vt = _dot_nt(wuvt_ref[...], ckvn).astype(BF16)
        for a in range(r.start, r.stop, vt_cols):
            vt_ref[a // tk, :, a % tk:a % tk + vt_cols] = vt[:, a - r.start:a - r.start + vt_cols]


def _in_proj(x2, pos2, gmix, win, qg, wuq, kvg, wuk, wuvt, rc):
    n, dm = x2.shape
    tm = IN_PROJ_ROWS
    w = RET_HEADS * RET_HEAD_DIM
    const = lambda i: (0, 0)
    row = lambda i: (i, 0)
    full = lambda a: pl.BlockSpec(a.shape, const)
    out_shapes = (
        jax.ShapeDtypeStruct((n, w), BF16),
        jax.ShapeDtypeStruct((n, w), BF16),
        jax.ShapeDtypeStruct((n, w), BF16),
        jax.ShapeDtypeStruct((n, w), F32),
        jax.ShapeDtypeStruct((n, MLA_HEADS * MLA_QK_PAD), BF16),
        jax.ShapeDtypeStruct((n, MLA_HEADS * MLA_NOPE_DIM), BF16),
        jax.ShapeDtypeStruct((n, LANES), BF16),
    )
    tk = ATTN_KV_BLOCK
    vt_shape = jax.ShapeDtypeStruct((n // tk, MLA_HEADS * MLA_V_DIM, tk), BF16)
    out_specs = tuple(pl.BlockSpec((tm, s.shape[1]), row) for s in out_shapes) + (
        pl.BlockSpec((tm // tk, MLA_HEADS * MLA_V_DIM, tk), lambda i: (i, 0, 0)),)
    q_scale = MLA_QK_DIM ** -0.5 * math.log2(math.e)
    return pl.pallas_call(
        functools.partial(_in_proj_kernel, q_scale=q_scale),
        out_shape=out_shapes + (vt_shape,),
        grid=(n // tm,),
        in_specs=[pl.BlockSpec((tm, dm), row), pl.BlockSpec((tm, 1), row),
                  full(gmix), full(win), full(qg), full(wuq), full(kvg), full(wuk), full(wuvt),
                  full(rc)],
        out_specs=out_specs,
        compiler_params=pltpu.CompilerParams(
            dimension_semantics=("arbitrary",), vmem_limit_bytes=VMEM_LIMIT_BYTES),
        name="in_proj",
    )(x2, pos2, gmix, win, qg, wuq, kvg, wuk, wuvt, rc)


def _retention_kernel(rq_ref, rk_ref, rv_ref, rg_ref, dmat_ref, qdec_ref, kdec_ref, out_ref,
                      state_ref, *, block_decay):
    @pl.when(pl.program_id(0) == 0)
    def _():
        state_ref[...] = jnp.zeros_like(state_ref)

    d = RET_HEAD_DIM
    blk = dmat_ref.shape[-1]
    nblk = rq_ref.shape[1] // blk
    seqs = range(rq_ref.shape[0])
    chains = [(t, b, hh) for t in range(nblk) for b in seqs for hh in range(RET_HEADS)]
    state = {(b, hh): state_ref[b, hh] for b in seqs for hh in range(RET_HEADS)}

    def front(c):
        t, b, hh = c
        rows = slice(t * blk, (t + 1) * blk)
        sl = slice(hh * d, (hh + 1) * d)
        q = rq_ref[b, rows, sl]
        v = rv_ref[b, rows, sl]
        kt = rk_ref[b, rows, sl].astype(F32).T
        st = state[b, hh]
        scores = _dot(q, kt.astype(BF16))
        carried = _dot(q, st.astype(BF16))
        update = _dot((kt * kdec_ref[hh][0:1, :]).astype(BF16), v)
        state[b, hh] = block_decay[hh] * st + update
        return scores, carried, v

    fronts = {0: front(chains[0])}
    for n, (t, b, hh) in enumerate(chains):
        if n + 1 < len(chains):
            fronts[n + 1] = front(chains[n + 1])
        scores, carried, v = fronts.pop(n)
        rows = slice(t * blk, (t + 1) * blk)
        sl = slice(hh * d, (hh + 1) * d)
        o = _dot((scores * dmat_ref[hh]).astype(BF16), v) + carried * qdec_ref[hh]
        mu = jnp.mean(o, axis=-1, keepdims=True)
        oc = o - mu
        var = jnp.mean(oc * oc, axis=-1, keepdims=True)
        out_ref[b, rows, sl] = (oc * lax.rsqrt(var + GN_EPS) * rg_ref[b, rows, sl]).astype(BF16)
    for (b, hh), st in state.items():
        state_ref[b, hh] = st


def _retention(rq, rk, rv, rg, batch, seq):
    n, w = rq.shape
    blk = RET_BLOCK
    nb = seq // blk
    log_g = np.log1p(-np.exp2(-5.0 - np.arange(RET_HEADS, dtype=np.float64)))
    idx = jnp.arange(blk, dtype=F32)
    lg = jnp.asarray(log_g, F32)
    dist = jnp.abs(idx[:, None] - idx[None, :])
    visible = (idx[None, :] // CHUNK) <= (idx[:, None] // CHUNK)
    dmat = jnp.where(visible[None], jnp.exp(lg[:, None, None] * dist[None]), 0.0)
    qdec = jnp.broadcast_to(jnp.exp(lg[:, None] * (idx + 1.0))[:, :, None], (RET_HEADS, blk, LANES))
    kdec = jnp.broadcast_to(jnp.exp(lg[:, None] * (blk - 1.0 - idx))[:, None, :], (RET_HEADS, 8, blk))
    block_decay = tuple(float(math.exp(g * blk)) for g in log_g)

    seq_block = pl.BlockSpec((batch, RET_STEP_BLOCKS * blk, w), lambda j: (0, j, 0))
    const3 = lambda j: (0, 0, 0)
    shaped = lambda a: a.reshape(batch, seq, w)
    out = pl.pallas_call(
        functools.partial(_retention_kernel, block_decay=block_decay),
        out_shape=jax.ShapeDtypeStruct((batch, seq, w), BF16),
        grid=(nb // RET_STEP_BLOCKS,),
        in_specs=[seq_block] * 4 + [
            pl.BlockSpec(dmat.shape, const3), pl.BlockSpec(qdec.shape, const3),
            pl.BlockSpec(kdec.shape, const3)],
        out_specs=seq_block,
        scratch_shapes=[pltpu.VMEM((batch, RET_HEADS, RET_HEAD_DIM, RET_HEAD_DIM), F32)],
        compiler_params=pltpu.CompilerParams(
            dimension_semantics=("arbitrary",), vmem_limit_bytes=VMEM_LIMIT_BYTES),
        name="retention",
    )(shaped(rq), shaped(rk), shaped(rv), shaped(rg), dmat, qdec, kdec)
    return out.reshape(n, w)


def _attention_kernel(q_ref, kn_ref, kr_ref, vt_ref, out_ref,
                      m_ref, l_ref, acc_ref, jump_ref, qt_ref):
    tq = q_ref.shape[0]
    gw = ATTN_Q_GROUP
    kb = ATTN_KEY_BLOCK
    tk = vt_ref.shape[-1]
    groups = range(tq // gw)
    diag_blocks = tq // kb
    i = pl.program_id(2)
    first_diag = i * diag_blocks

    def keys(j, size=kb):
        start = pl.multiple_of(j * kb, kb)
        return jnp.concatenate(
            [kn_ref[pl.ds(start, size), :], kr_ref[pl.ds(start, size), :]], axis=1)

    def values_t(j, size=kb):
        return jnp.concatenate([vt_ref[j * (kb // tk) + t] for t in range(size // tk)], axis=1)

    def chunk_mask(shape, key0, g):
        kc = (lax.broadcasted_iota(jnp.int32, shape, 0) + key0) // CHUNK
        qc = (lax.broadcasted_iota(jnp.int32, shape, 1) + g * gw) // CHUNK
        return kc <= qc

    def two_pass(g, k, vt, key0=None):
        cols = slice(g * gw, (g + 1) * gw)
        st = _dot_nt(k, q_ref[g * gw:(g + 1) * gw, :])
        if key0 is not None:
            st = jnp.where(chunk_mask(st.shape, key0, g), st, -1e30)
        m_old = m_ref[:, cols]
        m_new = jnp.maximum(m_old, jnp.max(st, axis=0, keepdims=True))
        alpha = jnp.exp2(m_old - m_new)
        p = jnp.exp2(st - m_new)
        l_ref[:, cols] = alpha * l_ref[:, cols] + jnp.sum(p, axis=0, keepdims=True)
        acc_ref[:, cols] = alpha * acc_ref[:, cols] + _dot(vt, p.astype(BF16))
        m_ref[:, cols] = m_new

    def single_pass(j, items, key0):
        k = keys(j)
        vt = values_t(j)
        used = sorted({g for g, _, _ in items})
        cols = {g: slice(g * gw, (g + 1) * gw) for g in used}
        qs = {g: qt_ref[:, cols[g]] for g in used}
        m_old = {g: m_ref[:, cols[g]] for g in used}
        l_new = {g: l_ref[:, cols[g]] for g in used}
        acc_new = {g: acc_ref[:, cols[g]] for g in used}
        smax = {}

        def qk(item):
            g, a, _ = item
            return _dot(k[a:a + ATTN_KEY_SLICE, :], qs[g])

        pending = [qk(item) for item in items[:ATTN_LOOKAHEAD]]
        for n, (g, a, masked) in enumerate(items):
            if n + ATTN_LOOKAHEAD < len(items):
                pending.append(qk(items[n + ATTN_LOOKAHEAD]))
            st = pending.pop(0)
            if masked:
                st = jnp.where(chunk_mask(st.shape, key0 + a, g), st, -1e30)
            p = jnp.exp2(st - m_old[g])
            cmax = jnp.max(st, axis=0, keepdims=True)
            smax[g] = cmax if g not in smax else jnp.maximum(smax[g], cmax)
            l_new[g] = l_new[g] + jnp.sum(p, axis=0, keepdims=True)
            acc_new[g] = acc_new[g] + _dot(vt[:, a:a + ATTN_KEY_SLICE], p.astype(BF16))
        for g in used:
            m_new = jnp.maximum(m_old[g], smax[g])
            alpha = jnp.exp2(m_old[g] - m_new)
            l_ref[:, cols[g]] = alpha * l_new[g]
            acc_ref[:, cols[g]] = alpha * acc_new[g]
            m_ref[:, cols[g]] = m_new
            jump_ref[:, cols[g]] = jnp.maximum(jump_ref[:, cols[g]], smax[g] - m_old[g])

    slices = range(0, kb, ATTN_KEY_SLICE)
    full_items = [(g, a, False) for a in slices for g in groups]

    def diag_items(d):
        return [(g, a, d * kb + a >= g * gw) for a in slices for g in groups
                if d * kb + a < (g + 1) * gw]

    def fast_body(j, carry):
        single_pass(j, full_items, 0)
        return carry

    def safe_body(j, carry):
        k = keys(j)
        vt = values_t(j)
        for g in groups:
            two_pass(g, k, vt)
        return carry

    qt_ref[...] = q_ref[...].astype(F32).T.astype(BF16)
    m_ref[...] = jnp.max(_dot_nt(keys(first_diag, CHUNK), q_ref[...]), axis=0, keepdims=True)
    l_ref[...] = jnp.zeros_like(l_ref)
    acc_ref[...] = jnp.zeros_like(acc_ref)
    jump_ref[...] = jnp.zeros_like(jump_ref)
    lax.fori_loop(0, first_diag, fast_body, 0)
    for d in range(diag_blocks):
        single_pass(first_diag + d, diag_items(d), d * kb)

    @pl.when(jnp.max(jump_ref[...]) > ATTN_MAX_JUMP)
    def _():
        m_ref[...] = jnp.full_like(m_ref, -1e30)
        l_ref[...] = jnp.zeros_like(l_ref)
        acc_ref[...] = jnp.zeros_like(acc_ref)
        lax.fori_loop(0, first_diag, safe_body, 0)
        for d in range(diag_blocks):
            for g in groups:
                visible = min(kb, (g + 1) * gw - d * kb)
                if visible > 0:
                    two_pass(g, keys(first_diag + d, visible), values_t(first_diag + d, visible),
                             d * kb)

    o = acc_ref[...] * (1.0 / l_ref[...])
    out_ref[...] = o.T.astype(BF16)


def _attention(q, kn, kr, vt, batch, seq):
    n = q.shape[0]
    tq = ATTN_Q_BLOCK
    tk = ATTN_KV_BLOCK
    nq = seq // tq
    return pl.pallas_call(
        _attention_kernel,
        out_shape=jax.ShapeDtypeStruct((n, MLA_HEADS * MLA_V_DIM), BF16),
        grid=(batch, MLA_HEADS, nq),
        in_specs=[pl.BlockSpec((tq, MLA_QK_PAD), lambda b, h, i: (b * nq + i, h)),
                  pl.BlockSpec((seq, MLA_NOPE_DIM), lambda b, h, i: (b, h)),
                  pl.BlockSpec((seq, LANES), lambda b, h, i: (b, 0)),
                  pl.BlockSpec((seq // tk, MLA_V_DIM, tk), lambda b, h, i: (b, h, 0))],
        out_specs=pl.BlockSpec((tq, MLA_V_DIM), lambda b, h, i: (b * nq + i, h)),
        scratch_shapes=[pltpu.VMEM((1, tq), F32), pltpu.VMEM((1, tq), F32),
                        pltpu.VMEM((MLA_V_DIM, tq), F32), pltpu.VMEM((1, tq), F32),
                        pltpu.VMEM((MLA_QK_PAD, tq), BF16)],
        compiler_params=pltpu.CompilerParams(
            dimension_semantics=("arbitrary", "arbitrary", "arbitrary"),
            vmem_limit_bytes=VMEM_LIMIT_BYTES),
        name="attention",
    )(q, kn, kr, vt)


def _out_ffn_kernel(x_ref, ret_ref, att_ref, p_ref, wo_ref, gffn_ref, wg_ref, wu_ref, wd_ref,
                    gple_ref, wpg_ref, wpp_ref, gfin_ref, out_ref, *, final_norm):
    tm = x_ref.shape[0]
    part = tm // OUT_PARTS
    parts = tuple(slice(i * part, (i + 1) * part) for i in range(OUT_PARTS))
    each = range(len(parts))
    o = [_dot(jnp.concatenate([ret_ref[r, :], att_ref[r, :]], axis=1), wo_ref[...]) for r in parts]
    pe = [_dot(p_ref[r, :].astype(BF16), wpp_ref[...]) for r in parts]
    x1 = [x_ref[r, :] + o[i] for i, r in enumerate(parts)]
    h = [_rms(x1[i], gffn_ref[...]).astype(BF16) for i in each]
    gu = [(_dot(h[i], wg_ref[...]), _dot(h[i], wu_ref[...])) for i in each]
    act = [(g * _sigmoid(g) * u).astype(BF16) for g, u in gu]
    x2 = [x1[i] + _dot(act[i], wd_ref[...]) for i in each]
    h2 = [_rms(x2[i], gple_ref[...]).astype(BF16) for i in each]
    gate = [_sigmoid(_dot(h2[i], wpg_ref[...])) for i in each]
    for i, r in enumerate(parts):
        x3 = x2[i] + gate[i] * pe[i]
        if final_norm:
            x3 = _rms(x3, gfin_ref[...])
        out_ref[r, :] = x3


def _out_ffn(x2, ret, att, p2, wo, gffn, wg, wu, wd, gple, wpg, wpp, gfin, final_norm):
    n, dm = x2.shape
    tm = OUT_ROWS
    row = lambda i: (i, 0)
    const = lambda i: (0, 0)
    resident = lambda a: pl.BlockSpec(a.shape, const, pipeline_mode=pl.Buffered(1))
    return pl.pallas_call(
        functools.partial(_out_ffn_kernel, final_norm=final_norm),
        out_shape=jax.ShapeDtypeStruct((n, dm), F32),
        grid=(n // tm,),
        in_specs=[pl.BlockSpec((tm, dm), row), pl.BlockSpec((tm, ret.shape[1]), row),
                  pl.BlockSpec((tm, att.shape[1]), row), pl.BlockSpec((tm, p2.shape[1]), row),
                  resident(wo), resident(gffn), resident(wg), resident(wu), resident(wd),
                  resident(gple), resident(wpg), resident(wpp), resident(gfin)],
        out_specs=pl.BlockSpec((tm, dm), row),
        compiler_params=pltpu.CompilerParams(
            dimension_semantics=("arbitrary",), vmem_limit_bytes=VMEM_LIMIT_BYTES),
        name="out_ffn",
    )(x2, ret, att, p2, wo, gffn, wg, wu, wd, gple, wpg, wpp, gfin)


def _prep_in_weights(w_in, w_uq, w_ukv):
    win = jnp.pad(w_in.astype(BF16), ((0, 0), (0, LANES - MLA_ROPE_DIM)))
    q_lora = w_uq.shape[0]
    uq = w_uq.reshape(q_lora, MLA_HEADS, MLA_QK_DIM)
    rope = uq[:, :, MLA_NOPE_DIM:]
    rot = jnp.concatenate([-rope[..., MLA_ROPE_DIM // 2:], rope[..., :MLA_ROPE_DIM // 2]], axis=-1)
    wuq = jnp.concatenate([uq, rot], axis=-1).reshape(q_lora, MLA_HEADS * MLA_QK_PAD)
    kv_lora = w_ukv.shape[0]
    ukv = w_ukv.reshape(kv_lora, MLA_HEADS, MLA_NOPE_DIM + MLA_V_DIM)
    wuk = ukv[:, :, :MLA_NOPE_DIM].reshape(kv_lora, -1)
    wuvt = ukv[:, :, MLA_NOPE_DIM:].reshape(kv_lora, -1).T
    return win, wuq.astype(BF16), wuk.astype(BF16), wuvt.astype(BF16)


def _rope_consts():
    def inv(half):
        return jnp.exp(-math.log(ROPE_BASE) * jnp.arange(half, dtype=F32) / half)
    inv_r = inv(RET_HEAD_DIM // 2)
    inv_m = inv(MLA_ROPE_DIM // 2)
    row = jnp.concatenate([inv_r, inv_m, jnp.zeros(LANES - inv_r.size - inv_m.size, F32)])
    return jnp.broadcast_to(row[None, :], (8, LANES))


def kernel(x, p, positions, mix_norm_g, w_in, q_norm_g, w_uq, kv_norm_g, w_ukv, w_o, ffn_norm_g,
           w_ffn_gate, w_ffn_up, w_ffn_down, ple_norm_g, w_ple_gate, w_ple_proj, final_norm_g):
    batch, seq, dm = x.shape
    depth = w_in.shape[0]
    n = batch * seq
    x2 = x.reshape(n, dm)
    pos2 = positions.reshape(n, 1)
    rc = _rope_consts()
    vec = lambda g: g.reshape(1, -1).astype(F32)
    for i in range(depth):
        win, wuq, wuk, wuvt = _prep_in_weights(w_in[i], w_uq[i], w_ukv[i])
        rq, rk, rv, rg, q, kn, kr, vt = _in_proj(
            x2, pos2, vec(mix_norm_g[i]), win, vec(q_norm_g[i]), wuq, vec(kv_norm_g[i]), wuk, wuvt,
            rc)
        ret = _retention(rq, rk, rv, rg, batch, seq)
        att = _attention(q, kn, kr, vt, batch, seq)
        x2 = _out_ffn(
            x2, ret, att, p[i].reshape(n, -1), w_o[i].astype(BF16), vec(ffn_norm_g[i]),
            w_ffn_gate[i].astype(BF16), w_ffn_up[i].astype(BF16), w_ffn_down[i].astype(BF16),
            vec(ple_norm_g[i]), w_ple_gate[i].astype(BF16), w_ple_proj[i].astype(BF16),
            vec(final_norm_g), final_norm=(i == depth - 1))
    return x2.reshape(batch, seq, dm)
```

```python
import functools
import math

import numpy as np
import jax
import jax.numpy as jnp
from jax import lax
from jax.experimental import pallas as pl
from jax.experimental.pallas import tpu as pltpu

F32 = jnp.float32
BF16 = jnp.bfloat16

CHUNK = 64
RET_HEADS = 4
RET_HEAD_DIM = 128
MLA_HEADS = 4
MLA_NOPE_DIM = 128
MLA_ROPE_DIM = 64
MLA_V_DIM = 128
MLA_QK_DIM = MLA_NOPE_DIM + MLA_ROPE_DIM
ROPE_BASE = 10000.0
RMS_EPS = 1e-6
GN_EPS = 1e-5

LANES = 128
MLA_QK_PAD = 2 * LANES
VMEM_LIMIT_BYTES = 56 * 1024 * 1024

IN_PROJ_ROWS = 1024
IN_PROJ_PARTS = 4
RET_BLOCK = 256
RET_STEP_BLOCKS = 4
ATTN_Q_BLOCK = 2048
ATTN_Q_GROUP = 512
ATTN_KEY_BLOCK = 1024
ATTN_KV_BLOCK = 512
ATTN_MAX_JUMP = 32.0
ATTN_KEY_SLICE = 256
ATTN_LOOKAHEAD = 3
OUT_ROWS = 512
OUT_PARTS = 2


def _rms(x, g):
    return x * lax.rsqrt(jnp.mean(x * x, axis=-1, keepdims=True) + RMS_EPS) * g


def _sigmoid(x):
    return 1.0 / (1.0 + jnp.exp(-x))


def _dot(a, b):
    return jnp.dot(a, b, preferred_element_type=F32)


def _dot_nt(a, b):
    return lax.dot_general(a, b, (((1,), (1,)), ((), ())), preferred_element_type=F32)


def _rot_half_lanes(x):
    return pltpu.roll(x, LANES // 2, 1)


def _in_proj_kernel(*refs, q_scale, n_cast):
    (x_ref, pos_ref, gmix_ref, win_ref, qg_ref, wuq_ref, kvg_ref, wuk_ref, wuvt_ref,
     rc_ref) = refs[:10]
    cast_in = refs[10:10 + n_cast]
    rq_ref, rk_ref, rv_ref, rg_ref, q_ref, kn_ref, kr_ref, vt_ref = refs[10 + n_cast:18 + n_cast]
    cast_out = refs[18 + n_cast:]
    for src, dst in zip(cast_in, cast_out):
        dst[...] = src[...].astype(BF16)

    tm = x_ref.shape[0]
    part = tm // IN_PROJ_PARTS
    parts = [slice(i * part, (i + 1) * part) for i in range(IN_PROJ_PARTS)]
    zs = [_dot(_rms(x_ref[r, :], gmix_ref[...]).astype(BF16), win_ref[...]) for r in parts]

    lane = lax.broadcasted_iota(jnp.int32, (1, LANES), 1)
    half = LANES // 2
    quarter = LANES // 4
    d = RET_HEAD_DIM
    w = RET_HEADS * d
    k_scale = d ** -0.5
    o = 4 * w
    q_lora = qg_ref.shape[-1]
    kv_lora = kvg_ref.shape[-1]
    tk = vt_ref.shape[-1]
    vt_cols = min(tk, part)

    for r, z in zip(parts, zs):
        anchor = lax.bitcast_convert_type(z[:, 0:1], jnp.uint32)
        zero = lax.shift_right_logical(lax.shift_right_logical(anchor, jnp.uint32(16)), jnp.uint32(16))
        pos = (pos_ref[r, :] + zero.astype(jnp.int32)).astype(F32)
        ang = pos * rc_ref[0:1, :]
        c = jnp.cos(ang)
        s = jnp.sin(ang)
        c_hi, s_hi = _rot_half_lanes(c), _rot_half_lanes(s)
        cos_r = jnp.where(lane < half, c, c_hi)
        sin_r = jnp.where(lane < half, -s, s_hi)
        c_q3 = pltpu.roll(c, 3 * quarter, 1)
        s_q3 = pltpu.roll(s, 3 * quarter, 1)
        cos_m = jnp.where(lane < quarter, c_hi, jnp.where(lane < half, c_q3, 0.0))
        sin_m = jnp.where(lane < quarter, s_hi, jnp.where(lane < half, s_q3, 0.0))

        for hh in range(RET_HEADS):
            q = z[:, hh * d:(hh + 1) * d]
            rq_ref[r, hh * d:(hh + 1) * d] = (q * cos_r + _rot_half_lanes(q) * sin_r).astype(BF16)
            k = z[:, w + hh * d:w + (hh + 1) * d]
            rk_ref[r, hh * d:(hh + 1) * d] = (
                (k * cos_r + _rot_half_lanes(k) * sin_r) * k_scale).astype(BF16)
        rv_ref[r, :] = z[:, 2 * w:3 * w].astype(BF16)
        g = z[:, 3 * w:4 * w]
        rg_ref[r, :] = g * _sigmoid(g)

        cq = z[:, o:o + q_lora]
        ckv = z[:, o + q_lora:o + q_lora + kv_lora]

        def rope_low_half(y):
            rot = jnp.where(lane < quarter, -pltpu.roll(y, 3 * quarter, 1), pltpu.roll(y, quarter, 1))
            return y * cos_m + rot * sin_m

        kpe = z[:, o + q_lora + kv_lora:o + q_lora + kv_lora + LANES]
        kr_ref[r, :] = rope_low_half(kpe).astype(BF16)

        qq = _dot(_rms(cq, qg_ref[...]).astype(BF16), wuq_ref[...])
        for hh in range(MLA_HEADS):
            b = hh * MLA_QK_PAD
            q_ref[r, b:b + LANES] = (qq[:, b:b + LANES] * q_scale).astype(BF16)
            y = qq[:, b + LANES:b + 2 * LANES]
            q_ref[r, b + LANES:b + 2 * LANES] = (
                (y * cos_m + _rot_half_lanes(y) * sin_m) * q_scale).astype(BF16)

        ckvn = _rms(ckv, kvg_ref[...]).astype(BF16)
        kn_ref[r, :] = _dot(ckvn, wuk_ref[...]).astype(BF16)
        vt = _dot_nt(wuvt_ref[...], ckvn).astype(BF16)
        for a in range(r.start, r.stop, vt_cols):
            vt_ref[a // tk, :, a % tk:a % tk + vt_cols] = vt[:, a - r.start:a - r.start + vt_cols]


def _in_proj(x2, pos2, gmix, win, qg, wuq, kvg, wuk, wuvt, rc, cast_weights):
    n, dm = x2.shape
    tm = IN_PROJ_ROWS
    steps = n // tm
    bf16_rows = 16
    for a in cast_weights:
        assert a.shape[0] % (steps * bf16_rows) == 0, a.shape
    cast_specs = [pl.BlockSpec((a.shape[0] // steps, a.shape[1]), lambda i: (i, 0))
                  for a in cast_weights]
    cast_shapes = tuple(jax.ShapeDtypeStruct(a.shape, BF16) for a in cast_weights)
    w = RET_HEADS * RET_HEAD_DIM
    const = lambda i: (0, 0)
    row = lambda i: (i, 0)
    full = lambda a: pl.BlockSpec(a.shape, const)
    out_shapes = (
        jax.ShapeDtypeStruct((n, w), BF16),
        jax.ShapeDtypeStruct((n, w), BF16),
        jax.ShapeDtypeStruct((n, w), BF16),
        jax.ShapeDtypeStruct((n, w), F32),
        jax.ShapeDtypeStruct((n, MLA_HEADS * MLA_QK_PAD), BF16),
        jax.ShapeDtypeStruct((n, MLA_HEADS * MLA_NOPE_DIM), BF16),
        jax.ShapeDtypeStruct((n, LANES), BF16),
    )
    tk = ATTN_KV_BLOCK
    vt_shape = jax.ShapeDtypeStruct((n // tk, MLA_HEADS * MLA_V_DIM, tk), BF16)
    out_specs = tuple(pl.BlockSpec((tm, s.shape[1]), row) for s in out_shapes) + (
        pl.BlockSpec((tm // tk, MLA_HEADS * MLA_V_DIM, tk), lambda i: (i, 0, 0)),)
    q_scale = MLA_QK_DIM ** -0.5 * math.log2(math.e)
    outs = pl.pallas_call(
        functools.partial(_in_proj_kernel, q_scale=q_scale, n_cast=len(cast_weights)),
        out_shape=out_shapes + (vt_shape,) + cast_shapes,
        grid=(steps,),
        in_specs=[pl.BlockSpec((tm, dm), row), pl.BlockSpec((tm, 1), row),
                  full(gmix), full(win), full(qg), full(wuq), full(kvg), full(wuk), full(wuvt),
                  full(rc)] + cast_specs,
        out_specs=out_specs + tuple(cast_specs),
        compiler_params=pltpu.CompilerParams(
            dimension_semantics=("arbitrary",), vmem_limit_bytes=VMEM_LIMIT_BYTES),
        name="in_proj",
    )(x2, pos2, gmix, win, qg, wuq, kvg, wuk, wuvt, rc, *cast_weights)
    n_main = len(out_shapes) + 1
    return outs[:n_main], outs[n_main:]


def _retention_kernel(rq_ref, rk_ref, rv_ref, rg_ref, dmat_ref, qdec_ref, kdec_ref, out_ref,
                      state_ref, *, block_decay):
    @pl.when(pl.program_id(0) == 0)
    def _():
        state_ref[...] = jnp.zeros_like(state_ref)

    d = RET_HEAD_DIM
    blk = dmat_ref.shape[-1]
    nblk = rq_ref.shape[1] // blk
    seqs = range(rq_ref.shape[0])
    chains = [(t, b, hh) for t in range(nblk) for b in seqs for hh in range(RET_HEADS)]
    state = {(b, hh): state_ref[b, hh] for b in seqs for hh in range(RET_HEADS)}

    def front(c):
        t, b, hh = c
        rows = slice(t * blk, (t + 1) * blk)
        sl = slice(hh * d, (hh + 1) * d)
        q = rq_ref[b, rows, sl]
        v = rv_ref[b, rows, sl]
        kt = rk_ref[b, rows, sl].astype(F32).T
        st = state[b, hh]
        scores = _dot(q, kt.astype(BF16))
        carried = _dot(q, st.astype(BF16))
        update = _dot((kt * kdec_ref[hh][0:1, :]).astype(BF16), v)
        state[b, hh] = block_decay[hh] * st + update
        return scores, carried, v

    fronts = {0: front(chains[0])}
    for n, (t, b, hh) in enumerate(chains):
        if n + 1 < len(chains):
            fronts[n + 1] = front(chains[n + 1])
        scores, carried, v = fronts.pop(n)
        rows = slice(t * blk, (t + 1) * blk)
        sl = slice(hh * d, (hh + 1) * d)
        o = _dot((scores * dmat_ref[hh]).astype(BF16), v) + carried * qdec_ref[hh]
        mu = jnp.mean(o, axis=-1, keepdims=True)
        oc = o - mu
        var = jnp.mean(oc * oc, axis=-1, keepdims=True)
        out_ref[b, rows, sl] = (oc * lax.rsqrt(var + GN_EPS) * rg_ref[b, rows, sl]).astype(BF16)
    for (b, hh), st in state.items():
        state_ref[b, hh] = st


def _retention(rq, rk, rv, rg, batch, seq):
    n, w = rq.shape
    blk = RET_BLOCK
    nb = seq // blk
    log_g = np.log1p(-np.exp2(-5.0 - np.arange(RET_HEADS, dtype=np.float64)))
    idx = jnp.arange(blk, dtype=F32)
    lg = jnp.asarray(log_g, F32)
    dist = jnp.abs(idx[:, None] - idx[None, :])
    visible = (idx[None, :] // CHUNK) <= (idx[:, None] // CHUNK)
    dmat = jnp.where(visible[None], jnp.exp(lg[:, None, None] * dist[None]), 0.0)
    qdec = jnp.broadcast_to(jnp.exp(lg[:, None] * (idx + 1.0))[:, :, None], (RET_HEADS, blk, LANES))
    kdec = jnp.broadcast_to(jnp.exp(lg[:, None] * (blk - 1.0 - idx))[:, None, :], (RET_HEADS, 8, blk))
    block_decay = tuple(float(math.exp(g * blk)) for g in log_g)

    seq_block = pl.BlockSpec((batch, RET_STEP_BLOCKS * blk, w), lambda j: (0, j, 0))
    const3 = lambda j: (0, 0, 0)
    shaped = lambda a: a.reshape(batch, seq, w)
    out = pl.pallas_call(
        functools.partial(_retention_kernel, block_decay=block_decay),
        out_shape=jax.ShapeDtypeStruct((batch, seq, w), BF16),
        grid=(nb // RET_STEP_BLOCKS,),
        in_specs=[seq_block] * 4 + [
            pl.BlockSpec(dmat.shape, const3), pl.BlockSpec(qdec.shape, const3),
            pl.BlockSpec(kdec.shape, const3)],
        out_specs=seq_block,
        scratch_shapes=[pltpu.VMEM((batch, RET_HEADS, RET_HEAD_DIM, RET_HEAD_DIM), F32)],
        compiler_params=pltpu.CompilerParams(
            dimension_semantics=("arbitrary",), vmem_limit_bytes=VMEM_LIMIT_BYTES),
        name="retention",
    )(shaped(rq), shaped(rk), shaped(rv), shaped(rg), dmat, qdec, kdec)
    return out.reshape(n, w)


def _attention_kernel(q_ref, kn_ref, kr_ref, vt_ref, out_ref,
                      m_ref, l_ref, acc_ref, jump_ref, qt_ref):
    tq = q_ref.shape[0]
    gw = ATTN_Q_GROUP
    kb = ATTN_KEY_BLOCK
    tk = vt_ref.shape[-1]
    groups = range(tq // gw)
    diag_blocks = tq // kb
    i = pl.program_id(2)
    first_diag = i * diag_blocks

    def keys(j, size=kb):
        start = pl.multiple_of(j * kb, kb)
        return jnp.concatenate(
            [kn_ref[pl.ds(start, size), :], kr_ref[pl.ds(start, size), :]], axis=1)

    def values_t(j, size=kb):
        return jnp.concatenate([vt_ref[j * (kb // tk) + t] for t in range(size // tk)], axis=1)

    def chunk_mask(shape, key0, g):
        kc = (lax.broadcasted_iota(jnp.int32, shape, 0) + key0) // CHUNK
        qc = (lax.broadcasted_iota(jnp.int32, shape, 1) + g * gw) // CHUNK
        return kc <= qc

    def two_pass(g, k, vt, key0=None):
        cols = slice(g * gw, (g + 1) * gw)
        st = _dot_nt(k, q_ref[g * gw:(g + 1) * gw, :])
        if key0 is not None:
            st = jnp.where(chunk_mask(st.shape, key0, g), st, -1e30)
        m_old = m_ref[:, cols]
        m_new = jnp.maximum(m_old, jnp.max(st, axis=0, keepdims=True))
        alpha = jnp.exp2(m_old - m_new)
        p = jnp.exp2(st - m_new)
        l_ref[:, cols] = alpha * l_ref[:, cols] + jnp.sum(p, axis=0, keepdims=True)
        acc_ref[:, cols] = alpha * acc_ref[:, cols] + _dot(vt, p.astype(BF16))
        m_ref[:, cols] = m_new

    def single_pass(j, nblocks, items):
        k = jnp.concatenate([keys(j + t) for t in range(nblocks)], axis=0)
        vt = jnp.concatenate([values_t(j + t) for t in range(nblocks)], axis=1)
        used = sorted({g for g, _, _ in items})
        cols = {g: slice(g * gw, (g + 1) * gw) for g in used}
        qs = {g: qt_ref[:, cols[g]] for g in used}
        m_old = {g: m_ref[:, cols[g]] for g in used}
        l_new = {g: l_ref[:, cols[g]] for g in used}
        acc_new = {g: acc_ref[:, cols[g]] for g in used}
        smax = {}

        def qk(item):
            g, a, _ = item
            return _dot(k[a:a + ATTN_KEY_SLICE, :], qs[g])

        pending = [qk(item) for item in items[:ATTN_LOOKAHEAD]]
        for n, (g, a, key0) in enumerate(items):
            if n + ATTN_LOOKAHEAD < len(items):
                pending.append(qk(items[n + ATTN_LOOKAHEAD]))
            st = pending.pop(0)
            if key0 is not None:
                st = jnp.where(chunk_mask(st.shape, key0, g), st, -1e30)
            p = jnp.exp2(st - m_old[g])
            cmax = jnp.max(st, axis=0, keepdims=True)
            smax[g] = cmax if g not in smax else jnp.maximum(smax[g], cmax)
            l_new[g] = l_new[g] + jnp.sum(p, axis=0, keepdims=True)
            acc_new[g] = acc_new[g] + _dot(vt[:, a:a + ATTN_KEY_SLICE], p.astype(BF16))
        for g in used:
            m_new = jnp.maximum(m_old[g], smax[g])
            alpha = jnp.exp2(m_old[g] - m_new)
            l_ref[:, cols[g]] = alpha * l_new[g]
            acc_ref[:, cols[g]] = alpha * acc_new[g]
            m_ref[:, cols[g]] = m_new
            jump_ref[:, cols[g]] = jnp.maximum(jump_ref[:, cols[g]], smax[g] - m_old[g])

    full_items = [(g, a, None) for a in range(0, kb, ATTN_KEY_SLICE) for g in groups]
    diag_items = [(g, a, a if a >= g * gw else None)
                  for a in range(0, tq, ATTN_KEY_SLICE) for g in groups if a < (g + 1) * gw]

    def fast_body(j, carry):
        single_pass(j, 1, full_items)
        return carry

    def safe_body(j, carry):
        k = keys(j)
        vt = values_t(j)
        for g in groups:
            two_pass(g, k, vt)
        return carry

    qt_ref[...] = q_ref[...].astype(F32).T.astype(BF16)
    m_ref[...] = jnp.max(_dot_nt(keys(first_diag, CHUNK), q_ref[...]), axis=0, keepdims=True)
    l_ref[...] = jnp.zeros_like(l_ref)
    acc_ref[...] = jnp.zeros_like(acc_ref)
    jump_ref[...] = jnp.zeros_like(jump_ref)
    lax.fori_loop(0, first_diag, fast_body, 0)
    single_pass(first_diag, diag_blocks, diag_items)

    @pl.when(jnp.max(jump_ref[...]) > ATTN_MAX_JUMP)
    def _():
        m_ref[...] = jnp.full_like(m_ref, -1e30)
        l_ref[...] = jnp.zeros_like(l_ref)
        acc_ref[...] = jnp.zeros_like(acc_ref)
        lax.fori_loop(0, first_diag, safe_body, 0)
        for d in range(diag_blocks):
            for g in groups:
                visible = min(kb, (g + 1) * gw - d * kb)
                if visible > 0:
                    two_pass(g, keys(first_diag + d, visible), values_t(first_diag + d, visible),
                             d * kb)

    o = acc_ref[...] * (1.0 / l_ref[...])
    out_ref[...] = o.T.astype(BF16)


def _attention(q, kn, kr, vt, batch, seq):
    n = q.shape[0]
    tq = ATTN_Q_BLOCK
    tk = ATTN_KV_BLOCK
    nq = seq // tq
    return pl.pallas_call(
        _attention_kernel,
        out_shape=jax.ShapeDtypeStruct((n, MLA_HEADS * MLA_V_DIM), BF16),
        grid=(batch, MLA_HEADS, nq),
        in_specs=[pl.BlockSpec((tq, MLA_QK_PAD), lambda b, h, i: (b * nq + i, h)),
                  pl.BlockSpec((seq, MLA_NOPE_DIM), lambda b, h, i: (b, h)),
                  pl.BlockSpec((seq, LANES), lambda b, h, i: (b, 0)),
                  pl.BlockSpec((seq // tk, MLA_V_DIM, tk), lambda b, h, i: (b, h, 0))],
        out_specs=pl.BlockSpec((tq, MLA_V_DIM), lambda b, h, i: (b * nq + i, h)),
        scratch_shapes=[pltpu.VMEM((1, tq), F32), pltpu.VMEM((1, tq), F32),
                        pltpu.VMEM((MLA_V_DIM, tq), F32), pltpu.VMEM((1, tq), F32),
                        pltpu.VMEM((MLA_QK_PAD, tq), BF16)],
        compiler_params=pltpu.CompilerParams(
            dimension_semantics=("arbitrary", "arbitrary", "arbitrary"),
            vmem_limit_bytes=VMEM_LIMIT_BYTES),
        name="attention",
    )(q, kn, kr, vt)


def _out_ffn_kernel(x_ref, ret_ref, att_ref, p_ref, wo_ref, gffn_ref, wg_ref, wu_ref, wd_ref,
                    gple_ref, wpg_ref, wpp_ref, gfin_ref, out_ref, *, final_norm):
    tm = x_ref.shape[0]
    part = tm // OUT_PARTS
    parts = tuple(slice(i * part, (i + 1) * part) for i in range(OUT_PARTS))
    each = range(len(parts))
    o = [_dot(jnp.concatenate([ret_ref[r, :], att_ref[r, :]], axis=1), wo_ref[...]) for r in parts]
    pe = [_dot(p_ref[r, :].astype(BF16), wpp_ref[...]) for r in parts]
    x1 = [x_ref[r, :] + o[i] for i, r in enumerate(parts)]
    h = [_rms(x1[i], gffn_ref[...]).astype(BF16) for i in each]
    gu = [(_dot(h[i], wg_ref[...]), _dot(h[i], wu_ref[...])) for i in each]
    act = [(g * _sigmoid(g) * u).astype(BF16) for g, u in gu]
    x2 = [x1[i] + _dot(act[i], wd_ref[...]) for i in each]
    h2 = [_rms(x2[i], gple_ref[...]).astype(BF16) for i in each]
    gate = [_sigmoid(_dot(h2[i], wpg_ref[...])) for i in each]
    for i, r in enumerate(parts):
        x3 = x2[i] + gate[i] * pe[i]
        if final_norm:
            x3 = _rms(x3, gfin_ref[...])
        out_ref[r, :] = x3


def _out_ffn(x2, ret, att, p2, wo, gffn, wg, wu, wd, gple, wpg, wpp, gfin, final_norm):
    n, dm = x2.shape
    tm = OUT_ROWS
    row = lambda i: (i, 0)
    const = lambda i: (0, 0)
    resident = lambda a: pl.BlockSpec(a.shape, const, pipeline_mode=pl.Buffered(1))
    return pl.pallas_call(
        functools.partial(_out_ffn_kernel, final_norm=final_norm),
        out_shape=jax.ShapeDtypeStruct((n, dm), F32),
        grid=(n // tm,),
        in_specs=[pl.BlockSpec((tm, dm), row), pl.BlockSpec((tm, ret.shape[1]), row),
                  pl.BlockSpec((tm, att.shape[1]), row), pl.BlockSpec((tm, p2.shape[1]), row),
                  resident(wo), resident(gffn), resident(wg), resident(wu), resident(wd),
                  resident(gple), resident(wpg), resident(wpp), resident(gfin)],
        out_specs=pl.BlockSpec((tm, dm), row),
        compiler_params=pltpu.CompilerParams(
            dimension_semantics=("arbitrary",), vmem_limit_bytes=VMEM_LIMIT_BYTES),
        name="out_ffn",
    )(x2, ret, att, p2, wo, gffn, wg, wu, wd, gple, wpg, wpp, gfin)


def _prep_in_weights(w_in, w_uq, w_ukv):
    win = jnp.pad(w_in.astype(BF16), ((0, 0), (0, LANES - MLA_ROPE_DIM)))
    q_lora = w_uq.shape[0]
    uq = w_uq.reshape(q_lora, MLA_HEADS, MLA_QK_DIM)
    rope = uq[:, :, MLA_NOPE_DIM:]
    rot = jnp.concatenate([-rope[..., MLA_ROPE_DIM // 2:], rope[..., :MLA_ROPE_DIM // 2]], axis=-1)
    wuq = jnp.concatenate([uq, rot], axis=-1).reshape(q_lora, MLA_HEADS * MLA_QK_PAD)
    kv_lora = w_ukv.shape[0]
    ukv = w_ukv.reshape(kv_lora, MLA_HEADS, MLA_NOPE_DIM + MLA_V_DIM)
    wuk = ukv[:, :, :MLA_NOPE_DIM].reshape(kv_lora, -1)
    wuvt = ukv[:, :, MLA_NOPE_DIM:].reshape(kv_lora, -1).T
    return win, wuq.astype(BF16), wuk.astype(BF16), wuvt.astype(BF16)


def _rope_consts():
    def inv(half):
        return jnp.exp(-math.log(ROPE_BASE) * jnp.arange(half, dtype=F32) / half)
    inv_r = inv(RET_HEAD_DIM // 2)
    inv_m = inv(MLA_ROPE_DIM // 2)
    row = jnp.concatenate([inv_r, inv_m, jnp.zeros(LANES - inv_r.size - inv_m.size, F32)])
    return jnp.broadcast_to(row[None, :], (8, LANES))


def kernel(x, p, positions, mix_norm_g, w_in, q_norm_g, w_uq, kv_norm_g, w_ukv, w_o, ffn_norm_g,
           w_ffn_gate, w_ffn_up, w_ffn_down, ple_norm_g, w_ple_gate, w_ple_proj, final_norm_g):
    batch, seq, dm = x.shape
    depth = w_in.shape[0]
    n = batch * seq
    x2 = x.reshape(n, dm)
    pos2 = positions.reshape(n, 1)
    rc = _rope_consts()
    vec = lambda g: g.reshape(1, -1).astype(F32)
    for i in range(depth):
        win, wuq, wuk, wuvt = _prep_in_weights(w_in[i], w_uq[i], w_ukv[i])
        (rq, rk, rv, rg, q, kn, kr, vt), (wo, wg, wu, wd, wpg, wpp) = _in_proj(
            x2, pos2, vec(mix_norm_g[i]), win, vec(q_norm_g[i]), wuq, vec(kv_norm_g[i]), wuk, wuvt,
            rc, [w_o[i], w_ffn_gate[i], w_ffn_up[i], w_ffn_down[i], w_ple_gate[i], w_ple_proj[i]])
        ret = _retention(rq, rk, rv, rg, batch, seq)
        att = _attention(q, kn, kr, vt, batch, seq)
        x2 = _out_ffn(
            x2, ret, att, p[i].reshape(n, -1), wo, vec(ffn_norm_g[i]), wg, wu, wd,
            vec(ple_norm_g[i]), wpg, wpp, vec(final_norm_g), final_norm=(i == depth - 1))
    return x2.reshape(batch, seq, dm)
```

```python
import functools
import math

import numpy as np
import jax
import jax.numpy as jnp
from jax import lax
from jax.experimental import pallas as pl
from jax.experimental.pallas import tpu as pltpu

F32 = jnp.float32
BF16 = jnp.bfloat16

CHUNK = 64
RET_HEADS = 4
RET_HEAD_DIM = 128
MLA_HEADS = 4
MLA_NOPE_DIM = 128
MLA_ROPE_DIM = 64
MLA_V_DIM = 128
MLA_QK_DIM = MLA_NOPE_DIM + MLA_ROPE_DIM
ROPE_BASE = 10000.0
RMS_EPS = 1e-6
GN_EPS = 1e-5

LANES = 128
MLA_QK_PAD = 2 * LANES
VMEM_LIMIT_BYTES = 56 * 1024 * 1024

IN_PROJ_ROWS = 1024
IN_PROJ_PARTS = 4
RET_BLOCK = 256
RET_STEP_BLOCKS = 4
ATTN_Q_BLOCK = 2048
ATTN_Q_GROUP = 512
ATTN_KEY_BLOCK = 1024
ATTN_KV_BLOCK = 512
ATTN_MAX_JUMP = 32.0
ATTN_KEY_SLICE = 256
ATTN_LOOKAHEAD = 3
OUT_ROWS = 512
OUT_PARTS = 2


def _rms(x, g):
    return x * lax.rsqrt(jnp.mean(x * x, axis=-1, keepdims=True) + RMS_EPS) * g


def _sigmoid(x):
    return 1.0 / (1.0 + jnp.exp(-x))


def _dot(a, b):
    return jnp.dot(a, b, preferred_element_type=F32)


def _dot_nt(a, b):
    return lax.dot_general(a, b, (((1,), (1,)), ((), ())), preferred_element_type=F32)


def _rot_half_lanes(x):
    return pltpu.roll(x, LANES // 2, 1)


def _in_proj_kernel(*refs, q_scale, n_cast):
    (x_ref, pos_ref, gmix_ref, win_ref, qg_ref, wuq_ref, kvg_ref, wuk_ref, wuvt_ref,
     rc_ref) = refs[:10]
    cast_in = refs[10:10 + n_cast]
    rq_ref, rk_ref, rv_ref, rg_ref, q_ref, kn_ref, kr_ref, vt_ref = refs[10 + n_cast:18 + n_cast]
    cast_out = refs[18 + n_cast:-1]
    win_bf16_ref = refs[-1]

    @pl.when(pl.program_id(0) == 0)
    def _():
        cols = win_ref.shape[1]
        win_bf16_ref[:, cols // LANES * LANES:] = jnp.zeros(
            (win_ref.shape[0], win_bf16_ref.shape[1] - cols // LANES * LANES), BF16)
        win_bf16_ref[:, :cols] = win_ref[...].astype(BF16)

    for src, dst in zip(cast_in, cast_out):
        dst[...] = src[...].astype(BF16)

    tm = x_ref.shape[0]
    part = tm // IN_PROJ_PARTS
    parts = [slice(i * part, (i + 1) * part) for i in range(IN_PROJ_PARTS)]
    zs = [_dot(_rms(x_ref[r, :], gmix_ref[...]).astype(BF16), win_bf16_ref[...])
          for r in parts]

    lane = lax.broadcasted_iota(jnp.int32, (1, LANES), 1)
    half = LANES // 2
    quarter = LANES // 4
    d = RET_HEAD_DIM
    w = RET_HEADS * d
    k_scale = d ** -0.5
    o = 4 * w
    q_lora = qg_ref.shape[-1]
    kv_lora = kvg_ref.shape[-1]
    tk = vt_ref.shape[-1]
    vt_cols = min(tk, part)

    for r, z in zip(parts, zs):
        anchor = lax.bitcast_convert_type(z[:, 0:1], jnp.uint32)
        zero = lax.shift_right_logical(lax.shift_right_logical(anchor, jnp.uint32(16)), jnp.uint32(16))
        pos = (pos_ref[r, :] + zero.astype(jnp.int32)).astype(F32)
        ang = pos * rc_ref[0:1, :]
        c = jnp.cos(ang)
        s = jnp.sin(ang)
        c_hi, s_hi = _rot_half_lanes(c), _rot_half_lanes(s)
        cos_r = jnp.where(lane < half, c, c_hi)
        sin_r = jnp.where(lane < half, -s, s_hi)
        c_q3 = pltpu.roll(c, 3 * quarter, 1)
        s_q3 = pltpu.roll(s, 3 * quarter, 1)
        cos_m = jnp.where(lane < quarter, c_hi, jnp.where(lane < half, c_q3, 0.0))
        sin_m = jnp.where(lane < quarter, s_hi, jnp.where(lane < half, s_q3, 0.0))

        for hh in range(RET_HEADS):
            q = z[:, hh * d:(hh + 1) * d]
            rq_ref[r, hh * d:(hh + 1) * d] = (q * cos_r + _rot_half_lanes(q) * sin_r).astype(BF16)
            k = z[:, w + hh * d:w + (hh + 1) * d]
            rk_ref[r, hh * d:(hh + 1) * d] = (
                (k * cos_r + _rot_half_lanes(k) * sin_r) * k_scale).astype(BF16)
        rv_ref[r, :] = z[:, 2 * w:3 * w].astype(BF16)
        g = z[:, 3 * w:4 * w]
        rg_ref[r, :] = g * _sigmoid(g)

        cq = z[:, o:o + q_lora]
        ckv = z[:, o + q_lora:o + q_lora + kv_lora]

        def rope_low_half(y):
            rot = jnp.where(lane < quarter, -pltpu.roll(y, 3 * quarter, 1), pltpu.roll(y, quarter, 1))
            return y * cos_m + rot * sin_m

        kpe = z[:, o + q_lora + kv_lora:o + q_lora + kv_lora + LANES]
        kr_ref[r, :] = rope_low_half(kpe).astype(BF16)

        qq = _dot(_rms(cq, qg_ref[...]).astype(BF16), wuq_ref[...])
        for hh in range(MLA_HEADS):
            b = hh * MLA_QK_PAD
            q_ref[r, b:b + LANES] = (qq[:, b:b + LANES] * q_scale).astype(BF16)
            y = qq[:, b + LANES:b + 2 * LANES]
            q_ref[r, b + LANES:b + 2 * LANES] = (
                (y * cos_m + _rot_half_lanes(y) * sin_m) * q_scale).astype(BF16)

        ckvn = _rms(ckv, kvg_ref[...]).astype(BF16)
        kn_ref[r, :] = _dot(ckvn, wuk_ref[...]).astype(BF16)
        vt = _dot_nt(wuvt_ref[...], ckvn).astype(BF16)
        for a in range(r.start, r.stop, vt_cols):
            vt_ref[a // tk, :, a % tk:a % tk + vt_cols] = vt[:, a - r.start:a - r.start + vt_cols]


def _in_proj(x2, pos2, gmix, win, qg, wuq, kvg, wuk, wuvt, rc, cast_weights):
    n, dm = x2.shape
    tm = IN_PROJ_ROWS
    steps = n // tm
    bf16_rows = 16
    for a in cast_weights:
        assert a.shape[0] % (steps * bf16_rows) == 0, a.shape
    cast_specs = [pl.BlockSpec((a.shape[0] // steps, a.shape[1]), lambda i: (i, 0))
                  for a in cast_weights]
    cast_shapes = tuple(jax.ShapeDtypeStruct(a.shape, BF16) for a in cast_weights)
    w = RET_HEADS * RET_HEAD_DIM
    const = lambda i: (0, 0)
    row = lambda i: (i, 0)
    full = lambda a: pl.BlockSpec(a.shape, const)
    out_shapes = (
        jax.ShapeDtypeStruct((n, w), BF16),
        jax.ShapeDtypeStruct((n, w), BF16),
        jax.ShapeDtypeStruct((n, w), BF16),
        jax.ShapeDtypeStruct((n, w), F32),
        jax.ShapeDtypeStruct((n, MLA_HEADS * MLA_QK_PAD), BF16),
        jax.ShapeDtypeStruct((n, MLA_HEADS * MLA_NOPE_DIM), BF16),
        jax.ShapeDtypeStruct((n, LANES), BF16),
    )
    tk = ATTN_KV_BLOCK
    vt_shape = jax.ShapeDtypeStruct((n // tk, MLA_HEADS * MLA_V_DIM, tk), BF16)
    out_specs = tuple(pl.BlockSpec((tm, s.shape[1]), row) for s in out_shapes) + (
        pl.BlockSpec((tm // tk, MLA_HEADS * MLA_V_DIM, tk), lambda i: (i, 0, 0)),)
    q_scale = MLA_QK_DIM ** -0.5 * math.log2(math.e)
    outs = pl.pallas_call(
        functools.partial(_in_proj_kernel, q_scale=q_scale, n_cast=len(cast_weights)),
        out_shape=out_shapes + (vt_shape,) + cast_shapes,
        grid=(steps,),
        in_specs=[pl.BlockSpec((tm, dm), row), pl.BlockSpec((tm, 1), row),
                  full(gmix), pl.BlockSpec(win.shape, const, pipeline_mode=pl.Buffered(1)),
                  full(qg), full(wuq), full(kvg), full(wuk), full(wuvt),
                  full(rc)] + cast_specs,
        out_specs=out_specs + tuple(cast_specs),
        scratch_shapes=[pltpu.VMEM((win.shape[0], -(-win.shape[1] // LANES) * LANES), BF16)],
        compiler_params=pltpu.CompilerParams(
            dimension_semantics=("arbitrary",), vmem_limit_bytes=VMEM_LIMIT_BYTES),
        name="in_proj",
    )(x2, pos2, gmix, win, qg, wuq, kvg, wuk, wuvt, rc, *cast_weights)
    n_main = len(out_shapes) + 1
    return outs[:n_main], outs[n_main:]


def _retention_kernel(rq_ref, rk_ref, rv_ref, rg_ref, dmat_ref, qdec_ref, kdec_ref, out_ref,
                      state_ref, *, block_decay):
    @pl.when(pl.program_id(0) == 0)
    def _():
        state_ref[...] = jnp.zeros_like(state_ref)

    d = RET_HEAD_DIM
    blk = dmat_ref.shape[-1]
    nblk = rq_ref.shape[1] // blk
    seqs = range(rq_ref.shape[0])
    chains = [(t, b, hh) for t in range(nblk) for b in seqs for hh in range(RET_HEADS)]
    state = {(b, hh): state_ref[b, hh] for b in seqs for hh in range(RET_HEADS)}

    def front(c):
        t, b, hh = c
        rows = slice(t * blk, (t + 1) * blk)
        sl = slice(hh * d, (hh + 1) * d)
        q = rq_ref[b, rows, sl]
        v = rv_ref[b, rows, sl]
        kt = rk_ref[b, rows, sl].astype(F32).T
        st = state[b, hh]
        scores = _dot(q, kt.astype(BF16))
        carried = _dot(q, st.astype(BF16))
        update = _dot((kt * kdec_ref[hh][0:1, :]).astype(BF16), v)
        state[b, hh] = block_decay[hh] * st + update
        return scores, carried, v

    fronts = {0: front(chains[0])}
    for n, (t, b, hh) in enumerate(chains):
        if n + 1 < len(chains):
            fronts[n + 1] = front(chains[n + 1])
        scores, carried, v = fronts.pop(n)
        rows = slice(t * blk, (t + 1) * blk)
        sl = slice(hh * d, (hh + 1) * d)
        o = _dot((scores * dmat_ref[hh]).astype(BF16), v) + carried * qdec_ref[hh]
        mu = jnp.mean(o, axis=-1, keepdims=True)
        oc = o - mu
        var = jnp.mean(oc * oc, axis=-1, keepdims=True)
        out_ref[b, rows, sl] = (oc * lax.rsqrt(var + GN_EPS) * rg_ref[b, rows, sl]).astype(BF16)
    for (b, hh), st in state.items():
        state_ref[b, hh] = st


def _retention(rq, rk, rv, rg, batch, seq):
    n, w = rq.shape
    blk = RET_BLOCK
    nb = seq // blk
    log_g = np.log1p(-np.exp2(-5.0 - np.arange(RET_HEADS, dtype=np.float64)))
    idx = jnp.arange(blk, dtype=F32)
    lg = jnp.asarray(log_g, F32)
    dist = jnp.abs(idx[:, None] - idx[None, :])
    visible = (idx[None, :] // CHUNK) <= (idx[:, None] // CHUNK)
    dmat = jnp.where(visible[None], jnp.exp(lg[:, None, None] * dist[None]), 0.0)
    qdec = jnp.broadcast_to(jnp.exp(lg[:, None] * (idx + 1.0))[:, :, None], (RET_HEADS, blk, LANES))
    kdec = jnp.broadcast_to(jnp.exp(lg[:, None] * (blk - 1.0 - idx))[:, None, :], (RET_HEADS, 8, blk))
    block_decay = tuple(float(math.exp(g * blk)) for g in log_g)

    seq_block = pl.BlockSpec((batch, RET_STEP_BLOCKS * blk, w), lambda j: (0, j, 0))
    const3 = lambda j: (0, 0, 0)
    shaped = lambda a: a.reshape(batch, seq, w)
    out = pl.pallas_call(
        functools.partial(_retention_kernel, block_decay=block_decay),
        out_shape=jax.ShapeDtypeStruct((batch, seq, w), BF16),
        grid=(nb // RET_STEP_BLOCKS,),
        in_specs=[seq_block] * 4 + [
            pl.BlockSpec(dmat.shape, const3), pl.BlockSpec(qdec.shape, const3),
            pl.BlockSpec(kdec.shape, const3)],
        out_specs=seq_block,
        scratch_shapes=[pltpu.VMEM((batch, RET_HEADS, RET_HEAD_DIM, RET_HEAD_DIM), F32)],
        compiler_params=pltpu.CompilerParams(
            dimension_semantics=("arbitrary",), vmem_limit_bytes=VMEM_LIMIT_BYTES),
        name="retention",
    )(shaped(rq), shaped(rk), shaped(rv), shaped(rg), dmat, qdec, kdec)
    return out.reshape(n, w)


def _attention_kernel(q_ref, kn_ref, kr_ref, vt_ref, out_ref,
                      m_ref, l_ref, acc_ref, jump_ref, qt_ref):
    tq = q_ref.shape[0]
    gw = ATTN_Q_GROUP
    kb = ATTN_KEY_BLOCK
    tk = vt_ref.shape[-1]
    groups = range(tq // gw)
    diag_blocks = tq // kb
    i = pl.program_id(2)
    first_diag = i * diag_blocks

    def keys(j, size=kb):
        start = pl.multiple_of(j * kb, kb)
        return jnp.concatenate(
            [kn_ref[pl.ds(start, size), :], kr_ref[pl.ds(start, size), :]], axis=1)

    def values_t(j, size=kb):
        return jnp.concatenate([vt_ref[j * (kb // tk) + t] for t in range(size // tk)], axis=1)

    def chunk_mask(shape, key0, g):
        kc = (lax.broadcasted_iota(jnp.int32, shape, 0) + key0) // CHUNK
        qc = (lax.broadcasted_iota(jnp.int32, shape, 1) + g * gw) // CHUNK
        return kc <= qc

    def two_pass(g, k, vt, key0=None):
        cols = slice(g * gw, (g + 1) * gw)
        st = _dot_nt(k, q_ref[g * gw:(g + 1) * gw, :])
        if key0 is not None:
            st = jnp.where(chunk_mask(st.shape, key0, g), st, -1e30)
        m_old = m_ref[:, cols]
        m_new = jnp.maximum(m_old, jnp.max(st, axis=0, keepdims=True))
        alpha = jnp.exp2(m_old - m_new)
        p = jnp.exp2(st - m_new)
        l_ref[:, cols] = alpha * l_ref[:, cols] + jnp.sum(p, axis=0, keepdims=True)
        acc_ref[:, cols] = alpha * acc_ref[:, cols] + _dot(vt, p.astype(BF16))
        m_ref[:, cols] = m_new

    def single_pass(j, nblocks, items):
        k = jnp.concatenate([keys(j + t) for t in range(nblocks)], axis=0)
        vt = jnp.concatenate([values_t(j + t) for t in range(nblocks)], axis=1)
        used = sorted({g for g, _, _ in items})
        cols = {g: slice(g * gw, (g + 1) * gw) for g in used}
        qs = {g: qt_ref[:, cols[g]] for g in used}
        m_old = {g: m_ref[:, cols[g]] for g in used}
        l_new = {g: l_ref[:, cols[g]] for g in used}
        acc_new = {g: acc_ref[:, cols[g]] for g in used}
        smax = {}

        def qk(item):
            g, a, _ = item
            return _dot(k[a:a + ATTN_KEY_SLICE, :], qs[g])

        pending = [qk(item) for item in items[:ATTN_LOOKAHEAD]]
        for n, (g, a, key0) in enumerate(items):
            if n + ATTN_LOOKAHEAD < len(items):
                pending.append(qk(items[n + ATTN_LOOKAHEAD]))
            st = pending.pop(0)
            if key0 is not None:
                st = jnp.where(chunk_mask(st.shape, key0, g), st, -1e30)
            p = jnp.exp2(st - m_old[g])
            cmax = jnp.max(st, axis=0, keepdims=True)
            smax[g] = cmax if g not in smax else jnp.maximum(smax[g], cmax)
            l_new[g] = l_new[g] + jnp.sum(p, axis=0, keepdims=True)
            acc_new[g] = acc_new[g] + _dot(vt[:, a:a + ATTN_KEY_SLICE], p.astype(BF16))
        for g in used:
            m_new = jnp.maximum(m_old[g], smax[g])
            alpha = jnp.exp2(m_old[g] - m_new)
            l_ref[:, cols[g]] = alpha * l_new[g]
            acc_ref[:, cols[g]] = alpha * acc_new[g]
            m_ref[:, cols[g]] = m_new
            jump_ref[:, cols[g]] = jnp.maximum(jump_ref[:, cols[g]], smax[g] - m_old[g])

    full_items = [(g, a, None) for a in range(0, kb, ATTN_KEY_SLICE) for g in groups]
    diag_items = [(g, a, a if a >= g * gw else None)
                  for a in range(0, tq, ATTN_KEY_SLICE) for g in groups if a < (g + 1) * gw]

    def fast_body(j, carry):
        single_pass(j, 1, full_items)
        return carry

    def safe_body(j, carry):
        k = keys(j)
        vt = values_t(j)
        for g in groups:
            two_pass(g, k, vt)
        return carry

    qt_ref[...] = q_ref[...].astype(F32).T.astype(BF16)
    m_ref[...] = jnp.max(_dot_nt(keys(first_diag, CHUNK), q_ref[...]), axis=0, keepdims=True)
    l_ref[...] = jnp.zeros_like(l_ref)
    acc_ref[...] = jnp.zeros_like(acc_ref)
    jump_ref[...] = jnp.zeros_like(jump_ref)
    lax.fori_loop(0, first_diag, fast_body, 0)
    single_pass(first_diag, diag_blocks, diag_items)

    @pl.when(jnp.max(jump_ref[...]) > ATTN_MAX_JUMP)
    def _():
        m_ref[...] = jnp.full_like(m_ref, -1e30)
        l_ref[...] = jnp.zeros_like(l_ref)
        acc_ref[...] = jnp.zeros_like(acc_ref)
        lax.fori_loop(0, first_diag, safe_body, 0)
        for d in range(diag_blocks):
            for g in groups:
                visible = min(kb, (g + 1) * gw - d * kb)
                if visible > 0:
                    two_pass(g, keys(first_diag + d, visible), values_t(first_diag + d, visible),
                             d * kb)

    o = acc_ref[...] * (1.0 / l_ref[...])
    out_ref[...] = o.T.astype(BF16)


def _attention(q, kn, kr, vt, batch, seq):
    n = q.shape[0]
    tq = ATTN_Q_BLOCK
    tk = ATTN_KV_BLOCK
    nq = seq // tq
    return pl.pallas_call(
        _attention_kernel,
        out_shape=jax.ShapeDtypeStruct((n, MLA_HEADS * MLA_V_DIM), BF16),
        grid=(batch, MLA_HEADS, nq),
        in_specs=[pl.BlockSpec((tq, MLA_QK_PAD), lambda b, h, i: (b * nq + i, h)),
                  pl.BlockSpec((seq, MLA_NOPE_DIM), lambda b, h, i: (b, h)),
                  pl.BlockSpec((seq, LANES), lambda b, h, i: (b, 0)),
                  pl.BlockSpec((seq // tk, MLA_V_DIM, tk), lambda b, h, i: (b, h, 0))],
        out_specs=pl.BlockSpec((tq, MLA_V_DIM), lambda b, h, i: (b * nq + i, h)),
        scratch_shapes=[pltpu.VMEM((1, tq), F32), pltpu.VMEM((1, tq), F32),
                        pltpu.VMEM((MLA_V_DIM, tq), F32), pltpu.VMEM((1, tq), F32),
                        pltpu.VMEM((MLA_QK_PAD, tq), BF16)],
        compiler_params=pltpu.CompilerParams(
            dimension_semantics=("arbitrary", "arbitrary", "arbitrary"),
            vmem_limit_bytes=VMEM_LIMIT_BYTES),
        name="attention",
    )(q, kn, kr, vt)


def _out_ffn_kernel(x_ref, ret_ref, att_ref, p_ref, wo_ref, gffn_ref, wg_ref, wu_ref, wd_ref,
                    gple_ref, wpg_ref, wpp_ref, gfin_ref, out_ref, *, final_norm):
    tm = x_ref.shape[0]
    part = tm // OUT_PARTS
    parts = tuple(slice(i * part, (i + 1) * part) for i in range(OUT_PARTS))
    each = range(len(parts))
    o = [_dot(jnp.concatenate([ret_ref[r, :], att_ref[r, :]], axis=1), wo_ref[...]) for r in parts]
    pe = [_dot(p_ref[r, :].astype(BF16), wpp_ref[...]) for r in parts]
    x1 = [x_ref[r, :] + o[i] for i, r in enumerate(parts)]
    h = [_rms(x1[i], gffn_ref[...]).astype(BF16) for i in each]
    gu = [(_dot(h[i], wg_ref[...]), _dot(h[i], wu_ref[...])) for i in each]
    act = [(g * _sigmoid(g) * u).astype(BF16) for g, u in gu]
    x2 = [x1[i] + _dot(act[i], wd_ref[...]) for i in each]
    h2 = [_rms(x2[i], gple_ref[...]).astype(BF16) for i in each]
    gate = [_sigmoid(_dot(h2[i], wpg_ref[...])) for i in each]
    for i, r in enumerate(parts):
        x3 = x2[i] + gate[i] * pe[i]
        if final_norm:
            x3 = _rms(x3, gfin_ref[...])
        out_ref[r, :] = x3


def _out_ffn(x2, ret, att, p2, wo, gffn, wg, wu, wd, gple, wpg, wpp, gfin, final_norm):
    n, dm = x2.shape
    tm = OUT_ROWS
    row = lambda i: (i, 0)
    const = lambda i: (0, 0)
    resident = lambda a: pl.BlockSpec(a.shape, const, pipeline_mode=pl.Buffered(1))
    return pl.pallas_call(
        functools.partial(_out_ffn_kernel, final_norm=final_norm),
        out_shape=jax.ShapeDtypeStruct((n, dm), F32),
        grid=(n // tm,),
        in_specs=[pl.BlockSpec((tm, dm), row), pl.BlockSpec((tm, ret.shape[1]), row),
                  pl.BlockSpec((tm, att.shape[1]), row), pl.BlockSpec((tm, p2.shape[1]), row),
                  resident(wo), resident(gffn), resident(wg), resident(wu), resident(wd),
                  resident(gple), resident(wpg), resident(wpp), resident(gfin)],
        out_specs=pl.BlockSpec((tm, dm), row),
        compiler_params=pltpu.CompilerParams(
            dimension_semantics=("arbitrary",), vmem_limit_bytes=VMEM_LIMIT_BYTES),
        name="out_ffn",
    )(x2, ret, att, p2, wo, gffn, wg, wu, wd, gple, wpg, wpp, gfin)


def _prep_in_weights(w_uq, w_ukv):
    q_lora = w_uq.shape[0]
    uq = w_uq.reshape(q_lora, MLA_HEADS, MLA_QK_DIM)
    rope = uq[:, :, MLA_NOPE_DIM:]
    rot = jnp.concatenate([-rope[..., MLA_ROPE_DIM // 2:], rope[..., :MLA_ROPE_DIM // 2]], axis=-1)
    wuq = jnp.concatenate([uq, rot], axis=-1).reshape(q_lora, MLA_HEADS * MLA_QK_PAD)
    kv_lora = w_ukv.shape[0]
    ukv = w_ukv.reshape(kv_lora, MLA_HEADS, MLA_NOPE_DIM + MLA_V_DIM)
    wuk = ukv[:, :, :MLA_NOPE_DIM].reshape(kv_lora, -1)
    wuvt = ukv[:, :, MLA_NOPE_DIM:].reshape(kv_lora, -1).T
    return wuq.astype(BF16), wuk.astype(BF16), wuvt.astype(BF16)


def _rope_consts():
    def inv(half):
        return jnp.exp(-math.log(ROPE_BASE) * jnp.arange(half, dtype=F32) / half)
    inv_r = inv(RET_HEAD_DIM // 2)
    inv_m = inv(MLA_ROPE_DIM // 2)
    row = jnp.concatenate([inv_r, inv_m, jnp.zeros(LANES - inv_r.size - inv_m.size, F32)])
    return jnp.broadcast_to(row[None, :], (8, LANES))


def kernel(x, p, positions, mix_norm_g, w_in, q_norm_g, w_uq, kv_norm_g, w_ukv, w_o, ffn_norm_g,
           w_ffn_gate, w_ffn_up, w_ffn_down, ple_norm_g, w_ple_gate, w_ple_proj, final_norm_g):
    batch, seq, dm = x.shape
    depth = w_in.shape[0]
    n = batch * seq
    x2 = x.reshape(n, dm)
    pos2 = positions.reshape(n, 1)
    rc = _rope_consts()
    vec = lambda g: g.reshape(1, -1).astype(F32)
    for i in range(depth):
        wuq, wuk, wuvt = _prep_in_weights(w_uq[i], w_ukv[i])
        (rq, rk, rv, rg, q, kn, kr, vt), (wo, wg, wu, wd, wpg, wpp) = _in_proj(
            x2, pos2, vec(mix_norm_g[i]), w_in[i], vec(q_norm_g[i]), wuq, vec(kv_norm_g[i]), wuk, wuvt,
            rc, [w_o[i], w_ffn_gate[i], w_ffn_up[i], w_ffn_down[i], w_ple_gate[i], w_ple_proj[i]])
        ret = _retention(rq, rk, rv, rg, batch, seq)
        att = _attention(q, kn, kr, vt, batch, seq)
        x2 = _out_ffn(
            x2, ret, att, p[i].reshape(n, -1), wo, vec(ffn_norm_g[i]), wg, wu, wd,
            vec(ple_norm_g[i]), wpg, wpp, vec(final_norm_g), final_norm=(i == depth - 1))
    return x2.reshape(batch, seq, dm)
```

```python
import functools
import math

import numpy as np
import jax
import jax.numpy as jnp
from jax import lax
from jax.experimental import pallas as pl
from jax.experimental.pallas import tpu as pltpu

F32 = jnp.float32
BF16 = jnp.bfloat16

CHUNK = 64
RET_HEADS = 4
RET_HEAD_DIM = 128
MLA_HEADS = 4
MLA_NOPE_DIM = 128
MLA_ROPE_DIM = 64
MLA_V_DIM = 128
MLA_QK_DIM = MLA_NOPE_DIM + MLA_ROPE_DIM
ROPE_BASE = 10000.0
RMS_EPS = 1e-6
GN_EPS = 1e-5

LANES = 128
MLA_QK_PAD = 2 * LANES
VMEM_LIMIT_BYTES = 56 * 1024 * 1024

IN_PROJ_ROWS = 1024
IN_PROJ_PARTS = 4
RET_BLOCK = 256
RET_STEP_BLOCKS = 4
ATTN_Q_BLOCK = 2048
ATTN_Q_GROUP = 512
ATTN_KEY_BLOCK = 1024
ATTN_KV_BLOCK = 512
ATTN_MAX_JUMP = 32.0
ATTN_KEY_SLICE = 256
ATTN_LOOKAHEAD = 3
OUT_ROWS = 512
OUT_PARTS = 2


def _rms(x, g):
    return x * lax.rsqrt(jnp.mean(x * x, axis=-1, keepdims=True) + RMS_EPS) * g


def _sigmoid(x):
    return 1.0 / (1.0 + jnp.exp(-x))


def _dot(a, b):
    return jnp.dot(a, b, preferred_element_type=F32)


def _dot_nt(a, b):
    return lax.dot_general(a, b, (((1,), (1,)), ((), ())), preferred_element_type=F32)


def _rot_half_lanes(x):
    return pltpu.roll(x, LANES // 2, 1)


def _in_proj_kernel(*refs, q_scale, n_cast):
    (x_ref, pos_ref, gmix_ref, win_ref, qg_ref, wuq_ref, kvg_ref, wuk_ref, wuvt_ref,
     rc_ref) = refs[:10]
    cast_in = refs[10:10 + n_cast]
    rq_ref, rk_ref, rv_ref, rg_ref, q_ref, kn_ref, kr_ref, vt_ref = refs[10 + n_cast:18 + n_cast]
    cast_out = refs[18 + n_cast:-1]
    win_bf16_ref = refs[-1]

    @pl.when(pl.program_id(0) == 0)
    def _():
        n_out = win_ref.shape[0]
        for a in range(0, n_out, LANES):
            rows = min(LANES, n_out - a)
            blk = win_ref[a:a + rows, :]
            if rows < LANES:
                blk = jnp.concatenate([blk, jnp.zeros((LANES - rows, blk.shape[1]), F32)], axis=0)
            win_bf16_ref[:, a:a + LANES] = blk.T.astype(BF16)

    for src, dst in zip(cast_in, cast_out):
        dst[...] = src[...].astype(BF16)

    tm = x_ref.shape[0]
    part = tm // IN_PROJ_PARTS
    parts = [slice(i * part, (i + 1) * part) for i in range(IN_PROJ_PARTS)]
    zs = [_dot(_rms(x_ref[r, :], gmix_ref[...]).astype(BF16), win_bf16_ref[...])
          for r in parts]

    lane = lax.broadcasted_iota(jnp.int32, (1, LANES), 1)
    half = LANES // 2
    quarter = LANES // 4
    d = RET_HEAD_DIM
    w = RET_HEADS * d
    k_scale = d ** -0.5
    o = 4 * w
    q_lora = qg_ref.shape[-1]
    kv_lora = kvg_ref.shape[-1]
    tk = vt_ref.shape[-1]
    vt_cols = min(tk, part)
    pos_rows = pos_ref[...].astype(F32)
    pos_cols = jnp.concatenate(
        [pos_rows, jnp.zeros((LANES - pos_rows.shape[0], LANES), F32)], axis=0).T

    for r, z in zip(parts, zs):
        anchor = lax.bitcast_convert_type(z[:, 0:1], jnp.uint32)
        zero = lax.shift_right_logical(lax.shift_right_logical(anchor, jnp.uint32(16)), jnp.uint32(16))
        pos = jnp.concatenate([pos_cols[:, c:c + 1] for c in range(r.start // LANES, r.stop // LANES)],
                              axis=0) + zero.astype(F32)
        ang = pos * rc_ref[0:1, :]
        c = jnp.cos(ang)
        s = jnp.sin(ang)
        c_hi, s_hi = _rot_half_lanes(c), _rot_half_lanes(s)
        cos_r = jnp.where(lane < half, c, c_hi)
        sin_r = jnp.where(lane < half, -s, s_hi)
        c_q3 = pltpu.roll(c, 3 * quarter, 1)
        s_q3 = pltpu.roll(s, 3 * quarter, 1)
        cos_m = jnp.where(lane < quarter, c_hi, jnp.where(lane < half, c_q3, 0.0))
        sin_m = jnp.where(lane < quarter, s_hi, jnp.where(lane < half, s_q3, 0.0))

        for hh in range(RET_HEADS):
            q = z[:, hh * d:(hh + 1) * d]
            rq_ref[r, hh * d:(hh + 1) * d] = (q * cos_r + _rot_half_lanes(q) * sin_r).astype(BF16)
            k = z[:, w + hh * d:w + (hh + 1) * d]
            rk_ref[r, hh * d:(hh + 1) * d] = (
                (k * cos_r + _rot_half_lanes(k) * sin_r) * k_scale).astype(BF16)
        rv_ref[r, :] = z[:, 2 * w:3 * w].astype(BF16)
        g = z[:, 3 * w:4 * w]
        rg_ref[r, :] = g * _sigmoid(g)

        cq = z[:, o:o + q_lora]
        ckv = z[:, o + q_lora:o + q_lora + kv_lora]

        def rope_low_half(y):
            rot = jnp.where(lane < quarter, -pltpu.roll(y, 3 * quarter, 1), pltpu.roll(y, quarter, 1))
            return y * cos_m + rot * sin_m

        kpe = z[:, o + q_lora + kv_lora:o + q_lora + kv_lora + LANES]
        kr_ref[r, :] = rope_low_half(kpe).astype(BF16)

        qq = _dot(_rms(cq, qg_ref[...]).astype(BF16), wuq_ref[...])
        for hh in range(MLA_HEADS):
            b = hh * MLA_QK_PAD
            q_ref[r, b:b + LANES] = (qq[:, b:b + LANES] * q_scale).astype(BF16)
            y = qq[:, b + LANES:b + 2 * LANES]
            q_ref[r, b + LANES:b + 2 * LANES] = (
                (y * cos_m + _rot_half_lanes(y) * sin_m) * q_scale).astype(BF16)

        ckvn = _rms(ckv, kvg_ref[...]).astype(BF16)
        kn_ref[r, :] = _dot(ckvn, wuk_ref[...]).astype(BF16)
        vt = _dot_nt(wuvt_ref[...], ckvn).astype(BF16)
        for a in range(r.start, r.stop, vt_cols):
            vt_ref[a // tk, :, a % tk:a % tk + vt_cols] = vt[:, a - r.start:a - r.start + vt_cols]


def _in_proj(x2, pos2, gmix, win, qg, wuq, kvg, wuk, wuvt, rc, cast_weights):
    n, dm = x2.shape
    tm = IN_PROJ_ROWS
    steps = n // tm
    bf16_rows = 16
    for a in cast_weights:
        assert a.shape[0] % (steps * bf16_rows) == 0, a.shape
    cast_specs = [pl.BlockSpec((a.shape[0] // steps, a.shape[1]), lambda i: (i, 0))
                  for a in cast_weights]
    cast_shapes = tuple(jax.ShapeDtypeStruct(a.shape, BF16) for a in cast_weights)
    w = RET_HEADS * RET_HEAD_DIM
    const = lambda i: (0, 0)
    row = lambda i: (i, 0)
    full = lambda a: pl.BlockSpec(a.shape, const)
    out_shapes = (
        jax.ShapeDtypeStruct((n, w), BF16),
        jax.ShapeDtypeStruct((n, w), BF16),
        jax.ShapeDtypeStruct((n, w), BF16),
        jax.ShapeDtypeStruct((n, w), F32),
        jax.ShapeDtypeStruct((n, MLA_HEADS * MLA_QK_PAD), BF16),
        jax.ShapeDtypeStruct((n, MLA_HEADS * MLA_NOPE_DIM), BF16),
        jax.ShapeDtypeStruct((n, LANES), BF16),
    )
    tk = ATTN_KV_BLOCK
    vt_shape = jax.ShapeDtypeStruct((n // tk, MLA_HEADS * MLA_V_DIM, tk), BF16)
    out_specs = tuple(pl.BlockSpec((tm, s.shape[1]), row) for s in out_shapes) + (
        pl.BlockSpec((tm // tk, MLA_HEADS * MLA_V_DIM, tk), lambda i: (i, 0, 0)),)
    q_scale = MLA_QK_DIM ** -0.5 * math.log2(math.e)
    outs = pl.pallas_call(
        functools.partial(_in_proj_kernel, q_scale=q_scale, n_cast=len(cast_weights)),
        out_shape=out_shapes + (vt_shape,) + cast_shapes,
        grid=(steps,),
        in_specs=[pl.BlockSpec((tm, dm), row), pl.BlockSpec((tm // LANES, LANES), row),
                  full(gmix), pl.BlockSpec(win.shape, const, pipeline_mode=pl.Buffered(1)),
                  full(qg), full(wuq), full(kvg), full(wuk), full(wuvt),
                  full(rc)] + cast_specs,
        out_specs=out_specs + tuple(cast_specs),
        scratch_shapes=[pltpu.VMEM((win.shape[1], -(-win.shape[0] // LANES) * LANES), BF16)],
        compiler_params=pltpu.CompilerParams(
            dimension_semantics=("arbitrary",), vmem_limit_bytes=VMEM_LIMIT_BYTES),
        name="in_proj",
    )(x2, pos2, gmix, win, qg, wuq, kvg, wuk, wuvt, rc, *cast_weights)
    n_main = len(out_shapes) + 1
    return outs[:n_main], outs[n_main:]


def _retention_kernel(rq_ref, rk_ref, rv_ref, rg_ref, dmat_ref, qdec_ref, kdec_ref, out_ref,
                      state_ref, *, block_decay):
    @pl.when(pl.program_id(0) == 0)
    def _():
        state_ref[...] = jnp.zeros_like(state_ref)

    d = RET_HEAD_DIM
    blk = dmat_ref.shape[-1]
    nblk = rq_ref.shape[1] // blk
    seqs = range(rq_ref.shape[0])
    chains = [(t, b, hh) for t in range(nblk) for b in seqs for hh in range(RET_HEADS)]
    state = {(b, hh): state_ref[b, hh] for b in seqs for hh in range(RET_HEADS)}

    def front(c):
        t, b, hh = c
        rows = slice(t * blk, (t + 1) * blk)
        sl = slice(hh * d, (hh + 1) * d)
        q = rq_ref[b, rows, sl]
        v = rv_ref[b, rows, sl]
        kt = rk_ref[b, rows, sl].astype(F32).T
        st = state[b, hh]
        scores = _dot(q, kt.astype(BF16))
        carried = _dot(q, st.astype(BF16))
        update = _dot((kt * kdec_ref[hh][0:1, :]).astype(BF16), v)
        state[b, hh] = block_decay[hh] * st + update
        return scores, carried, v

    fronts = {0: front(chains[0])}
    for n, (t, b, hh) in enumerate(chains):
        if n + 1 < len(chains):
            fronts[n + 1] = front(chains[n + 1])
        scores, carried, v = fronts.pop(n)
        rows = slice(t * blk, (t + 1) * blk)
        sl = slice(hh * d, (hh + 1) * d)
        o = _dot((scores * dmat_ref[hh]).astype(BF16), v) + carried * qdec_ref[hh]
        mu = jnp.mean(o, axis=-1, keepdims=True)
        oc = o - mu
        var = jnp.mean(oc * oc, axis=-1, keepdims=True)
        out_ref[b, rows, sl] = (oc * lax.rsqrt(var + GN_EPS) * rg_ref[b, rows, sl]).astype(BF16)
    for (b, hh), st in state.items():
        state_ref[b, hh] = st


def _retention(rq, rk, rv, rg, batch, seq):
    n, w = rq.shape
    blk = RET_BLOCK
    nb = seq // blk
    log_g = np.log1p(-np.exp2(-5.0 - np.arange(RET_HEADS, dtype=np.float64)))
    idx = jnp.arange(blk, dtype=F32)
    lg = jnp.asarray(log_g, F32)
    dist = jnp.abs(idx[:, None] - idx[None, :])
    visible = (idx[None, :] // CHUNK) <= (idx[:, None] // CHUNK)
    dmat = jnp.where(visible[None], jnp.exp(lg[:, None, None] * dist[None]), 0.0)
    qdec = jnp.broadcast_to(jnp.exp(lg[:, None] * (idx + 1.0))[:, :, None], (RET_HEADS, blk, LANES))
    kdec = jnp.broadcast_to(jnp.exp(lg[:, None] * (blk - 1.0 - idx))[:, None, :], (RET_HEADS, 8, blk))
    block_decay = tuple(float(math.exp(g * blk)) for g in log_g)

    seq_block = pl.BlockSpec((batch, RET_STEP_BLOCKS * blk, w), lambda j: (0, j, 0))
    const3 = lambda j: (0, 0, 0)
    shaped = lambda a: a.reshape(batch, seq, w)
    out = pl.pallas_call(
        functools.partial(_retention_kernel, block_decay=block_decay),
        out_shape=jax.ShapeDtypeStruct((batch, seq, w), BF16),
        grid=(nb // RET_STEP_BLOCKS,),
        in_specs=[seq_block] * 4 + [
            pl.BlockSpec(dmat.shape, const3), pl.BlockSpec(qdec.shape, const3),
            pl.BlockSpec(kdec.shape, const3)],
        out_specs=seq_block,
        scratch_shapes=[pltpu.VMEM((batch, RET_HEADS, RET_HEAD_DIM, RET_HEAD_DIM), F32)],
        compiler_params=pltpu.CompilerParams(
            dimension_semantics=("arbitrary",), vmem_limit_bytes=VMEM_LIMIT_BYTES),
        name="retention",
    )(shaped(rq), shaped(rk), shaped(rv), shaped(rg), dmat, qdec, kdec)
    return out.reshape(n, w)


def _attention_kernel(q_ref, kn_ref, kr_ref, vt_ref, out_ref,
                      m_ref, l_ref, acc_ref, jump_ref, qt_ref):
    tq = q_ref.shape[0]
    gw = ATTN_Q_GROUP
    kb = ATTN_KEY_BLOCK
    tk = vt_ref.shape[-1]
    groups = range(tq // gw)
    diag_blocks = tq // kb
    i = pl.program_id(2)
    first_diag = i * diag_blocks

    def keys(j, size=kb):
        start = pl.multiple_of(j * kb, kb)
        return jnp.concatenate(
            [kn_ref[pl.ds(start, size), :], kr_ref[pl.ds(start, size), :]], axis=1)

    def values_t(j, size=kb):
        return jnp.concatenate([vt_ref[j * (kb // tk) + t] for t in range(size // tk)], axis=1)

    def chunk_mask(shape, key0, g):
        kc = (lax.broadcasted_iota(jnp.int32, shape, 0) + key0) // CHUNK
        qc = (lax.broadcasted_iota(jnp.int32, shape, 1) + g * gw) // CHUNK
        return kc <= qc

    def two_pass(g, k, vt, key0=None):
        cols = slice(g * gw, (g + 1) * gw)
        st = _dot_nt(k, q_ref[g * gw:(g + 1) * gw, :])
        if key0 is not None:
            st = jnp.where(chunk_mask(st.shape, key0, g), st, -1e30)
        m_old = m_ref[:, cols]
        m_new = jnp.maximum(m_old, jnp.max(st, axis=0, keepdims=True))
        alpha = jnp.exp2(m_old - m_new)
        p = jnp.exp2(st - m_new)
        l_ref[:, cols] = alpha * l_ref[:, cols] + jnp.sum(p, axis=0, keepdims=True)
        acc_ref[:, cols] = alpha * acc_ref[:, cols] + _dot(vt, p.astype(BF16))
        m_ref[:, cols] = m_new

    def single_pass(j, nblocks, items):
        k = jnp.concatenate([keys(j + t) for t in range(nblocks)], axis=0)
        vt = jnp.concatenate([values_t(j + t) for t in range(nblocks)], axis=1)
        used = sorted({g for g, _, _ in items})
        cols = {g: slice(g * gw, (g + 1) * gw) for g in used}
        qs = {g: qt_ref[:, cols[g]] for g in used}
        m_old = {g: m_ref[:, cols[g]] for g in used}
        l_new = {g: l_ref[:, cols[g]] for g in used}
        acc_new = {g: acc_ref[:, cols[g]] for g in used}
        smax = {}

        def qk(item):
            g, a, _ = item
            return _dot(k[a:a + ATTN_KEY_SLICE, :], qs[g])

        pending = [qk(item) for item in items[:ATTN_LOOKAHEAD]]
        for n, (g, a, key0) in enumerate(items):
            if n + ATTN_LOOKAHEAD < len(items):
                pending.append(qk(items[n + ATTN_LOOKAHEAD]))
            st = pending.pop(0)
            if key0 is not None:
                st = jnp.where(chunk_mask(st.shape, key0, g), st, -1e30)
            p = jnp.exp2(st - m_old[g])
            cmax = jnp.max(st, axis=0, keepdims=True)
            smax[g] = cmax if g not in smax else jnp.maximum(smax[g], cmax)
            l_new[g] = l_new[g] + jnp.sum(p, axis=0, keepdims=True)
            acc_new[g] = acc_new[g] + _dot(vt[:, a:a + ATTN_KEY_SLICE], p.astype(BF16))
        for g in used:
            m_new = jnp.maximum(m_old[g], smax[g])
            alpha = jnp.exp2(m_old[g] - m_new)
            l_ref[:, cols[g]] = alpha * l_new[g]
            acc_ref[:, cols[g]] = alpha * acc_new[g]
            m_ref[:, cols[g]] = m_new
            jump_ref[:, cols[g]] = jnp.maximum(jump_ref[:, cols[g]], smax[g] - m_old[g])

    full_items = [(g, a, None) for a in range(0, kb, ATTN_KEY_SLICE) for g in groups]
    diag_items = [(g, a, a if a >= g * gw else None)
                  for a in range(0, tq, ATTN_KEY_SLICE) for g in groups if a < (g + 1) * gw]

    def fast_body(j, carry):
        single_pass(j, 1, full_items)
        return carry

    def safe_body(j, carry):
        k = keys(j)
        vt = values_t(j)
        for g in groups:
            two_pass(g, k, vt)
        return carry

    qt_ref[...] = q_ref[...].astype(F32).T.astype(BF16)
    m_ref[...] = jnp.max(_dot_nt(keys(first_diag, CHUNK), q_ref[...]), axis=0, keepdims=True)
    l_ref[...] = jnp.zeros_like(l_ref)
    acc_ref[...] = jnp.zeros_like(acc_ref)
    jump_ref[...] = jnp.zeros_like(jump_ref)
    lax.fori_loop(0, first_diag, fast_body, 0)
    single_pass(first_diag, diag_blocks, diag_items)

    @pl.when(jnp.max(jump_ref[...]) > ATTN_MAX_JUMP)
    def _():
        m_ref[...] = jnp.full_like(m_ref, -1e30)
        l_ref[...] = jnp.zeros_like(l_ref)
        acc_ref[...] = jnp.zeros_like(acc_ref)
        lax.fori_loop(0, first_diag, safe_body, 0)
        for d in range(diag_blocks):
            for g in groups:
                visible = min(kb, (g + 1) * gw - d * kb)
                if visible > 0:
                    two_pass(g, keys(first_diag + d, visible), values_t(first_diag + d, visible),
                             d * kb)

    o = acc_ref[...] * (1.0 / l_ref[...])
    out_ref[...] = o.T.astype(BF16)


def _attention(q, kn, kr, vt, batch, seq):
    n = q.shape[0]
    tq = ATTN_Q_BLOCK
    tk = ATTN_KV_BLOCK
    nq = seq // tq
    return pl.pallas_call(
        _attention_kernel,
        out_shape=jax.ShapeDtypeStruct((n, MLA_HEADS * MLA_V_DIM), BF16),
        grid=(batch, MLA_HEADS, nq),
        in_specs=[pl.BlockSpec((tq, MLA_QK_PAD), lambda b, h, i: (b * nq + i, h)),
                  pl.BlockSpec((seq, MLA_NOPE_DIM), lambda b, h, i: (b, h)),
                  pl.BlockSpec((seq, LANES), lambda b, h, i: (b, 0)),
                  pl.BlockSpec((seq // tk, MLA_V_DIM, tk), lambda b, h, i: (b, h, 0))],
        out_specs=pl.BlockSpec((tq, MLA_V_DIM), lambda b, h, i: (b * nq + i, h)),
        scratch_shapes=[pltpu.VMEM((1, tq), F32), pltpu.VMEM((1, tq), F32),
                        pltpu.VMEM((MLA_V_DIM, tq), F32), pltpu.VMEM((1, tq), F32),
                        pltpu.VMEM((MLA_QK_PAD, tq), BF16)],
        compiler_params=pltpu.CompilerParams(
            dimension_semantics=("arbitrary", "arbitrary", "arbitrary"),
            vmem_limit_bytes=VMEM_LIMIT_BYTES),
        name="attention",
    )(q, kn, kr, vt)


def _out_ffn_kernel(x_ref, ret_ref, att_ref, p_ref, wo_ref, gffn_ref, wg_ref, wu_ref, wd_ref,
                    gple_ref, wpg_ref, wpp_ref, gfin_ref, out_ref, *, final_norm):
    tm = x_ref.shape[0]
    part = tm // OUT_PARTS
    parts = tuple(slice(i * part, (i + 1) * part) for i in range(OUT_PARTS))
    each = range(len(parts))
    o = [_dot(jnp.concatenate([ret_ref[r, :], att_ref[r, :]], axis=1), wo_ref[...]) for r in parts]
    pe = [_dot(p_ref[r, :].astype(BF16), wpp_ref[...]) for r in parts]
    x1 = [x_ref[r, :] + o[i] for i, r in enumerate(parts)]
    h = [_rms(x1[i], gffn_ref[...]).astype(BF16) for i in each]
    gu = [(_dot(h[i], wg_ref[...]), _dot(h[i], wu_ref[...])) for i in each]
    act = [(g * _sigmoid(g) * u).astype(BF16) for g, u in gu]
    x2 = [x1[i] + _dot(act[i], wd_ref[...]) for i in each]
    h2 = [_rms(x2[i], gple_ref[...]).astype(BF16) for i in each]
    gate = [_sigmoid(_dot(h2[i], wpg_ref[...])) for i in each]
    for i, r in enumerate(parts):
        x3 = x2[i] + gate[i] * pe[i]
        if final_norm:
            x3 = _rms(x3, gfin_ref[...])
        out_ref[r, :] = x3


def _out_ffn(x2, ret, att, p2, wo, gffn, wg, wu, wd, gple, wpg, wpp, gfin, final_norm):
    n, dm = x2.shape
    tm = OUT_ROWS
    row = lambda i: (i, 0)
    const = lambda i: (0, 0)
    resident = lambda a: pl.BlockSpec(a.shape, const, pipeline_mode=pl.Buffered(1))
    return pl.pallas_call(
        functools.partial(_out_ffn_kernel, final_norm=final_norm),
        out_shape=jax.ShapeDtypeStruct((n, dm), F32),
        grid=(n // tm,),
        in_specs=[pl.BlockSpec((tm, dm), row), pl.BlockSpec((tm, ret.shape[1]), row),
                  pl.BlockSpec((tm, att.shape[1]), row), pl.BlockSpec((tm, p2.shape[1]), row),
                  resident(wo), resident(gffn), resident(wg), resident(wu), resident(wd),
                  resident(gple), resident(wpg), resident(wpp), resident(gfin)],
        out_specs=pl.BlockSpec((tm, dm), row),
        compiler_params=pltpu.CompilerParams(
            dimension_semantics=("arbitrary",), vmem_limit_bytes=VMEM_LIMIT_BYTES),
        name="out_ffn",
    )(x2, ret, att, p2, wo, gffn, wg, wu, wd, gple, wpg, wpp, gfin)


def _prep_in_weights(w_uq, w_ukv):
    q_lora = w_uq.shape[0]
    uq = w_uq.reshape(q_lora, MLA_HEADS, MLA_QK_DIM)
    rope = uq[:, :, MLA_NOPE_DIM:]
    rot = jnp.concatenate([-rope[..., MLA_ROPE_DIM // 2:], rope[..., :MLA_ROPE_DIM // 2]], axis=-1)
    wuq = jnp.concatenate([uq, rot], axis=-1).reshape(q_lora, MLA_HEADS * MLA_QK_PAD)
    kv_lora = w_ukv.shape[0]
    ukv = w_ukv.reshape(kv_lora, MLA_HEADS, MLA_NOPE_DIM + MLA_V_DIM)
    wuk = ukv[:, :, :MLA_NOPE_DIM].reshape(kv_lora, -1)
    wuvt = ukv[:, :, MLA_NOPE_DIM:].reshape(kv_lora, -1).T
    return wuq.astype(BF16), wuk.astype(BF16), wuvt.astype(BF16)


def _rope_consts():
    def inv(half):
        return jnp.exp(-math.log(ROPE_BASE) * jnp.arange(half, dtype=F32) / half)
    inv_r = inv(RET_HEAD_DIM // 2)
    inv_m = inv(MLA_ROPE_DIM // 2)
    row = jnp.concatenate([inv_r, inv_m, jnp.zeros(LANES - inv_r.size - inv_m.size, F32)])
    return jnp.broadcast_to(row[None, :], (8, LANES))


def kernel(x, p, positions, mix_norm_g, w_in, q_norm_g, w_uq, kv_norm_g, w_ukv, w_o, ffn_norm_g,
           w_ffn_gate, w_ffn_up, w_ffn_down, ple_norm_g, w_ple_gate, w_ple_proj, final_norm_g):
    batch, seq, dm = x.shape
    depth = w_in.shape[0]
    n = batch * seq
    x2 = x.reshape(n, dm)
    pos2 = positions.reshape(n // LANES, LANES)
    rc = _rope_consts()
    vec = lambda g: g.reshape(1, -1).astype(F32)
    for i in range(depth):
        wuq, wuk, wuvt = _prep_in_weights(w_uq[i], w_ukv[i])
        (rq, rk, rv, rg, q, kn, kr, vt), (wo, wg, wu, wd, wpg, wpp) = _in_proj(
            x2, pos2, vec(mix_norm_g[i]), w_in[i].T, vec(q_norm_g[i]), wuq, vec(kv_norm_g[i]), wuk,
            wuvt,
            rc, [w_o[i], w_ffn_gate[i], w_ffn_up[i], w_ffn_down[i], w_ple_gate[i], w_ple_proj[i]])
        ret = _retention(rq, rk, rv, rg, batch, seq)
        att = _attention(q, kn, kr, vt, batch, seq)
        x2 = _out_ffn(
            x2, ret, att, p[i].reshape(n, -1), wo, vec(ffn_norm_g[i]), wg, wu, wd,
            vec(ple_norm_g[i]), wpg, wpp, vec(final_norm_g), final_norm=(i == depth - 1))
    return x2.reshape(batch, seq, dm)
```

```python
import functools
import math

import numpy as np
import jax
import jax.numpy as jnp
from jax import lax
from jax.experimental import pallas as pl
from jax.experimental.pallas import tpu as pltpu

F32 = jnp.float32
BF16 = jnp.bfloat16

CHUNK = 64
RET_HEADS = 4
RET_HEAD_DIM = 128
MLA_HEADS = 4
MLA_NOPE_DIM = 128
MLA_ROPE_DIM = 64
MLA_V_DIM = 128
MLA_QK_DIM = MLA_NOPE_DIM + MLA_ROPE_DIM
ROPE_BASE = 10000.0
RMS_EPS = 1e-6
GN_EPS = 1e-5

LANES = 128
MLA_QK_PAD = 2 * LANES
VMEM_LIMIT_BYTES = 56 * 1024 * 1024

IN_PROJ_ROWS = 1024
IN_PROJ_PARTS = 4
RET_BLOCK = 256
RET_STEP_BLOCKS = 4
ATTN_Q_BLOCK = 2048
ATTN_Q_GROUP = 512
ATTN_KEY_BLOCK = 1024
ATTN_KV_BLOCK = 512
ATTN_MAX_JUMP = 32.0
ATTN_KEY_SLICE = 256
ATTN_LOOKAHEAD = 3
OUT_ROWS = 512
OUT_PARTS = 2


def _rms(x, g):
    return x * lax.rsqrt(jnp.mean(x * x, axis=-1, keepdims=True) + RMS_EPS) * g


def _sigmoid(x):
    return 1.0 / (1.0 + jnp.exp(-x))


def _dot(a, b):
    return jnp.dot(a, b, preferred_element_type=F32)


def _dot_nt(a, b):
    return lax.dot_general(a, b, (((1,), (1,)), ((), ())), preferred_element_type=F32)


def _rot_half_lanes(x):
    return pltpu.roll(x, LANES // 2, 1)


def _in_proj_kernel(*refs, q_scale, n_cast):
    (x_ref, pos_ref, gmix_ref, win_ref, qg_ref, wuq_ref, kvg_ref, wuk_ref, wuvt_ref,
     rc_ref) = refs[:10]
    cast_in = refs[10:10 + n_cast]
    rq_ref, rk_ref, rv_ref, rg_ref, q_ref, kn_ref, kr_ref, vt_ref = refs[10 + n_cast:18 + n_cast]
    cast_out = refs[18 + n_cast:-1]
    win_bf16_ref = refs[-1]

    @pl.when(pl.program_id(0) == 0)
    def _():
        n_out = win_ref.shape[0]
        for a in range(0, n_out, LANES):
            rows = min(LANES, n_out - a)
            blk = win_ref[a:a + rows, :]
            if rows < LANES:
                blk = jnp.concatenate([blk, jnp.zeros((LANES - rows, blk.shape[1]), F32)], axis=0)
            win_bf16_ref[:, a:a + LANES] = blk.T.astype(BF16)

    for src, dst in zip(cast_in, cast_out):
        dst[...] = src[...].astype(BF16)

    tm = x_ref.shape[0]
    part = tm // IN_PROJ_PARTS
    parts = [slice(i * part, (i + 1) * part) for i in range(IN_PROJ_PARTS)]
    zs = [_dot(_rms(x_ref[r, :], gmix_ref[...]).astype(BF16), win_bf16_ref[...])
          for r in parts]

    lane = lax.broadcasted_iota(jnp.int32, (1, LANES), 1)
    half = LANES // 2
    quarter = LANES // 4
    d = RET_HEAD_DIM
    w = RET_HEADS * d
    k_scale = d ** -0.5
    o = 4 * w
    q_lora = qg_ref.shape[-1]
    kv_lora = kvg_ref.shape[-1]
    tk = vt_ref.shape[-1]
    vt_cols = min(tk, part)
    pos_rows = pos_ref[...].astype(F32)
    pos_cols = jnp.concatenate(
        [pos_rows, jnp.zeros((LANES - pos_rows.shape[0], LANES), F32)], axis=0).T

    for r, z in zip(parts, zs):
        anchor = lax.bitcast_convert_type(z[:, 0:1], jnp.uint32)
        zero = lax.shift_right_logical(lax.shift_right_logical(anchor, jnp.uint32(16)), jnp.uint32(16))
        pos = jnp.concatenate([pos_cols[:, c:c + 1] for c in range(r.start // LANES, r.stop // LANES)],
                              axis=0) + zero.astype(F32)
        ang = pos * rc_ref[0:1, :]
        c = jnp.cos(ang)
        s = jnp.sin(ang)
        c_hi, s_hi = _rot_half_lanes(c), _rot_half_lanes(s)
        cos_r = jnp.where(lane < half, c, c_hi)
        sin_r = jnp.where(lane < half, -s, s_hi)
        c_q3 = pltpu.roll(c, 3 * quarter, 1)
        s_q3 = pltpu.roll(s, 3 * quarter, 1)
        cos_m = jnp.where(lane < quarter, c_hi, jnp.where(lane < half, c_q3, 0.0))
        sin_m = jnp.where(lane < quarter, s_hi, jnp.where(lane < half, s_q3, 0.0))

        for hh in range(RET_HEADS):
            q = z[:, hh * d:(hh + 1) * d]
            rq_ref[r, hh * d:(hh + 1) * d] = (q * cos_r + _rot_half_lanes(q) * sin_r).astype(BF16)
            k = z[:, w + hh * d:w + (hh + 1) * d]
            rk_ref[r, hh * d:(hh + 1) * d] = (
                (k * cos_r + _rot_half_lanes(k) * sin_r) * k_scale).astype(BF16)
        rv_ref[r, :] = z[:, 2 * w:3 * w].astype(BF16)
        g = z[:, 3 * w:4 * w]
        rg_ref[r, :] = g * _sigmoid(g)

        cq = z[:, o:o + q_lora]
        ckv = z[:, o + q_lora:o + q_lora + kv_lora]

        def rope_low_half(y):
            rot = jnp.where(lane < quarter, -pltpu.roll(y, 3 * quarter, 1), pltpu.roll(y, quarter, 1))
            return y * cos_m + rot * sin_m

        kpe = z[:, o + q_lora + kv_lora:o + q_lora + kv_lora + LANES]
        kr_ref[r, :] = rope_low_half(kpe).astype(BF16)

        qq = _dot(_rms(cq, qg_ref[...]).astype(BF16), wuq_ref[...])
        for hh in range(MLA_HEADS):
            b = hh * MLA_QK_PAD
            q_ref[r, b:b + LANES] = (qq[:, b:b + LANES] * q_scale).astype(BF16)
            y = qq[:, b + LANES:b + 2 * LANES]
            q_ref[r, b + LANES:b + 2 * LANES] = (
                (y * cos_m + _rot_half_lanes(y) * sin_m) * q_scale).astype(BF16)

        ckvn = _rms(ckv, kvg_ref[...]).astype(BF16)
        kn_ref[r, :] = _dot(ckvn, wuk_ref[...]).astype(BF16)
        vt = _dot_nt(wuvt_ref[...], ckvn).astype(BF16)
        for a in range(r.start, r.stop, vt_cols):
            vt_ref[a // tk, :, a % tk:a % tk + vt_cols] = vt[:, a - r.start:a - r.start + vt_cols]


def _in_proj(x2, pos2, gmix, win, qg, wuq, kvg, wuk, wuvt, rc, cast_weights):
    n, dm = x2.shape
    tm = IN_PROJ_ROWS
    steps = n // tm
    bf16_rows = 16
    for a in cast_weights:
        assert a.shape[0] % (steps * bf16_rows) == 0, a.shape
    cast_specs = [pl.BlockSpec((a.shape[0] // steps, a.shape[1]), lambda i: (i, 0))
                  for a in cast_weights]
    cast_shapes = tuple(jax.ShapeDtypeStruct(a.shape, BF16) for a in cast_weights)
    w = RET_HEADS * RET_HEAD_DIM
    const = lambda i: (0, 0)
    row = lambda i: (i, 0)
    full = lambda a: pl.BlockSpec(a.shape, const)
    out_shapes = (
        jax.ShapeDtypeStruct((n, w), BF16),
        jax.ShapeDtypeStruct((n, w), BF16),
        jax.ShapeDtypeStruct((n, w), BF16),
        jax.ShapeDtypeStruct((n, w), F32),
        jax.ShapeDtypeStruct((n, MLA_HEADS * MLA_QK_PAD), BF16),
        jax.ShapeDtypeStruct((n, MLA_HEADS * MLA_NOPE_DIM), BF16),
        jax.ShapeDtypeStruct((n, LANES), BF16),
    )
    tk = ATTN_KV_BLOCK
    vt_shape = jax.ShapeDtypeStruct((n // tk, MLA_HEADS * MLA_V_DIM, tk), BF16)
    out_specs = tuple(pl.BlockSpec((tm, s.shape[1]), row) for s in out_shapes) + (
        pl.BlockSpec((tm // tk, MLA_HEADS * MLA_V_DIM, tk), lambda i: (i, 0, 0)),)
    q_scale = MLA_QK_DIM ** -0.5 * math.log2(math.e)
    outs = pl.pallas_call(
        functools.partial(_in_proj_kernel, q_scale=q_scale, n_cast=len(cast_weights)),
        out_shape=out_shapes + (vt_shape,) + cast_shapes,
        grid=(steps,),
        in_specs=[pl.BlockSpec((tm, dm), row), pl.BlockSpec((tm // LANES, LANES), row),
                  full(gmix), pl.BlockSpec(win.shape, const, pipeline_mode=pl.Buffered(1)),
                  full(qg), full(wuq), full(kvg), full(wuk), full(wuvt),
                  full(rc)] + cast_specs,
        out_specs=out_specs + tuple(cast_specs),
        scratch_shapes=[pltpu.VMEM((win.shape[1], -(-win.shape[0] // LANES) * LANES), BF16)],
        compiler_params=pltpu.CompilerParams(
            dimension_semantics=("arbitrary",), vmem_limit_bytes=VMEM_LIMIT_BYTES),
        name="in_proj",
    )(x2, pos2, gmix, win, qg, wuq, kvg, wuk, wuvt, rc, *cast_weights)
    n_main = len(out_shapes) + 1
    return outs[:n_main], outs[n_main:]


def _retention_kernel(rq_ref, rk_ref, rv_ref, rg_ref, dmat_ref, qdec_ref, kdec_ref, out_ref,
                      state_ref, *, block_decay):
    @pl.when(pl.program_id(0) == 0)
    def _():
        state_ref[...] = jnp.zeros_like(state_ref)

    d = RET_HEAD_DIM
    blk = dmat_ref.shape[-1]
    nblk = rq_ref.shape[1] // blk
    seqs = range(rq_ref.shape[0])
    chains = [(t, b, hh) for t in range(nblk) for b in seqs for hh in range(RET_HEADS)]
    state = {(b, hh): state_ref[b, hh] for b in seqs for hh in range(RET_HEADS)}

    def front(c):
        t, b, hh = c
        rows = slice(t * blk, (t + 1) * blk)
        sl = slice(hh * d, (hh + 1) * d)
        q = rq_ref[b, rows, sl]
        v = rv_ref[b, rows, sl]
        kt = rk_ref[b, rows, sl].astype(F32).T
        st = state[b, hh]
        scores = _dot(q, kt.astype(BF16))
        carried = _dot(q, st.astype(BF16))
        update = _dot((kt * kdec_ref[hh][0:1, :]).astype(BF16), v)
        state[b, hh] = block_decay[hh] * st + update
        return scores, carried, v

    fronts = {0: front(chains[0])}
    for n, (t, b, hh) in enumerate(chains):
        if n + 1 < len(chains):
            fronts[n + 1] = front(chains[n + 1])
        scores, carried, v = fronts.pop(n)
        rows = slice(t * blk, (t + 1) * blk)
        sl = slice(hh * d, (hh + 1) * d)
        o = _dot((scores * dmat_ref[hh]).astype(BF16), v) + carried * qdec_ref[hh]
        mu = jnp.mean(o, axis=-1, keepdims=True)
        oc = o - mu
        var = jnp.mean(oc * oc, axis=-1, keepdims=True)
        out_ref[b, rows, sl] = (oc * lax.rsqrt(var + GN_EPS) * rg_ref[b, rows, sl]).astype(BF16)
    for (b, hh), st in state.items():
        state_ref[b, hh] = st


def _retention(rq, rk, rv, rg, batch, seq):
    n, w = rq.shape
    blk = RET_BLOCK
    nb = seq // blk
    log_g = np.log1p(-np.exp2(-5.0 - np.arange(RET_HEADS, dtype=np.float64)))
    idx = jnp.arange(blk, dtype=F32)
    lg = jnp.asarray(log_g, F32)
    dist = jnp.abs(idx[:, None] - idx[None, :])
    visible = (idx[None, :] // CHUNK) <= (idx[:, None] // CHUNK)
    dmat = jnp.where(visible[None], jnp.exp(lg[:, None, None] * dist[None]), 0.0)
    qdec = jnp.broadcast_to(jnp.exp(lg[:, None] * (idx + 1.0))[:, :, None], (RET_HEADS, blk, LANES))
    kdec = jnp.broadcast_to(jnp.exp(lg[:, None] * (blk - 1.0 - idx))[:, None, :], (RET_HEADS, 8, blk))
    block_decay = tuple(float(math.exp(g * blk)) for g in log_g)

    seq_block = pl.BlockSpec((batch, RET_STEP_BLOCKS * blk, w), lambda j: (0, j, 0))
    const3 = lambda j: (0, 0, 0)
    shaped = lambda a: a.reshape(batch, seq, w)
    out = pl.pallas_call(
        functools.partial(_retention_kernel, block_decay=block_decay),
        out_shape=jax.ShapeDtypeStruct((batch, seq, w), BF16),
        grid=(nb // RET_STEP_BLOCKS,),
        in_specs=[seq_block] * 4 + [
            pl.BlockSpec(dmat.shape, const3), pl.BlockSpec(qdec.shape, const3),
            pl.BlockSpec(kdec.shape, const3)],
        out_specs=seq_block,
        scratch_shapes=[pltpu.VMEM((batch, RET_HEADS, RET_HEAD_DIM, RET_HEAD_DIM), F32)],
        compiler_params=pltpu.CompilerParams(
            dimension_semantics=("arbitrary",), vmem_limit_bytes=VMEM_LIMIT_BYTES),
        name="retention",
    )(shaped(rq), shaped(rk), shaped(rv), shaped(rg), dmat, qdec, kdec)
    return out.reshape(n, w)


def _attention_kernel(q_ref, kn_ref, kr_ref, vt_ref, out_ref,
                      m_ref, l_ref, acc_ref, jump_ref, qt_ref):
    tq = q_ref.shape[0]
    gw = ATTN_Q_GROUP
    kb = ATTN_KEY_BLOCK
    tk = vt_ref.shape[-1]
    groups = range(tq // gw)
    diag_blocks = tq // kb
    i = pl.program_id(2)
    first_diag = i * diag_blocks

    def keys(j, size=kb):
        start = pl.multiple_of(j * kb, kb)
        return jnp.concatenate(
            [kn_ref[pl.ds(start, size), :], kr_ref[pl.ds(start, size), :]], axis=1)

    def values_t(j, size=kb):
        return jnp.concatenate([vt_ref[j * (kb // tk) + t] for t in range(size // tk)], axis=1)

    def chunk_mask(shape, key0, g):
        kc = (lax.broadcasted_iota(jnp.int32, shape, 0) + key0) // CHUNK
        qc = (lax.broadcasted_iota(jnp.int32, shape, 1) + g * gw) // CHUNK
        return kc <= qc

    def two_pass(g, k, vt, key0=None):
        cols = slice(g * gw, (g + 1) * gw)
        st = _dot_nt(k, q_ref[g * gw:(g + 1) * gw, :])
        if key0 is not None:
            st = jnp.where(chunk_mask(st.shape, key0, g), st, -1e30)
        m_old = m_ref[:, cols]
        m_new = jnp.maximum(m_old, jnp.max(st, axis=0, keepdims=True))
        alpha = jnp.exp2(m_old - m_new)
        p = jnp.exp2(st - m_new)
        l_ref[:, cols] = alpha * l_ref[:, cols] + jnp.sum(p, axis=0, keepdims=True)
        acc_ref[:, cols] = alpha * acc_ref[:, cols] + _dot(vt, p.astype(BF16))
        m_ref[:, cols] = m_new

    def single_pass(j, nblocks, items):
        k = jnp.concatenate([keys(j + t) for t in range(nblocks)], axis=0)
        vt = jnp.concatenate([values_t(j + t) for t in range(nblocks)], axis=1)
        used = sorted({g for g, _, _ in items})
        cols = {g: slice(g * gw, (g + 1) * gw) for g in used}
        qs = {g: qt_ref[:, cols[g]] for g in used}
        m_old = {g: m_ref[:, cols[g]] for g in used}
        l_new = {g: l_ref[:, cols[g]] for g in used}
        acc_new = {g: acc_ref[:, cols[g]] for g in used}
        smax = {}

        def qk(item):
            g, a, _ = item
            return _dot(k[a:a + ATTN_KEY_SLICE, :], qs[g])

        pending = [qk(item) for item in items[:ATTN_LOOKAHEAD]]
        for n, (g, a, key0) in enumerate(items):
            if n + ATTN_LOOKAHEAD < len(items):
                pending.append(qk(items[n + ATTN_LOOKAHEAD]))
            st = pending.pop(0)
            if key0 is not None:
                st = jnp.where(chunk_mask(st.shape, key0, g), st, -1e30)
            p = jnp.exp2(st - m_old[g])
            cmax = jnp.max(st, axis=0, keepdims=True)
            smax[g] = cmax if g not in smax else jnp.maximum(smax[g], cmax)
            l_new[g] = l_new[g] + jnp.sum(p, axis=0, keepdims=True)
            acc_new[g] = acc_new[g] + _dot(vt[:, a:a + ATTN_KEY_SLICE], p.astype(BF16))
        for g in used:
            m_new = jnp.maximum(m_old[g], smax[g])
            alpha = jnp.exp2(m_old[g] - m_new)
            l_ref[:, cols[g]] = alpha * l_new[g]
            acc_ref[:, cols[g]] = alpha * acc_new[g]
            m_ref[:, cols[g]] = m_new
            jump_ref[:, cols[g]] = jnp.maximum(jump_ref[:, cols[g]], smax[g] - m_old[g])

    full_items = [(g, a, None) for a in range(0, tq, ATTN_KEY_SLICE) for g in groups]
    diag_items = [(g, a, a if a >= g * gw else None)
                  for a in range(0, tq, ATTN_KEY_SLICE) for g in groups if a < (g + 1) * gw]

    def fast_body(j, carry):
        single_pass(j * diag_blocks, diag_blocks, full_items)
        return carry

    def safe_body(j, carry):
        k = keys(j)
        vt = values_t(j)
        for g in groups:
            two_pass(g, k, vt)
        return carry

    qt_ref[...] = q_ref[...].astype(F32).T.astype(BF16)
    m_ref[...] = jnp.max(_dot_nt(keys(first_diag, CHUNK), q_ref[...]), axis=0, keepdims=True)
    l_ref[...] = jnp.zeros_like(l_ref)
    acc_ref[...] = jnp.zeros_like(acc_ref)
    jump_ref[...] = jnp.zeros_like(jump_ref)
    lax.fori_loop(0, i, fast_body, 0)
    single_pass(first_diag, diag_blocks, diag_items)

    @pl.when(jnp.max(jump_ref[...]) > ATTN_MAX_JUMP)
    def _():
        m_ref[...] = jnp.full_like(m_ref, -1e30)
        l_ref[...] = jnp.zeros_like(l_ref)
        acc_ref[...] = jnp.zeros_like(acc_ref)
        lax.fori_loop(0, first_diag, safe_body, 0)
        for d in range(diag_blocks):
            for g in groups:
                visible = min(kb, (g + 1) * gw - d * kb)
                if visible > 0:
                    two_pass(g, keys(first_diag + d, visible), values_t(first_diag + d, visible),
                             d * kb)

    o = acc_ref[...] * (1.0 / l_ref[...])
    out_ref[...] = o.T.astype(BF16)


def _attention(q, kn, kr, vt, batch, seq):
    n = q.shape[0]
    tq = ATTN_Q_BLOCK
    tk = ATTN_KV_BLOCK
    nq = seq // tq
    return pl.pallas_call(
        _attention_kernel,
        out_shape=jax.ShapeDtypeStruct((n, MLA_HEADS * MLA_V_DIM), BF16),
        grid=(batch, MLA_HEADS, nq),
        in_specs=[pl.BlockSpec((tq, MLA_QK_PAD), lambda b, h, i: (b * nq + i, h)),
                  pl.BlockSpec((seq, MLA_NOPE_DIM), lambda b, h, i: (b, h)),
                  pl.BlockSpec((seq, LANES), lambda b, h, i: (b, 0)),
                  pl.BlockSpec((seq // tk, MLA_V_DIM, tk), lambda b, h, i: (b, h, 0))],
        out_specs=pl.BlockSpec((tq, MLA_V_DIM), lambda b, h, i: (b * nq + i, h)),
        scratch_shapes=[pltpu.VMEM((1, tq), F32), pltpu.VMEM((1, tq), F32),
                        pltpu.VMEM((MLA_V_DIM, tq), F32), pltpu.VMEM((1, tq), F32),
                        pltpu.VMEM((MLA_QK_PAD, tq), BF16)],
        compiler_params=pltpu.CompilerParams(
            dimension_semantics=("arbitrary", "arbitrary", "arbitrary"),
            vmem_limit_bytes=VMEM_LIMIT_BYTES),
        name="attention",
    )(q, kn, kr, vt)


def _out_ffn_kernel(x_ref, ret_ref, att_ref, p_ref, wo_ref, gffn_ref, wg_ref, wu_ref, wd_ref,
                    gple_ref, wpg_ref, wpp_ref, gfin_ref, out_ref, *, final_norm):
    tm = x_ref.shape[0]
    part = tm // OUT_PARTS
    parts = tuple(slice(i * part, (i + 1) * part) for i in range(OUT_PARTS))
    each = range(len(parts))
    o = [_dot(jnp.concatenate([ret_ref[r, :], att_ref[r, :]], axis=1), wo_ref[...]) for r in parts]
    pe = [_dot(p_ref[r, :].astype(BF16), wpp_ref[...]) for r in parts]
    x1 = [x_ref[r, :] + o[i] for i, r in enumerate(parts)]
    h = [_rms(x1[i], gffn_ref[...]).astype(BF16) for i in each]
    gu = [(_dot(h[i], wg_ref[...]), _dot(h[i], wu_ref[...])) for i in each]
    act = [(g * _sigmoid(g) * u).astype(BF16) for g, u in gu]
    x2 = [x1[i] + _dot(act[i], wd_ref[...]) for i in each]
    h2 = [_rms(x2[i], gple_ref[...]).astype(BF16) for i in each]
    gate = [_sigmoid(_dot(h2[i], wpg_ref[...])) for i in each]
    for i, r in enumerate(parts):
        x3 = x2[i] + gate[i] * pe[i]
        if final_norm:
            x3 = _rms(x3, gfin_ref[...])
        out_ref[r, :] = x3


def _out_ffn(x2, ret, att, p2, wo, gffn, wg, wu, wd, gple, wpg, wpp, gfin, final_norm):
    n, dm = x2.shape
    tm = OUT_ROWS
    row = lambda i: (i, 0)
    const = lambda i: (0, 0)
    resident = lambda a: pl.BlockSpec(a.shape, const, pipeline_mode=pl.Buffered(1))
    return pl.pallas_call(
        functools.partial(_out_ffn_kernel, final_norm=final_norm),
        out_shape=jax.ShapeDtypeStruct((n, dm), F32),
        grid=(n // tm,),
        in_specs=[pl.BlockSpec((tm, dm), row), pl.BlockSpec((tm, ret.shape[1]), row),
                  pl.BlockSpec((tm, att.shape[1]), row), pl.BlockSpec((tm, p2.shape[1]), row),
                  resident(wo), resident(gffn), resident(wg), resident(wu), resident(wd),
                  resident(gple), resident(wpg), resident(wpp), resident(gfin)],
        out_specs=pl.BlockSpec((tm, dm), row),
        compiler_params=pltpu.CompilerParams(
            dimension_semantics=("arbitrary",), vmem_limit_bytes=VMEM_LIMIT_BYTES),
        name="out_ffn",
    )(x2, ret, att, p2, wo, gffn, wg, wu, wd, gple, wpg, wpp, gfin)


def _prep_in_weights(w_uq, w_ukv):
    q_lora = w_uq.shape[0]
    uq = w_uq.reshape(q_lora, MLA_HEADS, MLA_QK_DIM)
    rope = uq[:, :, MLA_NOPE_DIM:]
    rot = jnp.concatenate([-rope[..., MLA_ROPE_DIM // 2:], rope[..., :MLA_ROPE_DIM // 2]], axis=-1)
    wuq = jnp.concatenate([uq, rot], axis=-1).reshape(q_lora, MLA_HEADS * MLA_QK_PAD)
    kv_lora = w_ukv.shape[0]
    ukv = w_ukv.reshape(kv_lora, MLA_HEADS, MLA_NOPE_DIM + MLA_V_DIM)
    wuk = ukv[:, :, :MLA_NOPE_DIM].reshape(kv_lora, -1)
    wuvt = ukv[:, :, MLA_NOPE_DIM:].reshape(kv_lora, -1).T
    return wuq.astype(BF16), wuk.astype(BF16), wuvt.astype(BF16)


def _rope_consts():
    def inv(half):
        return jnp.exp(-math.log(ROPE_BASE) * jnp.arange(half, dtype=F32) / half)
    inv_r = inv(RET_HEAD_DIM // 2)
    inv_m = inv(MLA_ROPE_DIM // 2)
    row = jnp.concatenate([inv_r, inv_m, jnp.zeros(LANES - inv_r.size - inv_m.size, F32)])
    return jnp.broadcast_to(row[None, :], (8, LANES))


def kernel(x, p, positions, mix_norm_g, w_in, q_norm_g, w_uq, kv_norm_g, w_ukv, w_o, ffn_norm_g,
           w_ffn_gate, w_ffn_up, w_ffn_down, ple_norm_g, w_ple_gate, w_ple_proj, final_norm_g):
    batch, seq, dm = x.shape
    depth = w_in.shape[0]
    n = batch * seq
    x2 = x.reshape(n, dm)
    pos2 = positions.reshape(n // LANES, LANES)
    rc = _rope_consts()
    vec = lambda g: g.reshape(1, -1).astype(F32)
    for i in range(depth):
        wuq, wuk, wuvt = _prep_in_weights(w_uq[i], w_ukv[i])
        (rq, rk, rv, rg, q, kn, kr, vt), (wo, wg, wu, wd, wpg, wpp) = _in_proj(
            x2, pos2, vec(mix_norm_g[i]), w_in[i].T, vec(q_norm_g[i]), wuq, vec(kv_norm_g[i]), wuk,
            wuvt,
            rc, [w_o[i], w_ffn_gate[i], w_ffn_up[i], w_ffn_down[i], w_ple_gate[i], w_ple_proj[i]])
        ret = _retention(rq, rk, rv, rg, batch, seq)
        att = _attention(q, kn, kr, vt, batch, seq)
        x2 = _out_ffn(
            x2, ret, att, p[i].reshape(n, -1), wo, vec(ffn_norm_g[i]), wg, wu, wd,
            vec(ple_norm_g[i]), wpg, wpp, vec(final_norm_g), final_norm=(i == depth - 1))
    return x2.reshape(batch, seq, dm)
```

```python
import functools
import math

import numpy as np
import jax
import jax.numpy as jnp
from jax import lax
from jax.experimental import pallas as pl
from jax.experimental.pallas import tpu as pltpu

F32 = jnp.float32
BF16 = jnp.bfloat16

CHUNK = 64
RET_HEADS = 4
RET_HEAD_DIM = 128
MLA_HEADS = 4
MLA_NOPE_DIM = 128
MLA_ROPE_DIM = 64
MLA_V_DIM = 128
MLA_QK_DIM = MLA_NOPE_DIM + MLA_ROPE_DIM
ROPE_BASE = 10000.0
RMS_EPS = 1e-6
GN_EPS = 1e-5

LANES = 128
MLA_QK_PAD = 2 * LANES
VMEM_LIMIT_BYTES = 56 * 1024 * 1024

IN_PROJ_ROWS = 1024
IN_PROJ_PARTS = 4
RET_BLOCK = 256
RET_STEP_BLOCKS = 4
ATTN_Q_BLOCK = 2048
ATTN_Q_GROUP = 512
ATTN_KEY_BLOCK = 1024
ATTN_KV_BLOCK = 512
ATTN_MAX_JUMP = 32.0
ATTN_KEY_SLICE = 256
ATTN_LOOKAHEAD = 3
OUT_ROWS = 512
OUT_PARTS = 2


def _rms(x, g):
    return x * lax.rsqrt(jnp.mean(x * x, axis=-1, keepdims=True) + RMS_EPS) * g


def _sigmoid(x):
    return 1.0 / (1.0 + jnp.exp(-x))


def _dot(a, b):
    return jnp.dot(a, b, preferred_element_type=F32)


def _dot_nt(a, b):
    return lax.dot_general(a, b, (((1,), (1,)), ((), ())), preferred_element_type=F32)


def _dot_tn(a, b):
    return lax.dot_general(a, b, (((0,), (0,)), ((), ())), preferred_element_type=F32)


def _rot_half_lanes(x):
    return pltpu.roll(x, LANES // 2, 1)


def _in_proj_kernel(*refs, q_scale, n_cast):
    (x_ref, pos_ref, gmix_ref, win_ref, qg_ref, wuq_ref, kvg_ref, wuk_ref, wuvt_ref,
     rc_ref) = refs[:10]
    cast_in = refs[10:10 + n_cast]
    rq_ref, rk_ref, rv_ref, rg_ref, q_ref, kn_ref, kr_ref, vt_ref = refs[10 + n_cast:18 + n_cast]
    cast_out = refs[18 + n_cast:-1]
    win_bf16_ref = refs[-1]

    @pl.when(pl.program_id(0) == 0)
    def _():
        n_out = win_ref.shape[0]
        for a in range(0, n_out, LANES):
            rows = min(LANES, n_out - a)
            blk = win_ref[a:a + rows, :]
            if rows < LANES:
                blk = jnp.concatenate([blk, jnp.zeros((LANES - rows, blk.shape[1]), F32)], axis=0)
            win_bf16_ref[:, a:a + LANES] = blk.T.astype(BF16)

    for src, dst in zip(cast_in, cast_out):
        dst[...] = src[...].astype(BF16)

    tm = x_ref.shape[0]
    part = tm // IN_PROJ_PARTS
    parts = [slice(i * part, (i + 1) * part) for i in range(IN_PROJ_PARTS)]
    zs = [_dot(_rms(x_ref[r, :], gmix_ref[...]).astype(BF16), win_bf16_ref[...])
          for r in parts]

    lane = lax.broadcasted_iota(jnp.int32, (1, LANES), 1)
    half = LANES // 2
    quarter = LANES // 4
    d = RET_HEAD_DIM
    w = RET_HEADS * d
    k_scale = d ** -0.5
    o = 4 * w
    q_lora = qg_ref.shape[-1]
    kv_lora = kvg_ref.shape[-1]
    tk = vt_ref.shape[-1]
    vt_cols = min(tk, part)
    pos_rows = pos_ref[...].astype(F32)
    pos_cols = jnp.concatenate(
        [pos_rows, jnp.zeros((LANES - pos_rows.shape[0], LANES), F32)], axis=0).T

    for r, z in zip(parts, zs):
        anchor = lax.bitcast_convert_type(z[:, 0:1], jnp.uint32)
        zero = lax.shift_right_logical(lax.shift_right_logical(anchor, jnp.uint32(16)), jnp.uint32(16))
        pos = jnp.concatenate([pos_cols[:, c:c + 1] for c in range(r.start // LANES, r.stop // LANES)],
                              axis=0) + zero.astype(F32)
        ang = pos * rc_ref[0:1, :]
        c = jnp.cos(ang)
        s = jnp.sin(ang)
        c_hi, s_hi = _rot_half_lanes(c), _rot_half_lanes(s)
        cos_r = jnp.where(lane < half, c, c_hi)
        sin_r = jnp.where(lane < half, -s, s_hi)
        c_q3 = pltpu.roll(c, 3 * quarter, 1)
        s_q3 = pltpu.roll(s, 3 * quarter, 1)
        cos_m = jnp.where(lane < quarter, c_hi, jnp.where(lane < half, c_q3, 0.0))
        sin_m = jnp.where(lane < quarter, s_hi, jnp.where(lane < half, s_q3, 0.0))

        for hh in range(RET_HEADS):
            q = z[:, hh * d:(hh + 1) * d]
            rq_ref[r, hh * d:(hh + 1) * d] = (q * cos_r + _rot_half_lanes(q) * sin_r).astype(BF16)
            k = z[:, w + hh * d:w + (hh + 1) * d]
            rk_ref[r, hh * d:(hh + 1) * d] = (
                (k * cos_r + _rot_half_lanes(k) * sin_r) * k_scale).astype(BF16)
        rv_ref[r, :] = z[:, 2 * w:3 * w].astype(BF16)
        g = z[:, 3 * w:4 * w]
        rg_ref[r, :] = g * _sigmoid(g)

        cq = z[:, o:o + q_lora]
        ckv = z[:, o + q_lora:o + q_lora + kv_lora]

        def rope_low_half(y):
            rot = jnp.where(lane < quarter, -pltpu.roll(y, 3 * quarter, 1), pltpu.roll(y, quarter, 1))
            return y * cos_m + rot * sin_m

        kpe = z[:, o + q_lora + kv_lora:o + q_lora + kv_lora + LANES]
        kr_ref[r, :] = rope_low_half(kpe).astype(BF16)

        qq = _dot(_rms(cq, qg_ref[...]).astype(BF16), wuq_ref[...])
        for hh in range(MLA_HEADS):
            b = hh * MLA_QK_PAD
            q_ref[r, b:b + LANES] = (qq[:, b:b + LANES] * q_scale).astype(BF16)
            y = qq[:, b + LANES:b + 2 * LANES]
            q_ref[r, b + LANES:b + 2 * LANES] = (
                (y * cos_m + _rot_half_lanes(y) * sin_m) * q_scale).astype(BF16)

        ckvn = _rms(ckv, kvg_ref[...]).astype(BF16)
        kn_ref[r, :] = _dot(ckvn, wuk_ref[...]).astype(BF16)
        vt = _dot_nt(wuvt_ref[...], ckvn).astype(BF16)
        for a in range(r.start, r.stop, vt_cols):
            vt_ref[a // tk, :, a % tk:a % tk + vt_cols] = vt[:, a - r.start:a - r.start + vt_cols]


def _in_proj(x2, pos2, gmix, win, qg, wuq, kvg, wuk, wuvt, rc, cast_weights):
    n, dm = x2.shape
    tm = IN_PROJ_ROWS
    steps = n // tm
    bf16_rows = 16
    for a in cast_weights:
        assert a.shape[0] % (steps * bf16_rows) == 0, a.shape
    cast_specs = [pl.BlockSpec((a.shape[0] // steps, a.shape[1]), lambda i: (i, 0))
                  for a in cast_weights]
    cast_shapes = tuple(jax.ShapeDtypeStruct(a.shape, BF16) for a in cast_weights)
    w = RET_HEADS * RET_HEAD_DIM
    const = lambda i: (0, 0)
    row = lambda i: (i, 0)
    full = lambda a: pl.BlockSpec(a.shape, const)
    out_shapes = (
        jax.ShapeDtypeStruct((n, w), BF16),
        jax.ShapeDtypeStruct((n, w), BF16),
        jax.ShapeDtypeStruct((n, w), BF16),
        jax.ShapeDtypeStruct((n, w), F32),
        jax.ShapeDtypeStruct((n, MLA_HEADS * MLA_QK_PAD), BF16),
        jax.ShapeDtypeStruct((n, MLA_HEADS * MLA_NOPE_DIM), BF16),
        jax.ShapeDtypeStruct((n, LANES), BF16),
    )
    tk = ATTN_KV_BLOCK
    vt_shape = jax.ShapeDtypeStruct((n // tk, MLA_HEADS * MLA_V_DIM, tk), BF16)
    out_specs = tuple(pl.BlockSpec((tm, s.shape[1]), row) for s in out_shapes) + (
        pl.BlockSpec((tm // tk, MLA_HEADS * MLA_V_DIM, tk), lambda i: (i, 0, 0)),)
    q_scale = MLA_QK_DIM ** -0.5 * math.log2(math.e)
    outs = pl.pallas_call(
        functools.partial(_in_proj_kernel, q_scale=q_scale, n_cast=len(cast_weights)),
        out_shape=out_shapes + (vt_shape,) + cast_shapes,
        grid=(steps,),
        in_specs=[pl.BlockSpec((tm, dm), row), pl.BlockSpec((tm // LANES, LANES), row),
                  full(gmix), pl.BlockSpec(win.shape, const, pipeline_mode=pl.Buffered(1)),
                  full(qg), full(wuq), full(kvg), full(wuk), full(wuvt),
                  full(rc)] + cast_specs,
        out_specs=out_specs + tuple(cast_specs),
        scratch_shapes=[pltpu.VMEM((win.shape[1], -(-win.shape[0] // LANES) * LANES), BF16)],
        compiler_params=pltpu.CompilerParams(
            dimension_semantics=("arbitrary",), vmem_limit_bytes=VMEM_LIMIT_BYTES),
        name="in_proj",
    )(x2, pos2, gmix, win, qg, wuq, kvg, wuk, wuvt, rc, *cast_weights)
    n_main = len(out_shapes) + 1
    return outs[:n_main], outs[n_main:]


def _retention_kernel(rq_ref, rk_ref, rv_ref, rg_ref, dmat_ref, qdec_ref, kdec_ref, out_ref,
                      state_ref, *, block_decay):
    @pl.when(pl.program_id(0) == 0)
    def _():
        state_ref[...] = jnp.zeros_like(state_ref)

    d = RET_HEAD_DIM
    blk = dmat_ref.shape[-1]
    nblk = rq_ref.shape[1] // blk
    seqs = range(rq_ref.shape[0])
    chains = [(t, b, hh) for t in range(nblk) for b in seqs for hh in range(RET_HEADS)]
    state = {(b, hh): state_ref[b, hh] for b in seqs for hh in range(RET_HEADS)}

    def front(c):
        t, b, hh = c
        rows = slice(t * blk, (t + 1) * blk)
        sl = slice(hh * d, (hh + 1) * d)
        q = rq_ref[b, rows, sl]
        v = rv_ref[b, rows, sl]
        kt = rk_ref[b, rows, sl].astype(F32).T
        st = state[b, hh]
        scores = _dot(q, kt.astype(BF16))
        carried = _dot(q, st.astype(BF16))
        update = _dot((kt * kdec_ref[hh][0:1, :]).astype(BF16), v)
        state[b, hh] = block_decay[hh] * st + update
        return scores, carried, v

    fronts = {0: front(chains[0])}
    for n, (t, b, hh) in enumerate(chains):
        if n + 1 < len(chains):
            fronts[n + 1] = front(chains[n + 1])
        scores, carried, v = fronts.pop(n)
        rows = slice(t * blk, (t + 1) * blk)
        sl = slice(hh * d, (hh + 1) * d)
        o = _dot((scores * dmat_ref[hh]).astype(BF16), v) + carried * qdec_ref[hh]
        mu = jnp.mean(o, axis=-1, keepdims=True)
        oc = o - mu
        var = jnp.mean(oc * oc, axis=-1, keepdims=True)
        out_ref[b, rows, sl] = (oc * lax.rsqrt(var + GN_EPS) * rg_ref[b, rows, sl]).astype(BF16)
    for (b, hh), st in state.items():
        state_ref[b, hh] = st


def _retention(rq, rk, rv, rg, batch, seq):
    n, w = rq.shape
    blk = RET_BLOCK
    nb = seq // blk
    log_g = np.log1p(-np.exp2(-5.0 - np.arange(RET_HEADS, dtype=np.float64)))
    idx = jnp.arange(blk, dtype=F32)
    lg = jnp.asarray(log_g, F32)
    dist = jnp.abs(idx[:, None] - idx[None, :])
    visible = (idx[None, :] // CHUNK) <= (idx[:, None] // CHUNK)
    dmat = jnp.where(visible[None], jnp.exp(lg[:, None, None] * dist[None]), 0.0)
    qdec = jnp.broadcast_to(jnp.exp(lg[:, None] * (idx + 1.0))[:, :, None], (RET_HEADS, blk, LANES))
    kdec = jnp.broadcast_to(jnp.exp(lg[:, None] * (blk - 1.0 - idx))[:, None, :], (RET_HEADS, 8, blk))
    block_decay = tuple(float(math.exp(g * blk)) for g in log_g)

    seq_block = pl.BlockSpec((batch, RET_STEP_BLOCKS * blk, w), lambda j: (0, j, 0))
    const3 = lambda j: (0, 0, 0)
    shaped = lambda a: a.reshape(batch, seq, w)
    out = pl.pallas_call(
        functools.partial(_retention_kernel, block_decay=block_decay),
        out_shape=jax.ShapeDtypeStruct((batch, seq, w), BF16),
        grid=(nb // RET_STEP_BLOCKS,),
        in_specs=[seq_block] * 4 + [
            pl.BlockSpec(dmat.shape, const3), pl.BlockSpec(qdec.shape, const3),
            pl.BlockSpec(kdec.shape, const3)],
        out_specs=seq_block,
        scratch_shapes=[pltpu.VMEM((batch, RET_HEADS, RET_HEAD_DIM, RET_HEAD_DIM), F32)],
        compiler_params=pltpu.CompilerParams(
            dimension_semantics=("arbitrary",), vmem_limit_bytes=VMEM_LIMIT_BYTES),
        name="retention",
    )(shaped(rq), shaped(rk), shaped(rv), shaped(rg), dmat, qdec, kdec)
    return out.reshape(n, w)


def _attention_kernel(q_ref, kn_ref, kr_ref, vt_ref, out_ref,
                      m_ref, l_ref, acc_ref, jump_ref, qt_ref):
    tq = q_ref.shape[0]
    gw = ATTN_Q_GROUP
    kb = ATTN_KEY_BLOCK
    tk = vt_ref.shape[-1]
    groups = range(tq // gw)
    diag_blocks = tq // kb
    i = pl.program_id(2)
    first_diag = i * diag_blocks

    def keys(j, size=kb):
        start = pl.multiple_of(j * kb, kb)
        return jnp.concatenate(
            [kn_ref[pl.ds(start, size), :], kr_ref[pl.ds(start, size), :]], axis=1)

    def values_t(j, size=kb):
        return jnp.concatenate([vt_ref[j * (kb // tk) + t] for t in range(size // tk)], axis=1)

    def chunk_mask(shape, key0, g):
        kc = (lax.broadcasted_iota(jnp.int32, shape, 0) + key0) // CHUNK
        qc = (lax.broadcasted_iota(jnp.int32, shape, 1) + g * gw) // CHUNK
        return kc <= qc

    def two_pass(g, k, vt, key0=None):
        cols = slice(g * gw, (g + 1) * gw)
        st = _dot_nt(k, q_ref[g * gw:(g + 1) * gw, :])
        if key0 is not None:
            st = jnp.where(chunk_mask(st.shape, key0, g), st, -1e30)
        m_old = m_ref[:, cols]
        m_new = jnp.maximum(m_old, jnp.max(st, axis=0, keepdims=True))
        alpha = jnp.exp2(m_old - m_new)
        p = jnp.exp2(st - m_new)
        l_ref[:, cols] = alpha * l_ref[:, cols] + jnp.sum(p, axis=0, keepdims=True)
        acc_ref[:, cols] = alpha * acc_ref[:, cols] + _dot(vt, p.astype(BF16))
        m_ref[:, cols] = m_new

    def single_pass(j, nblocks, items):
        k = jnp.concatenate([keys(j + t) for t in range(nblocks)], axis=0)
        vt = jnp.concatenate([values_t(j + t) for t in range(nblocks)], axis=1)
        used = sorted({g for g, _, _ in items})
        cols = {g: slice(g * gw, (g + 1) * gw) for g in used}
        qs = {g: qt_ref[:, cols[g]] for g in used}
        m_old = {g: m_ref[:, cols[g]] for g in used}
        l_new = {g: l_ref[:, cols[g]] for g in used}
        acc_new = {g: acc_ref[:, cols[g]] for g in used}
        smax = {}

        def qk(item):
            g, a, _ = item
            return _dot(k[a:a + ATTN_KEY_SLICE, :], qs[g])

        pending = [qk(item) for item in items[:ATTN_LOOKAHEAD]]
        for n, (g, a, key0) in enumerate(items):
            if n + ATTN_LOOKAHEAD < len(items):
                pending.append(qk(items[n + ATTN_LOOKAHEAD]))
            st = pending.pop(0)
            if key0 is not None:
                st = jnp.where(chunk_mask(st.shape, key0, g), st, -1e30)
            p = jnp.exp2(st - m_old[g])
            cmax = jnp.max(st, axis=0, keepdims=True)
            smax[g] = cmax if g not in smax else jnp.maximum(smax[g], cmax)
            l_new[g] = l_new[g] + jnp.sum(p, axis=0, keepdims=True)
            acc_new[g] = acc_new[g] + _dot(vt[:, a:a + ATTN_KEY_SLICE], p.astype(BF16))
        for g in used:
            m_new = jnp.maximum(m_old[g], smax[g])
            alpha = jnp.exp2(m_old[g] - m_new)
            l_ref[:, cols[g]] = alpha * l_new[g]
            acc_ref[:, cols[g]] = alpha * acc_new[g]
            m_ref[:, cols[g]] = m_new
            jump_ref[:, cols[g]] = jnp.maximum(jump_ref[:, cols[g]], smax[g] - m_old[g])

    full_items = [(g, a, None) for a in range(0, tq, ATTN_KEY_SLICE) for g in groups]
    diag_items = [(g, a, a if a >= g * gw else None)
                  for a in range(0, tq, ATTN_KEY_SLICE) for g in groups if a < (g + 1) * gw]

    def fast_body(j, carry):
        single_pass(j * diag_blocks, diag_blocks, full_items)
        return carry

    def safe_body(j, carry):
        k = keys(j)
        vt = values_t(j)
        for g in groups:
            two_pass(g, k, vt)
        return carry

    qt_ref[...] = q_ref[...].astype(F32).T.astype(BF16)
    m_ref[...] = jnp.max(_dot_nt(keys(first_diag, CHUNK), q_ref[...]), axis=0, keepdims=True)
    l_ref[...] = jnp.zeros_like(l_ref)
    acc_ref[...] = jnp.zeros_like(acc_ref)
    jump_ref[...] = jnp.zeros_like(jump_ref)
    lax.fori_loop(0, i, fast_body, 0)
    single_pass(first_diag, diag_blocks, diag_items)

    @pl.when(jnp.max(jump_ref[...]) > ATTN_MAX_JUMP)
    def _():
        m_ref[...] = jnp.full_like(m_ref, -1e30)
        l_ref[...] = jnp.zeros_like(l_ref)
        acc_ref[...] = jnp.zeros_like(acc_ref)
        lax.fori_loop(0, first_diag, safe_body, 0)
        for d in range(diag_blocks):
            for g in groups:
                visible = min(kb, (g + 1) * gw - d * kb)
                if visible > 0:
                    two_pass(g, keys(first_diag + d, visible), values_t(first_diag + d, visible),
                             d * kb)

    out_ref[...] = (acc_ref[...] * (1.0 / l_ref[...])).astype(BF16)


def _attention(q, kn, kr, vt, batch, seq):
    n = q.shape[0]
    tq = ATTN_Q_BLOCK
    tk = ATTN_KV_BLOCK
    nq = seq // tq
    return pl.pallas_call(
        _attention_kernel,
        out_shape=jax.ShapeDtypeStruct((MLA_HEADS * MLA_V_DIM, n), BF16),
        grid=(batch, MLA_HEADS, nq),
        in_specs=[pl.BlockSpec((tq, MLA_QK_PAD), lambda b, h, i: (b * nq + i, h)),
                  pl.BlockSpec((seq, MLA_NOPE_DIM), lambda b, h, i: (b, h)),
                  pl.BlockSpec((seq, LANES), lambda b, h, i: (b, 0)),
                  pl.BlockSpec((seq // tk, MLA_V_DIM, tk), lambda b, h, i: (b, h, 0))],
        out_specs=pl.BlockSpec((MLA_V_DIM, tq), lambda b, h, i: (h, b * nq + i)),
        scratch_shapes=[pltpu.VMEM((1, tq), F32), pltpu.VMEM((1, tq), F32),
                        pltpu.VMEM((MLA_V_DIM, tq), F32), pltpu.VMEM((1, tq), F32),
                        pltpu.VMEM((MLA_QK_PAD, tq), BF16)],
        compiler_params=pltpu.CompilerParams(
            dimension_semantics=("arbitrary", "arbitrary", "arbitrary"),
            vmem_limit_bytes=VMEM_LIMIT_BYTES),
        name="attention",
    )(q, kn, kr, vt)


def _out_ffn_kernel(x_ref, ret_ref, att_ref, p_ref, wo_ref, gffn_ref, wg_ref, wu_ref, wd_ref,
                    gple_ref, wpg_ref, wpp_ref, gfin_ref, out_ref, *, final_norm):
    tm = x_ref.shape[0]
    part = tm // OUT_PARTS
    parts = tuple(slice(i * part, (i + 1) * part) for i in range(OUT_PARTS))
    each = range(len(parts))
    n_ret = ret_ref.shape[1]
    o = [_dot(ret_ref[r, :], wo_ref[:n_ret, :]) + _dot_tn(att_ref[:, r], wo_ref[n_ret:, :])
         for r in parts]
    pe = [_dot(p_ref[r, :].astype(BF16), wpp_ref[...]) for r in parts]
    x1 = [x_ref[r, :] + o[i] for i, r in enumerate(parts)]
    h = [_rms(x1[i], gffn_ref[...]).astype(BF16) for i in each]
    gu = [(_dot(h[i], wg_ref[...]), _dot(h[i], wu_ref[...])) for i in each]
    act = [(g * _sigmoid(g) * u).astype(BF16) for g, u in gu]
    x2 = [x1[i] + _dot(act[i], wd_ref[...]) for i in each]
    h2 = [_rms(x2[i], gple_ref[...]).astype(BF16) for i in each]
    gate = [_sigmoid(_dot(h2[i], wpg_ref[...])) for i in each]
    for i, r in enumerate(parts):
        x3 = x2[i] + gate[i] * pe[i]
        if final_norm:
            x3 = _rms(x3, gfin_ref[...])
        out_ref[r, :] = x3


def _out_ffn(x2, ret, att, p2, wo, gffn, wg, wu, wd, gple, wpg, wpp, gfin, final_norm):
    n, dm = x2.shape
    tm = OUT_ROWS
    row = lambda i: (i, 0)
    const = lambda i: (0, 0)
    resident = lambda a: pl.BlockSpec(a.shape, const, pipeline_mode=pl.Buffered(1))
    return pl.pallas_call(
        functools.partial(_out_ffn_kernel, final_norm=final_norm),
        out_shape=jax.ShapeDtypeStruct((n, dm), F32),
        grid=(n // tm,),
        in_specs=[pl.BlockSpec((tm, dm), row), pl.BlockSpec((tm, ret.shape[1]), row),
                  pl.BlockSpec((att.shape[0], tm), lambda i: (0, i)),
                  pl.BlockSpec((tm, p2.shape[1]), row),
                  resident(wo), resident(gffn), resident(wg), resident(wu), resident(wd),
                  resident(gple), resident(wpg), resident(wpp), resident(gfin)],
        out_specs=pl.BlockSpec((tm, dm), row),
        compiler_params=pltpu.CompilerParams(
            dimension_semantics=("arbitrary",), vmem_limit_bytes=VMEM_LIMIT_BYTES),
        name="out_ffn",
    )(x2, ret, att, p2, wo, gffn, wg, wu, wd, gple, wpg, wpp, gfin)


def _prep_in_weights(w_uq, w_ukv):
    q_lora = w_uq.shape[0]
    uq = w_uq.reshape(q_lora, MLA_HEADS, MLA_QK_DIM)
    rope = uq[:, :, MLA_NOPE_DIM:]
    rot = jnp.concatenate([-rope[..., MLA_ROPE_DIM // 2:], rope[..., :MLA_ROPE_DIM // 2]], axis=-1)
    wuq = jnp.concatenate([uq, rot], axis=-1).reshape(q_lora, MLA_HEADS * MLA_QK_PAD)
    kv_lora = w_ukv.shape[0]
    ukv = w_ukv.reshape(kv_lora, MLA_HEADS, MLA_NOPE_DIM + MLA_V_DIM)
    wuk = ukv[:, :, :MLA_NOPE_DIM].reshape(kv_lora, -1)
    wuvt = ukv[:, :, MLA_NOPE_DIM:].reshape(kv_lora, -1).T
    return wuq.astype(BF16), wuk.astype(BF16), wuvt.astype(BF16)


def _rope_consts():
    def inv(half):
        return jnp.exp(-math.log(ROPE_BASE) * jnp.arange(half, dtype=F32) / half)
    inv_r = inv(RET_HEAD_DIM // 2)
    inv_m = inv(MLA_ROPE_DIM // 2)
    row = jnp.concatenate([inv_r, inv_m, jnp.zeros(LANES - inv_r.size - inv_m.size, F32)])
    return jnp.broadcast_to(row[None, :], (8, LANES))


def kernel(x, p, positions, mix_norm_g, w_in, q_norm_g, w_uq, kv_norm_g, w_ukv, w_o, ffn_norm_g,
           w_ffn_gate, w_ffn_up, w_ffn_down, ple_norm_g, w_ple_gate, w_ple_proj, final_norm_g):
    batch, seq, dm = x.shape
    depth = w_in.shape[0]
    n = batch * seq
    x2 = x.reshape(n, dm)
    pos2 = positions.reshape(n // LANES, LANES)
    rc = _rope_consts()
    vec = lambda g: g.reshape(1, -1).astype(F32)
    for i in range(depth):
        wuq, wuk, wuvt = _prep_in_weights(w_uq[i], w_ukv[i])
        (rq, rk, rv, rg, q, kn, kr, vt), (wo, wg, wu, wd, wpg, wpp) = _in_proj(
            x2, pos2, vec(mix_norm_g[i]), w_in[i].T, vec(q_norm_g[i]), wuq, vec(kv_norm_g[i]), wuk,
            wuvt,
            rc, [w_o[i], w_ffn_gate[i], w_ffn_up[i], w_ffn_down[i], w_ple_gate[i], w_ple_proj[i]])
        ret = _retention(rq, rk, rv, rg, batch, seq)
        att = _attention(q, kn, kr, vt, batch, seq)
        x2 = _out_ffn(
            x2, ret, att, p[i].reshape(n, -1), wo, vec(ffn_norm_g[i]), wg, wu, wd,
            vec(ple_norm_g[i]), wpg, wpp, vec(final_norm_g), final_norm=(i == depth - 1))
    return x2.reshape(batch, seq, dm)
```

```python
import functools
import math

import numpy as np
import jax
import jax.numpy as jnp
from jax import lax
from jax.experimental import pallas as pl
from jax.experimental.pallas import tpu as pltpu

F32 = jnp.float32
BF16 = jnp.bfloat16

CHUNK = 64
RET_HEADS = 4
RET_HEAD_DIM = 128
MLA_HEADS = 4
MLA_NOPE_DIM = 128
MLA_ROPE_DIM = 64
MLA_V_DIM = 128
MLA_QK_DIM = MLA_NOPE_DIM + MLA_ROPE_DIM
ROPE_BASE = 10000.0
RMS_EPS = 1e-6
GN_EPS = 1e-5

LANES = 128
MLA_QK_PAD = 2 * LANES
VMEM_LIMIT_BYTES = 56 * 1024 * 1024

IN_PROJ_ROWS = 1024
IN_PROJ_PARTS = 4
RET_BLOCK = 256
RET_STEP_BLOCKS = 4
ATTN_Q_BLOCK = 2048
ATTN_Q_GROUP = 512
ATTN_KEY_BLOCK = 1024
ATTN_KV_BLOCK = 512
ATTN_MAX_JUMP = 32.0
ATTN_KEY_SLICE = 256
ATTN_LOOKAHEAD = 3
OUT_ROWS = 512
OUT_PARTS = 2


def _rms(x, g):
    return x * lax.rsqrt(jnp.mean(x * x, axis=-1, keepdims=True) + RMS_EPS) * g


def _sigmoid(x):
    return 1.0 / (1.0 + jnp.exp(-x))


def _dot(a, b):
    return jnp.dot(a, b, preferred_element_type=F32)


def _dot_nt(a, b):
    return lax.dot_general(a, b, (((1,), (1,)), ((), ())), preferred_element_type=F32)


def _dot_tn(a, b):
    return lax.dot_general(a, b, (((0,), (0,)), ((), ())), preferred_element_type=F32)


def _rot_half_lanes(x):
    return pltpu.roll(x, LANES // 2, 1)


def _in_proj_kernel(*refs, q_scale, n_cast):
    (x_ref, pos_ref, gmix_ref, win_ref, qg_ref, wuq_ref, kvg_ref, wuk_ref, wuvt_ref,
     rc_ref) = refs[:10]
    cast_in = refs[10:10 + n_cast]
    rq_ref, rk_ref, rv_ref, rg_ref, q_ref, kn_ref, kr_ref, vt_ref = refs[10 + n_cast:18 + n_cast]
    cast_out = refs[18 + n_cast:-1]
    win_bf16_ref = refs[-1]

    @pl.when(pl.program_id(0) == 0)
    def _():
        n_out = win_ref.shape[0]
        for a in range(0, n_out, LANES):
            rows = min(LANES, n_out - a)
            blk = win_ref[a:a + rows, :]
            if rows < LANES:
                blk = jnp.concatenate([blk, jnp.zeros((LANES - rows, blk.shape[1]), F32)], axis=0)
            win_bf16_ref[:, a:a + LANES] = blk.T.astype(BF16)

    for src, dst in zip(cast_in, cast_out):
        dst[...] = src[...].astype(BF16)

    tm = x_ref.shape[0]
    part = tm // IN_PROJ_PARTS
    parts = [slice(i * part, (i + 1) * part) for i in range(IN_PROJ_PARTS)]
    zs = [_dot(_rms(x_ref[r, :], gmix_ref[...]).astype(BF16), win_bf16_ref[...])
          for r in parts]

    lane = lax.broadcasted_iota(jnp.int32, (1, LANES), 1)
    half = LANES // 2
    quarter = LANES // 4
    d = RET_HEAD_DIM
    w = RET_HEADS * d
    k_scale = d ** -0.5
    o = 4 * w
    q_lora = qg_ref.shape[-1]
    kv_lora = kvg_ref.shape[-1]
    tk = vt_ref.shape[-1]
    vt_cols = min(tk, part)
    pos_rows = pos_ref[...].astype(F32)
    pos_cols = jnp.concatenate(
        [pos_rows, jnp.zeros((LANES - pos_rows.shape[0], LANES), F32)], axis=0).T

    for r, z in zip(parts, zs):
        anchor = lax.bitcast_convert_type(z[:, 0:1], jnp.uint32)
        zero = lax.shift_right_logical(lax.shift_right_logical(anchor, jnp.uint32(16)), jnp.uint32(16))
        pos = jnp.concatenate([pos_cols[:, c:c + 1] for c in range(r.start // LANES, r.stop // LANES)],
                              axis=0) + zero.astype(F32)
        h2, h4 = part // 2, part // 4
        ang2 = jnp.where(lane < half, pos[:h2], pos[h2:]) * rc_ref[0:1, :]
        c2, s2 = jnp.cos(ang2), jnp.sin(ang2)
        c2h, s2h = _rot_half_lanes(c2), _rot_half_lanes(s2)
        cos_r = jnp.concatenate([jnp.where(lane < half, c2, c2h),
                                 jnp.where(lane < half, c2h, c2)], axis=0)
        sin_r = jnp.concatenate([jnp.where(lane < half, -s2, s2h),
                                 jnp.where(lane < half, -s2h, s2)], axis=0)
        ang4 = jnp.where(lane < quarter, pos[:h4], jnp.where(
            lane < half, pos[h4:h2], jnp.where(lane < 3 * quarter, pos[h2:h2 + h4], pos[h2 + h4:])
        )) * rc_ref[1:2, :]
        c4 = {0: jnp.cos(ang4)}
        s4 = {0: jnp.sin(ang4)}
        for k in (1, 2, 3):
            c4[k] = pltpu.roll(c4[0], k * quarter, 1)
            s4[k] = pltpu.roll(s4[0], k * quarter, 1)

        def low_half_table(t):
            return jnp.concatenate(
                [jnp.where(lane < quarter, t[(4 - k) % 4],
                           jnp.where(lane < half, t[(5 - k) % 4], 0.0)) for k in range(4)], axis=0)

        cos_m = low_half_table(c4)
        sin_m = low_half_table(s4)

        for hh in range(RET_HEADS):
            q = z[:, hh * d:(hh + 1) * d]
            rq_ref[r, hh * d:(hh + 1) * d] = (q * cos_r + _rot_half_lanes(q) * sin_r).astype(BF16)
            k = z[:, w + hh * d:w + (hh + 1) * d]
            rk_ref[r, hh * d:(hh + 1) * d] = (
                (k * cos_r + _rot_half_lanes(k) * sin_r) * k_scale).astype(BF16)
        rv_ref[r, :] = z[:, 2 * w:3 * w].astype(BF16)
        g = z[:, 3 * w:4 * w]
        rg_ref[r, :] = g * _sigmoid(g)

        cq = z[:, o:o + q_lora]
        ckv = z[:, o + q_lora:o + q_lora + kv_lora]

        def rope_low_half(y):
            rot = jnp.where(lane < quarter, -pltpu.roll(y, 3 * quarter, 1), pltpu.roll(y, quarter, 1))
            return y * cos_m + rot * sin_m

        kpe = z[:, o + q_lora + kv_lora:o + q_lora + kv_lora + LANES]
        kr_ref[r, :] = rope_low_half(kpe).astype(BF16)

        qq = _dot(_rms(cq, qg_ref[...]).astype(BF16), wuq_ref[...])
        for hh in range(MLA_HEADS):
            b = hh * MLA_QK_PAD
            q_ref[r, b:b + LANES] = (qq[:, b:b + LANES] * q_scale).astype(BF16)
            y = qq[:, b + LANES:b + 2 * LANES]
            q_ref[r, b + LANES:b + 2 * LANES] = (
                (y * cos_m + _rot_half_lanes(y) * sin_m) * q_scale).astype(BF16)

        ckvn = _rms(ckv, kvg_ref[...]).astype(BF16)
        kn_ref[r, :] = _dot(ckvn, wuk_ref[...]).astype(BF16)
        vt = _dot_nt(wuvt_ref[...], ckvn).astype(BF16)
        for a in range(r.start, r.stop, vt_cols):
            vt_ref[a // tk, :, a % tk:a % tk + vt_cols] = vt[:, a - r.start:a - r.start + vt_cols]


def _in_proj(x2, pos2, gmix, win, qg, wuq, kvg, wuk, wuvt, rc, cast_weights):
    n, dm = x2.shape
    tm = IN_PROJ_ROWS
    assert n % tm == 0 and tm % (IN_PROJ_PARTS * LANES) == 0 and tm // LANES % 8 == 0, (n, tm)
    steps = n // tm
    bf16_rows = 16
    for a in cast_weights:
        assert a.shape[0] % (steps * bf16_rows) == 0, a.shape
    cast_specs = [pl.BlockSpec((a.shape[0] // steps, a.shape[1]), lambda i: (i, 0))
                  for a in cast_weights]
    cast_shapes = tuple(jax.ShapeDtypeStruct(a.shape, BF16) for a in cast_weights)
    w = RET_HEADS * RET_HEAD_DIM
    const = lambda i: (0, 0)
    row = lambda i: (i, 0)
    full = lambda a: pl.BlockSpec(a.shape, const)
    out_shapes = (
        jax.ShapeDtypeStruct((n, w), BF16),
        jax.ShapeDtypeStruct((n, w), BF16),
        jax.ShapeDtypeStruct((n, w), BF16),
        jax.ShapeDtypeStruct((n, w), F32),
        jax.ShapeDtypeStruct((n, MLA_HEADS * MLA_QK_PAD), BF16),
        jax.ShapeDtypeStruct((n, MLA_HEADS * MLA_NOPE_DIM), BF16),
        jax.ShapeDtypeStruct((n, LANES), BF16),
    )
    tk = ATTN_KV_BLOCK
    vt_shape = jax.ShapeDtypeStruct((n // tk, MLA_HEADS * MLA_V_DIM, tk), BF16)
    out_specs = tuple(pl.BlockSpec((tm, s.shape[1]), row) for s in out_shapes) + (
        pl.BlockSpec((tm // tk, MLA_HEADS * MLA_V_DIM, tk), lambda i: (i, 0, 0)),)
    q_scale = MLA_QK_DIM ** -0.5 * math.log2(math.e)
    outs = pl.pallas_call(
        functools.partial(_in_proj_kernel, q_scale=q_scale, n_cast=len(cast_weights)),
        out_shape=out_shapes + (vt_shape,) + cast_shapes,
        grid=(steps,),
        in_specs=[pl.BlockSpec((tm, dm), row), pl.BlockSpec((tm // LANES, LANES), row),
                  full(gmix), pl.BlockSpec(win.shape, const, pipeline_mode=pl.Buffered(1)),
                  full(qg), full(wuq), full(kvg), full(wuk), full(wuvt),
                  full(rc)] + cast_specs,
        out_specs=out_specs + tuple(cast_specs),
        scratch_shapes=[pltpu.VMEM((win.shape[1], -(-win.shape[0] // LANES) * LANES), BF16)],
        compiler_params=pltpu.CompilerParams(
            dimension_semantics=("arbitrary",), vmem_limit_bytes=VMEM_LIMIT_BYTES),
        name="in_proj",
    )(x2, pos2, gmix, win, qg, wuq, kvg, wuk, wuvt, rc, *cast_weights)
    n_main = len(out_shapes) + 1
    return outs[:n_main], outs[n_main:]


def _retention_kernel(rq_ref, rk_ref, rv_ref, rg_ref, dmat_ref, qdec_ref, kdec_ref, out_ref,
                      state_ref, *, block_decay):
    @pl.when(pl.program_id(0) == 0)
    def _():
        state_ref[...] = jnp.zeros_like(state_ref)

    d = RET_HEAD_DIM
    blk = dmat_ref.shape[-1]
    nblk = rq_ref.shape[1] // blk
    seqs = range(rq_ref.shape[0])
    chains = [(t, b, hh) for t in range(nblk) for b in seqs for hh in range(RET_HEADS)]
    state = {(b, hh): state_ref[b, hh] for b in seqs for hh in range(RET_HEADS)}

    def front(c):
        t, b, hh = c
        rows = slice(t * blk, (t + 1) * blk)
        sl = slice(hh * d, (hh + 1) * d)
        q = rq_ref[b, rows, sl]
        v = rv_ref[b, rows, sl]
        kt = rk_ref[b, rows, sl].astype(F32).T
        st = state[b, hh]
        scores = _dot(q, kt.astype(BF16))
        carried = _dot(q, st.astype(BF16))
        update = _dot((kt * kdec_ref[hh][0:1, :]).astype(BF16), v)
        state[b, hh] = block_decay[hh] * st + update
        return scores, carried, v

    fronts = {0: front(chains[0])}
    for n, (t, b, hh) in enumerate(chains):
        if n + 1 < len(chains):
            fronts[n + 1] = front(chains[n + 1])
        scores, carried, v = fronts.pop(n)
        rows = slice(t * blk, (t + 1) * blk)
        sl = slice(hh * d, (hh + 1) * d)
        o = _dot((scores * dmat_ref[hh]).astype(BF16), v) + carried * qdec_ref[hh]
        mu = jnp.mean(o, axis=-1, keepdims=True)
        oc = o - mu
        var = jnp.mean(oc * oc, axis=-1, keepdims=True)
        out_ref[b, rows, sl] = (oc * lax.rsqrt(var + GN_EPS) * rg_ref[b, rows, sl]).astype(BF16)
    for (b, hh), st in state.items():
        state_ref[b, hh] = st


def _retention(rq, rk, rv, rg, batch, seq):
    n, w = rq.shape
    blk = RET_BLOCK
    assert seq % (RET_STEP_BLOCKS * blk) == 0 and blk % CHUNK == 0, (seq, blk)
    nb = seq // blk
    log_g = np.log1p(-np.exp2(-5.0 - np.arange(RET_HEADS, dtype=np.float64)))
    idx = jnp.arange(blk, dtype=F32)
    lg = jnp.asarray(log_g, F32)
    dist = jnp.abs(idx[:, None] - idx[None, :])
    visible = (idx[None, :] // CHUNK) <= (idx[:, None] // CHUNK)
    dmat = jnp.where(visible[None], jnp.exp(lg[:, None, None] * dist[None]), 0.0)
    qdec = jnp.broadcast_to(jnp.exp(lg[:, None] * (idx + 1.0))[:, :, None], (RET_HEADS, blk, LANES))
    kdec = jnp.broadcast_to(jnp.exp(lg[:, None] * (blk - 1.0 - idx))[:, None, :], (RET_HEADS, 8, blk))
    block_decay = tuple(float(math.exp(g * blk)) for g in log_g)

    seq_block = pl.BlockSpec((batch, RET_STEP_BLOCKS * blk, w), lambda j: (0, j, 0))
    const3 = lambda j: (0, 0, 0)
    shaped = lambda a: a.reshape(batch, seq, w)
    out = pl.pallas_call(
        functools.partial(_retention_kernel, block_decay=block_decay),
        out_shape=jax.ShapeDtypeStruct((batch, seq, w), BF16),
        grid=(nb // RET_STEP_BLOCKS,),
        in_specs=[seq_block] * 4 + [
            pl.BlockSpec(dmat.shape, const3), pl.BlockSpec(qdec.shape, const3),
            pl.BlockSpec(kdec.shape, const3)],
        out_specs=seq_block,
        scratch_shapes=[pltpu.VMEM((batch, RET_HEADS, RET_HEAD_DIM, RET_HEAD_DIM), F32)],
        compiler_params=pltpu.CompilerParams(
            dimension_semantics=("arbitrary",), vmem_limit_bytes=VMEM_LIMIT_BYTES),
        name="retention",
    )(shaped(rq), shaped(rk), shaped(rv), shaped(rg), dmat, qdec, kdec)
    return out.reshape(n, w)


def _attention_kernel(q_ref, kn_ref, kr_ref, vt_ref, out_ref,
                      m_ref, l_ref, acc_ref, jump_ref, qt_ref):
    tq = q_ref.shape[0]
    gw = ATTN_Q_GROUP
    kb = ATTN_KEY_BLOCK
    tk = vt_ref.shape[-1]
    groups = range(tq // gw)
    diag_blocks = tq // kb
    i = pl.program_id(2)
    first_diag = i * diag_blocks

    def keys(j, size=kb):
        start = pl.multiple_of(j * kb, kb)
        return jnp.concatenate(
            [kn_ref[pl.ds(start, size), :], kr_ref[pl.ds(start, size), :]], axis=1)

    def values_t(j, size=kb):
        return jnp.concatenate([vt_ref[j * (kb // tk) + t] for t in range(size // tk)], axis=1)

    def chunk_mask(shape, key0, g):
        kc = (lax.broadcasted_iota(jnp.int32, shape, 0) + key0) // CHUNK
        qc = (lax.broadcasted_iota(jnp.int32, shape, 1) + g * gw) // CHUNK
        return kc <= qc

    def two_pass(g, k, vt, key0=None):
        cols = slice(g * gw, (g + 1) * gw)
        st = _dot_nt(k, q_ref[g * gw:(g + 1) * gw, :])
        if key0 is not None:
            st = jnp.where(chunk_mask(st.shape, key0, g), st, -1e30)
        m_old = m_ref[:, cols]
        m_new = jnp.maximum(m_old, jnp.max(st, axis=0, keepdims=True))
        alpha = jnp.exp2(m_old - m_new)
        p = jnp.exp2(st - m_new)
        l_ref[:, cols] = alpha * l_ref[:, cols] + jnp.sum(p, axis=0, keepdims=True)
        acc_ref[:, cols] = alpha * acc_ref[:, cols] + _dot(vt, p.astype(BF16))
        m_ref[:, cols] = m_new

    def single_pass(j, nblocks, items):
        k = jnp.concatenate([keys(j + t) for t in range(nblocks)], axis=0)
        vt = jnp.concatenate([values_t(j + t) for t in range(nblocks)], axis=1)
        used = sorted({g for g, _, _ in items})
        cols = {g: slice(g * gw, (g + 1) * gw) for g in used}
        qs = {g: qt_ref[:, cols[g]] for g in used}
        m_old = {g: m_ref[:, cols[g]] for g in used}
        l_new = {g: l_ref[:, cols[g]] for g in used}
        acc_new = {g: acc_ref[:, cols[g]] for g in used}
        smax = {}

        def qk(item):
            g, a, _ = item
            return _dot(k[a:a + ATTN_KEY_SLICE, :], qs[g])

        pending = [qk(item) for item in items[:ATTN_LOOKAHEAD]]
        for n, (g, a, key0) in enumerate(items):
            if n + ATTN_LOOKAHEAD < len(items):
                pending.append(qk(items[n + ATTN_LOOKAHEAD]))
            st = pending.pop(0)
            if key0 is not None:
                st = jnp.where(chunk_mask(st.shape, key0, g), st, -1e30)
            p = jnp.exp2(st - m_old[g])
            cmax = jnp.max(st, axis=0, keepdims=True)
            smax[g] = cmax if g not in smax else jnp.maximum(smax[g], cmax)
            l_new[g] = l_new[g] + jnp.sum(p, axis=0, keepdims=True)
            acc_new[g] = acc_new[g] + _dot(vt[:, a:a + ATTN_KEY_SLICE], p.astype(BF16))
        for g in used:
            m_new = jnp.maximum(m_old[g], smax[g])
            alpha = jnp.exp2(m_old[g] - m_new)
            l_ref[:, cols[g]] = alpha * l_new[g]
            acc_ref[:, cols[g]] = alpha * acc_new[g]
            m_ref[:, cols[g]] = m_new
            jump_ref[:, cols[g]] = jnp.maximum(jump_ref[:, cols[g]], smax[g] - m_old[g])

    full_items = [(g, a, None) for a in range(0, tq, ATTN_KEY_SLICE) for g in groups]
    diag_items = [(g, a, a if a >= g * gw else None)
                  for a in range(0, tq, ATTN_KEY_SLICE) for g in groups if a < (g + 1) * gw]

    def fast_body(j, carry):
        single_pass(j * diag_blocks, diag_blocks, full_items)
        return carry

    def safe_body(j, carry):
        k = keys(j)
        vt = values_t(j)
        for g in groups:
            two_pass(g, k, vt)
        return carry

    qt_ref[...] = q_ref[...].astype(F32).T.astype(BF16)
    m_ref[...] = jnp.max(_dot_nt(keys(first_diag, CHUNK), q_ref[...]), axis=0, keepdims=True)
    l_ref[...] = jnp.zeros_like(l_ref)
    acc_ref[...] = jnp.zeros_like(acc_ref)
    jump_ref[...] = jnp.zeros_like(jump_ref)
    lax.fori_loop(0, i, fast_body, 0)
    single_pass(first_diag, diag_blocks, diag_items)

    @pl.when(jnp.max(jump_ref[...]) > ATTN_MAX_JUMP)
    def _():
        m_ref[...] = jnp.full_like(m_ref, -1e30)
        l_ref[...] = jnp.zeros_like(l_ref)
        acc_ref[...] = jnp.zeros_like(acc_ref)
        lax.fori_loop(0, first_diag, safe_body, 0)
        for d in range(diag_blocks):
            for g in groups:
                visible = min(kb, (g + 1) * gw - d * kb)
                if visible > 0:
                    two_pass(g, keys(first_diag + d, visible), values_t(first_diag + d, visible),
                             d * kb)

    out_ref[...] = (acc_ref[...] * (1.0 / l_ref[...])).astype(BF16)


def _attention(q, kn, kr, vt, batch, seq):
    n = q.shape[0]
    tq = ATTN_Q_BLOCK
    tk = ATTN_KV_BLOCK
    assert seq % tq == 0 and tq % ATTN_KEY_BLOCK == 0 and tq % ATTN_Q_GROUP == 0, (seq, tq)
    assert ATTN_KEY_BLOCK % tk == 0 and ATTN_Q_GROUP % tk == 0 and tk % ATTN_KEY_SLICE == 0
    assert ATTN_KEY_SLICE % CHUNK == 0 and ATTN_Q_GROUP % ATTN_KEY_SLICE == 0
    nq = seq // tq
    return pl.pallas_call(
        _attention_kernel,
        out_shape=jax.ShapeDtypeStruct((MLA_HEADS * MLA_V_DIM, n), BF16),
        grid=(batch, MLA_HEADS, nq),
        in_specs=[pl.BlockSpec((tq, MLA_QK_PAD), lambda b, h, i: (b * nq + i, h)),
                  pl.BlockSpec((seq, MLA_NOPE_DIM), lambda b, h, i: (b, h)),
                  pl.BlockSpec((seq, LANES), lambda b, h, i: (b, 0)),
                  pl.BlockSpec((seq // tk, MLA_V_DIM, tk), lambda b, h, i: (b, h, 0))],
        out_specs=pl.BlockSpec((MLA_V_DIM, tq), lambda b, h, i: (h, b * nq + i)),
        scratch_shapes=[pltpu.VMEM((1, tq), F32), pltpu.VMEM((1, tq), F32),
                        pltpu.VMEM((MLA_V_DIM, tq), F32), pltpu.VMEM((1, tq), F32),
                        pltpu.VMEM((MLA_QK_PAD, tq), BF16)],
        compiler_params=pltpu.CompilerParams(
            dimension_semantics=("arbitrary", "arbitrary", "arbitrary"),
            vmem_limit_bytes=VMEM_LIMIT_BYTES),
        name="attention",
    )(q, kn, kr, vt)


def _out_ffn_kernel(x_ref, ret_ref, att_ref, p_ref, wo_ref, gffn_ref, wg_ref, wu_ref, wd_ref,
                    gple_ref, wpg_ref, wpp_ref, gfin_ref, out_ref, *, final_norm):
    tm = x_ref.shape[0]
    part = tm // OUT_PARTS
    parts = tuple(slice(i * part, (i + 1) * part) for i in range(OUT_PARTS))
    each = range(len(parts))
    n_ret = ret_ref.shape[1]
    o = [_dot(ret_ref[r, :], wo_ref[:n_ret, :]) + _dot_tn(att_ref[:, r], wo_ref[n_ret:, :])
         for r in parts]
    pe = [_dot(p_ref[r, :].astype(BF16), wpp_ref[...]) for r in parts]
    x1 = [x_ref[r, :] + o[i] for i, r in enumerate(parts)]
    h = [_rms(x1[i], gffn_ref[...]).astype(BF16) for i in each]
    gu = [(_dot(h[i], wg_ref[...]), _dot(h[i], wu_ref[...])) for i in each]
    act = [(g * _sigmoid(g) * u).astype(BF16) for g, u in gu]
    x2 = [x1[i] + _dot(act[i], wd_ref[...]) for i in each]
    h2 = [_rms(x2[i], gple_ref[...]).astype(BF16) for i in each]
    gate = [_sigmoid(_dot(h2[i], wpg_ref[...])) for i in each]
    for i, r in enumerate(parts):
        x3 = x2[i] + gate[i] * pe[i]
        if final_norm:
            x3 = _rms(x3, gfin_ref[...])
        out_ref[r, :] = x3


def _out_ffn(x2, ret, att, p2, wo, gffn, wg, wu, wd, gple, wpg, wpp, gfin, final_norm):
    n, dm = x2.shape
    tm = OUT_ROWS
    assert n % tm == 0 and tm % (OUT_PARTS * LANES) == 0, (n, tm)
    row = lambda i: (i, 0)
    const = lambda i: (0, 0)
    resident = lambda a: pl.BlockSpec(a.shape, const, pipeline_mode=pl.Buffered(1))
    return pl.pallas_call(
        functools.partial(_out_ffn_kernel, final_norm=final_norm),
        out_shape=jax.ShapeDtypeStruct((n, dm), F32),
        grid=(n // tm,),
        in_specs=[pl.BlockSpec((tm, dm), row), pl.BlockSpec((tm, ret.shape[1]), row),
                  pl.BlockSpec((att.shape[0], tm), lambda i: (0, i)),
                  pl.BlockSpec((tm, p2.shape[1]), row),
                  resident(wo), resident(gffn), resident(wg), resident(wu), resident(wd),
                  resident(gple), resident(wpg), resident(wpp), resident(gfin)],
        out_specs=pl.BlockSpec((tm, dm), row),
        compiler_params=pltpu.CompilerParams(
            dimension_semantics=("arbitrary",), vmem_limit_bytes=VMEM_LIMIT_BYTES),
        name="out_ffn",
    )(x2, ret, att, p2, wo, gffn, wg, wu, wd, gple, wpg, wpp, gfin)


def _prep_in_weights(w_uq, w_ukv):
    q_lora = w_uq.shape[0]
    uq = w_uq.reshape(q_lora, MLA_HEADS, MLA_QK_DIM)
    rope = uq[:, :, MLA_NOPE_DIM:]
    rot = jnp.concatenate([-rope[..., MLA_ROPE_DIM // 2:], rope[..., :MLA_ROPE_DIM // 2]], axis=-1)
    wuq = jnp.concatenate([uq, rot], axis=-1).reshape(q_lora, MLA_HEADS * MLA_QK_PAD)
    kv_lora = w_ukv.shape[0]
    ukv = w_ukv.reshape(kv_lora, MLA_HEADS, MLA_NOPE_DIM + MLA_V_DIM)
    wuk = ukv[:, :, :MLA_NOPE_DIM].reshape(kv_lora, -1)
    wuvt = ukv[:, :, MLA_NOPE_DIM:].reshape(kv_lora, -1).T
    return wuq.astype(BF16), wuk.astype(BF16), wuvt.astype(BF16)


def _rope_consts():
    def inv(half):
        return jnp.exp(-math.log(ROPE_BASE) * jnp.arange(half, dtype=F32) / half)
    inv_r = inv(RET_HEAD_DIM // 2)
    inv_m = inv(MLA_ROPE_DIM // 2)
    rows = [jnp.tile(inv_r, LANES // inv_r.size), jnp.tile(inv_m, LANES // inv_m.size)]
    return jnp.concatenate([jnp.stack(rows), jnp.zeros((6, LANES), F32)], axis=0)


def kernel(x, p, positions, mix_norm_g, w_in, q_norm_g, w_uq, kv_norm_g, w_ukv, w_o, ffn_norm_g,
           w_ffn_gate, w_ffn_up, w_ffn_down, ple_norm_g, w_ple_gate, w_ple_proj, final_norm_g):
    batch, seq, dm = x.shape
    depth = w_in.shape[0]
    n = batch * seq
    x2 = x.reshape(n, dm)
    pos2 = positions.reshape(n // LANES, LANES)
    rc = _rope_consts()
    vec = lambda g: g.reshape(1, -1).astype(F32)
    for i in range(depth):
        wuq, wuk, wuvt = _prep_in_weights(w_uq[i], w_ukv[i])
        (rq, rk, rv, rg, q, kn, kr, vt), (wo, wg, wu, wd, wpg, wpp) = _in_proj(
            x2, pos2, vec(mix_norm_g[i]), w_in[i].T, vec(q_norm_g[i]), wuq, vec(kv_norm_g[i]), wuk,
            wuvt,
            rc, [w_o[i], w_ffn_gate[i], w_ffn_up[i], w_ffn_down[i], w_ple_gate[i], w_ple_proj[i]])
        ret = _retention(rq, rk, rv, rg, batch, seq)
        att = _attention(q, kn, kr, vt, batch, seq)
        x2 = _out_ffn(
            x2, ret, att, p[i].reshape(n, -1), wo, vec(ffn_norm_g[i]), wg, wu, wd,
            vec(ple_norm_g[i]), wpg, wpp, vec(final_norm_g), final_norm=(i == depth - 1))
    return x2.reshape(batch, seq, dm)
```

```python
import functools
import math

import numpy as np
import jax
import jax.numpy as jnp
from jax import lax
from jax.experimental import pallas as pl
from jax.experimental.pallas import tpu as pltpu

F32 = jnp.float32
BF16 = jnp.bfloat16

CHUNK = 64
RET_HEADS = 4
RET_HEAD_DIM = 128
MLA_HEADS = 4
MLA_NOPE_DIM = 128
MLA_ROPE_DIM = 64
MLA_V_DIM = 128
MLA_QK_DIM = MLA_NOPE_DIM + MLA_ROPE_DIM
ROPE_BASE = 10000.0
RMS_EPS = 1e-6
GN_EPS = 1e-5

LANES = 128
MLA_QK_PAD = 2 * LANES
VMEM_LIMIT_BYTES = 56 * 1024 * 1024
OUT_VMEM_LIMIT_BYTES = 60 * 1024 * 1024

IN_PROJ_ROWS = 1024
IN_PROJ_PARTS = 4
RET_BLOCK = 256
RET_STEP_BLOCKS = 4
ATTN_Q_BLOCK = 2048
ATTN_Q_GROUP = 512
ATTN_KEY_BLOCK = 1024
ATTN_KV_BLOCK = 512
ATTN_MAX_JUMP = 32.0
ATTN_KEY_SLICE = 256
ATTN_LOOKAHEAD = 3
OUT_ROWS = 1024
OUT_PARTS = 4
OUT_INTERLEAVE = 2


def _rms(x, g):
    return x * lax.rsqrt(jnp.mean(x * x, axis=-1, keepdims=True) + RMS_EPS) * g


def _sigmoid(x):
    return 1.0 / (1.0 + jnp.exp(-x))


def _dot(a, b):
    return jnp.dot(a, b, preferred_element_type=F32)


def _dot_nt(a, b):
    return lax.dot_general(a, b, (((1,), (1,)), ((), ())), preferred_element_type=F32)


def _dot_tn(a, b):
    return lax.dot_general(a, b, (((0,), (0,)), ((), ())), preferred_element_type=F32)


def _rot_half_lanes(x):
    return pltpu.roll(x, LANES // 2, 1)


def _in_proj_kernel(*refs, q_scale, n_cast):
    (x_ref, pos_ref, gmix_ref, win_ref, qg_ref, wuq_ref, kvg_ref, wuk_ref, wuvt_ref,
     rc_ref) = refs[:10]
    cast_in = refs[10:10 + n_cast]
    rq_ref, rk_ref, rv_ref, rg_ref, q_ref, kn_ref, kr_ref, vt_ref = refs[10 + n_cast:18 + n_cast]
    cast_out = refs[18 + n_cast:-1]
    win_bf16_ref = refs[-1]

    @pl.when(pl.program_id(0) == 0)
    def _():
        n_out = win_ref.shape[0]
        for a in range(0, n_out, LANES):
            rows = min(LANES, n_out - a)
            blk = win_ref[a:a + rows, :]
            if rows < LANES:
                blk = jnp.concatenate([blk, jnp.zeros((LANES - rows, blk.shape[1]), F32)], axis=0)
            win_bf16_ref[:, a:a + LANES] = blk.T.astype(BF16)

    for src, dst in zip(cast_in, cast_out):
        dst[...] = src[...].astype(BF16)

    tm = x_ref.shape[0]
    part = tm // IN_PROJ_PARTS
    parts = [slice(i * part, (i + 1) * part) for i in range(IN_PROJ_PARTS)]
    zs = [_dot(_rms(x_ref[r, :], gmix_ref[...]).astype(BF16), win_bf16_ref[...])
          for r in parts]

    lane = lax.broadcasted_iota(jnp.int32, (1, LANES), 1)
    half = LANES // 2
    quarter = LANES // 4
    d = RET_HEAD_DIM
    w = RET_HEADS * d
    k_scale = d ** -0.5
    o = 4 * w
    q_lora = qg_ref.shape[-1]
    kv_lora = kvg_ref.shape[-1]
    tk = vt_ref.shape[-1]
    vt_cols = min(tk, part)
    pos_rows = pos_ref[...].astype(F32)
    pos_cols = jnp.concatenate(
        [pos_rows, jnp.zeros((LANES - pos_rows.shape[0], LANES), F32)], axis=0).T

    for r, z in zip(parts, zs):
        anchor = lax.bitcast_convert_type(z[:, 0:1], jnp.uint32)
        zero = lax.shift_right_logical(lax.shift_right_logical(anchor, jnp.uint32(16)), jnp.uint32(16))
        pos = jnp.concatenate([pos_cols[:, c:c + 1] for c in range(r.start // LANES, r.stop // LANES)],
                              axis=0) + zero.astype(F32)
        h2, h4 = part // 2, part // 4
        ang2 = jnp.where(lane < half, pos[:h2], pos[h2:]) * rc_ref[0:1, :]
        c2, s2 = jnp.cos(ang2), jnp.sin(ang2)
        c2h, s2h = _rot_half_lanes(c2), _rot_half_lanes(s2)
        cos_r = jnp.concatenate([jnp.where(lane < half, c2, c2h),
                                 jnp.where(lane < half, c2h, c2)], axis=0)
        sin_r = jnp.concatenate([jnp.where(lane < half, -s2, s2h),
                                 jnp.where(lane < half, -s2h, s2)], axis=0)
        ang4 = jnp.where(lane < quarter, pos[:h4], jnp.where(
            lane < half, pos[h4:h2], jnp.where(lane < 3 * quarter, pos[h2:h2 + h4], pos[h2 + h4:])
        )) * rc_ref[1:2, :]
        c4 = {0: jnp.cos(ang4)}
        s4 = {0: jnp.sin(ang4)}
        for k in (1, 2, 3):
            c4[k] = pltpu.roll(c4[0], k * quarter, 1)
            s4[k] = pltpu.roll(s4[0], k * quarter, 1)

        def low_half_table(t):
            return jnp.concatenate(
                [jnp.where(lane < quarter, t[(4 - k) % 4],
                           jnp.where(lane < half, t[(5 - k) % 4], 0.0)) for k in range(4)], axis=0)

        cos_m = low_half_table(c4)
        sin_m = low_half_table(s4)

        for hh in range(RET_HEADS):
            q = z[:, hh * d:(hh + 1) * d]
            rq_ref[r, hh * d:(hh + 1) * d] = (q * cos_r + _rot_half_lanes(q) * sin_r).astype(BF16)
            k = z[:, w + hh * d:w + (hh + 1) * d]
            rk_ref[r, hh * d:(hh + 1) * d] = (
                (k * cos_r + _rot_half_lanes(k) * sin_r) * k_scale).astype(BF16)
        rv_ref[r, :] = z[:, 2 * w:3 * w].astype(BF16)
        g = z[:, 3 * w:4 * w]
        rg_ref[r, :] = g * _sigmoid(g)

        cq = z[:, o:o + q_lora]
        ckv = z[:, o + q_lora:o + q_lora + kv_lora]

        def rope_low_half(y):
            rot = jnp.where(lane < quarter, -pltpu.roll(y, 3 * quarter, 1), pltpu.roll(y, quarter, 1))
            return y * cos_m + rot * sin_m

        kpe = z[:, o + q_lora + kv_lora:o + q_lora + kv_lora + LANES]
        kr_ref[r, :] = rope_low_half(kpe).astype(BF16)

        qq = _dot(_rms(cq, qg_ref[...]).astype(BF16), wuq_ref[...])
        for hh in range(MLA_HEADS):
            b = hh * MLA_QK_PAD
            q_ref[r, b:b + LANES] = (qq[:, b:b + LANES] * q_scale).astype(BF16)
            y = qq[:, b + LANES:b + 2 * LANES]
            q_ref[r, b + LANES:b + 2 * LANES] = (
                (y * cos_m + _rot_half_lanes(y) * sin_m) * q_scale).astype(BF16)

        ckvn = _rms(ckv, kvg_ref[...]).astype(BF16)
        kn_ref[r, :] = _dot(ckvn, wuk_ref[...]).astype(BF16)
        vt = _dot_nt(wuvt_ref[...], ckvn).astype(BF16)
        for a in range(r.start, r.stop, vt_cols):
            vt_ref[a // tk, :, a % tk:a % tk + vt_cols] = vt[:, a - r.start:a - r.start + vt_cols]


def _in_proj(x2, pos2, gmix, win, qg, wuq, kvg, wuk, wuvt, rc, cast_weights):
    n, dm = x2.shape
    tm = IN_PROJ_ROWS
    assert n % tm == 0 and tm % (IN_PROJ_PARTS * LANES) == 0 and tm // LANES % 8 == 0, (n, tm)
    steps = n // tm
    bf16_rows = 16
    for a in cast_weights:
        assert a.shape[0] % (steps * bf16_rows) == 0, a.shape
    cast_specs = [pl.BlockSpec((a.shape[0] // steps, a.shape[1]), lambda i: (i, 0))
                  for a in cast_weights]
    cast_shapes = tuple(jax.ShapeDtypeStruct(a.shape, BF16) for a in cast_weights)
    w = RET_HEADS * RET_HEAD_DIM
    const = lambda i: (0, 0)
    row = lambda i: (i, 0)
    full = lambda a: pl.BlockSpec(a.shape, const)
    out_shapes = (
        jax.ShapeDtypeStruct((n, w), BF16),
        jax.ShapeDtypeStruct((n, w), BF16),
        jax.ShapeDtypeStruct((n, w), BF16),
        jax.ShapeDtypeStruct((n, w), F32),
        jax.ShapeDtypeStruct((n, MLA_HEADS * MLA_QK_PAD), BF16),
        jax.ShapeDtypeStruct((n, MLA_HEADS * MLA_NOPE_DIM), BF16),
        jax.ShapeDtypeStruct((n, LANES), BF16),
    )
    tk = ATTN_KV_BLOCK
    vt_shape = jax.ShapeDtypeStruct((n // tk, MLA_HEADS * MLA_V_DIM, tk), BF16)
    out_specs = tuple(pl.BlockSpec((tm, s.shape[1]), row) for s in out_shapes) + (
        pl.BlockSpec((tm // tk, MLA_HEADS * MLA_V_DIM, tk), lambda i: (i, 0, 0)),)
    q_scale = MLA_QK_DIM ** -0.5 * math.log2(math.e)
    outs = pl.pallas_call(
        functools.partial(_in_proj_kernel, q_scale=q_scale, n_cast=len(cast_weights)),
        out_shape=out_shapes + (vt_shape,) + cast_shapes,
        grid=(steps,),
        in_specs=[pl.BlockSpec((tm, dm), row), pl.BlockSpec((tm // LANES, LANES), row),
                  full(gmix), pl.BlockSpec(win.shape, const, pipeline_mode=pl.Buffered(1)),
                  full(qg), full(wuq), full(kvg), full(wuk), full(wuvt),
                  full(rc)] + cast_specs,
        out_specs=out_specs + tuple(cast_specs),
        scratch_shapes=[pltpu.VMEM((win.shape[1], -(-win.shape[0] // LANES) * LANES), BF16)],
        compiler_params=pltpu.CompilerParams(
            dimension_semantics=("arbitrary",), vmem_limit_bytes=VMEM_LIMIT_BYTES),
        name="in_proj",
    )(x2, pos2, gmix, win, qg, wuq, kvg, wuk, wuvt, rc, *cast_weights)
    n_main = len(out_shapes) + 1
    return outs[:n_main], outs[n_main:]


def _retention_kernel(rq_ref, rk_ref, rv_ref, rg_ref, dmat_ref, qdec_ref, kdec_ref, out_ref,
                      state_ref, *, block_decay):
    @pl.when(pl.program_id(0) == 0)
    def _():
        state_ref[...] = jnp.zeros_like(state_ref)

    d = RET_HEAD_DIM
    blk = dmat_ref.shape[-1]
    nblk = rq_ref.shape[1] // blk
    seqs = range(rq_ref.shape[0])
    chains = [(t, b, hh) for t in range(nblk) for b in seqs for hh in range(RET_HEADS)]
    state = {(b, hh): state_ref[b, hh] for b in seqs for hh in range(RET_HEADS)}

    def front(c):
        t, b, hh = c
        rows = slice(t * blk, (t + 1) * blk)
        sl = slice(hh * d, (hh + 1) * d)
        q = rq_ref[b, rows, sl]
        v = rv_ref[b, rows, sl]
        kt = rk_ref[b, rows, sl].astype(F32).T
        st = state[b, hh]
        scores = _dot(q, kt.astype(BF16))
        carried = _dot(q, st.astype(BF16))
        update = _dot((kt * kdec_ref[hh][0:1, :]).astype(BF16), v)
        state[b, hh] = block_decay[hh] * st + update
        return scores, carried, v

    fronts = {0: front(chains[0])}
    for n, (t, b, hh) in enumerate(chains):
        if n + 1 < len(chains):
            fronts[n + 1] = front(chains[n + 1])
        scores, carried, v = fronts.pop(n)
        rows = slice(t * blk, (t + 1) * blk)
        sl = slice(hh * d, (hh + 1) * d)
        o = _dot((scores * dmat_ref[hh]).astype(BF16), v) + carried * qdec_ref[hh]
        mu = jnp.mean(o, axis=-1, keepdims=True)
        oc = o - mu
        var = jnp.mean(oc * oc, axis=-1, keepdims=True)
        out_ref[b, rows, sl] = (oc * lax.rsqrt(var + GN_EPS) * rg_ref[b, rows, sl]).astype(BF16)
    for (b, hh), st in state.items():
        state_ref[b, hh] = st


def _retention(rq, rk, rv, rg, batch, seq):
    n, w = rq.shape
    blk = RET_BLOCK
    assert seq % (RET_STEP_BLOCKS * blk) == 0 and blk % CHUNK == 0, (seq, blk)
    nb = seq // blk
    log_g = np.log1p(-np.exp2(-5.0 - np.arange(RET_HEADS, dtype=np.float64)))
    idx = jnp.arange(blk, dtype=F32)
    lg = jnp.asarray(log_g, F32)
    dist = jnp.abs(idx[:, None] - idx[None, :])
    visible = (idx[None, :] // CHUNK) <= (idx[:, None] // CHUNK)
    dmat = jnp.where(visible[None], jnp.exp(lg[:, None, None] * dist[None]), 0.0)
    qdec = jnp.broadcast_to(jnp.exp(lg[:, None] * (idx + 1.0))[:, :, None], (RET_HEADS, blk, LANES))
    kdec = jnp.broadcast_to(jnp.exp(lg[:, None] * (blk - 1.0 - idx))[:, None, :], (RET_HEADS, 8, blk))
    block_decay = tuple(float(math.exp(g * blk)) for g in log_g)

    seq_block = pl.BlockSpec((batch, RET_STEP_BLOCKS * blk, w), lambda j: (0, j, 0))
    const3 = lambda j: (0, 0, 0)
    shaped = lambda a: a.reshape(batch, seq, w)
    out = pl.pallas_call(
        functools.partial(_retention_kernel, block_decay=block_decay),
        out_shape=jax.ShapeDtypeStruct((batch, seq, w), BF16),
        grid=(nb // RET_STEP_BLOCKS,),
        in_specs=[seq_block] * 4 + [
            pl.BlockSpec(dmat.shape, const3), pl.BlockSpec(qdec.shape, const3),
            pl.BlockSpec(kdec.shape, const3)],
        out_specs=seq_block,
        scratch_shapes=[pltpu.VMEM((batch, RET_HEADS, RET_HEAD_DIM, RET_HEAD_DIM), F32)],
        compiler_params=pltpu.CompilerParams(
            dimension_semantics=("arbitrary",), vmem_limit_bytes=VMEM_LIMIT_BYTES),
        name="retention",
    )(shaped(rq), shaped(rk), shaped(rv), shaped(rg), dmat, qdec, kdec)
    return out.reshape(n, w)


def _attention_kernel(q_ref, kn_ref, kr_ref, vt_ref, out_ref,
                      m_ref, l_ref, acc_ref, jump_ref, qt_ref):
    tq = q_ref.shape[0]
    gw = ATTN_Q_GROUP
    kb = ATTN_KEY_BLOCK
    tk = vt_ref.shape[-1]
    groups = range(tq // gw)
    diag_blocks = tq // kb
    i = pl.program_id(2)
    first_diag = i * diag_blocks

    def keys(j, size=kb):
        start = pl.multiple_of(j * kb, kb)
        return jnp.concatenate(
            [kn_ref[pl.ds(start, size), :], kr_ref[pl.ds(start, size), :]], axis=1)

    def values_t(j, size=kb):
        return jnp.concatenate([vt_ref[j * (kb // tk) + t] for t in range(size // tk)], axis=1)

    def chunk_mask(shape, key0, g):
        kc = (lax.broadcasted_iota(jnp.int32, shape, 0) + key0) // CHUNK
        qc = (lax.broadcasted_iota(jnp.int32, shape, 1) + g * gw) // CHUNK
        return kc <= qc

    def two_pass(g, k, vt, key0=None):
        cols = slice(g * gw, (g + 1) * gw)
        st = _dot_nt(k, q_ref[g * gw:(g + 1) * gw, :])
        if key0 is not None:
            st = jnp.where(chunk_mask(st.shape, key0, g), st, -1e30)
        m_old = m_ref[:, cols]
        m_new = jnp.maximum(m_old, jnp.max(st, axis=0, keepdims=True))
        alpha = jnp.exp2(m_old - m_new)
        p = jnp.exp2(st - m_new)
        l_ref[:, cols] = alpha * l_ref[:, cols] + jnp.sum(p, axis=0, keepdims=True)
        acc_ref[:, cols] = alpha * acc_ref[:, cols] + _dot(vt, p.astype(BF16))
        m_ref[:, cols] = m_new

    def single_pass(j, nblocks, items):
        k = jnp.concatenate([keys(j + t) for t in range(nblocks)], axis=0)
        vt = jnp.concatenate([values_t(j + t) for t in range(nblocks)], axis=1)
        used = sorted({g for g, _, _ in items})
        cols = {g: slice(g * gw, (g + 1) * gw) for g in used}
        qs = {g: qt_ref[:, cols[g]] for g in used}
        m_old = {g: m_ref[:, cols[g]] for g in used}
        l_new = {g: l_ref[:, cols[g]] for g in used}
        acc_new = {g: acc_ref[:, cols[g]] for g in used}
        smax = {}

        def qk(item):
            g, a, _ = item
            return _dot(k[a:a + ATTN_KEY_SLICE, :], qs[g])

        pending = [qk(item) for item in items[:ATTN_LOOKAHEAD]]
        for n, (g, a, key0) in enumerate(items):
            if n + ATTN_LOOKAHEAD < len(items):
                pending.append(qk(items[n + ATTN_LOOKAHEAD]))
            st = pending.pop(0)
            if key0 is not None:
                st = jnp.where(chunk_mask(st.shape, key0, g), st, -1e30)
            p = jnp.exp2(st - m_old[g])
            cmax = jnp.max(st, axis=0, keepdims=True)
            smax[g] = cmax if g not in smax else jnp.maximum(smax[g], cmax)
            l_new[g] = l_new[g] + jnp.sum(p, axis=0, keepdims=True)
            acc_new[g] = acc_new[g] + _dot(vt[:, a:a + ATTN_KEY_SLICE], p.astype(BF16))
        for g in used:
            m_new = jnp.maximum(m_old[g], smax[g])
            alpha = jnp.exp2(m_old[g] - m_new)
            l_ref[:, cols[g]] = alpha * l_new[g]
            acc_ref[:, cols[g]] = alpha * acc_new[g]
            m_ref[:, cols[g]] = m_new
            jump_ref[:, cols[g]] = jnp.maximum(jump_ref[:, cols[g]], smax[g] - m_old[g])

    full_items = [(g, a, None) for a in range(0, tq, ATTN_KEY_SLICE) for g in groups]
    diag_items = [(g, a, a if a >= g * gw else None)
                  for a in range(0, tq, ATTN_KEY_SLICE) for g in groups if a < (g + 1) * gw]

    def fast_body(j, carry):
        single_pass(j * diag_blocks, diag_blocks, full_items)
        return carry

    def safe_body(j, carry):
        k = keys(j)
        vt = values_t(j)
        for g in groups:
            two_pass(g, k, vt)
        return carry

    qt_ref[...] = q_ref[...].T
    m_ref[...] = jnp.max(_dot_nt(keys(first_diag, CHUNK), q_ref[...]), axis=0, keepdims=True)
    l_ref[...] = jnp.zeros_like(l_ref)
    acc_ref[...] = jnp.zeros_like(acc_ref)
    jump_ref[...] = jnp.zeros_like(jump_ref)
    lax.fori_loop(0, i, fast_body, 0)
    single_pass(first_diag, diag_blocks, diag_items)

    @pl.when(jnp.max(jump_ref[...]) > ATTN_MAX_JUMP)
    def _():
        m_ref[...] = jnp.full_like(m_ref, -1e30)
        l_ref[...] = jnp.zeros_like(l_ref)
        acc_ref[...] = jnp.zeros_like(acc_ref)
        lax.fori_loop(0, first_diag, safe_body, 0)
        for d in range(diag_blocks):
            for g in groups:
                visible = min(kb, (g + 1) * gw - d * kb)
                if visible > 0:
                    two_pass(g, keys(first_diag + d, visible), values_t(first_diag + d, visible),
                             d * kb)

    out_ref[...] = (acc_ref[...] * (1.0 / l_ref[...])).astype(BF16)


def _attention(q, kn, kr, vt, batch, seq):
    n = q.shape[0]
    tq = ATTN_Q_BLOCK
    tk = ATTN_KV_BLOCK
    assert seq % tq == 0 and tq % ATTN_KEY_BLOCK == 0 and tq % ATTN_Q_GROUP == 0, (seq, tq)
    assert ATTN_KEY_BLOCK % tk == 0 and ATTN_Q_GROUP % tk == 0 and tk % ATTN_KEY_SLICE == 0
    assert ATTN_KEY_SLICE % CHUNK == 0 and ATTN_Q_GROUP % ATTN_KEY_SLICE == 0
    nq = seq // tq
    return pl.pallas_call(
        _attention_kernel,
        out_shape=jax.ShapeDtypeStruct((MLA_HEADS * MLA_V_DIM, n), BF16),
        grid=(batch, MLA_HEADS, nq),
        in_specs=[pl.BlockSpec((tq, MLA_QK_PAD), lambda b, h, i: (b * nq + i, h)),
                  pl.BlockSpec((seq, MLA_NOPE_DIM), lambda b, h, i: (b, h)),
                  pl.BlockSpec((seq, LANES), lambda b, h, i: (b, 0)),
                  pl.BlockSpec((seq // tk, MLA_V_DIM, tk), lambda b, h, i: (b, h, 0))],
        out_specs=pl.BlockSpec((MLA_V_DIM, tq), lambda b, h, i: (h, b * nq + i)),
        scratch_shapes=[pltpu.VMEM((1, tq), F32), pltpu.VMEM((1, tq), F32),
                        pltpu.VMEM((MLA_V_DIM, tq), F32), pltpu.VMEM((1, tq), F32),
                        pltpu.VMEM((MLA_QK_PAD, tq), BF16)],
        compiler_params=pltpu.CompilerParams(
            dimension_semantics=("arbitrary", "arbitrary", "arbitrary"),
            vmem_limit_bytes=VMEM_LIMIT_BYTES),
        name="attention",
    )(q, kn, kr, vt)


def _out_ffn_kernel(x_ref, ret_ref, att_ref, p_ref, wo_ref, gffn_ref, wg_ref, wu_ref, wd_ref,
                    gple_ref, wpg_ref, wpp_ref, gfin_ref, out_ref, *, final_norm):
    tm = x_ref.shape[0]
    part = tm // OUT_PARTS
    n_ret = ret_ref.shape[1]
    for first in range(0, OUT_PARTS, OUT_INTERLEAVE):
        parts = tuple(slice(i * part, (i + 1) * part) for i in range(first, first + OUT_INTERLEAVE))
        each = range(len(parts))
        o = [_dot(ret_ref[r, :], wo_ref[:n_ret, :]) + _dot_tn(att_ref[:, r], wo_ref[n_ret:, :])
             for r in parts]
        pe = [_dot(p_ref[r, :].astype(BF16), wpp_ref[...]) for r in parts]
        x1 = [x_ref[r, :] + o[i] for i, r in enumerate(parts)]
        h = [_rms(x1[i], gffn_ref[...]).astype(BF16) for i in each]
        gu = [(_dot(h[i], wg_ref[...]), _dot(h[i], wu_ref[...])) for i in each]
        act = [(g * _sigmoid(g) * u).astype(BF16) for g, u in gu]
        x2 = [x1[i] + _dot(act[i], wd_ref[...]) for i in each]
        h2 = [_rms(x2[i], gple_ref[...]).astype(BF16) for i in each]
        gate = [_sigmoid(_dot(h2[i], wpg_ref[...])) for i in each]
        for i, r in enumerate(parts):
            x3 = x2[i] + gate[i] * pe[i]
            if final_norm:
                x3 = _rms(x3, gfin_ref[...])
            out_ref[r, :] = x3


def _out_ffn(x2, ret, att, p2, wo, gffn, wg, wu, wd, gple, wpg, wpp, gfin, final_norm):
    n, dm = x2.shape
    tm = OUT_ROWS
    assert n % tm == 0 and tm % (OUT_PARTS * LANES) == 0, (n, tm)
    row = lambda i: (i, 0)
    const = lambda i: (0, 0)
    resident = lambda a: pl.BlockSpec(a.shape, const, pipeline_mode=pl.Buffered(1))
    return pl.pallas_call(
        functools.partial(_out_ffn_kernel, final_norm=final_norm),
        out_shape=jax.ShapeDtypeStruct((n, dm), F32),
        grid=(n // tm,),
        in_specs=[pl.BlockSpec((tm, dm), row), pl.BlockSpec((tm, ret.shape[1]), row),
                  pl.BlockSpec((att.shape[0], tm), lambda i: (0, i)),
                  pl.BlockSpec((tm, p2.shape[1]), row),
                  resident(wo), resident(gffn), resident(wg), resident(wu), resident(wd),
                  resident(gple), resident(wpg), resident(wpp), resident(gfin)],
        out_specs=pl.BlockSpec((tm, dm), row),
        compiler_params=pltpu.CompilerParams(
            dimension_semantics=("arbitrary",), vmem_limit_bytes=OUT_VMEM_LIMIT_BYTES),
        name="out_ffn",
    )(x2, ret, att, p2, wo, gffn, wg, wu, wd, gple, wpg, wpp, gfin)


def _prep_in_weights(w_uq, w_ukv):
    q_lora = w_uq.shape[0]
    uq = w_uq.reshape(q_lora, MLA_HEADS, MLA_QK_DIM)
    rope = uq[:, :, MLA_NOPE_DIM:]
    rot = jnp.concatenate([-rope[..., MLA_ROPE_DIM // 2:], rope[..., :MLA_ROPE_DIM // 2]], axis=-1)
    wuq = jnp.concatenate([uq, rot], axis=-1).reshape(q_lora, MLA_HEADS * MLA_QK_PAD)
    kv_lora = w_ukv.shape[0]
    ukv = w_ukv.reshape(kv_lora, MLA_HEADS, MLA_NOPE_DIM + MLA_V_DIM)
    wuk = ukv[:, :, :MLA_NOPE_DIM].reshape(kv_lora, -1)
    wuvt = ukv[:, :, MLA_NOPE_DIM:].reshape(kv_lora, -1).T
    return wuq.astype(BF16), wuk.astype(BF16), wuvt.astype(BF16)


def _rope_consts():
    def inv(half):
        return jnp.exp(-math.log(ROPE_BASE) * jnp.arange(half, dtype=F32) / half)
    inv_r = inv(RET_HEAD_DIM // 2)
    inv_m = inv(MLA_ROPE_DIM // 2)
    rows = [jnp.tile(inv_r, LANES // inv_r.size), jnp.tile(inv_m, LANES // inv_m.size)]
    return jnp.concatenate([jnp.stack(rows), jnp.zeros((6, LANES), F32)], axis=0)


def kernel(x, p, positions, mix_norm_g, w_in, q_norm_g, w_uq, kv_norm_g, w_ukv, w_o, ffn_norm_g,
           w_ffn_gate, w_ffn_up, w_ffn_down, ple_norm_g, w_ple_gate, w_ple_proj, final_norm_g):
    batch, seq, dm = x.shape
    depth = w_in.shape[0]
    n = batch * seq
    x2 = x.reshape(n, dm)
    pos2 = positions.reshape(n // LANES, LANES)
    rc = _rope_consts()
    vec = lambda g: g.reshape(1, -1).astype(F32)
    for i in range(depth):
        wuq, wuk, wuvt = _prep_in_weights(w_uq[i], w_ukv[i])
        (rq, rk, rv, rg, q, kn, kr, vt), (wo, wg, wu, wd, wpg, wpp) = _in_proj(
            x2, pos2, vec(mix_norm_g[i]), w_in[i].T, vec(q_norm_g[i]), wuq, vec(kv_norm_g[i]), wuk,
            wuvt,
            rc, [w_o[i], w_ffn_gate[i], w_ffn_up[i], w_ffn_down[i], w_ple_gate[i], w_ple_proj[i]])
        ret = _retention(rq, rk, rv, rg, batch, seq)
        att = _attention(q, kn, kr, vt, batch, seq)
        x2 = _out_ffn(
            x2, ret, att, p[i].reshape(n, -1), wo, vec(ffn_norm_g[i]), wg, wu, wd,
            vec(ple_norm_g[i]), wpg, wpp, vec(final_norm_g), final_norm=(i == depth - 1))
    return x2.reshape(batch, seq, dm)
```

```python
import functools
import math

import numpy as np
import jax
import jax.numpy as jnp
from jax import lax
from jax.experimental import pallas as pl
from jax.experimental.pallas import tpu as pltpu

F32 = jnp.float32
BF16 = jnp.bfloat16

CHUNK = 64
RET_HEADS = 4
RET_HEAD_DIM = 128
MLA_HEADS = 4
MLA_NOPE_DIM = 128
MLA_ROPE_DIM = 64
MLA_V_DIM = 128
MLA_QK_DIM = MLA_NOPE_DIM + MLA_ROPE_DIM
ROPE_BASE = 10000.0
RMS_EPS = 1e-6
GN_EPS = 1e-5

LANES = 128
MLA_QK_PAD = 2 * LANES
VMEM_LIMIT_BYTES = 56 * 1024 * 1024
OUT_VMEM_LIMIT_BYTES = 60 * 1024 * 1024
SMALL_VMEM_LIMIT_BYTES = 32 * 1024 * 1024

IN_PROJ_ROWS = 1024
IN_PROJ_PARTS = 4
RET_BLOCK = 256
RET_STEP_BLOCKS = 4
ATTN_Q_BLOCK = 2048
ATTN_Q_GROUP = 512
ATTN_KEY_BLOCK = 1024
ATTN_KV_BLOCK = 512
ATTN_MAX_JUMP = 32.0
ATTN_KEY_SLICE = 256
ATTN_LOOKAHEAD = 3
OUT_ROWS = 1024
OUT_PARTS = 4
OUT_INTERLEAVE = 2


def _rms(x, g):
    return x * lax.rsqrt(jnp.mean(x * x, axis=-1, keepdims=True) + RMS_EPS) * g


def _sigmoid(x):
    return 1.0 / (1.0 + jnp.exp(-x))


def _dot(a, b):
    return jnp.dot(a, b, preferred_element_type=F32)


def _dot_nt(a, b):
    return lax.dot_general(a, b, (((1,), (1,)), ((), ())), preferred_element_type=F32)


def _dot_tn(a, b):
    return lax.dot_general(a, b, (((0,), (0,)), ((), ())), preferred_element_type=F32)


def _rot_half_lanes(x):
    return pltpu.roll(x, LANES // 2, 1)


def _in_proj_kernel(*refs, q_scale, n_cast):
    (x_ref, pos_ref, gmix_ref, win_ref, qg_ref, wuq_ref, kvg_ref, wuk_ref, wuvt_ref,
     rc_ref) = refs[:10]
    cast_in = refs[10:10 + n_cast]
    rq_ref, rk_ref, rv_ref, rg_ref, q_ref, kn_ref, kr_ref, vt_ref = refs[10 + n_cast:18 + n_cast]
    cast_out = refs[18 + n_cast:-1]
    win_bf16_ref = refs[-1]

    @pl.when(pl.program_id(0) == 0)
    def _():
        n_out = win_ref.shape[0]
        for a in range(0, n_out, LANES):
            rows = min(LANES, n_out - a)
            blk = win_ref[a:a + rows, :]
            if rows < LANES:
                blk = jnp.concatenate([blk, jnp.zeros((LANES - rows, blk.shape[1]), F32)], axis=0)
            win_bf16_ref[:, a:a + LANES] = blk.T.astype(BF16)

    for src, dst in zip(cast_in, cast_out):
        dst[...] = src[...].astype(BF16)

    tm = x_ref.shape[0]
    part = tm // IN_PROJ_PARTS
    parts = [slice(i * part, (i + 1) * part) for i in range(IN_PROJ_PARTS)]
    zs = [_dot(_rms(x_ref[r, :], gmix_ref[...]).astype(BF16), win_bf16_ref[...])
          for r in parts]

    lane = lax.broadcasted_iota(jnp.int32, (1, LANES), 1)
    half = LANES // 2
    quarter = LANES // 4
    d = RET_HEAD_DIM
    w = RET_HEADS * d
    k_scale = d ** -0.5
    o = 4 * w
    q_lora = qg_ref.shape[-1]
    kv_lora = kvg_ref.shape[-1]
    tk = vt_ref.shape[-1]
    vt_cols = min(tk, part)
    pos_rows = pos_ref[...].astype(F32)
    pos_cols = jnp.concatenate(
        [pos_rows, jnp.zeros((LANES - pos_rows.shape[0], LANES), F32)], axis=0).T

    for r, z in zip(parts, zs):
        anchor = lax.bitcast_convert_type(z[:, 0:1], jnp.uint32)
        zero = lax.shift_right_logical(lax.shift_right_logical(anchor, jnp.uint32(16)), jnp.uint32(16))
        pos = jnp.concatenate([pos_cols[:, c:c + 1] for c in range(r.start // LANES, r.stop // LANES)],
                              axis=0) + zero.astype(F32)
        h2, h4 = part // 2, part // 4
        ang2 = jnp.where(lane < half, pos[:h2], pos[h2:]) * rc_ref[0:1, :]
        c2, s2 = jnp.cos(ang2), jnp.sin(ang2)
        c2h, s2h = _rot_half_lanes(c2), _rot_half_lanes(s2)
        cos_r = jnp.concatenate([jnp.where(lane < half, c2, c2h),
                                 jnp.where(lane < half, c2h, c2)], axis=0)
        sin_r = jnp.concatenate([jnp.where(lane < half, -s2, s2h),
                                 jnp.where(lane < half, -s2h, s2)], axis=0)
        ang4 = jnp.where(lane < quarter, pos[:h4], jnp.where(
            lane < half, pos[h4:h2], jnp.where(lane < 3 * quarter, pos[h2:h2 + h4], pos[h2 + h4:])
        )) * rc_ref[1:2, :]
        c4 = {0: jnp.cos(ang4)}
        s4 = {0: jnp.sin(ang4)}
        for k in (1, 2, 3):
            c4[k] = pltpu.roll(c4[0], k * quarter, 1)
            s4[k] = pltpu.roll(s4[0], k * quarter, 1)

        def low_half_table(t):
            return jnp.concatenate(
                [jnp.where(lane < quarter, t[(4 - k) % 4],
                           jnp.where(lane < half, t[(5 - k) % 4], 0.0)) for k in range(4)], axis=0)

        cos_m = low_half_table(c4)
        sin_m = low_half_table(s4)

        for hh in range(RET_HEADS):
            q = z[:, hh * d:(hh + 1) * d]
            rq_ref[r, hh * d:(hh + 1) * d] = (q * cos_r + _rot_half_lanes(q) * sin_r).astype(BF16)
            k = z[:, w + hh * d:w + (hh + 1) * d]
            rk_ref[r, hh * d:(hh + 1) * d] = (
                (k * cos_r + _rot_half_lanes(k) * sin_r) * k_scale).astype(BF16)
        rv_ref[r, :] = z[:, 2 * w:3 * w].astype(BF16)
        g = z[:, 3 * w:4 * w]
        rg_ref[r, :] = g * _sigmoid(g)

        cq = z[:, o:o + q_lora]
        ckv = z[:, o + q_lora:o + q_lora + kv_lora]

        def rope_low_half(y):
            rot = jnp.where(lane < quarter, -pltpu.roll(y, 3 * quarter, 1), pltpu.roll(y, quarter, 1))
            return y * cos_m + rot * sin_m

        kpe = z[:, o + q_lora + kv_lora:o + q_lora + kv_lora + LANES]
        kr_ref[r, :] = rope_low_half(kpe).astype(BF16)

        qq = _dot(_rms(cq, qg_ref[...]).astype(BF16), wuq_ref[...])
        for hh in range(MLA_HEADS):
            b = hh * MLA_QK_PAD
            q_ref[r, b:b + LANES] = (qq[:, b:b + LANES] * q_scale).astype(BF16)
            y = qq[:, b + LANES:b + 2 * LANES]
            q_ref[r, b + LANES:b + 2 * LANES] = (
                (y * cos_m + _rot_half_lanes(y) * sin_m) * q_scale).astype(BF16)

        ckvn = _rms(ckv, kvg_ref[...]).astype(BF16)
        kn_ref[r, :] = _dot(ckvn, wuk_ref[...]).astype(BF16)
        vt = _dot_nt(wuvt_ref[...], ckvn).astype(BF16)
        for a in range(r.start, r.stop, vt_cols):
            vt_ref[a // tk, :, a % tk:a % tk + vt_cols] = vt[:, a - r.start:a - r.start + vt_cols]


def _in_proj(x2, pos2, gmix, win, qg, wuq, kvg, wuk, wuvt, rc, cast_weights):
    n, dm = x2.shape
    tm = IN_PROJ_ROWS
    assert n % tm == 0 and tm % (IN_PROJ_PARTS * LANES) == 0 and tm // LANES % 8 == 0, (n, tm)
    steps = n // tm
    bf16_rows = 16
    for a in cast_weights:
        assert a.shape[0] % (steps * bf16_rows) == 0, a.shape
    cast_specs = [pl.BlockSpec((a.shape[0] // steps, a.shape[1]), lambda i: (i, 0))
                  for a in cast_weights]
    cast_shapes = tuple(jax.ShapeDtypeStruct(a.shape, BF16) for a in cast_weights)
    w = RET_HEADS * RET_HEAD_DIM
    const = lambda i: (0, 0)
    row = lambda i: (i, 0)
    full = lambda a: pl.BlockSpec(a.shape, const)
    out_shapes = (
        jax.ShapeDtypeStruct((n, w), BF16),
        jax.ShapeDtypeStruct((n, w), BF16),
        jax.ShapeDtypeStruct((n, w), BF16),
        jax.ShapeDtypeStruct((n, w), F32),
        jax.ShapeDtypeStruct((n, MLA_HEADS * MLA_QK_PAD), BF16),
        jax.ShapeDtypeStruct((n, MLA_HEADS * MLA_NOPE_DIM), BF16),
        jax.ShapeDtypeStruct((n, LANES), BF16),
    )
    tk = ATTN_KV_BLOCK
    vt_shape = jax.ShapeDtypeStruct((n // tk, MLA_HEADS * MLA_V_DIM, tk), BF16)
    out_specs = tuple(pl.BlockSpec((tm, s.shape[1]), row) for s in out_shapes) + (
        pl.BlockSpec((tm // tk, MLA_HEADS * MLA_V_DIM, tk), lambda i: (i, 0, 0)),)
    q_scale = MLA_QK_DIM ** -0.5 * math.log2(math.e)
    outs = pl.pallas_call(
        functools.partial(_in_proj_kernel, q_scale=q_scale, n_cast=len(cast_weights)),
        out_shape=out_shapes + (vt_shape,) + cast_shapes,
        grid=(steps,),
        in_specs=[pl.BlockSpec((tm, dm), row), pl.BlockSpec((tm // LANES, LANES), row),
                  full(gmix), pl.BlockSpec(win.shape, const, pipeline_mode=pl.Buffered(1)),
                  full(qg), full(wuq), full(kvg), full(wuk), full(wuvt),
                  full(rc)] + cast_specs,
        out_specs=out_specs + tuple(cast_specs),
        scratch_shapes=[pltpu.VMEM((win.shape[1], -(-win.shape[0] // LANES) * LANES), BF16)],
        compiler_params=pltpu.CompilerParams(
            dimension_semantics=("arbitrary",), vmem_limit_bytes=VMEM_LIMIT_BYTES),
        name="in_proj",
    )(x2, pos2, gmix, win, qg, wuq, kvg, wuk, wuvt, rc, *cast_weights)
    n_main = len(out_shapes) + 1
    return outs[:n_main], outs[n_main:]


def _retention_kernel(rq_ref, rk_ref, rv_ref, rg_ref, dmat_ref, qdec_ref, kdec_ref, out_ref,
                      state_ref, *, block_decay):
    @pl.when(pl.program_id(0) == 0)
    def _():
        state_ref[...] = jnp.zeros_like(state_ref)

    d = RET_HEAD_DIM
    blk = dmat_ref.shape[-1]
    nblk = rq_ref.shape[1] // blk
    seqs = range(rq_ref.shape[0])
    chains = [(t, b, hh) for t in range(nblk) for b in seqs for hh in range(RET_HEADS)]
    state = {(b, hh): state_ref[b, hh] for b in seqs for hh in range(RET_HEADS)}

    def front(c):
        t, b, hh = c
        rows = slice(t * blk, (t + 1) * blk)
        sl = slice(hh * d, (hh + 1) * d)
        q = rq_ref[b, rows, sl]
        v = rv_ref[b, rows, sl]
        kt = rk_ref[b, rows, sl].astype(F32).T
        st = state[b, hh]
        scores = _dot(q, kt.astype(BF16))
        carried = _dot(q, st.astype(BF16))
        update = _dot((kt * kdec_ref[hh][0:1, :]).astype(BF16), v)
        state[b, hh] = block_decay[hh] * st + update
        return scores, carried, v

    fronts = {0: front(chains[0])}
    for n, (t, b, hh) in enumerate(chains):
        if n + 1 < len(chains):
            fronts[n + 1] = front(chains[n + 1])
        scores, carried, v = fronts.pop(n)
        rows = slice(t * blk, (t + 1) * blk)
        sl = slice(hh * d, (hh + 1) * d)
        o = _dot((scores * dmat_ref[hh]).astype(BF16), v) + carried * qdec_ref[hh]
        mu = jnp.mean(o, axis=-1, keepdims=True)
        oc = o - mu
        var = jnp.mean(oc * oc, axis=-1, keepdims=True)
        out_ref[b, rows, sl] = (oc * lax.rsqrt(var + GN_EPS) * rg_ref[b, rows, sl]).astype(BF16)
    for (b, hh), st in state.items():
        state_ref[b, hh] = st


def _retention(rq, rk, rv, rg, batch, seq):
    n, w = rq.shape
    blk = RET_BLOCK
    assert seq % (RET_STEP_BLOCKS * blk) == 0 and blk % CHUNK == 0, (seq, blk)
    nb = seq // blk
    log_g = np.log1p(-np.exp2(-5.0 - np.arange(RET_HEADS, dtype=np.float64)))
    idx = jnp.arange(blk, dtype=F32)
    lg = jnp.asarray(log_g, F32)
    dist = jnp.abs(idx[:, None] - idx[None, :])
    visible = (idx[None, :] // CHUNK) <= (idx[:, None] // CHUNK)
    dmat = jnp.where(visible[None], jnp.exp(lg[:, None, None] * dist[None]), 0.0)
    qdec = jnp.broadcast_to(jnp.exp(lg[:, None] * (idx + 1.0))[:, :, None], (RET_HEADS, blk, LANES))
    kdec = jnp.broadcast_to(jnp.exp(lg[:, None] * (blk - 1.0 - idx))[:, None, :], (RET_HEADS, 8, blk))
    block_decay = tuple(float(math.exp(g * blk)) for g in log_g)

    seq_block = pl.BlockSpec((batch, RET_STEP_BLOCKS * blk, w), lambda j: (0, j, 0))
    const3 = lambda j: (0, 0, 0)
    shaped = lambda a: a.reshape(batch, seq, w)
    out = pl.pallas_call(
        functools.partial(_retention_kernel, block_decay=block_decay),
        out_shape=jax.ShapeDtypeStruct((batch, seq, w), BF16),
        grid=(nb // RET_STEP_BLOCKS,),
        in_specs=[seq_block] * 4 + [
            pl.BlockSpec(dmat.shape, const3), pl.BlockSpec(qdec.shape, const3),
            pl.BlockSpec(kdec.shape, const3)],
        out_specs=seq_block,
        scratch_shapes=[pltpu.VMEM((batch, RET_HEADS, RET_HEAD_DIM, RET_HEAD_DIM), F32)],
        compiler_params=pltpu.CompilerParams(
            dimension_semantics=("arbitrary",), vmem_limit_bytes=SMALL_VMEM_LIMIT_BYTES),
        name="retention",
    )(shaped(rq), shaped(rk), shaped(rv), shaped(rg), dmat, qdec, kdec)
    return out.reshape(n, w)


def _attention_kernel(q_ref, kn_ref, kr_ref, vt_ref, out_ref,
                      m_ref, l_ref, acc_ref, jump_ref, qt_ref):
    tq = q_ref.shape[0]
    gw = ATTN_Q_GROUP
    kb = ATTN_KEY_BLOCK
    tk = vt_ref.shape[-1]
    groups = range(tq // gw)
    diag_blocks = tq // kb
    i = pl.program_id(2)
    first_diag = i * diag_blocks

    def keys(j, size=kb):
        start = pl.multiple_of(j * kb, kb)
        return jnp.concatenate(
            [kn_ref[pl.ds(start, size), :], kr_ref[pl.ds(start, size), :]], axis=1)

    def values_t(j, size=kb):
        return jnp.concatenate([vt_ref[j * (kb // tk) + t] for t in range(size // tk)], axis=1)

    def chunk_mask(shape, key0, g):
        kc = (lax.broadcasted_iota(jnp.int32, shape, 0) + key0) // CHUNK
        qc = (lax.broadcasted_iota(jnp.int32, shape, 1) + g * gw) // CHUNK
        return kc <= qc

    def two_pass(g, k, vt, key0=None):
        cols = slice(g * gw, (g + 1) * gw)
        st = _dot_nt(k, q_ref[g * gw:(g + 1) * gw, :])
        if key0 is not None:
            st = jnp.where(chunk_mask(st.shape, key0, g), st, -1e30)
        m_old = m_ref[:, cols]
        m_new = jnp.maximum(m_old, jnp.max(st, axis=0, keepdims=True))
        alpha = jnp.exp2(m_old - m_new)
        p = jnp.exp2(st - m_new)
        l_ref[:, cols] = alpha * l_ref[:, cols] + jnp.sum(p, axis=0, keepdims=True)
        acc_ref[:, cols] = alpha * acc_ref[:, cols] + _dot(vt, p.astype(BF16))
        m_ref[:, cols] = m_new

    def single_pass(j, nblocks, items):
        k = jnp.concatenate([keys(j + t) for t in range(nblocks)], axis=0)
        vt = jnp.concatenate([values_t(j + t) for t in range(nblocks)], axis=1)
        used = sorted({g for g, _, _ in items})
        cols = {g: slice(g * gw, (g + 1) * gw) for g in used}
        qs = {g: qt_ref[:, cols[g]] for g in used}
        m_old = {g: m_ref[:, cols[g]] for g in used}
        l_new = {g: l_ref[:, cols[g]] for g in used}
        acc_new = {g: acc_ref[:, cols[g]] for g in used}
        smax = {}

        def qk(item):
            g, a, _ = item
            return _dot(k[a:a + ATTN_KEY_SLICE, :], qs[g])

        pending = [qk(item) for item in items[:ATTN_LOOKAHEAD]]
        for n, (g, a, key0) in enumerate(items):
            if n + ATTN_LOOKAHEAD < len(items):
                pending.append(qk(items[n + ATTN_LOOKAHEAD]))
            st = pending.pop(0)
            if key0 is not None:
                st = jnp.where(chunk_mask(st.shape, key0, g), st, -1e30)
            p = jnp.exp2(st - m_old[g])
            cmax = jnp.max(st, axis=0, keepdims=True)
            smax[g] = cmax if g not in smax else jnp.maximum(smax[g], cmax)
            l_new[g] = l_new[g] + jnp.sum(p, axis=0, keepdims=True)
            acc_new[g] = acc_new[g] + _dot(vt[:, a:a + ATTN_KEY_SLICE], p.astype(BF16))
        for g in used:
            m_new = jnp.maximum(m_old[g], smax[g])
            alpha = jnp.exp2(m_old[g] - m_new)
            l_ref[:, cols[g]] = alpha * l_new[g]
            acc_ref[:, cols[g]] = alpha * acc_new[g]
            m_ref[:, cols[g]] = m_new
            jump_ref[:, cols[g]] = jnp.maximum(jump_ref[:, cols[g]], smax[g] - m_old[g])

    full_items = [(g, a, None) for a in range(0, tq, ATTN_KEY_SLICE) for g in groups]
    diag_items = [(g, a, a if a >= g * gw else None)
                  for a in range(0, tq, ATTN_KEY_SLICE) for g in groups if a < (g + 1) * gw]

    def fast_body(j, carry):
        single_pass(j * diag_blocks, diag_blocks, full_items)
        return carry

    def safe_body(j, carry):
        k = keys(j)
        vt = values_t(j)
        for g in groups:
            two_pass(g, k, vt)
        return carry

    qt_ref[...] = q_ref[...].T
    m_ref[...] = jnp.max(_dot_nt(keys(first_diag, CHUNK), q_ref[...]), axis=0, keepdims=True)
    l_ref[...] = jnp.zeros_like(l_ref)
    acc_ref[...] = jnp.zeros_like(acc_ref)
    jump_ref[...] = jnp.zeros_like(jump_ref)
    lax.fori_loop(0, i, fast_body, 0)
    single_pass(first_diag, diag_blocks, diag_items)

    @pl.when(jnp.max(jump_ref[...]) > ATTN_MAX_JUMP)
    def _():
        m_ref[...] = jnp.full_like(m_ref, -1e30)
        l_ref[...] = jnp.zeros_like(l_ref)
        acc_ref[...] = jnp.zeros_like(acc_ref)
        lax.fori_loop(0, first_diag, safe_body, 0)
        for d in range(diag_blocks):
            for g in groups:
                visible = min(kb, (g + 1) * gw - d * kb)
                if visible > 0:
                    two_pass(g, keys(first_diag + d, visible), values_t(first_diag + d, visible),
                             d * kb)

    out_ref[...] = (acc_ref[...] * (1.0 / l_ref[...])).astype(BF16)


def _attention(q, kn, kr, vt, batch, seq):
    n = q.shape[0]
    tq = ATTN_Q_BLOCK
    tk = ATTN_KV_BLOCK
    assert seq % tq == 0 and tq % ATTN_KEY_BLOCK == 0 and tq % ATTN_Q_GROUP == 0, (seq, tq)
    assert ATTN_KEY_BLOCK % tk == 0 and ATTN_Q_GROUP % tk == 0 and tk % ATTN_KEY_SLICE == 0
    assert ATTN_KEY_SLICE % CHUNK == 0 and ATTN_Q_GROUP % ATTN_KEY_SLICE == 0
    nq = seq // tq
    return pl.pallas_call(
        _attention_kernel,
        out_shape=jax.ShapeDtypeStruct((MLA_HEADS * MLA_V_DIM, n), BF16),
        grid=(batch, MLA_HEADS, nq),
        in_specs=[pl.BlockSpec((tq, MLA_QK_PAD), lambda b, h, i: (b * nq + i, h)),
                  pl.BlockSpec((seq, MLA_NOPE_DIM), lambda b, h, i: (b, h)),
                  pl.BlockSpec((seq, LANES), lambda b, h, i: (b, 0)),
                  pl.BlockSpec((seq // tk, MLA_V_DIM, tk), lambda b, h, i: (b, h, 0))],
        out_specs=pl.BlockSpec((MLA_V_DIM, tq), lambda b, h, i: (h, b * nq + i)),
        scratch_shapes=[pltpu.VMEM((1, tq), F32), pltpu.VMEM((1, tq), F32),
                        pltpu.VMEM((MLA_V_DIM, tq), F32), pltpu.VMEM((1, tq), F32),
                        pltpu.VMEM((MLA_QK_PAD, tq), BF16)],
        compiler_params=pltpu.CompilerParams(
            dimension_semantics=("arbitrary", "arbitrary", "arbitrary"),
            vmem_limit_bytes=SMALL_VMEM_LIMIT_BYTES),
        name="attention",
    )(q, kn, kr, vt)


def _out_ffn_kernel(x_ref, ret_ref, att_ref, p_ref, wo_ref, gffn_ref, wg_ref, wu_ref, wd_ref,
                    gple_ref, wpg_ref, wpp_ref, gfin_ref, out_ref, *, final_norm):
    tm = x_ref.shape[0]
    part = tm // OUT_PARTS
    n_ret = ret_ref.shape[1]
    for first in range(0, OUT_PARTS, OUT_INTERLEAVE):
        parts = tuple(slice(i * part, (i + 1) * part) for i in range(first, first + OUT_INTERLEAVE))
        each = range(len(parts))
        o = [_dot(ret_ref[r, :], wo_ref[:n_ret, :]) + _dot_tn(att_ref[:, r], wo_ref[n_ret:, :])
             for r in parts]
        pe = [_dot(p_ref[r, :].astype(BF16), wpp_ref[...]) for r in parts]
        x1 = [x_ref[r, :] + o[i] for i, r in enumerate(parts)]
        h = [_rms(x1[i], gffn_ref[...]).astype(BF16) for i in each]
        gu = [(_dot(h[i], wg_ref[...]), _dot(h[i], wu_ref[...])) for i in each]
        act = [(g * _sigmoid(g) * u).astype(BF16) for g, u in gu]
        x2 = [x1[i] + _dot(act[i], wd_ref[...]) for i in each]
        h2 = [_rms(x2[i], gple_ref[...]).astype(BF16) for i in each]
        gate = [_sigmoid(_dot(h2[i], wpg_ref[...])) for i in each]
        for i, r in enumerate(parts):
            x3 = x2[i] + gate[i] * pe[i]
            if final_norm:
                x3 = _rms(x3, gfin_ref[...])
            out_ref[r, :] = x3


def _out_ffn(x2, ret, att, p2, wo, gffn, wg, wu, wd, gple, wpg, wpp, gfin, final_norm):
    n, dm = x2.shape
    tm = OUT_ROWS
    assert n % tm == 0 and tm % (OUT_PARTS * LANES) == 0, (n, tm)
    row = lambda i: (i, 0)
    const = lambda i: (0, 0)
    resident = lambda a: pl.BlockSpec(a.shape, const, pipeline_mode=pl.Buffered(1))
    return pl.pallas_call(
        functools.partial(_out_ffn_kernel, final_norm=final_norm),
        out_shape=jax.ShapeDtypeStruct((n, dm), F32),
        grid=(n // tm,),
        in_specs=[pl.BlockSpec((tm, dm), row), pl.BlockSpec((tm, ret.shape[1]), row),
                  pl.BlockSpec((att.shape[0], tm), lambda i: (0, i)),
                  pl.BlockSpec((tm, p2.shape[1]), row),
                  resident(wo), resident(gffn), resident(wg), resident(wu), resident(wd),
                  resident(gple), resident(wpg), resident(wpp), resident(gfin)],
        out_specs=pl.BlockSpec((tm, dm), row),
        compiler_params=pltpu.CompilerParams(
            dimension_semantics=("arbitrary",), vmem_limit_bytes=OUT_VMEM_LIMIT_BYTES),
        name="out_ffn",
    )(x2, ret, att, p2, wo, gffn, wg, wu, wd, gple, wpg, wpp, gfin)


def _prep_in_weights(w_uq, w_ukv):
    q_lora = w_uq.shape[0]
    uq = w_uq.reshape(q_lora, MLA_HEADS, MLA_QK_DIM)
    rope = uq[:, :, MLA_NOPE_DIM:]
    rot = jnp.concatenate([-rope[..., MLA_ROPE_DIM // 2:], rope[..., :MLA_ROPE_DIM // 2]], axis=-1)
    wuq = jnp.concatenate([uq, rot], axis=-1).reshape(q_lora, MLA_HEADS * MLA_QK_PAD)
    kv_lora = w_ukv.shape[0]
    ukv = w_ukv.reshape(kv_lora, MLA_HEADS, MLA_NOPE_DIM + MLA_V_DIM)
    wuk = ukv[:, :, :MLA_NOPE_DIM].reshape(kv_lora, -1)
    wuvt = ukv[:, :, MLA_NOPE_DIM:].reshape(kv_lora, -1).T
    return wuq.astype(BF16), wuk.astype(BF16), wuvt.astype(BF16)


def _rope_consts():
    def inv(half):
        return jnp.exp(-math.log(ROPE_BASE) * jnp.arange(half, dtype=F32) / half)
    inv_r = inv(RET_HEAD_DIM // 2)
    inv_m = inv(MLA_ROPE_DIM // 2)
    rows = [jnp.tile(inv_r, LANES // inv_r.size), jnp.tile(inv_m, LANES // inv_m.size)]
    return jnp.concatenate([jnp.stack(rows), jnp.zeros((6, LANES), F32)], axis=0)


def kernel(x, p, positions, mix_norm_g, w_in, q_norm_g, w_uq, kv_norm_g, w_ukv, w_o, ffn_norm_g,
           w_ffn_gate, w_ffn_up, w_ffn_down, ple_norm_g, w_ple_gate, w_ple_proj, final_norm_g):
    batch, seq, dm = x.shape
    depth = w_in.shape[0]
    n = batch * seq
    x2 = x.reshape(n, dm)
    pos2 = positions.reshape(n // LANES, LANES)
    rc = _rope_consts()
    vec = lambda g: g.reshape(1, -1).astype(F32)
    for i in range(depth):
        wuq, wuk, wuvt = _prep_in_weights(w_uq[i], w_ukv[i])
        (rq, rk, rv, rg, q, kn, kr, vt), (wo, wg, wu, wd, wpg, wpp) = _in_proj(
            x2, pos2, vec(mix_norm_g[i]), w_in[i].T, vec(q_norm_g[i]), wuq, vec(kv_norm_g[i]), wuk,
            wuvt,
            rc, [w_o[i], w_ffn_gate[i], w_ffn_up[i], w_ffn_down[i], w_ple_gate[i], w_ple_proj[i]])
        ret = _retention(rq, rk, rv, rg, batch, seq)
        att = _attention(q, kn, kr, vt, batch, seq)
        x2 = _out_ffn(
            x2, ret, att, p[i].reshape(n, -1), wo, vec(ffn_norm_g[i]), wg, wu, wd,
            vec(ple_norm_g[i]), wpg, wpp, vec(final_norm_g), final_norm=(i == depth - 1))
    return x2.reshape(batch, seq, dm)
```

```python
import functools
import math

import numpy as np
import jax
import jax.numpy as jnp
from jax import lax
from jax.experimental import pallas as pl
from jax.experimental.pallas import tpu as pltpu

F32 = jnp.float32
BF16 = jnp.bfloat16

CHUNK = 64
RET_HEADS = 4
RET_HEAD_DIM = 128
MLA_HEADS = 4
MLA_NOPE_DIM = 128
MLA_ROPE_DIM = 64
MLA_V_DIM = 128
MLA_QK_DIM = MLA_NOPE_DIM + MLA_ROPE_DIM
ROPE_BASE = 10000.0
RMS_EPS = 1e-6
GN_EPS = 1e-5

LANES = 128
MLA_QK_PAD = 2 * LANES
VMEM_LIMIT_BYTES = 56 * 1024 * 1024
OUT_VMEM_LIMIT_BYTES = 60 * 1024 * 1024

IN_PROJ_ROWS = 1024
IN_PROJ_PARTS = 4
RET_BLOCK = 256
RET_STEP_BLOCKS = 4
ATTN_Q_BLOCK = 2048
ATTN_Q_GROUP = 512
ATTN_KEY_BLOCK = 1024
ATTN_KV_BLOCK = 512
ATTN_MAX_JUMP = 32.0
ATTN_KEY_SLICE = 256
ATTN_LOOKAHEAD = 3
OUT_ROWS = 1024
OUT_PARTS = 4
OUT_INTERLEAVE = 2


def _rms(x, g):
    return x * lax.rsqrt(jnp.mean(x * x, axis=-1, keepdims=True) + RMS_EPS) * g


def _sigmoid(x):
    return 1.0 / (1.0 + jnp.exp(-x))


def _dot(a, b):
    return jnp.dot(a, b, preferred_element_type=F32)


def _dot_nt(a, b):
    return lax.dot_general(a, b, (((1,), (1,)), ((), ())), preferred_element_type=F32)


def _dot_tn(a, b):
    return lax.dot_general(a, b, (((0,), (0,)), ((), ())), preferred_element_type=F32)


def _rot_half_lanes(x):
    return pltpu.roll(x, LANES // 2, 1)


def _in_proj_kernel(*refs, q_scale, n_cast):
    (x_ref, pos_ref, gmix_ref, win_ref, qg_ref, wuq_ref, kvg_ref, wuk_ref, wuvt_ref,
     rc_ref) = refs[:10]
    cast_in = refs[10:10 + n_cast]
    rq_ref, rk_ref, rv_ref, rg_ref, q_ref, kn_ref, kr_ref, vt_ref = refs[10 + n_cast:18 + n_cast]
    cast_out = refs[18 + n_cast:-1]
    win_bf16_ref = refs[-1]

    @pl.when(pl.program_id(0) == 0)
    def _():
        n_out = win_ref.shape[0]
        for a in range(0, n_out, LANES):
            rows = min(LANES, n_out - a)
            blk = win_ref[a:a + rows, :]
            if rows < LANES:
                blk = jnp.concatenate([blk, jnp.zeros((LANES - rows, blk.shape[1]), F32)], axis=0)
            win_bf16_ref[:, a:a + LANES] = blk.T.astype(BF16)

    for src, dst in zip(cast_in, cast_out):
        dst[...] = src[...].astype(BF16)

    tm = x_ref.shape[0]
    part = tm // IN_PROJ_PARTS
    parts = [slice(i * part, (i + 1) * part) for i in range(IN_PROJ_PARTS)]
    zs = [_dot(_rms(x_ref[r, :], gmix_ref[...]).astype(BF16), win_bf16_ref[...])
          for r in parts]

    lane = lax.broadcasted_iota(jnp.int32, (1, LANES), 1)
    half = LANES // 2
    quarter = LANES // 4
    d = RET_HEAD_DIM
    w = RET_HEADS * d
    k_scale = d ** -0.5
    o = 4 * w
    q_lora = qg_ref.shape[-1]
    kv_lora = kvg_ref.shape[-1]
    tk = vt_ref.shape[-1]
    vt_cols = min(tk, part)
    pos_rows = pos_ref[...].astype(F32)
    pos_cols = jnp.concatenate(
        [pos_rows, jnp.zeros((LANES - pos_rows.shape[0], LANES), F32)], axis=0).T

    for r, z in zip(parts, zs):
        anchor = lax.bitcast_convert_type(z[:, 0:1], jnp.uint32)
        zero = lax.shift_right_logical(lax.shift_right_logical(anchor, jnp.uint32(16)), jnp.uint32(16))
        pos = jnp.concatenate([pos_cols[:, c:c + 1] for c in range(r.start // LANES, r.stop // LANES)],
                              axis=0) + zero.astype(F32)
        h2, h4 = part // 2, part // 4
        ang2 = jnp.where(lane < half, pos[:h2], pos[h2:]) * rc_ref[0:1, :]
        c2, s2 = jnp.cos(ang2), jnp.sin(ang2)
        c2h, s2h = _rot_half_lanes(c2), _rot_half_lanes(s2)
        cos_r = jnp.concatenate([jnp.where(lane < half, c2, c2h),
                                 jnp.where(lane < half, c2h, c2)], axis=0)
        sin_r = jnp.concatenate([jnp.where(lane < half, -s2, s2h),
                                 jnp.where(lane < half, -s2h, s2)], axis=0)
        ang4 = jnp.where(lane < quarter, pos[:h4], jnp.where(
            lane < half, pos[h4:h2], jnp.where(lane < 3 * quarter, pos[h2:h2 + h4], pos[h2 + h4:])
        )) * rc_ref[1:2, :]
        c4 = {0: jnp.cos(ang4)}
        s4 = {0: jnp.sin(ang4)}
        for k in (1, 2, 3):
            c4[k] = pltpu.roll(c4[0], k * quarter, 1)
            s4[k] = pltpu.roll(s4[0], k * quarter, 1)

        def low_half_table(t):
            return jnp.concatenate(
                [jnp.where(lane < quarter, t[(4 - k) % 4],
                           jnp.where(lane < half, t[(5 - k) % 4], 0.0)) for k in range(4)], axis=0)

        cos_m = low_half_table(c4)
        sin_m = low_half_table(s4)

        for hh in range(RET_HEADS):
            q = z[:, hh * d:(hh + 1) * d]
            rq_ref[r, hh * d:(hh + 1) * d] = (q * cos_r + _rot_half_lanes(q) * sin_r).astype(BF16)
            k = z[:, w + hh * d:w + (hh + 1) * d]
            rk_ref[r, hh * d:(hh + 1) * d] = (
                (k * cos_r + _rot_half_lanes(k) * sin_r) * k_scale).astype(BF16)
        rv_ref[r, :] = z[:, 2 * w:3 * w].astype(BF16)
        g = z[:, 3 * w:4 * w]
        rg_ref[r, :] = g * _sigmoid(g)

        cq = z[:, o:o + q_lora]
        ckv = z[:, o + q_lora:o + q_lora + kv_lora]

        def rope_low_half(y):
            rot = jnp.where(lane < quarter, -pltpu.roll(y, 3 * quarter, 1), pltpu.roll(y, quarter, 1))
            return y * cos_m + rot * sin_m

        kpe = z[:, o + q_lora + kv_lora:o + q_lora + kv_lora + LANES]
        kr_ref[r, :] = rope_low_half(kpe).astype(BF16)

        qq = _dot(_rms(cq, qg_ref[...]).astype(BF16), wuq_ref[...])
        for hh in range(MLA_HEADS):
            b = hh * MLA_QK_PAD
            q_ref[r, b:b + LANES] = (qq[:, b:b + LANES] * q_scale).astype(BF16)
            y = qq[:, b + LANES:b + 2 * LANES]
            q_ref[r, b + LANES:b + 2 * LANES] = (
                (y * cos_m + _rot_half_lanes(y) * sin_m) * q_scale).astype(BF16)

        ckvn = _rms(ckv, kvg_ref[...]).astype(BF16)
        kn_ref[r, :] = _dot(ckvn, wuk_ref[...]).astype(BF16)
        vt = _dot_nt(wuvt_ref[...], ckvn).astype(BF16)
        for a in range(r.start, r.stop, vt_cols):
            vt_ref[a // tk, :, a % tk:a % tk + vt_cols] = vt[:, a - r.start:a - r.start + vt_cols]


def _in_proj(x2, pos2, gmix, win, qg, wuq, kvg, wuk, wuvt, rc, cast_weights):
    n, dm = x2.shape
    tm = IN_PROJ_ROWS
    assert n % tm == 0 and tm % (IN_PROJ_PARTS * LANES) == 0 and tm // LANES % 8 == 0, (n, tm)
    steps = n // tm
    bf16_rows = 16
    for a in cast_weights:
        assert a.shape[0] % (steps * bf16_rows) == 0, a.shape
    cast_specs = [pl.BlockSpec((a.shape[0] // steps, a.shape[1]), lambda i: (i, 0))
                  for a in cast_weights]
    cast_shapes = tuple(jax.ShapeDtypeStruct(a.shape, BF16) for a in cast_weights)
    w = RET_HEADS * RET_HEAD_DIM
    const = lambda i: (0, 0)
    row = lambda i: (i, 0)
    full = lambda a: pl.BlockSpec(a.shape, const)
    out_shapes = (
        jax.ShapeDtypeStruct((n, w), BF16),
        jax.ShapeDtypeStruct((n, w), BF16),
        jax.ShapeDtypeStruct((n, w), BF16),
        jax.ShapeDtypeStruct((n, w), F32),
        jax.ShapeDtypeStruct((n, MLA_HEADS * MLA_QK_PAD), BF16),
        jax.ShapeDtypeStruct((n, MLA_HEADS * MLA_NOPE_DIM), BF16),
        jax.ShapeDtypeStruct((n, LANES), BF16),
    )
    tk = ATTN_KV_BLOCK
    vt_shape = jax.ShapeDtypeStruct((n // tk, MLA_HEADS * MLA_V_DIM, tk), BF16)
    out_specs = tuple(pl.BlockSpec((tm, s.shape[1]), row) for s in out_shapes) + (
        pl.BlockSpec((tm // tk, MLA_HEADS * MLA_V_DIM, tk), lambda i: (i, 0, 0)),)
    q_scale = MLA_QK_DIM ** -0.5 * math.log2(math.e)
    outs = pl.pallas_call(
        functools.partial(_in_proj_kernel, q_scale=q_scale, n_cast=len(cast_weights)),
        out_shape=out_shapes + (vt_shape,) + cast_shapes,
        grid=(steps,),
        in_specs=[pl.BlockSpec((tm, dm), row), pl.BlockSpec((tm // LANES, LANES), row),
                  full(gmix), pl.BlockSpec(win.shape, const, pipeline_mode=pl.Buffered(1)),
                  full(qg), full(wuq), full(kvg), full(wuk), full(wuvt),
                  full(rc)] + cast_specs,
        out_specs=out_specs + tuple(cast_specs),
        scratch_shapes=[pltpu.VMEM((win.shape[1], -(-win.shape[0] // LANES) * LANES), BF16)],
        compiler_params=pltpu.CompilerParams(
            dimension_semantics=("arbitrary",), vmem_limit_bytes=VMEM_LIMIT_BYTES),
        name="in_proj",
    )(x2, pos2, gmix, win, qg, wuq, kvg, wuk, wuvt, rc, *cast_weights)
    n_main = len(out_shapes) + 1
    return outs[:n_main], outs[n_main:]


def _retention_kernel(rq_ref, rk_ref, rv_ref, rg_ref, dmat_ref, qdec_ref, kdec_ref, out_ref,
                      state_ref, *, block_decay):
    @pl.when(pl.program_id(0) == 0)
    def _():
        state_ref[...] = jnp.zeros_like(state_ref)

    d = RET_HEAD_DIM
    blk = dmat_ref.shape[-1]
    nblk = rq_ref.shape[1] // blk
    seqs = range(rq_ref.shape[0])
    chains = [(t, b, hh) for t in range(nblk) for b in seqs for hh in range(RET_HEADS)]
    state = {(b, hh): state_ref[b, hh] for b in seqs for hh in range(RET_HEADS)}

    def front(c):
        t, b, hh = c
        rows = slice(t * blk, (t + 1) * blk)
        sl = slice(hh * d, (hh + 1) * d)
        q = rq_ref[b, rows, sl]
        v = rv_ref[b, rows, sl]
        kt_bf16 = rk_ref[b, rows, sl].T
        kt = kt_bf16.astype(F32)
        st = state[b, hh]
        scores = _dot(q, kt_bf16)
        carried = _dot(q, st.astype(BF16))
        update = _dot((kt * kdec_ref[hh][0:1, :]).astype(BF16), v)
        state[b, hh] = block_decay[hh] * st + update
        return scores, carried, v

    fronts = {0: front(chains[0])}
    for n, (t, b, hh) in enumerate(chains):
        if n + 1 < len(chains):
            fronts[n + 1] = front(chains[n + 1])
        scores, carried, v = fronts.pop(n)
        rows = slice(t * blk, (t + 1) * blk)
        sl = slice(hh * d, (hh + 1) * d)
        o = _dot((scores * dmat_ref[hh]).astype(BF16), v) + carried * qdec_ref[hh]
        mu = jnp.mean(o, axis=-1, keepdims=True)
        oc = o - mu
        var = jnp.mean(oc * oc, axis=-1, keepdims=True)
        out_ref[b, rows, sl] = (oc * lax.rsqrt(var + GN_EPS) * rg_ref[b, rows, sl]).astype(BF16)
    for (b, hh), st in state.items():
        state_ref[b, hh] = st


def _retention(rq, rk, rv, rg, batch, seq):
    n, w = rq.shape
    blk = RET_BLOCK
    assert seq % (RET_STEP_BLOCKS * blk) == 0 and blk % CHUNK == 0, (seq, blk)
    nb = seq // blk
    log_g = np.log1p(-np.exp2(-5.0 - np.arange(RET_HEADS, dtype=np.float64)))
    idx = jnp.arange(blk, dtype=F32)
    lg = jnp.asarray(log_g, F32)
    dist = jnp.abs(idx[:, None] - idx[None, :])
    visible = (idx[None, :] // CHUNK) <= (idx[:, None] // CHUNK)
    dmat = jnp.where(visible[None], jnp.exp(lg[:, None, None] * dist[None]), 0.0)
    qdec = jnp.broadcast_to(jnp.exp(lg[:, None] * (idx + 1.0))[:, :, None], (RET_HEADS, blk, LANES))
    kdec = jnp.broadcast_to(jnp.exp(lg[:, None] * (blk - 1.0 - idx))[:, None, :], (RET_HEADS, 8, blk))
    block_decay = tuple(float(math.exp(g * blk)) for g in log_g)

    seq_block = pl.BlockSpec((batch, RET_STEP_BLOCKS * blk, w), lambda j: (0, j, 0))
    const3 = lambda j: (0, 0, 0)
    shaped = lambda a: a.reshape(batch, seq, w)
    out = pl.pallas_call(
        functools.partial(_retention_kernel, block_decay=block_decay),
        out_shape=jax.ShapeDtypeStruct((batch, seq, w), BF16),
        grid=(nb // RET_STEP_BLOCKS,),
        in_specs=[seq_block] * 4 + [
            pl.BlockSpec(dmat.shape, const3), pl.BlockSpec(qdec.shape, const3),
            pl.BlockSpec(kdec.shape, const3)],
        out_specs=seq_block,
        scratch_shapes=[pltpu.VMEM((batch, RET_HEADS, RET_HEAD_DIM, RET_HEAD_DIM), F32)],
        compiler_params=pltpu.CompilerParams(
            dimension_semantics=("arbitrary",), vmem_limit_bytes=VMEM_LIMIT_BYTES),
        name="retention",
    )(shaped(rq), shaped(rk), shaped(rv), shaped(rg), dmat, qdec, kdec)
    return out.reshape(n, w)


def _attention_kernel(q_ref, kn_ref, kr_ref, vt_ref, out_ref,
                      m_ref, l_ref, acc_ref, jump_ref, qt_ref):
    tq = q_ref.shape[0]
    gw = ATTN_Q_GROUP
    kb = ATTN_KEY_BLOCK
    tk = vt_ref.shape[-1]
    groups = range(tq // gw)
    diag_blocks = tq // kb
    i = pl.program_id(2)
    first_diag = i * diag_blocks

    def keys(j, size=kb):
        start = pl.multiple_of(j * kb, kb)
        return jnp.concatenate(
            [kn_ref[pl.ds(start, size), :], kr_ref[pl.ds(start, size), :]], axis=1)

    def values_t(j, size=kb):
        return jnp.concatenate([vt_ref[j * (kb // tk) + t] for t in range(size // tk)], axis=1)

    def chunk_mask(shape, key0, g):
        kc = (lax.broadcasted_iota(jnp.int32, shape, 0) + key0) // CHUNK
        qc = (lax.broadcasted_iota(jnp.int32, shape, 1) + g * gw) // CHUNK
        return kc <= qc

    def two_pass(g, k, vt, key0=None):
        cols = slice(g * gw, (g + 1) * gw)
        st = _dot_nt(k, q_ref[g * gw:(g + 1) * gw, :])
        if key0 is not None:
            st = jnp.where(chunk_mask(st.shape, key0, g), st, -1e30)
        m_old = m_ref[:, cols]
        m_new = jnp.maximum(m_old, jnp.max(st, axis=0, keepdims=True))
        alpha = jnp.exp2(m_old - m_new)
        p = jnp.exp2(st - m_new)
        l_ref[:, cols] = alpha * l_ref[:, cols] + jnp.sum(p, axis=0, keepdims=True)
        acc_ref[:, cols] = alpha * acc_ref[:, cols] + _dot(vt, p.astype(BF16))
        m_ref[:, cols] = m_new

    def single_pass(j, nblocks, items):
        k = jnp.concatenate([keys(j + t) for t in range(nblocks)], axis=0)
        vt = jnp.concatenate([values_t(j + t) for t in range(nblocks)], axis=1)
        used = sorted({g for g, _, _ in items})
        cols = {g: slice(g * gw, (g + 1) * gw) for g in used}
        qs = {g: qt_ref[:, cols[g]] for g in used}
        m_old = {g: m_ref[:, cols[g]] for g in used}
        l_new = {g: l_ref[:, cols[g]] for g in used}
        acc_new = {g: acc_ref[:, cols[g]] for g in used}
        smax = {}

        def qk(item):
            g, a, _ = item
            return _dot(k[a:a + ATTN_KEY_SLICE, :], qs[g])

        pending = [qk(item) for item in items[:ATTN_LOOKAHEAD]]
        for n, (g, a, key0) in enumerate(items):
            if n + ATTN_LOOKAHEAD < len(items):
                pending.append(qk(items[n + ATTN_LOOKAHEAD]))
            st = pending.pop(0)
            if key0 is not None:
                st = jnp.where(chunk_mask(st.shape, key0, g), st, -1e30)
            p = jnp.exp2(st - m_old[g])
            cmax = jnp.max(st, axis=0, keepdims=True)
            smax[g] = cmax if g not in smax else jnp.maximum(smax[g], cmax)
            l_new[g] = l_new[g] + jnp.sum(p, axis=0, keepdims=True)
            acc_new[g] = acc_new[g] + _dot(vt[:, a:a + ATTN_KEY_SLICE], p.astype(BF16))
        for g in used:
            m_new = jnp.maximum(m_old[g], smax[g])
            alpha = jnp.exp2(m_old[g] - m_new)
            l_ref[:, cols[g]] = alpha * l_new[g]
            acc_ref[:, cols[g]] = alpha * acc_new[g]
            m_ref[:, cols[g]] = m_new
            jump_ref[:, cols[g]] = jnp.maximum(jump_ref[:, cols[g]], smax[g] - m_old[g])

    full_items = [(g, a, None) for a in range(0, tq, ATTN_KEY_SLICE) for g in groups]
    diag_items = [(g, a, a if a >= g * gw else None)
                  for a in range(0, tq, ATTN_KEY_SLICE) for g in groups if a < (g + 1) * gw]

    def fast_body(j, carry):
        single_pass(j * diag_blocks, diag_blocks, full_items)
        return carry

    def safe_body(j, carry):
        k = keys(j)
        vt = values_t(j)
        for g in groups:
            two_pass(g, k, vt)
        return carry

    qt_ref[...] = q_ref[...].T
    m_ref[...] = jnp.max(_dot_nt(keys(first_diag, CHUNK), q_ref[...]), axis=0, keepdims=True)
    l_ref[...] = jnp.zeros_like(l_ref)
    acc_ref[...] = jnp.zeros_like(acc_ref)
    jump_ref[...] = jnp.zeros_like(jump_ref)
    lax.fori_loop(0, i, fast_body, 0)
    single_pass(first_diag, diag_blocks, diag_items)

    @pl.when(jnp.max(jump_ref[...]) > ATTN_MAX_JUMP)
    def _():
        m_ref[...] = jnp.full_like(m_ref, -1e30)
        l_ref[...] = jnp.zeros_like(l_ref)
        acc_ref[...] = jnp.zeros_like(acc_ref)
        lax.fori_loop(0, first_diag, safe_body, 0)
        for d in range(diag_blocks):
            for g in groups:
                visible = min(kb, (g + 1) * gw - d * kb)
                if visible > 0:
                    two_pass(g, keys(first_diag + d, visible), values_t(first_diag + d, visible),
                             d * kb)

    out_ref[...] = (acc_ref[...] * (1.0 / l_ref[...])).astype(BF16)


def _attention(q, kn, kr, vt, batch, seq):
    n = q.shape[0]
    tq = ATTN_Q_BLOCK
    tk = ATTN_KV_BLOCK
    assert seq % tq == 0 and tq % ATTN_KEY_BLOCK == 0 and tq % ATTN_Q_GROUP == 0, (seq, tq)
    assert ATTN_KEY_BLOCK % tk == 0 and ATTN_Q_GROUP % tk == 0 and tk % ATTN_KEY_SLICE == 0
    assert ATTN_KEY_SLICE % CHUNK == 0 and ATTN_Q_GROUP % ATTN_KEY_SLICE == 0
    nq = seq // tq
    return pl.pallas_call(
        _attention_kernel,
        out_shape=jax.ShapeDtypeStruct((MLA_HEADS * MLA_V_DIM, n), BF16),
        grid=(batch, MLA_HEADS, nq),
        in_specs=[pl.BlockSpec((tq, MLA_QK_PAD), lambda b, h, i: (b * nq + i, h)),
                  pl.BlockSpec((seq, MLA_NOPE_DIM), lambda b, h, i: (b, h)),
                  pl.BlockSpec((seq, LANES), lambda b, h, i: (b, 0)),
                  pl.BlockSpec((seq // tk, MLA_V_DIM, tk), lambda b, h, i: (b, h, 0))],
        out_specs=pl.BlockSpec((MLA_V_DIM, tq), lambda b, h, i: (h, b * nq + i)),
        scratch_shapes=[pltpu.VMEM((1, tq), F32), pltpu.VMEM((1, tq), F32),
                        pltpu.VMEM((MLA_V_DIM, tq), F32), pltpu.VMEM((1, tq), F32),
                        pltpu.VMEM((MLA_QK_PAD, tq), BF16)],
        compiler_params=pltpu.CompilerParams(
            dimension_semantics=("arbitrary", "arbitrary", "arbitrary"),
            vmem_limit_bytes=VMEM_LIMIT_BYTES),
        name="attention",
    )(q, kn, kr, vt)


def _out_ffn_kernel(x_ref, ret_ref, att_ref, p_ref, wo_ref, gffn_ref, wg_ref, wu_ref, wd_ref,
                    gple_ref, wpg_ref, wpp_ref, gfin_ref, out_ref, *, final_norm):
    tm = x_ref.shape[0]
    part = tm // OUT_PARTS
    n_ret = ret_ref.shape[1]
    for first in range(0, OUT_PARTS, OUT_INTERLEAVE):
        parts = tuple(slice(i * part, (i + 1) * part) for i in range(first, first + OUT_INTERLEAVE))
        each = range(len(parts))
        o = [_dot(ret_ref[r, :], wo_ref[:n_ret, :]) + _dot_tn(att_ref[:, r], wo_ref[n_ret:, :])
             for r in parts]
        pe = [_dot(p_ref[r, :].astype(BF16), wpp_ref[...]) for r in parts]
        x1 = [x_ref[r, :] + o[i] for i, r in enumerate(parts)]
        h = [_rms(x1[i], gffn_ref[...]).astype(BF16) for i in each]
        gu = [(_dot(h[i], wg_ref[...]), _dot(h[i], wu_ref[...])) for i in each]
        act = [(g * _sigmoid(g) * u).astype(BF16) for g, u in gu]
        x2 = [x1[i] + _dot(act[i], wd_ref[...]) for i in each]
        h2 = [_rms(x2[i], gple_ref[...]).astype(BF16) for i in each]
        gate = [_sigmoid(_dot(h2[i], wpg_ref[...])) for i in each]
        for i, r in enumerate(parts):
            x3 = x2[i] + gate[i] * pe[i]
            if final_norm:
                x3 = _rms(x3, gfin_ref[...])
            out_ref[r, :] = x3


def _out_ffn(x2, ret, att, p2, wo, gffn, wg, wu, wd, gple, wpg, wpp, gfin, final_norm):
    n, dm = x2.shape
    tm = OUT_ROWS
    assert n % tm == 0 and tm % (OUT_PARTS * LANES) == 0, (n, tm)
    row = lambda i: (i, 0)
    const = lambda i: (0, 0)
    resident = lambda a: pl.BlockSpec(a.shape, const, pipeline_mode=pl.Buffered(1))
    return pl.pallas_call(
        functools.partial(_out_ffn_kernel, final_norm=final_norm),
        out_shape=jax.ShapeDtypeStruct((n, dm), F32),
        grid=(n // tm,),
        in_specs=[pl.BlockSpec((tm, dm), row), pl.BlockSpec((tm, ret.shape[1]), row),
                  pl.BlockSpec((att.shape[0], tm), lambda i: (0, i)),
                  pl.BlockSpec((tm, p2.shape[1]), row),
                  resident(wo), resident(gffn), resident(wg), resident(wu), resident(wd),
                  resident(gple), resident(wpg), resident(wpp), resident(gfin)],
        out_specs=pl.BlockSpec((tm, dm), row),
        compiler_params=pltpu.CompilerParams(
            dimension_semantics=("arbitrary",), vmem_limit_bytes=OUT_VMEM_LIMIT_BYTES),
        name="out_ffn",
    )(x2, ret, att, p2, wo, gffn, wg, wu, wd, gple, wpg, wpp, gfin)


def _prep_in_weights(w_uq, w_ukv):
    q_lora = w_uq.shape[0]
    uq = w_uq.reshape(q_lora, MLA_HEADS, MLA_QK_DIM)
    rope = uq[:, :, MLA_NOPE_DIM:]
    rot = jnp.concatenate([-rope[..., MLA_ROPE_DIM // 2:], rope[..., :MLA_ROPE_DIM // 2]], axis=-1)
    wuq = jnp.concatenate([uq, rot], axis=-1).reshape(q_lora, MLA_HEADS * MLA_QK_PAD)
    kv_lora = w_ukv.shape[0]
    ukv = w_ukv.reshape(kv_lora, MLA_HEADS, MLA_NOPE_DIM + MLA_V_DIM)
    wuk = ukv[:, :, :MLA_NOPE_DIM].reshape(kv_lora, -1)
    wuvt = ukv[:, :, MLA_NOPE_DIM:].reshape(kv_lora, -1).T
    return wuq.astype(BF16), wuk.astype(BF16), wuvt.astype(BF16)


def _rope_consts():
    def inv(half):
        return jnp.exp(-math.log(ROPE_BASE) * jnp.arange(half, dtype=F32) / half)
    inv_r = inv(RET_HEAD_DIM // 2)
    inv_m = inv(MLA_ROPE_DIM // 2)
    rows = [jnp.tile(inv_r, LANES // inv_r.size), jnp.tile(inv_m, LANES // inv_m.size)]
    return jnp.concatenate([jnp.stack(rows), jnp.zeros((6, LANES), F32)], axis=0)


def kernel(x, p, positions, mix_norm_g, w_in, q_norm_g, w_uq, kv_norm_g, w_ukv, w_o, ffn_norm_g,
           w_ffn_gate, w_ffn_up, w_ffn_down, ple_norm_g, w_ple_gate, w_ple_proj, final_norm_g):
    batch, seq, dm = x.shape
    depth = w_in.shape[0]
    n = batch * seq
    x2 = x.reshape(n, dm)
    pos2 = positions.reshape(n // LANES, LANES)
    rc = _rope_consts()
    vec = lambda g: g.reshape(1, -1).astype(F32)
    for i in range(depth):
        wuq, wuk, wuvt = _prep_in_weights(w_uq[i], w_ukv[i])
        (rq, rk, rv, rg, q, kn, kr, vt), (wo, wg, wu, wd, wpg, wpp) = _in_proj(
            x2, pos2, vec(mix_norm_g[i]), w_in[i].T, vec(q_norm_g[i]), wuq, vec(kv_norm_g[i]), wuk,
            wuvt,
            rc, [w_o[i], w_ffn_gate[i], w_ffn_up[i], w_ffn_down[i], w_ple_gate[i], w_ple_proj[i]])
        ret = _retention(rq, rk, rv, rg, batch, seq)
        att = _attention(q, kn, kr, vt, batch, seq)
        x2 = _out_ffn(
            x2, ret, att, p[i].reshape(n, -1), wo, vec(ffn_norm_g[i]), wg, wu, wd,
            vec(ple_norm_g[i]), wpg, wpp, vec(final_norm_g), final_norm=(i == depth - 1))
    return x2.reshape(batch, seq, dm)
```

```python
import functools
import math

import numpy as np
import jax
import jax.numpy as jnp
from jax import lax
from jax.experimental import pallas as pl
from jax.experimental.pallas import tpu as pltpu

F32 = jnp.float32
BF16 = jnp.bfloat16

CHUNK = 64
RET_HEADS = 4
RET_HEAD_DIM = 128
MLA_HEADS = 4
MLA_NOPE_DIM = 128
MLA_ROPE_DIM = 64
MLA_V_DIM = 128
MLA_QK_DIM = MLA_NOPE_DIM + MLA_ROPE_DIM
ROPE_BASE = 10000.0
RMS_EPS = 1e-6
GN_EPS = 1e-5

LANES = 128
MLA_QK_PAD = 2 * LANES
VMEM_LIMIT_BYTES = 56 * 1024 * 1024
OUT_VMEM_LIMIT_BYTES = 60 * 1024 * 1024

IN_PROJ_ROWS = 1024
IN_PROJ_PARTS = 4
RET_BLOCK = 256
RET_STEP_BLOCKS = 4
ATTN_Q_BLOCK = 2048
ATTN_Q_GROUP = 512
ATTN_KEY_BLOCK = 1024
ATTN_KV_BLOCK = 512
ATTN_MAX_JUMP = 32.0
ATTN_KEY_SLICE = 256
ATTN_LOOKAHEAD = 3
OUT_ROWS = 1024
OUT_PARTS = 4
OUT_INTERLEAVE = 2


def _rms(x, g):
    return x * lax.rsqrt(jnp.mean(x * x, axis=-1, keepdims=True) + RMS_EPS) * g


def _sigmoid(x):
    return 1.0 / (1.0 + jnp.exp(-x))


def _dot(a, b):
    return jnp.dot(a, b, preferred_element_type=F32)


def _dot_nt(a, b):
    return lax.dot_general(a, b, (((1,), (1,)), ((), ())), preferred_element_type=F32)


def _dot_tn(a, b):
    return lax.dot_general(a, b, (((0,), (0,)), ((), ())), preferred_element_type=F32)


def _rot_half_lanes(x):
    return pltpu.roll(x, LANES // 2, 1)


def _in_proj_kernel(*refs, q_scale, n_cast):
    (x_ref, pos_ref, gmix_ref, win_ref, qg_ref, wuq_ref, kvg_ref, wuk_ref, wuvt_ref,
     rc_ref) = refs[:10]
    cast_in = refs[10:10 + n_cast]
    rq_ref, rk_ref, rv_ref, rg_ref, q_ref, kn_ref, kr_ref, vt_ref = refs[10 + n_cast:18 + n_cast]
    cast_out = refs[18 + n_cast:-1]
    win_bf16_ref = refs[-1]

    @pl.when(pl.program_id(0) == 0)
    def _():
        n_out = win_ref.shape[0]
        for a in range(0, n_out, LANES):
            rows = min(LANES, n_out - a)
            blk = win_ref[a:a + rows, :]
            if rows < LANES:
                blk = jnp.concatenate([blk, jnp.zeros((LANES - rows, blk.shape[1]), F32)], axis=0)
            win_bf16_ref[:, a:a + LANES] = blk.T.astype(BF16)

    for src, dst in zip(cast_in, cast_out):
        dst[...] = src[...].astype(BF16)

    tm = x_ref.shape[0]
    part = tm // IN_PROJ_PARTS
    parts = [slice(i * part, (i + 1) * part) for i in range(IN_PROJ_PARTS)]
    zs = [_dot(_rms(x_ref[r, :], gmix_ref[...]).astype(BF16), win_bf16_ref[...])
          for r in parts]

    lane = lax.broadcasted_iota(jnp.int32, (1, LANES), 1)
    half = LANES // 2
    quarter = LANES // 4
    d = RET_HEAD_DIM
    w = RET_HEADS * d
    k_scale = d ** -0.5
    o = 4 * w
    q_lora = qg_ref.shape[-1]
    kv_lora = kvg_ref.shape[-1]
    tk = vt_ref.shape[-1]
    vt_cols = min(tk, part)
    pos_rows = pos_ref[...].astype(F32)
    pos_cols = jnp.concatenate(
        [pos_rows, jnp.zeros((LANES - pos_rows.shape[0], LANES), F32)], axis=0).T

    for r, z in zip(parts, zs):
        anchor = lax.bitcast_convert_type(z[:, 0:1], jnp.uint32)
        zero = lax.shift_right_logical(lax.shift_right_logical(anchor, jnp.uint32(16)), jnp.uint32(16))
        pos = jnp.concatenate([pos_cols[:, c:c + 1] for c in range(r.start // LANES, r.stop // LANES)],
                              axis=0) + zero.astype(F32)
        h2, h4 = part // 2, part // 4
        ang2 = jnp.where(lane < half, pos[:h2], pos[h2:]) * rc_ref[0:1, :]
        c2, s2 = jnp.cos(ang2), jnp.sin(ang2)
        c2h, s2h = _rot_half_lanes(c2), _rot_half_lanes(s2)
        cos_r = jnp.concatenate([jnp.where(lane < half, c2, c2h),
                                 jnp.where(lane < half, c2h, c2)], axis=0)
        sin_r = jnp.concatenate([jnp.where(lane < half, -s2, s2h),
                                 jnp.where(lane < half, -s2h, s2)], axis=0)
        ang4 = jnp.where(lane < quarter, pos[:h4], jnp.where(
            lane < half, pos[h4:h2], jnp.where(lane < 3 * quarter, pos[h2:h2 + h4], pos[h2 + h4:])
        )) * rc_ref[1:2, :]
        c4 = {0: jnp.cos(ang4)}
        s4 = {0: jnp.sin(ang4)}
        for k in (1, 2, 3):
            c4[k] = pltpu.roll(c4[0], k * quarter, 1)
            s4[k] = pltpu.roll(s4[0], k * quarter, 1)

        def low_half_table(t):
            return jnp.concatenate(
                [jnp.where(lane < quarter, t[(4 - k) % 4],
                           jnp.where(lane < half, t[(5 - k) % 4], 0.0)) for k in range(4)], axis=0)

        cos_m = low_half_table(c4)
        sin_m = low_half_table(s4)

        for hh in range(RET_HEADS):
            q = z[:, hh * d:(hh + 1) * d]
            rq_ref[r, hh * d:(hh + 1) * d] = (q * cos_r + _rot_half_lanes(q) * sin_r).astype(BF16)
            k = z[:, w + hh * d:w + (hh + 1) * d]
            rk_ref[r, hh * d:(hh + 1) * d] = (
                (k * cos_r + _rot_half_lanes(k) * sin_r) * k_scale).astype(BF16)
        rv_ref[r, :] = z[:, 2 * w:3 * w].astype(BF16)
        g = z[:, 3 * w:4 * w]
        rg_ref[r, :] = g * _sigmoid(g)

        cq = z[:, o:o + q_lora]
        ckv = z[:, o + q_lora:o + q_lora + kv_lora]

        def rope_low_half(y):
            rot = jnp.where(lane < quarter, -pltpu.roll(y, 3 * quarter, 1), pltpu.roll(y, quarter, 1))
            return y * cos_m + rot * sin_m

        kpe = z[:, o + q_lora + kv_lora:o + q_lora + kv_lora + LANES]
        kr_ref[r, :] = rope_low_half(kpe).astype(BF16)

        qq = _dot(_rms(cq, qg_ref[...]).astype(BF16), wuq_ref[...])
        for hh in range(MLA_HEADS):
            b = hh * MLA_QK_PAD
            q_ref[hh, r, 0:LANES] = (qq[:, b:b + LANES] * q_scale).astype(BF16)
            y = qq[:, b + LANES:b + 2 * LANES]
            q_ref[hh, r, LANES:2 * LANES] = (
                (y * cos_m + _rot_half_lanes(y) * sin_m) * q_scale).astype(BF16)

        ckvn = _rms(ckv, kvg_ref[...]).astype(BF16)
        kn = _dot(ckvn, wuk_ref[...]).astype(BF16)
        for hh in range(MLA_HEADS):
            kn_ref[hh, r, :] = kn[:, hh * MLA_NOPE_DIM:(hh + 1) * MLA_NOPE_DIM]
        vt = _dot_nt(wuvt_ref[...], ckvn).astype(BF16)
        for a in range(r.start, r.stop, vt_cols):
            vt_ref[a // tk, :, a % tk:a % tk + vt_cols] = vt[:, a - r.start:a - r.start + vt_cols]


def _in_proj(x2, pos2, gmix, win, qg, wuq, kvg, wuk, wuvt, rc, cast_weights):
    n, dm = x2.shape
    tm = IN_PROJ_ROWS
    assert n % tm == 0 and tm % (IN_PROJ_PARTS * LANES) == 0 and tm // LANES % 8 == 0, (n, tm)
    steps = n // tm
    bf16_rows = 16
    for a in cast_weights:
        assert a.shape[0] % (steps * bf16_rows) == 0, a.shape
    cast_specs = [pl.BlockSpec((a.shape[0] // steps, a.shape[1]), lambda i: (i, 0))
                  for a in cast_weights]
    cast_shapes = tuple(jax.ShapeDtypeStruct(a.shape, BF16) for a in cast_weights)
    w = RET_HEADS * RET_HEAD_DIM
    const = lambda i: (0, 0)
    row = lambda i: (i, 0)
    full = lambda a: pl.BlockSpec(a.shape, const)
    out_shapes = (
        jax.ShapeDtypeStruct((n, w), BF16),
        jax.ShapeDtypeStruct((n, w), BF16),
        jax.ShapeDtypeStruct((n, w), BF16),
        jax.ShapeDtypeStruct((n, w), F32),
        jax.ShapeDtypeStruct((MLA_HEADS, n, MLA_QK_PAD), BF16),
        jax.ShapeDtypeStruct((MLA_HEADS, n, MLA_NOPE_DIM), BF16),
        jax.ShapeDtypeStruct((n, LANES), BF16),
    )
    tk = ATTN_KV_BLOCK
    vt_shape = jax.ShapeDtypeStruct((n // tk, MLA_HEADS * MLA_V_DIM, tk), BF16)
    out_specs = tuple(
        pl.BlockSpec((tm, s.shape[1]), row) if len(s.shape) == 2
        else pl.BlockSpec((s.shape[0], tm, s.shape[2]), lambda i: (0, i, 0))
        for s in out_shapes) + (
        pl.BlockSpec((tm // tk, MLA_HEADS * MLA_V_DIM, tk), lambda i: (i, 0, 0)),)
    q_scale = MLA_QK_DIM ** -0.5 * math.log2(math.e)
    outs = pl.pallas_call(
        functools.partial(_in_proj_kernel, q_scale=q_scale, n_cast=len(cast_weights)),
        out_shape=out_shapes + (vt_shape,) + cast_shapes,
        grid=(steps,),
        in_specs=[pl.BlockSpec((tm, dm), row), pl.BlockSpec((tm // LANES, LANES), row),
                  full(gmix), pl.BlockSpec(win.shape, const, pipeline_mode=pl.Buffered(1)),
                  full(qg), full(wuq), full(kvg), full(wuk), full(wuvt),
                  full(rc)] + cast_specs,
        out_specs=out_specs + tuple(cast_specs),
        scratch_shapes=[pltpu.VMEM((win.shape[1], -(-win.shape[0] // LANES) * LANES), BF16)],
        compiler_params=pltpu.CompilerParams(
            dimension_semantics=("arbitrary",), vmem_limit_bytes=VMEM_LIMIT_BYTES),
        name="in_proj",
    )(x2, pos2, gmix, win, qg, wuq, kvg, wuk, wuvt, rc, *cast_weights)
    n_main = len(out_shapes) + 1
    return outs[:n_main], outs[n_main:]


def _retention_kernel(rq_ref, rk_ref, rv_ref, rg_ref, dmat_ref, qdec_ref, kdec_ref, out_ref,
                      state_ref, *, block_decay):
    @pl.when(pl.program_id(0) == 0)
    def _():
        state_ref[...] = jnp.zeros_like(state_ref)

    d = RET_HEAD_DIM
    blk = dmat_ref.shape[-1]
    nblk = rq_ref.shape[1] // blk
    seqs = range(rq_ref.shape[0])
    chains = [(t, b, hh) for t in range(nblk) for b in seqs for hh in range(RET_HEADS)]
    state = {(b, hh): state_ref[b, hh] for b in seqs for hh in range(RET_HEADS)}

    def front(c):
        t, b, hh = c
        rows = slice(t * blk, (t + 1) * blk)
        sl = slice(hh * d, (hh + 1) * d)
        q = rq_ref[b, rows, sl]
        v = rv_ref[b, rows, sl]
        kt_bf16 = rk_ref[b, rows, sl].T
        kt = kt_bf16.astype(F32)
        st = state[b, hh]
        scores = _dot(q, kt_bf16)
        carried = _dot(q, st.astype(BF16))
        update = _dot((kt * kdec_ref[hh][0:1, :]).astype(BF16), v)
        state[b, hh] = block_decay[hh] * st + update
        return scores, carried, v

    fronts = {0: front(chains[0])}
    for n, (t, b, hh) in enumerate(chains):
        if n + 1 < len(chains):
            fronts[n + 1] = front(chains[n + 1])
        scores, carried, v = fronts.pop(n)
        rows = slice(t * blk, (t + 1) * blk)
        sl = slice(hh * d, (hh + 1) * d)
        o = _dot((scores * dmat_ref[hh]).astype(BF16), v) + carried * qdec_ref[hh]
        mu = jnp.mean(o, axis=-1, keepdims=True)
        oc = o - mu
        var = jnp.mean(oc * oc, axis=-1, keepdims=True)
        out_ref[b, rows, sl] = (oc * lax.rsqrt(var + GN_EPS) * rg_ref[b, rows, sl]).astype(BF16)
    for (b, hh), st in state.items():
        state_ref[b, hh] = st


def _retention(rq, rk, rv, rg, batch, seq):
    n, w = rq.shape
    blk = RET_BLOCK
    assert seq % (RET_STEP_BLOCKS * blk) == 0 and blk % CHUNK == 0, (seq, blk)
    nb = seq // blk
    log_g = np.log1p(-np.exp2(-5.0 - np.arange(RET_HEADS, dtype=np.float64)))
    idx = jnp.arange(blk, dtype=F32)
    lg = jnp.asarray(log_g, F32)
    dist = jnp.abs(idx[:, None] - idx[None, :])
    visible = (idx[None, :] // CHUNK) <= (idx[:, None] // CHUNK)
    dmat = jnp.where(visible[None], jnp.exp(lg[:, None, None] * dist[None]), 0.0)
    qdec = jnp.broadcast_to(jnp.exp(lg[:, None] * (idx + 1.0))[:, :, None], (RET_HEADS, blk, LANES))
    kdec = jnp.broadcast_to(jnp.exp(lg[:, None] * (blk - 1.0 - idx))[:, None, :], (RET_HEADS, 8, blk))
    block_decay = tuple(float(math.exp(g * blk)) for g in log_g)

    seq_block = pl.BlockSpec((batch, RET_STEP_BLOCKS * blk, w), lambda j: (0, j, 0))
    const3 = lambda j: (0, 0, 0)
    shaped = lambda a: a.reshape(batch, seq, w)
    out = pl.pallas_call(
        functools.partial(_retention_kernel, block_decay=block_decay),
        out_shape=jax.ShapeDtypeStruct((batch, seq, w), BF16),
        grid=(nb // RET_STEP_BLOCKS,),
        in_specs=[seq_block] * 4 + [
            pl.BlockSpec(dmat.shape, const3), pl.BlockSpec(qdec.shape, const3),
            pl.BlockSpec(kdec.shape, const3)],
        out_specs=seq_block,
        scratch_shapes=[pltpu.VMEM((batch, RET_HEADS, RET_HEAD_DIM, RET_HEAD_DIM), F32)],
        compiler_params=pltpu.CompilerParams(
            dimension_semantics=("arbitrary",), vmem_limit_bytes=VMEM_LIMIT_BYTES),
        name="retention",
    )(shaped(rq), shaped(rk), shaped(rv), shaped(rg), dmat, qdec, kdec)
    return out.reshape(n, w)


def _attention_kernel(q_ref, kn_ref, kr_ref, vt_ref, out_ref,
                      m_ref, l_ref, acc_ref, jump_ref, qt_ref):
    tq = q_ref.shape[0]
    gw = ATTN_Q_GROUP
    kb = ATTN_KEY_BLOCK
    tk = vt_ref.shape[-1]
    groups = range(tq // gw)
    diag_blocks = tq // kb
    i = pl.program_id(2)
    first_diag = i * diag_blocks

    def keys(j, size=kb):
        start = pl.multiple_of(j * kb, kb)
        return jnp.concatenate(
            [kn_ref[pl.ds(start, size), :], kr_ref[pl.ds(start, size), :]], axis=1)

    def values_t(j, size=kb):
        return jnp.concatenate([vt_ref[j * (kb // tk) + t] for t in range(size // tk)], axis=1)

    def chunk_mask(shape, key0, g):
        kc = (lax.broadcasted_iota(jnp.int32, shape, 0) + key0) // CHUNK
        qc = (lax.broadcasted_iota(jnp.int32, shape, 1) + g * gw) // CHUNK
        return kc <= qc

    def two_pass(g, k, vt, key0=None):
        cols = slice(g * gw, (g + 1) * gw)
        st = _dot_nt(k, q_ref[g * gw:(g + 1) * gw, :])
        if key0 is not None:
            st = jnp.where(chunk_mask(st.shape, key0, g), st, -1e30)
        m_old = m_ref[:, cols]
        m_new = jnp.maximum(m_old, jnp.max(st, axis=0, keepdims=True))
        alpha = jnp.exp2(m_old - m_new)
        p = jnp.exp2(st - m_new)
        l_ref[:, cols] = alpha * l_ref[:, cols] + jnp.sum(p, axis=0, keepdims=True)
        acc_ref[:, cols] = alpha * acc_ref[:, cols] + _dot(vt, p.astype(BF16))
        m_ref[:, cols] = m_new

    def single_pass(j, nblocks, items):
        k = jnp.concatenate([keys(j + t) for t in range(nblocks)], axis=0)
        vt = jnp.concatenate([values_t(j + t) for t in range(nblocks)], axis=1)
        used = sorted({g for g, _, _ in items})
        cols = {g: slice(g * gw, (g + 1) * gw) for g in used}
        qs = {g: qt_ref[:, cols[g]] for g in used}
        m_old = {g: m_ref[:, cols[g]] for g in used}
        l_new = {g: l_ref[:, cols[g]] for g in used}
        acc_new = {g: acc_ref[:, cols[g]] for g in used}
        smax = {}

        def qk(item):
            g, a, _ = item
            return _dot(k[a:a + ATTN_KEY_SLICE, :], qs[g])

        pending = [qk(item) for item in items[:ATTN_LOOKAHEAD]]
        for n, (g, a, key0) in enumerate(items):
            if n + ATTN_LOOKAHEAD < len(items):
                pending.append(qk(items[n + ATTN_LOOKAHEAD]))
            st = pending.pop(0)
            if key0 is not None:
                st = jnp.where(chunk_mask(st.shape, key0, g), st, -1e30)
            p = jnp.exp2(st - m_old[g])
            cmax = jnp.max(st, axis=0, keepdims=True)
            smax[g] = cmax if g not in smax else jnp.maximum(smax[g], cmax)
            l_new[g] = l_new[g] + jnp.sum(p, axis=0, keepdims=True)
            acc_new[g] = acc_new[g] + _dot(vt[:, a:a + ATTN_KEY_SLICE], p.astype(BF16))
        for g in used:
            m_new = jnp.maximum(m_old[g], smax[g])
            alpha = jnp.exp2(m_old[g] - m_new)
            l_ref[:, cols[g]] = alpha * l_new[g]
            acc_ref[:, cols[g]] = alpha * acc_new[g]
            m_ref[:, cols[g]] = m_new
            jump_ref[:, cols[g]] = jnp.maximum(jump_ref[:, cols[g]], smax[g] - m_old[g])

    full_items = [(g, a, None) for a in range(0, tq, ATTN_KEY_SLICE) for g in groups]
    diag_items = [(g, a, a if a >= g * gw else None)
                  for a in range(0, tq, ATTN_KEY_SLICE) for g in groups if a < (g + 1) * gw]

    def fast_body(j, carry):
        single_pass(j * diag_blocks, diag_blocks, full_items)
        return carry

    def safe_body(j, carry):
        k = keys(j)
        vt = values_t(j)
        for g in groups:
            two_pass(g, k, vt)
        return carry

    qt_ref[...] = q_ref[...].T
    m_ref[...] = jnp.max(_dot_nt(keys(first_diag, CHUNK), q_ref[...]), axis=0, keepdims=True)
    l_ref[...] = jnp.zeros_like(l_ref)
    acc_ref[...] = jnp.zeros_like(acc_ref)
    jump_ref[...] = jnp.zeros_like(jump_ref)
    lax.fori_loop(0, i, fast_body, 0)
    single_pass(first_diag, diag_blocks, diag_items)

    @pl.when(jnp.max(jump_ref[...]) > ATTN_MAX_JUMP)
    def _():
        m_ref[...] = jnp.full_like(m_ref, -1e30)
        l_ref[...] = jnp.zeros_like(l_ref)
        acc_ref[...] = jnp.zeros_like(acc_ref)
        lax.fori_loop(0, first_diag, safe_body, 0)
        for d in range(diag_blocks):
            for g in groups:
                visible = min(kb, (g + 1) * gw - d * kb)
                if visible > 0:
                    two_pass(g, keys(first_diag + d, visible), values_t(first_diag + d, visible),
                             d * kb)

    out_ref[...] = (acc_ref[...] * (1.0 / l_ref[...])).astype(BF16)


def _attention(q, kn, kr, vt, batch, seq):
    n = q.shape[1]
    tq = ATTN_Q_BLOCK
    tk = ATTN_KV_BLOCK
    assert seq % tq == 0 and tq % ATTN_KEY_BLOCK == 0 and tq % ATTN_Q_GROUP == 0, (seq, tq)
    assert ATTN_KEY_BLOCK % tk == 0 and ATTN_Q_GROUP % tk == 0 and tk % ATTN_KEY_SLICE == 0
    assert ATTN_KEY_SLICE % CHUNK == 0 and ATTN_Q_GROUP % ATTN_KEY_SLICE == 0
    nq = seq // tq
    return pl.pallas_call(
        _attention_kernel,
        out_shape=jax.ShapeDtypeStruct((MLA_HEADS * MLA_V_DIM, n), BF16),
        grid=(batch, MLA_HEADS, nq),
        in_specs=[pl.BlockSpec((None, tq, MLA_QK_PAD), lambda b, h, i: (h, b * nq + i, 0)),
                  pl.BlockSpec((None, seq, MLA_NOPE_DIM), lambda b, h, i: (h, b, 0)),
                  pl.BlockSpec((seq, LANES), lambda b, h, i: (b, 0)),
                  pl.BlockSpec((seq // tk, MLA_V_DIM, tk), lambda b, h, i: (b, h, 0))],
        out_specs=pl.BlockSpec((MLA_V_DIM, tq), lambda b, h, i: (h, b * nq + i)),
        scratch_shapes=[pltpu.VMEM((1, tq), F32), pltpu.VMEM((1, tq), F32),
                        pltpu.VMEM((MLA_V_DIM, tq), F32), pltpu.VMEM((1, tq), F32),
                        pltpu.VMEM((MLA_QK_PAD, tq), BF16)],
        compiler_params=pltpu.CompilerParams(
            dimension_semantics=("arbitrary", "arbitrary", "arbitrary"),
            vmem_limit_bytes=VMEM_LIMIT_BYTES),
        name="attention",
    )(q, kn, kr, vt)


def _out_ffn_kernel(x_ref, ret_ref, att_ref, p_ref, wo_ref, gffn_ref, wg_ref, wu_ref, wd_ref,
                    gple_ref, wpg_ref, wpp_ref, gfin_ref, out_ref, *, final_norm):
    tm = x_ref.shape[0]
    part = tm // OUT_PARTS
    n_ret = ret_ref.shape[1]
    for first in range(0, OUT_PARTS, OUT_INTERLEAVE):
        parts = tuple(slice(i * part, (i + 1) * part) for i in range(first, first + OUT_INTERLEAVE))
        each = range(len(parts))
        o = [_dot(ret_ref[r, :], wo_ref[:n_ret, :]) + _dot_tn(att_ref[:, r], wo_ref[n_ret:, :])
             for r in parts]
        pe = [_dot(p_ref[r, :].astype(BF16), wpp_ref[...]) for r in parts]
        x1 = [x_ref[r, :] + o[i] for i, r in enumerate(parts)]
        h = [_rms(x1[i], gffn_ref[...]).astype(BF16) for i in each]
        gu = [(_dot(h[i], wg_ref[...]), _dot(h[i], wu_ref[...])) for i in each]
        act = [(g * _sigmoid(g) * u).astype(BF16) for g, u in gu]
        x2 = [x1[i] + _dot(act[i], wd_ref[...]) for i in each]
        h2 = [_rms(x2[i], gple_ref[...]).astype(BF16) for i in each]
        gate = [_sigmoid(_dot(h2[i], wpg_ref[...])) for i in each]
        for i, r in enumerate(parts):
            x3 = x2[i] + gate[i] * pe[i]
            if final_norm:
                x3 = _rms(x3, gfin_ref[...])
            out_ref[r, :] = x3


def _out_ffn(x2, ret, att, p2, wo, gffn, wg, wu, wd, gple, wpg, wpp, gfin, final_norm):
    n, dm = x2.shape
    tm = OUT_ROWS
    assert n % tm == 0 and tm % (OUT_PARTS * LANES) == 0, (n, tm)
    row = lambda i: (i, 0)
    const = lambda i: (0, 0)
    resident = lambda a: pl.BlockSpec(a.shape, const, pipeline_mode=pl.Buffered(1))
    return pl.pallas_call(
        functools.partial(_out_ffn_kernel, final_norm=final_norm),
        out_shape=jax.ShapeDtypeStruct((n, dm), F32),
        grid=(n // tm,),
        in_specs=[pl.BlockSpec((tm, dm), row), pl.BlockSpec((tm, ret.shape[1]), row),
                  pl.BlockSpec((att.shape[0], tm), lambda i: (0, i)),
                  pl.BlockSpec((tm, p2.shape[1]), row),
                  resident(wo), resident(gffn), resident(wg), resident(wu), resident(wd),
                  resident(gple), resident(wpg), resident(wpp), resident(gfin)],
        out_specs=pl.BlockSpec((tm, dm), row),
        compiler_params=pltpu.CompilerParams(
            dimension_semantics=("arbitrary",), vmem_limit_bytes=OUT_VMEM_LIMIT_BYTES),
        name="out_ffn",
    )(x2, ret, att, p2, wo, gffn, wg, wu, wd, gple, wpg, wpp, gfin)


def _prep_in_weights(w_uq, w_ukv):
    q_lora = w_uq.shape[0]
    uq = w_uq.reshape(q_lora, MLA_HEADS, MLA_QK_DIM)
    rope = uq[:, :, MLA_NOPE_DIM:]
    rot = jnp.concatenate([-rope[..., MLA_ROPE_DIM // 2:], rope[..., :MLA_ROPE_DIM // 2]], axis=-1)
    wuq = jnp.concatenate([uq, rot], axis=-1).reshape(q_lora, MLA_HEADS * MLA_QK_PAD)
    kv_lora = w_ukv.shape[0]
    ukv = w_ukv.reshape(kv_lora, MLA_HEADS, MLA_NOPE_DIM + MLA_V_DIM)
    wuk = ukv[:, :, :MLA_NOPE_DIM].reshape(kv_lora, -1)
    wuvt = ukv[:, :, MLA_NOPE_DIM:].reshape(kv_lora, -1).T
    return wuq.astype(BF16), wuk.astype(BF16), wuvt.astype(BF16)


def _rope_consts():
    def inv(half):
        return jnp.exp(-math.log(ROPE_BASE) * jnp.arange(half, dtype=F32) / half)
    inv_r = inv(RET_HEAD_DIM // 2)
    inv_m = inv(MLA_ROPE_DIM // 2)
    rows = [jnp.tile(inv_r, LANES // inv_r.size), jnp.tile(inv_m, LANES // inv_m.size)]
    return jnp.concatenate([jnp.stack(rows), jnp.zeros((6, LANES), F32)], axis=0)


def kernel(x, p, positions, mix_norm_g, w_in, q_norm_g, w_uq, kv_norm_g, w_ukv, w_o, ffn_norm_g,
           w_ffn_gate, w_ffn_up, w_ffn_down, ple_norm_g, w_ple_gate, w_ple_proj, final_norm_g):
    batch, seq, dm = x.shape
    depth = w_in.shape[0]
    n = batch * seq
    x2 = x.reshape(n, dm)
    pos2 = positions.reshape(n // LANES, LANES)
    rc = _rope_consts()
    vec = lambda g: g.reshape(1, -1).astype(F32)
    for i in range(depth):
        wuq, wuk, wuvt = _prep_in_weights(w_uq[i], w_ukv[i])
        (rq, rk, rv, rg, q, kn, kr, vt), (wo, wg, wu, wd, wpg, wpp) = _in_proj(
            x2, pos2, vec(mix_norm_g[i]), w_in[i].T, vec(q_norm_g[i]), wuq, vec(kv_norm_g[i]), wuk,
            wuvt,
            rc, [w_o[i], w_ffn_gate[i], w_ffn_up[i], w_ffn_down[i], w_ple_gate[i], w_ple_proj[i]])
        ret = _retention(rq, rk, rv, rg, batch, seq)
        att = _attention(q, kn, kr, vt, batch, seq)
        x2 = _out_ffn(
            x2, ret, att, p[i].reshape(n, -1), wo, vec(ffn_norm_g[i]), wg, wu, wd,
            vec(ple_norm_g[i]), wpg, wpp, vec(final_norm_g), final_norm=(i == depth - 1))
    return x2.reshape(batch, seq, dm)
```

```python
import functools
import math

import numpy as np
import jax
import jax.numpy as jnp
from jax import lax
from jax.experimental import pallas as pl
from jax.experimental.pallas import tpu as pltpu

F32 = jnp.float32
BF16 = jnp.bfloat16

CHUNK = 64
RET_HEADS = 4
RET_HEAD_DIM = 128
MLA_HEADS = 4
MLA_NOPE_DIM = 128
MLA_ROPE_DIM = 64
MLA_V_DIM = 128
MLA_QK_DIM = MLA_NOPE_DIM + MLA_ROPE_DIM
ROPE_BASE = 10000.0
RMS_EPS = 1e-6
GN_EPS = 1e-5

LANES = 128
MLA_QK_PAD = 2 * LANES
VMEM_LIMIT_BYTES = 56 * 1024 * 1024
OUT_VMEM_LIMIT_BYTES = 60 * 1024 * 1024

IN_PROJ_ROWS = 1024
IN_PROJ_PARTS = 4
RET_BLOCK = 256
RET_STEP_BLOCKS = 4
ATTN_Q_BLOCK = 2048
ATTN_Q_GROUP = 512
ATTN_KEY_BLOCK = 1024
ATTN_KV_BLOCK = 512
ATTN_MAX_JUMP = 32.0
ATTN_KEY_SLICE = 256
ATTN_LOOKAHEAD = 3
OUT_ROWS = 1024
OUT_PARTS = 4
OUT_INTERLEAVE = 2


def _rms(x, g):
    return x * lax.rsqrt(jnp.mean(x * x, axis=-1, keepdims=True) + RMS_EPS) * g


def _sigmoid(x):
    return 1.0 / (1.0 + jnp.exp(-x))


def _dot(a, b):
    return jnp.dot(a, b, preferred_element_type=F32)


def _dot_nt(a, b):
    return lax.dot_general(a, b, (((1,), (1,)), ((), ())), preferred_element_type=F32)


def _dot_tn(a, b):
    return lax.dot_general(a, b, (((0,), (0,)), ((), ())), preferred_element_type=F32)


def _rot_half_lanes(x):
    return pltpu.roll(x, LANES // 2, 1)


def _in_proj_kernel(*refs, q_scale, n_cast):
    (x_ref, pos_ref, gmix_ref, win_ref, qg_ref, wuq_ref, kvg_ref, wuk_ref, wuvt_ref,
     rc_ref) = refs[:10]
    cast_in = refs[10:10 + n_cast]
    rq_ref, rk_ref, rv_ref, rg_ref, q_ref, kn_ref, kr_ref, vt_ref = refs[10 + n_cast:18 + n_cast]
    cast_out = refs[18 + n_cast:-1]
    win_bf16_ref = refs[-1]

    @pl.when(pl.program_id(0) == 0)
    def _():
        n_out = win_ref.shape[0]
        for a in range(0, n_out, LANES):
            rows = min(LANES, n_out - a)
            blk = win_ref[a:a + rows, :]
            if rows < LANES:
                blk = jnp.concatenate([blk, jnp.zeros((LANES - rows, blk.shape[1]), F32)], axis=0)
            win_bf16_ref[:, a:a + LANES] = blk.T.astype(BF16)

    for src, dst in zip(cast_in, cast_out):
        dst[...] = src[...].astype(BF16)

    tm = x_ref.shape[0]
    part = tm // IN_PROJ_PARTS
    parts = [slice(i * part, (i + 1) * part) for i in range(IN_PROJ_PARTS)]
    zs = [_dot(_rms(x_ref[r, :], gmix_ref[...]).astype(BF16), win_bf16_ref[...])
          for r in parts]

    lane = lax.broadcasted_iota(jnp.int32, (1, LANES), 1)
    half = LANES // 2
    quarter = LANES // 4
    d = RET_HEAD_DIM
    w = RET_HEADS * d
    k_scale = d ** -0.5
    o = 4 * w
    q_lora = qg_ref.shape[-1]
    kv_lora = kvg_ref.shape[-1]
    tk = vt_ref.shape[-1]
    vt_cols = min(tk, part)
    pos_rows = pos_ref[...].astype(F32)
    pos_cols = jnp.concatenate(
        [pos_rows, jnp.zeros((LANES - pos_rows.shape[0], LANES), F32)], axis=0).T

    for r, z in zip(parts, zs):
        anchor = lax.bitcast_convert_type(z[:, 0:1], jnp.uint32)
        zero = lax.shift_right_logical(lax.shift_right_logical(anchor, jnp.uint32(16)), jnp.uint32(16))
        pos = jnp.concatenate([pos_cols[:, c:c + 1] for c in range(r.start // LANES, r.stop // LANES)],
                              axis=0) + zero.astype(F32)
        h2, h4 = part // 2, part // 4
        ang2 = jnp.where(lane < half, pos[:h2], pos[h2:]) * rc_ref[0:1, :]
        c2, s2 = jnp.cos(ang2), jnp.sin(ang2)
        c2h, s2h = _rot_half_lanes(c2), _rot_half_lanes(s2)
        cos_r = jnp.concatenate([jnp.where(lane < half, c2, c2h),
                                 jnp.where(lane < half, c2h, c2)], axis=0)
        sin_r = jnp.concatenate([jnp.where(lane < half, -s2, s2h),
                                 jnp.where(lane < half, -s2h, s2)], axis=0)
        ang4 = jnp.where(lane < quarter, pos[:h4], jnp.where(
            lane < half, pos[h4:h2], jnp.where(lane < 3 * quarter, pos[h2:h2 + h4], pos[h2 + h4:])
        )) * rc_ref[1:2, :]
        c4 = {0: jnp.cos(ang4)}
        s4 = {0: jnp.sin(ang4)}
        for k in (1, 2, 3):
            c4[k] = pltpu.roll(c4[0], k * quarter, 1)
            s4[k] = pltpu.roll(s4[0], k * quarter, 1)

        def low_half_table(t):
            return jnp.concatenate(
                [jnp.where(lane < quarter, t[(4 - k) % 4],
                           jnp.where(lane < half, t[(5 - k) % 4], 0.0)) for k in range(4)], axis=0)

        cos_m = low_half_table(c4)
        sin_m = low_half_table(s4)

        for hh in range(RET_HEADS):
            q = z[:, hh * d:(hh + 1) * d]
            rq_ref[r, hh * d:(hh + 1) * d] = (q * cos_r + _rot_half_lanes(q) * sin_r).astype(BF16)
            k = z[:, w + hh * d:w + (hh + 1) * d]
            rk_ref[r, hh * d:(hh + 1) * d] = (
                (k * cos_r + _rot_half_lanes(k) * sin_r) * k_scale).astype(BF16)
        rv_ref[r, :] = z[:, 2 * w:3 * w].astype(BF16)
        g = z[:, 3 * w:4 * w]
        rg_ref[r, :] = g * _sigmoid(g)

        cq = z[:, o:o + q_lora]
        ckv = z[:, o + q_lora:o + q_lora + kv_lora]

        def rope_low_half(y):
            rot = jnp.where(lane < quarter, -pltpu.roll(y, 3 * quarter, 1), pltpu.roll(y, quarter, 1))
            return y * cos_m + rot * sin_m

        kpe = z[:, o + q_lora + kv_lora:o + q_lora + kv_lora + LANES]
        kr_ref[r, :] = rope_low_half(kpe).astype(BF16)

        qq = _dot(_rms(cq, qg_ref[...]).astype(BF16), wuq_ref[...])
        for hh in range(MLA_HEADS):
            b = hh * MLA_QK_PAD
            q_ref[hh, r, 0:LANES] = (qq[:, b:b + LANES] * q_scale).astype(BF16)
            y = qq[:, b + LANES:b + 2 * LANES]
            q_ref[hh, r, LANES:2 * LANES] = (
                (y * cos_m + _rot_half_lanes(y) * sin_m) * q_scale).astype(BF16)

        ckvn = _rms(ckv, kvg_ref[...]).astype(BF16)
        kn = _dot(ckvn, wuk_ref[...]).astype(BF16)
        for hh in range(MLA_HEADS):
            kn_ref[hh, r, :] = kn[:, hh * MLA_NOPE_DIM:(hh + 1) * MLA_NOPE_DIM]
        vt = _dot_nt(wuvt_ref[...], ckvn).astype(BF16)
        for a in range(r.start, r.stop, vt_cols):
            for hh in range(MLA_HEADS):
                vt_ref[hh, a // tk, :, a % tk:a % tk + vt_cols] = vt[
                    hh * MLA_V_DIM:(hh + 1) * MLA_V_DIM, a - r.start:a - r.start + vt_cols]


def _in_proj(x2, pos2, gmix, win, qg, wuq, kvg, wuk, wuvt, rc, cast_weights):
    n, dm = x2.shape
    tm = IN_PROJ_ROWS
    assert n % tm == 0 and tm % (IN_PROJ_PARTS * LANES) == 0 and tm // LANES % 8 == 0, (n, tm)
    steps = n // tm
    bf16_rows = 16
    for a in cast_weights:
        assert a.shape[0] % (steps * bf16_rows) == 0, a.shape
    cast_specs = [pl.BlockSpec((a.shape[0] // steps, a.shape[1]), lambda i: (i, 0))
                  for a in cast_weights]
    cast_shapes = tuple(jax.ShapeDtypeStruct(a.shape, BF16) for a in cast_weights)
    w = RET_HEADS * RET_HEAD_DIM
    const = lambda i: (0, 0)
    row = lambda i: (i, 0)
    full = lambda a: pl.BlockSpec(a.shape, const)
    out_shapes = (
        jax.ShapeDtypeStruct((n, w), BF16),
        jax.ShapeDtypeStruct((n, w), BF16),
        jax.ShapeDtypeStruct((n, w), BF16),
        jax.ShapeDtypeStruct((n, w), F32),
        jax.ShapeDtypeStruct((MLA_HEADS, n, MLA_QK_PAD), BF16),
        jax.ShapeDtypeStruct((MLA_HEADS, n, MLA_NOPE_DIM), BF16),
        jax.ShapeDtypeStruct((n, LANES), BF16),
    )
    tk = ATTN_KV_BLOCK
    vt_shape = jax.ShapeDtypeStruct((MLA_HEADS, n // tk, MLA_V_DIM, tk), BF16)
    out_specs = tuple(
        pl.BlockSpec((tm, s.shape[1]), row) if len(s.shape) == 2
        else pl.BlockSpec((s.shape[0], tm, s.shape[2]), lambda i: (0, i, 0))
        for s in out_shapes) + (
        pl.BlockSpec((MLA_HEADS, tm // tk, MLA_V_DIM, tk), lambda i: (0, i, 0, 0)),)
    q_scale = MLA_QK_DIM ** -0.5 * math.log2(math.e)
    outs = pl.pallas_call(
        functools.partial(_in_proj_kernel, q_scale=q_scale, n_cast=len(cast_weights)),
        out_shape=out_shapes + (vt_shape,) + cast_shapes,
        grid=(steps,),
        in_specs=[pl.BlockSpec((tm, dm), row), pl.BlockSpec((tm // LANES, LANES), row),
                  full(gmix), pl.BlockSpec(win.shape, const, pipeline_mode=pl.Buffered(1)),
                  full(qg), full(wuq), full(kvg), full(wuk), full(wuvt),
                  full(rc)] + cast_specs,
        out_specs=out_specs + tuple(cast_specs),
        scratch_shapes=[pltpu.VMEM((win.shape[1], -(-win.shape[0] // LANES) * LANES), BF16)],
        compiler_params=pltpu.CompilerParams(
            dimension_semantics=("arbitrary",), vmem_limit_bytes=VMEM_LIMIT_BYTES),
        name="in_proj",
    )(x2, pos2, gmix, win, qg, wuq, kvg, wuk, wuvt, rc, *cast_weights)
    n_main = len(out_shapes) + 1
    return outs[:n_main], outs[n_main:]


def _retention_kernel(rq_ref, rk_ref, rv_ref, rg_ref, dmat_ref, qdec_ref, kdec_ref, out_ref,
                      state_ref, *, block_decay):
    @pl.when(pl.program_id(0) == 0)
    def _():
        state_ref[...] = jnp.zeros_like(state_ref)

    d = RET_HEAD_DIM
    blk = dmat_ref.shape[-1]
    nblk = rq_ref.shape[1] // blk
    seqs = range(rq_ref.shape[0])
    chains = [(t, b, hh) for t in range(nblk) for b in seqs for hh in range(RET_HEADS)]
    state = {(b, hh): state_ref[b, hh] for b in seqs for hh in range(RET_HEADS)}

    def front(c):
        t, b, hh = c
        rows = slice(t * blk, (t + 1) * blk)
        sl = slice(hh * d, (hh + 1) * d)
        q = rq_ref[b, rows, sl]
        v = rv_ref[b, rows, sl]
        kt_bf16 = rk_ref[b, rows, sl].T
        kt = kt_bf16.astype(F32)
        st = state[b, hh]
        scores = _dot(q, kt_bf16)
        carried = _dot(q, st.astype(BF16))
        update = _dot((kt * kdec_ref[hh][0:1, :]).astype(BF16), v)
        state[b, hh] = block_decay[hh] * st + update
        return scores, carried, v

    fronts = {0: front(chains[0])}
    for n, (t, b, hh) in enumerate(chains):
        if n + 1 < len(chains):
            fronts[n + 1] = front(chains[n + 1])
        scores, carried, v = fronts.pop(n)
        rows = slice(t * blk, (t + 1) * blk)
        sl = slice(hh * d, (hh + 1) * d)
        o = _dot((scores * dmat_ref[hh]).astype(BF16), v) + carried * qdec_ref[hh]
        mu = jnp.mean(o, axis=-1, keepdims=True)
        oc = o - mu
        var = jnp.mean(oc * oc, axis=-1, keepdims=True)
        out_ref[b, rows, sl] = (oc * lax.rsqrt(var + GN_EPS) * rg_ref[b, rows, sl]).astype(BF16)
    for (b, hh), st in state.items():
        state_ref[b, hh] = st


def _retention(rq, rk, rv, rg, batch, seq):
    n, w = rq.shape
    blk = RET_BLOCK
    assert seq % (RET_STEP_BLOCKS * blk) == 0 and blk % CHUNK == 0, (seq, blk)
    nb = seq // blk
    log_g = np.log1p(-np.exp2(-5.0 - np.arange(RET_HEADS, dtype=np.float64)))
    idx = jnp.arange(blk, dtype=F32)
    lg = jnp.asarray(log_g, F32)
    dist = jnp.abs(idx[:, None] - idx[None, :])
    visible = (idx[None, :] // CHUNK) <= (idx[:, None] // CHUNK)
    dmat = jnp.where(visible[None], jnp.exp(lg[:, None, None] * dist[None]), 0.0)
    qdec = jnp.broadcast_to(jnp.exp(lg[:, None] * (idx + 1.0))[:, :, None], (RET_HEADS, blk, LANES))
    kdec = jnp.broadcast_to(jnp.exp(lg[:, None] * (blk - 1.0 - idx))[:, None, :], (RET_HEADS, 8, blk))
    block_decay = tuple(float(math.exp(g * blk)) for g in log_g)

    seq_block = pl.BlockSpec((batch, RET_STEP_BLOCKS * blk, w), lambda j: (0, j, 0))
    const3 = lambda j: (0, 0, 0)
    shaped = lambda a: a.reshape(batch, seq, w)
    out = pl.pallas_call(
        functools.partial(_retention_kernel, block_decay=block_decay),
        out_shape=jax.ShapeDtypeStruct((batch, seq, w), BF16),
        grid=(nb // RET_STEP_BLOCKS,),
        in_specs=[seq_block] * 4 + [
            pl.BlockSpec(dmat.shape, const3), pl.BlockSpec(qdec.shape, const3),
            pl.BlockSpec(kdec.shape, const3)],
        out_specs=seq_block,
        scratch_shapes=[pltpu.VMEM((batch, RET_HEADS, RET_HEAD_DIM, RET_HEAD_DIM), F32)],
        compiler_params=pltpu.CompilerParams(
            dimension_semantics=("arbitrary",), vmem_limit_bytes=VMEM_LIMIT_BYTES),
        name="retention",
    )(shaped(rq), shaped(rk), shaped(rv), shaped(rg), dmat, qdec, kdec)
    return out.reshape(n, w)


def _attention_kernel(q_ref, kn_ref, kr_ref, vt_ref, out_ref,
                      m_ref, l_ref, acc_ref, jump_ref, qt_ref):
    tq = q_ref.shape[0]
    gw = ATTN_Q_GROUP
    kb = ATTN_KEY_BLOCK
    tk = vt_ref.shape[-1]
    groups = range(tq // gw)
    diag_blocks = tq // kb
    i = pl.program_id(2)
    first_diag = i * diag_blocks

    def keys(j, size=kb):
        start = pl.multiple_of(j * kb, kb)
        return jnp.concatenate(
            [kn_ref[pl.ds(start, size), :], kr_ref[pl.ds(start, size), :]], axis=1)

    def values_t(j, size=kb):
        return jnp.concatenate([vt_ref[j * (kb // tk) + t] for t in range(size // tk)], axis=1)

    def chunk_mask(shape, key0, g):
        kc = (lax.broadcasted_iota(jnp.int32, shape, 0) + key0) // CHUNK
        qc = (lax.broadcasted_iota(jnp.int32, shape, 1) + g * gw) // CHUNK
        return kc <= qc

    def two_pass(g, k, vt, key0=None):
        cols = slice(g * gw, (g + 1) * gw)
        st = _dot_nt(k, q_ref[g * gw:(g + 1) * gw, :])
        if key0 is not None:
            st = jnp.where(chunk_mask(st.shape, key0, g), st, -1e30)
        m_old = m_ref[:, cols]
        m_new = jnp.maximum(m_old, jnp.max(st, axis=0, keepdims=True))
        alpha = jnp.exp2(m_old - m_new)
        p = jnp.exp2(st - m_new)
        l_ref[:, cols] = alpha * l_ref[:, cols] + jnp.sum(p, axis=0, keepdims=True)
        acc_ref[:, cols] = alpha * acc_ref[:, cols] + _dot(vt, p.astype(BF16))
        m_ref[:, cols] = m_new

    def single_pass(j, nblocks, items):
        k = jnp.concatenate([keys(j + t) for t in range(nblocks)], axis=0)
        vt = jnp.concatenate([values_t(j + t) for t in range(nblocks)], axis=1)
        used = sorted({g for g, _, _ in items})
        cols = {g: slice(g * gw, (g + 1) * gw) for g in used}
        qs = {g: qt_ref[:, cols[g]] for g in used}
        m_old = {g: m_ref[:, cols[g]] for g in used}
        l_new = {g: l_ref[:, cols[g]] for g in used}
        acc_new = {g: acc_ref[:, cols[g]] for g in used}
        smax = {}

        def qk(item):
            g, a, _ = item
            return _dot(k[a:a + ATTN_KEY_SLICE, :], qs[g])

        pending = [qk(item) for item in items[:ATTN_LOOKAHEAD]]
        for n, (g, a, key0) in enumerate(items):
            if n + ATTN_LOOKAHEAD < len(items):
                pending.append(qk(items[n + ATTN_LOOKAHEAD]))
            st = pending.pop(0)
            if key0 is not None:
                st = jnp.where(chunk_mask(st.shape, key0, g), st, -1e30)
            p = jnp.exp2(st - m_old[g])
            cmax = jnp.max(st, axis=0, keepdims=True)
            smax[g] = cmax if g not in smax else jnp.maximum(smax[g], cmax)
            l_new[g] = l_new[g] + jnp.sum(p, axis=0, keepdims=True)
            acc_new[g] = acc_new[g] + _dot(vt[:, a:a + ATTN_KEY_SLICE], p.astype(BF16))
        for g in used:
            m_new = jnp.maximum(m_old[g], smax[g])
            alpha = jnp.exp2(m_old[g] - m_new)
            l_ref[:, cols[g]] = alpha * l_new[g]
            acc_ref[:, cols[g]] = alpha * acc_new[g]
            m_ref[:, cols[g]] = m_new
            jump_ref[:, cols[g]] = jnp.maximum(jump_ref[:, cols[g]], smax[g] - m_old[g])

    full_items = [(g, a, None) for a in range(0, tq, ATTN_KEY_SLICE) for g in groups]
    diag_items = [(g, a, a if a >= g * gw else None)
                  for a in range(0, tq, ATTN_KEY_SLICE) for g in groups if a < (g + 1) * gw]

    def fast_body(j, carry):
        single_pass(j * diag_blocks, diag_blocks, full_items)
        return carry

    def safe_body(j, carry):
        k = keys(j)
        vt = values_t(j)
        for g in groups:
            two_pass(g, k, vt)
        return carry

    qt_ref[...] = q_ref[...].T
    m_ref[...] = jnp.max(_dot_nt(keys(first_diag, CHUNK), q_ref[...]), axis=0, keepdims=True)
    l_ref[...] = jnp.zeros_like(l_ref)
    acc_ref[...] = jnp.zeros_like(acc_ref)
    jump_ref[...] = jnp.zeros_like(jump_ref)
    lax.fori_loop(0, i, fast_body, 0)
    single_pass(first_diag, diag_blocks, diag_items)

    @pl.when(jnp.max(jump_ref[...]) > ATTN_MAX_JUMP)
    def _():
        m_ref[...] = jnp.full_like(m_ref, -1e30)
        l_ref[...] = jnp.zeros_like(l_ref)
        acc_ref[...] = jnp.zeros_like(acc_ref)
        lax.fori_loop(0, first_diag, safe_body, 0)
        for d in range(diag_blocks):
            for g in groups:
                visible = min(kb, (g + 1) * gw - d * kb)
                if visible > 0:
                    two_pass(g, keys(first_diag + d, visible), values_t(first_diag + d, visible),
                             d * kb)

    out_ref[...] = (acc_ref[...] * (1.0 / l_ref[...])).astype(BF16)


def _attention(q, kn, kr, vt, batch, seq):
    n = q.shape[1]
    tq = ATTN_Q_BLOCK
    tk = ATTN_KV_BLOCK
    assert seq % tq == 0 and tq % ATTN_KEY_BLOCK == 0 and tq % ATTN_Q_GROUP == 0, (seq, tq)
    assert ATTN_KEY_BLOCK % tk == 0 and ATTN_Q_GROUP % tk == 0 and tk % ATTN_KEY_SLICE == 0
    assert ATTN_KEY_SLICE % CHUNK == 0 and ATTN_Q_GROUP % ATTN_KEY_SLICE == 0
    nq = seq // tq
    return pl.pallas_call(
        _attention_kernel,
        out_shape=jax.ShapeDtypeStruct((MLA_HEADS * MLA_V_DIM, n), BF16),
        grid=(batch, MLA_HEADS, nq),
        in_specs=[pl.BlockSpec((None, tq, MLA_QK_PAD), lambda b, h, i: (h, b * nq + i, 0)),
                  pl.BlockSpec((None, seq, MLA_NOPE_DIM), lambda b, h, i: (h, b, 0)),
                  pl.BlockSpec((seq, LANES), lambda b, h, i: (b, 0)),
                  pl.BlockSpec((None, seq // tk, MLA_V_DIM, tk), lambda b, h, i: (h, b, 0, 0))],
        out_specs=pl.BlockSpec((MLA_V_DIM, tq), lambda b, h, i: (h, b * nq + i)),
        scratch_shapes=[pltpu.VMEM((1, tq), F32), pltpu.VMEM((1, tq), F32),
                        pltpu.VMEM((MLA_V_DIM, tq), F32), pltpu.VMEM((1, tq), F32),
                        pltpu.VMEM((MLA_QK_PAD, tq), BF16)],
        compiler_params=pltpu.CompilerParams(
            dimension_semantics=("arbitrary", "arbitrary", "arbitrary"),
            vmem_limit_bytes=VMEM_LIMIT_BYTES),
        name="attention",
    )(q, kn, kr, vt)


def _out_ffn_kernel(x_ref, ret_ref, att_ref, p_ref, wo_ref, gffn_ref, wg_ref, wu_ref, wd_ref,
                    gple_ref, wpg_ref, wpp_ref, gfin_ref, out_ref, *, final_norm):
    tm = x_ref.shape[0]
    part = tm // OUT_PARTS
    n_ret = ret_ref.shape[1]
    for first in range(0, OUT_PARTS, OUT_INTERLEAVE):
        parts = tuple(slice(i * part, (i + 1) * part) for i in range(first, first + OUT_INTERLEAVE))
        each = range(len(parts))
        o = [_dot(ret_ref[r, :], wo_ref[:n_ret, :]) + _dot_tn(att_ref[:, r], wo_ref[n_ret:, :])
             for r in parts]
        pe = [_dot(p_ref[r, :].astype(BF16), wpp_ref[...]) for r in parts]
        x1 = [x_ref[r, :] + o[i] for i, r in enumerate(parts)]
        h = [_rms(x1[i], gffn_ref[...]).astype(BF16) for i in each]
        gu = [(_dot(h[i], wg_ref[...]), _dot(h[i], wu_ref[...])) for i in each]
        act = [(g * _sigmoid(g) * u).astype(BF16) for g, u in gu]
        x2 = [x1[i] + _dot(act[i], wd_ref[...]) for i in each]
        h2 = [_rms(x2[i], gple_ref[...]).astype(BF16) for i in each]
        gate = [_sigmoid(_dot(h2[i], wpg_ref[...])) for i in each]
        for i, r in enumerate(parts):
            x3 = x2[i] + gate[i] * pe[i]
            if final_norm:
                x3 = _rms(x3, gfin_ref[...])
            out_ref[r, :] = x3


def _out_ffn(x2, ret, att, p2, wo, gffn, wg, wu, wd, gple, wpg, wpp, gfin, final_norm):
    n, dm = x2.shape
    tm = OUT_ROWS
    assert n % tm == 0 and tm % (OUT_PARTS * LANES) == 0, (n, tm)
    row = lambda i: (i, 0)
    const = lambda i: (0, 0)
    resident = lambda a: pl.BlockSpec(a.shape, const, pipeline_mode=pl.Buffered(1))
    return pl.pallas_call(
        functools.partial(_out_ffn_kernel, final_norm=final_norm),
        out_shape=jax.ShapeDtypeStruct((n, dm), F32),
        grid=(n // tm,),
        in_specs=[pl.BlockSpec((tm, dm), row), pl.BlockSpec((tm, ret.shape[1]), row),
                  pl.BlockSpec((att.shape[0], tm), lambda i: (0, i)),
                  pl.BlockSpec((tm, p2.shape[1]), row),
                  resident(wo), resident(gffn), resident(wg), resident(wu), resident(wd),
                  resident(gple), resident(wpg), resident(wpp), resident(gfin)],
        out_specs=pl.BlockSpec((tm, dm), row),
        compiler_params=pltpu.CompilerParams(
            dimension_semantics=("arbitrary",), vmem_limit_bytes=OUT_VMEM_LIMIT_BYTES),
        name="out_ffn",
    )(x2, ret, att, p2, wo, gffn, wg, wu, wd, gple, wpg, wpp, gfin)


def _prep_in_weights(w_uq, w_ukv):
    q_lora = w_uq.shape[0]
    uq = w_uq.reshape(q_lora, MLA_HEADS, MLA_QK_DIM)
    rope = uq[:, :, MLA_NOPE_DIM:]
    rot = jnp.concatenate([-rope[..., MLA_ROPE_DIM // 2:], rope[..., :MLA_ROPE_DIM // 2]], axis=-1)
    wuq = jnp.concatenate([uq, rot], axis=-1).reshape(q_lora, MLA_HEADS * MLA_QK_PAD)
    kv_lora = w_ukv.shape[0]
    ukv = w_ukv.reshape(kv_lora, MLA_HEADS, MLA_NOPE_DIM + MLA_V_DIM)
    wuk = ukv[:, :, :MLA_NOPE_DIM].reshape(kv_lora, -1)
    wuvt = ukv[:, :, MLA_NOPE_DIM:].reshape(kv_lora, -1).T
    return wuq.astype(BF16), wuk.astype(BF16), wuvt.astype(BF16)


def _rope_consts():
    def inv(half):
        return jnp.exp(-math.log(ROPE_BASE) * jnp.arange(half, dtype=F32) / half)
    inv_r = inv(RET_HEAD_DIM // 2)
    inv_m = inv(MLA_ROPE_DIM // 2)
    rows = [jnp.tile(inv_r, LANES // inv_r.size), jnp.tile(inv_m, LANES // inv_m.size)]
    return jnp.concatenate([jnp.stack(rows), jnp.zeros((6, LANES), F32)], axis=0)


def kernel(x, p, positions, mix_norm_g, w_in, q_norm_g, w_uq, kv_norm_g, w_ukv, w_o, ffn_norm_g,
           w_ffn_gate, w_ffn_up, w_ffn_down, ple_norm_g, w_ple_gate, w_ple_proj, final_norm_g):
    batch, seq, dm = x.shape
    depth = w_in.shape[0]
    n = batch * seq
    x2 = x.reshape(n, dm)
    pos2 = positions.reshape(n // LANES, LANES)
    rc = _rope_consts()
    vec = lambda g: g.reshape(1, -1).astype(F32)
    for i in range(depth):
        wuq, wuk, wuvt = _prep_in_weights(w_uq[i], w_ukv[i])
        (rq, rk, rv, rg, q, kn, kr, vt), (wo, wg, wu, wd, wpg, wpp) = _in_proj(
            x2, pos2, vec(mix_norm_g[i]), w_in[i].T, vec(q_norm_g[i]), wuq, vec(kv_norm_g[i]), wuk,
            wuvt,
            rc, [w_o[i], w_ffn_gate[i], w_ffn_up[i], w_ffn_down[i], w_ple_gate[i], w_ple_proj[i]])
        ret = _retention(rq, rk, rv, rg, batch, seq)
        att = _attention(q, kn, kr, vt, batch, seq)
        x2 = _out_ffn(
            x2, ret, att, p[i].reshape(n, -1), wo, vec(ffn_norm_g[i]), wg, wu, wd,
            vec(ple_norm_g[i]), wpg, wpp, vec(final_norm_g), final_norm=(i == depth - 1))
    return x2.reshape(batch, seq, dm)
```

```python
import functools
import math

import numpy as np
import jax
import jax.numpy as jnp
from jax import lax
from jax.experimental import pallas as pl
from jax.experimental.pallas import tpu as pltpu

F32 = jnp.float32
BF16 = jnp.bfloat16

CHUNK = 64
RET_HEADS = 4
RET_HEAD_DIM = 128
MLA_HEADS = 4
MLA_NOPE_DIM = 128
MLA_ROPE_DIM = 64
MLA_V_DIM = 128
MLA_QK_DIM = MLA_NOPE_DIM + MLA_ROPE_DIM
ROPE_BASE = 10000.0
RMS_EPS = 1e-6
GN_EPS = 1e-5

LANES = 128
MLA_QK_PAD = 2 * LANES
VMEM_LIMIT_BYTES = 56 * 1024 * 1024
OUT_VMEM_LIMIT_BYTES = 60 * 1024 * 1024

IN_PROJ_ROWS = 1024
IN_PROJ_PARTS = 4
RET_BLOCK = 256
RET_STEP_BLOCKS = 4
ATTN_Q_BLOCK = 2048
ATTN_Q_GROUP = 512
ATTN_KEY_BLOCK = 1024
ATTN_KV_BLOCK = 512
ATTN_MAX_JUMP = 32.0
ATTN_KEY_SLICE = 256
ATTN_LOOKAHEAD = 3
OUT_ROWS = 1024
OUT_PARTS = 4
OUT_INTERLEAVE = 2


def _rms(x, g):
    return x * lax.rsqrt(jnp.mean(x * x, axis=-1, keepdims=True) + RMS_EPS) * g


def _sigmoid(x):
    return 1.0 / (1.0 + jnp.exp(-x))


def _dot(a, b):
    return jnp.dot(a, b, preferred_element_type=F32)


def _dot_nt(a, b):
    return lax.dot_general(a, b, (((1,), (1,)), ((), ())), preferred_element_type=F32)


def _dot_tn(a, b):
    return lax.dot_general(a, b, (((0,), (0,)), ((), ())), preferred_element_type=F32)


def _rot_half_lanes(x):
    return pltpu.roll(x, LANES // 2, 1)


def _in_proj_kernel(*refs, q_scale, n_cast):
    (x_ref, pos_ref, gmix_ref, win_ref, qg_ref, wuq_ref, kvg_ref, wuk_ref, wuvt_ref,
     rc_ref) = refs[:10]
    cast_in = refs[10:10 + n_cast]
    rq_ref, rk_ref, rv_ref, rg_ref, q_ref, kn_ref, kr_ref, vt_ref = refs[10 + n_cast:18 + n_cast]
    cast_out = refs[18 + n_cast:-1]
    win_bf16_ref = refs[-1]

    @pl.when(pl.program_id(0) == 0)
    def _():
        n_out = win_ref.shape[0]
        for a in range(0, n_out, LANES):
            rows = min(LANES, n_out - a)
            blk = win_ref[a:a + rows, :]
            if rows < LANES:
                blk = jnp.concatenate([blk, jnp.zeros((LANES - rows, blk.shape[1]), F32)], axis=0)
            win_bf16_ref[:, a:a + LANES] = blk.T.astype(BF16)

    for src, dst in zip(cast_in, cast_out):
        dst[...] = src[...].astype(BF16)

    tm = x_ref.shape[0]
    part = tm // IN_PROJ_PARTS
    parts = [slice(i * part, (i + 1) * part) for i in range(IN_PROJ_PARTS)]
    zs = [_dot(_rms(x_ref[r, :], gmix_ref[...]).astype(BF16), win_bf16_ref[...])
          for r in parts]

    lane = lax.broadcasted_iota(jnp.int32, (1, LANES), 1)
    half = LANES // 2
    quarter = LANES // 4
    d = RET_HEAD_DIM
    w = RET_HEADS * d
    k_scale = d ** -0.5
    o = 4 * w
    q_lora = qg_ref.shape[-1]
    kv_lora = kvg_ref.shape[-1]
    tk = vt_ref.shape[-1]
    vt_cols = min(tk, part)
    pos_rows = pos_ref[...].astype(F32)
    pos_cols = jnp.concatenate(
        [pos_rows, jnp.zeros((LANES - pos_rows.shape[0], LANES), F32)], axis=0).T

    for r, z in zip(parts, zs):
        anchor = lax.bitcast_convert_type(z[:, 0:1], jnp.uint32)
        zero = lax.shift_right_logical(lax.shift_right_logical(anchor, jnp.uint32(16)), jnp.uint32(16))
        pos = jnp.concatenate([pos_cols[:, c:c + 1] for c in range(r.start // LANES, r.stop // LANES)],
                              axis=0) + zero.astype(F32)
        h2, h4 = part // 2, part // 4
        ang2 = jnp.where(lane < half, pos[:h2], pos[h2:]) * rc_ref[0:1, :]
        c2, s2 = jnp.cos(ang2), jnp.sin(ang2)
        c2h, s2h = _rot_half_lanes(c2), _rot_half_lanes(s2)
        cos_r = jnp.concatenate([jnp.where(lane < half, c2, c2h),
                                 jnp.where(lane < half, c2h, c2)], axis=0)
        sin_r = jnp.concatenate([jnp.where(lane < half, -s2, s2h),
                                 jnp.where(lane < half, -s2h, s2)], axis=0)
        ang4 = jnp.where(lane < quarter, pos[:h4], jnp.where(
            lane < half, pos[h4:h2], jnp.where(lane < 3 * quarter, pos[h2:h2 + h4], pos[h2 + h4:])
        )) * rc_ref[1:2, :]
        c4 = {0: jnp.cos(ang4)}
        s4 = {0: jnp.sin(ang4)}
        for k in (1, 2, 3):
            c4[k] = pltpu.roll(c4[0], k * quarter, 1)
            s4[k] = pltpu.roll(s4[0], k * quarter, 1)

        def low_half_table(t):
            return jnp.concatenate(
                [jnp.where(lane < quarter, t[(4 - k) % 4],
                           jnp.where(lane < half, t[(5 - k) % 4], 0.0)) for k in range(4)], axis=0)

        cos_m = low_half_table(c4)
        sin_m = low_half_table(s4)

        for hh in range(RET_HEADS):
            q = z[:, hh * d:(hh + 1) * d]
            rq_ref[r, hh * d:(hh + 1) * d] = (q * cos_r + _rot_half_lanes(q) * sin_r).astype(BF16)
            k = z[:, w + hh * d:w + (hh + 1) * d]
            rk_ref[r, hh * d:(hh + 1) * d] = (
                (k * cos_r + _rot_half_lanes(k) * sin_r) * k_scale).astype(BF16)
        rv_ref[r, :] = z[:, 2 * w:3 * w].astype(BF16)
        g = z[:, 3 * w:4 * w]
        rg_ref[r, :] = (g * _sigmoid(g)).astype(BF16)

        cq = z[:, o:o + q_lora]
        ckv = z[:, o + q_lora:o + q_lora + kv_lora]

        def rope_low_half(y):
            rot = jnp.where(lane < quarter, -pltpu.roll(y, 3 * quarter, 1), pltpu.roll(y, quarter, 1))
            return y * cos_m + rot * sin_m

        kpe = z[:, o + q_lora + kv_lora:o + q_lora + kv_lora + LANES]
        kr_ref[r, :] = rope_low_half(kpe).astype(BF16)

        qq = _dot(_rms(cq, qg_ref[...]).astype(BF16), wuq_ref[...])
        for hh in range(MLA_HEADS):
            b = hh * MLA_QK_PAD
            q_ref[hh, r, 0:LANES] = (qq[:, b:b + LANES] * q_scale).astype(BF16)
            y = qq[:, b + LANES:b + 2 * LANES]
            q_ref[hh, r, LANES:2 * LANES] = (
                (y * cos_m + _rot_half_lanes(y) * sin_m) * q_scale).astype(BF16)

        ckvn = _rms(ckv, kvg_ref[...]).astype(BF16)
        kn = _dot(ckvn, wuk_ref[...]).astype(BF16)
        for hh in range(MLA_HEADS):
            kn_ref[hh, r, :] = kn[:, hh * MLA_NOPE_DIM:(hh + 1) * MLA_NOPE_DIM]
        vt = _dot_nt(wuvt_ref[...], ckvn).astype(BF16)
        for a in range(r.start, r.stop, vt_cols):
            vt_ref[a // tk, :, a % tk:a % tk + vt_cols] = vt[:, a - r.start:a - r.start + vt_cols]


def _in_proj(x2, pos2, gmix, win, qg, wuq, kvg, wuk, wuvt, rc, cast_weights):
    n, dm = x2.shape
    tm = IN_PROJ_ROWS
    assert n % tm == 0 and tm % (IN_PROJ_PARTS * LANES) == 0 and tm // LANES % 8 == 0, (n, tm)
    steps = n // tm
    bf16_rows = 16
    for a in cast_weights:
        assert a.shape[0] % (steps * bf16_rows) == 0, a.shape
    cast_specs = [pl.BlockSpec((a.shape[0] // steps, a.shape[1]), lambda i: (i, 0))
                  for a in cast_weights]
    cast_shapes = tuple(jax.ShapeDtypeStruct(a.shape, BF16) for a in cast_weights)
    w = RET_HEADS * RET_HEAD_DIM
    const = lambda i: (0, 0)
    row = lambda i: (i, 0)
    full = lambda a: pl.BlockSpec(a.shape, const)
    out_shapes = (
        jax.ShapeDtypeStruct((n, w), BF16),
        jax.ShapeDtypeStruct((n, w), BF16),
        jax.ShapeDtypeStruct((n, w), BF16),
        jax.ShapeDtypeStruct((n, w), BF16),
        jax.ShapeDtypeStruct((MLA_HEADS, n, MLA_QK_PAD), BF16),
        jax.ShapeDtypeStruct((MLA_HEADS, n, MLA_NOPE_DIM), BF16),
        jax.ShapeDtypeStruct((n, LANES), BF16),
    )
    tk = ATTN_KV_BLOCK
    vt_shape = jax.ShapeDtypeStruct((n // tk, MLA_HEADS * MLA_V_DIM, tk), BF16)
    out_specs = tuple(
        pl.BlockSpec((tm, s.shape[1]), row) if len(s.shape) == 2
        else pl.BlockSpec((s.shape[0], tm, s.shape[2]), lambda i: (0, i, 0))
        for s in out_shapes) + (
        pl.BlockSpec((tm // tk, MLA_HEADS * MLA_V_DIM, tk), lambda i: (i, 0, 0)),)
    q_scale = MLA_QK_DIM ** -0.5 * math.log2(math.e)
    outs = pl.pallas_call(
        functools.partial(_in_proj_kernel, q_scale=q_scale, n_cast=len(cast_weights)),
        out_shape=out_shapes + (vt_shape,) + cast_shapes,
        grid=(steps,),
        in_specs=[pl.BlockSpec((tm, dm), row), pl.BlockSpec((tm // LANES, LANES), row),
                  full(gmix), pl.BlockSpec(win.shape, const, pipeline_mode=pl.Buffered(1)),
                  full(qg), full(wuq), full(kvg), full(wuk), full(wuvt),
                  full(rc)] + cast_specs,
        out_specs=out_specs + tuple(cast_specs),
        scratch_shapes=[pltpu.VMEM((win.shape[1], -(-win.shape[0] // LANES) * LANES), BF16)],
        compiler_params=pltpu.CompilerParams(
            dimension_semantics=("arbitrary",), vmem_limit_bytes=VMEM_LIMIT_BYTES),
        name="in_proj",
    )(x2, pos2, gmix, win, qg, wuq, kvg, wuk, wuvt, rc, *cast_weights)
    n_main = len(out_shapes) + 1
    return outs[:n_main], outs[n_main:]


def _retention_kernel(rq_ref, rk_ref, rv_ref, rg_ref, dmat_ref, qdec_ref, kdec_ref, out_ref,
                      state_ref, *, block_decay):
    @pl.when(pl.program_id(0) == 0)
    def _():
        state_ref[...] = jnp.zeros_like(state_ref)

    d = RET_HEAD_DIM
    blk = dmat_ref.shape[-1]
    nblk = rq_ref.shape[1] // blk
    seqs = range(rq_ref.shape[0])
    chains = [(t, b, hh) for t in range(nblk) for b in seqs for hh in range(RET_HEADS)]
    state = {(b, hh): state_ref[b, hh] for b in seqs for hh in range(RET_HEADS)}

    def front(c):
        t, b, hh = c
        rows = slice(t * blk, (t + 1) * blk)
        sl = slice(hh * d, (hh + 1) * d)
        q = rq_ref[b, rows, sl]
        v = rv_ref[b, rows, sl]
        kt_bf16 = rk_ref[b, rows, sl].T
        kt = kt_bf16.astype(F32)
        st = state[b, hh]
        scores = _dot(q, kt_bf16)
        carried = _dot(q, st.astype(BF16))
        update = _dot((kt * kdec_ref[hh][0:1, :]).astype(BF16), v)
        state[b, hh] = block_decay[hh] * st + update
        return scores, carried, v

    fronts = {0: front(chains[0])}
    for n, (t, b, hh) in enumerate(chains):
        if n + 1 < len(chains):
            fronts[n + 1] = front(chains[n + 1])
        scores, carried, v = fronts.pop(n)
        rows = slice(t * blk, (t + 1) * blk)
        sl = slice(hh * d, (hh + 1) * d)
        o = _dot((scores * dmat_ref[hh]).astype(BF16), v) + carried * qdec_ref[hh]
        mu = jnp.mean(o, axis=-1, keepdims=True)
        oc = o - mu
        var = jnp.mean(oc * oc, axis=-1, keepdims=True)
        out_ref[b, rows, sl] = (oc * lax.rsqrt(var + GN_EPS) * rg_ref[b, rows, sl]).astype(BF16)
    for (b, hh), st in state.items():
        state_ref[b, hh] = st


def _retention(rq, rk, rv, rg, batch, seq):
    n, w = rq.shape
    blk = RET_BLOCK
    assert seq % (RET_STEP_BLOCKS * blk) == 0 and blk % CHUNK == 0, (seq, blk)
    nb = seq // blk
    log_g = np.log1p(-np.exp2(-5.0 - np.arange(RET_HEADS, dtype=np.float64)))
    idx = jnp.arange(blk, dtype=F32)
    lg = jnp.asarray(log_g, F32)
    dist = jnp.abs(idx[:, None] - idx[None, :])
    visible = (idx[None, :] // CHUNK) <= (idx[:, None] // CHUNK)
    dmat = jnp.where(visible[None], jnp.exp(lg[:, None, None] * dist[None]), 0.0)
    qdec = jnp.broadcast_to(jnp.exp(lg[:, None] * (idx + 1.0))[:, :, None], (RET_HEADS, blk, LANES))
    kdec = jnp.broadcast_to(jnp.exp(lg[:, None] * (blk - 1.0 - idx))[:, None, :], (RET_HEADS, 8, blk))
    block_decay = tuple(float(math.exp(g * blk)) for g in log_g)

    seq_block = pl.BlockSpec((batch, RET_STEP_BLOCKS * blk, w), lambda j: (0, j, 0))
    const3 = lambda j: (0, 0, 0)
    shaped = lambda a: a.reshape(batch, seq, w)
    out = pl.pallas_call(
        functools.partial(_retention_kernel, block_decay=block_decay),
        out_shape=jax.ShapeDtypeStruct((batch, seq, w), BF16),
        grid=(nb // RET_STEP_BLOCKS,),
        in_specs=[seq_block] * 4 + [
            pl.BlockSpec(dmat.shape, const3), pl.BlockSpec(qdec.shape, const3),
            pl.BlockSpec(kdec.shape, const3)],
        out_specs=seq_block,
        scratch_shapes=[pltpu.VMEM((batch, RET_HEADS, RET_HEAD_DIM, RET_HEAD_DIM), F32)],
        compiler_params=pltpu.CompilerParams(
            dimension_semantics=("arbitrary",), vmem_limit_bytes=VMEM_LIMIT_BYTES),
        name="retention",
    )(shaped(rq), shaped(rk), shaped(rv), shaped(rg), dmat, qdec, kdec)
    return out.reshape(n, w)


def _attention_kernel(q_ref, kn_ref, kr_ref, vt_ref, out_ref,
                      m_ref, l_ref, acc_ref, jump_ref, qt_ref):
    tq = q_ref.shape[0]
    gw = ATTN_Q_GROUP
    kb = ATTN_KEY_BLOCK
    tk = vt_ref.shape[-1]
    groups = range(tq // gw)
    diag_blocks = tq // kb
    i = pl.program_id(2)
    first_diag = i * diag_blocks

    def keys(j, size=kb):
        start = pl.multiple_of(j * kb, kb)
        return jnp.concatenate(
            [kn_ref[pl.ds(start, size), :], kr_ref[pl.ds(start, size), :]], axis=1)

    def values_t(j, size=kb):
        return jnp.concatenate([vt_ref[j * (kb // tk) + t] for t in range(size // tk)], axis=1)

    def chunk_mask(shape, key0, g):
        kc = (lax.broadcasted_iota(jnp.int32, shape, 0) + key0) // CHUNK
        qc = (lax.broadcasted_iota(jnp.int32, shape, 1) + g * gw) // CHUNK
        return kc <= qc

    def two_pass(g, k, vt, key0=None):
        cols = slice(g * gw, (g + 1) * gw)
        st = _dot_nt(k, q_ref[g * gw:(g + 1) * gw, :])
        if key0 is not None:
            st = jnp.where(chunk_mask(st.shape, key0, g), st, -1e30)
        m_old = m_ref[:, cols]
        m_new = jnp.maximum(m_old, jnp.max(st, axis=0, keepdims=True))
        alpha = jnp.exp2(m_old - m_new)
        p = jnp.exp2(st - m_new)
        l_ref[:, cols] = alpha * l_ref[:, cols] + jnp.sum(p, axis=0, keepdims=True)
        acc_ref[:, cols] = alpha * acc_ref[:, cols] + _dot(vt, p.astype(BF16))
        m_ref[:, cols] = m_new

    def single_pass(j, nblocks, items):
        k = jnp.concatenate([keys(j + t) for t in range(nblocks)], axis=0)
        vt = jnp.concatenate([values_t(j + t) for t in range(nblocks)], axis=1)
        used = sorted({g for g, _, _ in items})
        cols = {g: slice(g * gw, (g + 1) * gw) for g in used}
        qs = {g: qt_ref[:, cols[g]] for g in used}
        m_old = {g: m_ref[:, cols[g]] for g in used}
        l_new = {g: l_ref[:, cols[g]] for g in used}
        acc_new = {g: acc_ref[:, cols[g]] for g in used}
        smax = {}

        def qk(item):
            g, a, _ = item
            return _dot(k[a:a + ATTN_KEY_SLICE, :], qs[g])

        pending = [qk(item) for item in items[:ATTN_LOOKAHEAD]]
        for n, (g, a, key0) in enumerate(items):
            if n + ATTN_LOOKAHEAD < len(items):
                pending.append(qk(items[n + ATTN_LOOKAHEAD]))
            st = pending.pop(0)
            if key0 is not None:
                st = jnp.where(chunk_mask(st.shape, key0, g), st, -1e30)
            p = jnp.exp2(st - m_old[g])
            cmax = jnp.max(st, axis=0, keepdims=True)
            smax[g] = cmax if g not in smax else jnp.maximum(smax[g], cmax)
            l_new[g] = l_new[g] + jnp.sum(p, axis=0, keepdims=True)
            acc_new[g] = acc_new[g] + _dot(vt[:, a:a + ATTN_KEY_SLICE], p.astype(BF16))
        for g in used:
            m_new = jnp.maximum(m_old[g], smax[g])
            alpha = jnp.exp2(m_old[g] - m_new)
            l_ref[:, cols[g]] = alpha * l_new[g]
            acc_ref[:, cols[g]] = alpha * acc_new[g]
            m_ref[:, cols[g]] = m_new
            jump_ref[:, cols[g]] = jnp.maximum(jump_ref[:, cols[g]], smax[g] - m_old[g])

    full_items = [(g, a, None) for a in range(0, tq, ATTN_KEY_SLICE) for g in groups]
    diag_items = [(g, a, a if a >= g * gw else None)
                  for a in range(0, tq, ATTN_KEY_SLICE) for g in groups if a < (g + 1) * gw]

    def fast_body(j, carry):
        single_pass(j * diag_blocks, diag_blocks, full_items)
        return carry

    def safe_body(j, carry):
        k = keys(j)
        vt = values_t(j)
        for g in groups:
            two_pass(g, k, vt)
        return carry

    qt_ref[...] = q_ref[...].T
    m_ref[...] = jnp.max(_dot_nt(keys(first_diag, CHUNK), q_ref[...]), axis=0, keepdims=True)
    l_ref[...] = jnp.zeros_like(l_ref)
    acc_ref[...] = jnp.zeros_like(acc_ref)
    jump_ref[...] = jnp.zeros_like(jump_ref)
    lax.fori_loop(0, i, fast_body, 0)
    single_pass(first_diag, diag_blocks, diag_items)

    @pl.when(jnp.max(jump_ref[...]) > ATTN_MAX_JUMP)
    def _():
        m_ref[...] = jnp.full_like(m_ref, -1e30)
        l_ref[...] = jnp.zeros_like(l_ref)
        acc_ref[...] = jnp.zeros_like(acc_ref)
        lax.fori_loop(0, first_diag, safe_body, 0)
        for d in range(diag_blocks):
            for g in groups:
                visible = min(kb, (g + 1) * gw - d * kb)
                if visible > 0:
                    two_pass(g, keys(first_diag + d, visible), values_t(first_diag + d, visible),
                             d * kb)

    out_ref[...] = (acc_ref[...] * (1.0 / l_ref[...])).astype(BF16)


def _attention(q, kn, kr, vt, batch, seq):
    n = q.shape[1]
    tq = ATTN_Q_BLOCK
    tk = ATTN_KV_BLOCK
    assert seq % tq == 0 and tq % ATTN_KEY_BLOCK == 0 and tq % ATTN_Q_GROUP == 0, (seq, tq)
    assert ATTN_KEY_BLOCK % tk == 0 and ATTN_Q_GROUP % tk == 0 and tk % ATTN_KEY_SLICE == 0
    assert ATTN_KEY_SLICE % CHUNK == 0 and ATTN_Q_GROUP % ATTN_KEY_SLICE == 0
    nq = seq // tq
    return pl.pallas_call(
        _attention_kernel,
        out_shape=jax.ShapeDtypeStruct((MLA_HEADS * MLA_V_DIM, n), BF16),
        grid=(batch, MLA_HEADS, nq),
        in_specs=[pl.BlockSpec((None, tq, MLA_QK_PAD), lambda b, h, i: (h, b * nq + i, 0)),
                  pl.BlockSpec((None, seq, MLA_NOPE_DIM), lambda b, h, i: (h, b, 0)),
                  pl.BlockSpec((seq, LANES), lambda b, h, i: (b, 0)),
                  pl.BlockSpec((seq // tk, MLA_V_DIM, tk), lambda b, h, i: (b, h, 0))],
        out_specs=pl.BlockSpec((MLA_V_DIM, tq), lambda b, h, i: (h, b * nq + i)),
        scratch_shapes=[pltpu.VMEM((1, tq), F32), pltpu.VMEM((1, tq), F32),
                        pltpu.VMEM((MLA_V_DIM, tq), F32), pltpu.VMEM((1, tq), F32),
                        pltpu.VMEM((MLA_QK_PAD, tq), BF16)],
        compiler_params=pltpu.CompilerParams(
            dimension_semantics=("arbitrary", "arbitrary", "arbitrary"),
            vmem_limit_bytes=VMEM_LIMIT_BYTES),
        name="attention",
    )(q, kn, kr, vt)


def _out_ffn_kernel(x_ref, ret_ref, att_ref, p_ref, wo_ref, gffn_ref, wg_ref, wu_ref, wd_ref,
                    gple_ref, wpg_ref, wpp_ref, gfin_ref, out_ref, *, final_norm):
    tm = x_ref.shape[0]
    part = tm // OUT_PARTS
    n_ret = ret_ref.shape[1]
    for first in range(0, OUT_PARTS, OUT_INTERLEAVE):
        parts = tuple(slice(i * part, (i + 1) * part) for i in range(first, first + OUT_INTERLEAVE))
        each = range(len(parts))
        o = [_dot(ret_ref[r, :], wo_ref[:n_ret, :]) + _dot_tn(att_ref[:, r], wo_ref[n_ret:, :])
             for r in parts]
        pe = [_dot(p_ref[r, :].astype(BF16), wpp_ref[...]) for r in parts]
        x1 = [x_ref[r, :] + o[i] for i, r in enumerate(parts)]
        h = [_rms(x1[i], gffn_ref[...]).astype(BF16) for i in each]
        gu = [(_dot(h[i], wg_ref[...]), _dot(h[i], wu_ref[...])) for i in each]
        act = [(g * _sigmoid(g) * u).astype(BF16) for g, u in gu]
        x2 = [x1[i] + _dot(act[i], wd_ref[...]) for i in each]
        h2 = [_rms(x2[i], gple_ref[...]).astype(BF16) for i in each]
        gate = [_sigmoid(_dot(h2[i], wpg_ref[...])) for i in each]
        for i, r in enumerate(parts):
            x3 = x2[i] + gate[i] * pe[i]
            if final_norm:
                x3 = _rms(x3, gfin_ref[...])
            out_ref[r, :] = x3


def _out_ffn(x2, ret, att, p2, wo, gffn, wg, wu, wd, gple, wpg, wpp, gfin, final_norm):
    n, dm = x2.shape
    tm = OUT_ROWS
    assert n % tm == 0 and tm % (OUT_PARTS * LANES) == 0, (n, tm)
    row = lambda i: (i, 0)
    const = lambda i: (0, 0)
    resident = lambda a: pl.BlockSpec(a.shape, const, pipeline_mode=pl.Buffered(1))
    return pl.pallas_call(
        functools.partial(_out_ffn_kernel, final_norm=final_norm),
        out_shape=jax.ShapeDtypeStruct((n, dm), F32),
        grid=(n // tm,),
        in_specs=[pl.BlockSpec((tm, dm), row), pl.BlockSpec((tm, ret.shape[1]), row),
                  pl.BlockSpec((att.shape[0], tm), lambda i: (0, i)),
                  pl.BlockSpec((tm, p2.shape[1]), row),
                  resident(wo), resident(gffn), resident(wg), resident(wu), resident(wd),
                  resident(gple), resident(wpg), resident(wpp), resident(gfin)],
        out_specs=pl.BlockSpec((tm, dm), row),
        compiler_params=pltpu.CompilerParams(
            dimension_semantics=("arbitrary",), vmem_limit_bytes=OUT_VMEM_LIMIT_BYTES),
        name="out_ffn",
    )(x2, ret, att, p2, wo, gffn, wg, wu, wd, gple, wpg, wpp, gfin)


def _prep_in_weights(w_uq, w_ukv):
    q_lora = w_uq.shape[0]
    uq = w_uq.reshape(q_lora, MLA_HEADS, MLA_QK_DIM)
    rope = uq[:, :, MLA_NOPE_DIM:]
    rot = jnp.concatenate([-rope[..., MLA_ROPE_DIM // 2:], rope[..., :MLA_ROPE_DIM // 2]], axis=-1)
    wuq = jnp.concatenate([uq, rot], axis=-1).reshape(q_lora, MLA_HEADS * MLA_QK_PAD)
    kv_lora = w_ukv.shape[0]
    ukv = w_ukv.reshape(kv_lora, MLA_HEADS, MLA_NOPE_DIM + MLA_V_DIM)
    wuk = ukv[:, :, :MLA_NOPE_DIM].reshape(kv_lora, -1)
    wuvt = ukv[:, :, MLA_NOPE_DIM:].reshape(kv_lora, -1).T
    return wuq.astype(BF16), wuk.astype(BF16), wuvt.astype(BF16)


def _rope_consts():
    def inv(half):
        return jnp.exp(-math.log(ROPE_BASE) * jnp.arange(half, dtype=F32) / half)
    inv_r = inv(RET_HEAD_DIM // 2)
    inv_m = inv(MLA_ROPE_DIM // 2)
    rows = [jnp.tile(inv_r, LANES // inv_r.size), jnp.tile(inv_m, LANES // inv_m.size)]
    return jnp.concatenate([jnp.stack(rows), jnp.zeros((6, LANES), F32)], axis=0)


def kernel(x, p, positions, mix_norm_g, w_in, q_norm_g, w_uq, kv_norm_g, w_ukv, w_o, ffn_norm_g,
           w_ffn_gate, w_ffn_up, w_ffn_down, ple_norm_g, w_ple_gate, w_ple_proj, final_norm_g):
    batch, seq, dm = x.shape
    depth = w_in.shape[0]
    n = batch * seq
    x2 = x.reshape(n, dm)
    pos2 = positions.reshape(n // LANES, LANES)
    rc = _rope_consts()
    vec = lambda g: g.reshape(1, -1).astype(F32)
    for i in range(depth):
        wuq, wuk, wuvt = _prep_in_weights(w_uq[i], w_ukv[i])
        (rq, rk, rv, rg, q, kn, kr, vt), (wo, wg, wu, wd, wpg, wpp) = _in_proj(
            x2, pos2, vec(mix_norm_g[i]), w_in[i].T, vec(q_norm_g[i]), wuq, vec(kv_norm_g[i]), wuk,
            wuvt,
            rc, [w_o[i], w_ffn_gate[i], w_ffn_up[i], w_ffn_down[i], w_ple_gate[i], w_ple_proj[i]])
        ret = _retention(rq, rk, rv, rg, batch, seq)
        att = _attention(q, kn, kr, vt, batch, seq)
        x2 = _out_ffn(
            x2, ret, att, p[i].reshape(n, -1), wo, vec(ffn_norm_g[i]), wg, wu, wd,
            vec(ple_norm_g[i]), wpg, wpp, vec(final_norm_g), final_norm=(i == depth - 1))
    return x2.reshape(batch, seq, dm)
```
